```python
import jax
import jax.numpy as jnp
from jax import lax
import numpy as np

D_MODEL = 1024
BATCH = 2
SEQ = 16384
DEPTH = 4

GRID_W = 64
CTX_LEN = 256
HEAD_DIM = 64
N_BRANCH = 4
BRANCH_W = 256
A_WIDTH = 256
NA_HEADS = 4
NA_WIN_R = 8
NA_WIN_C = 16
GQA_Q_HEADS = 4
GQA_KV_HEADS = 2
GQA_GROUP = GQA_Q_HEADS // GQA_KV_HEADS
Q_BLOCK = 128
ROPE_THETA = 10000.0
RWKV_HEADS = 4
RW = RWKV_HEADS * HEAD_DIM
LORA_W = 64
LORA_A = 64
LORA_G = 128
RW_SHIFT = 3 * RW + LORA_W + LORA_A
D_FF = 2816
NORM_EPS = 1e-6
LN_X_EPS = 64e-5

COLS_A = 3 * A_WIDTH
COLS_NA = 3 * NA_HEADS * HEAD_DIM
COLS_GQA = (GQA_Q_HEADS + 2 * GQA_KV_HEADS) * HEAD_DIM
COLS_RW = RW_SHIFT + LORA_G
D_IN = COLS_A + COLS_NA + COLS_GQA + COLS_RW

kernel_name = 'hybrid_parallel_dit_block'


def rmsnorm(x, g):
    xf = x.astype(jnp.float32)
    y = xf * lax.rsqrt(jnp.mean(xf * xf, axis=-1, keepdims=True) + NORM_EPS)
    return (y * g.astype(jnp.float32)).astype(x.dtype)


def modulate(h, shift, scale):
    return h * (1 + scale) + shift


def dwconv3(x, w):
    xp = jnp.pad(x, ((0, 0), (1, 1), (0, 0)))
    return xp[:, :-2] * w[0] + xp[:, 1:-1] * w[1] + xp[:, 2:] * w[2]


def to_heads(x, n_heads):
    b, t, _ = x.shape
    return x.reshape(b, t, n_heads, HEAD_DIM).transpose(0, 2, 1, 3)


def from_heads(x):
    b, h, t, d = x.shape
    return x.transpose(0, 2, 1, 3).reshape(b, t, h * d)


def axial_rope_tables(n_tok):
    t = jnp.arange(n_tok)
    row = (t // GRID_W).astype(jnp.float32)
    col = (t % GRID_W).astype(jnp.float32)
    n_freq = HEAD_DIM // 4
    inv_freq = ROPE_THETA ** (-jnp.arange(n_freq, dtype=jnp.float32) / n_freq)
    ang = jnp.concatenate([row[:, None] * inv_freq, col[:, None] * inv_freq], axis=-1)
    return jnp.cos(ang), jnp.sin(ang)


def apply_rope(x, cos, sin):
    xf = x.astype(jnp.float32).reshape(x.shape[:-1] + (HEAD_DIM // 2, 2))
    x1, x2 = xf[..., 0], xf[..., 1]
    out = jnp.stack([x1 * cos - x2 * sin, x1 * sin + x2 * cos], axis=-1)
    return out.reshape(x.shape).astype(x.dtype)


def grouped_attend(q, k, v):
    s = jnp.einsum('bngqd,bnkd->bngqk', q, k, preferred_element_type=jnp.float32)
    p = jax.nn.softmax(s, axis=-1).astype(v.dtype)
    return jnp.einsum('bngqk,bnkd->bngqd', p, v)


def mixer_a(z, w_conv):
    bg, cg, xin = z[..., :A_WIDTH], z[..., A_WIDTH:2 * A_WIDTH], z[..., 2 * A_WIDTH:]
    return bg * dwconv3(cg * xin, w_conv)


def na_tables(rows):
    wr = min(NA_WIN_R, rows)
    i = jnp.arange(rows)
    j = jnp.arange(GRID_W)
    rs = jnp.clip(i - wr // 2, 0, rows - wr)
    cs = jnp.clip(j - NA_WIN_C // 2, 0, GRID_W - NA_WIN_C)
    kr = rs[:, None] + jnp.arange(wr)
    kc = cs[:, None] + jnp.arange(NA_WIN_C)
    key_idx = kr[:, None, :, None] * GRID_W + kc[None, :, None, :]
    dr = kr - i[:, None] + (NA_WIN_R - 1)
    dc = kc - j[:, None] + (NA_WIN_C - 1)
    bias_idx = dr[:, None, :, None] * (2 * NA_WIN_C - 1) + dc[None, :, None, :]
    n_keys = wr * NA_WIN_C
    return key_idx.reshape(rows, GRID_W, n_keys), bias_idx.reshape(rows, GRID_W, n_keys)


def na_latent(q, k, v, kc, vc, bias_flat, key_idx, bias_idx):
    b, h, t, d = q.shape
    rows = t // GRID_W
    q_rows = jnp.moveaxis(q.reshape(b, h, rows, GRID_W, d), 2, 0)

    def one_row(args):
        q_row, kidx, bidx = args
        k_nb = jnp.take(k, kidx, axis=2)
        v_nb = jnp.take(v, kidx, axis=2)
        s_loc = jnp.einsum('bhqd,bhqkd->bhqk', q_row, k_nb, preferred_element_type=jnp.float32)
        s_loc = s_loc + bias_flat[:, bidx].astype(jnp.float32)
        s_ctx = jnp.einsum('bhqd,bhkd->bhqk', q_row, kc, preferred_element_type=jnp.float32)
        p = jax.nn.softmax(jnp.concatenate([s_loc, s_ctx], axis=-1), axis=-1).astype(v.dtype)
        nk = k_nb.shape[3]
        return (jnp.einsum('bhqk,bhqkd->bhqd', p[..., :nk], v_nb)
                + jnp.einsum('bhqk,bhkd->bhqd', p[..., nk:], vc))

    out = lax.map(one_row, (q_rows, key_idx, bias_idx))
    return jnp.moveaxis(out, 0, 2).reshape(b, h, t, d)


def mixer_b(zc, zl, na_bias, key_idx, bias_idx, with_ctx):
    scale = HEAD_DIM ** -0.5
    w = NA_HEADS * HEAD_DIM
    qc, kc, vc = to_heads(zc[..., :w], NA_HEADS), to_heads(zc[..., w:2 * w], NA_HEADS), to_heads(zc[..., 2 * w:], NA_HEADS)
    ql, kl, vl = to_heads(zl[..., :w], NA_HEADS), to_heads(zl[..., w:2 * w], NA_HEADS), to_heads(zl[..., 2 * w:], NA_HEADS)
    yl = na_latent(ql * scale, kl, vl, kc, vc, na_bias.reshape(NA_HEADS, -1), key_idx, bias_idx)
    yc = None
    if with_ctx:
        yc = from_heads(grouped_attend((qc * scale)[:, :, None], kc, vc)[:, :, 0])
    return yc, from_heads(yl)


def gqa_qkv(z, q_norm, k_norm):
    nq = GQA_Q_HEADS * HEAD_DIM
    nkv = GQA_KV_HEADS * HEAD_DIM
    q = rmsnorm(to_heads(z[..., :nq], GQA_Q_HEADS), q_norm) * HEAD_DIM ** -0.5
    k = rmsnorm(to_heads(z[..., nq:nq + nkv], GQA_KV_HEADS), k_norm)
    v = to_heads(z[..., nq + nkv:], GQA_KV_HEADS)
    return q, k, v


def mixer_c(zc, zl, q_norm, k_norm, cos, sin, with_ctx):
    qc, kc, vc = gqa_qkv(zc, q_norm, k_norm)
    ql, kl, vl = gqa_qkv(zl, q_norm, k_norm)
    ql = apply_rope(ql, cos, sin)
    kl = apply_rope(kl, cos, sin)
    k_all = jnp.concatenate([kl, kc], axis=2)
    v_all = jnp.concatenate([vl, vc], axis=2)
    b, _, t, d = ql.shape
    q_blocks = jnp.moveaxis(ql.reshape(b, GQA_KV_HEADS, GQA_GROUP, t // Q_BLOCK, Q_BLOCK, d), 3, 0)
    out = lax.map(lambda blk: grouped_attend(blk, k_all, v_all), q_blocks)
    yl = jnp.moveaxis(out, 0, 3).reshape(b, GQA_Q_HEADS, t, d)
    yc = None
    if with_ctx:
        lc = qc.shape[2]
        oc = grouped_attend(qc.reshape(b, GQA_KV_HEADS, GQA_GROUP, lc, d), kc, vc)
        yc = from_heads(oc.reshape(b, GQA_Q_HEADS, lc, d))
    return yc, from_heads(yl)


def rwkv_streams(z, mu, w0, w2, a0, a2, k_k, k_a, reverse):
    b, t, _ = z.shape
    zs = z[..., :RW_SHIFT]
    if reverse:
        nb = jnp.pad(zs[:, 1:], ((0, 0), (0, 1), (0, 0)))
    else:
        nb = jnp.pad(zs[:, :-1], ((0, 0), (1, 0), (0, 0)))
    zs = zs + mu * (nb - zs)
    r, k, v = zs[..., :RW], zs[..., RW:2 * RW], zs[..., 2 * RW:3 * RW]
    lw = zs[..., 3 * RW:3 * RW + LORA_W]
    la = zs[..., 3 * RW + LORA_W:]
    w_log = (w0 + jnp.tanh(lw) @ w2).astype(jnp.float32)
    w = jnp.exp(-jnp.exp(-jax.nn.softplus(-w_log) - 0.5))
    a = jax.nn.sigmoid(a0 + la @ a2)
    kk = (k * k_k).reshape(b, t, RWKV_HEADS, HEAD_DIM).astype(jnp.float32)
    kk = kk * lax.rsqrt(jnp.maximum(jnp.sum(kk * kk, axis=-1, keepdims=True), 1e-24))
    k = k * (1 + (a - 1) * k_a)
    shp = (b, t, RWKV_HEADS, HEAD_DIM)
    return r.reshape(shp), w.reshape(shp), k.reshape(shp), v.reshape(shp), kk, a.reshape(shp)


def rwkv_scan(s0, r, w, k, v, kk, a, reverse, emit):
    seq = tuple(jnp.moveaxis(u.astype(jnp.float32), 1, 0) for u in (r, w, k, v, kk, a))

    def step(s, inp):
        r_t, w_t, k_t, v_t, kk_t, a_t = inp
        sa = jnp.einsum('bhvk,bhk->bhv', s, kk_t)
        s = (s * w_t[:, :, None, :] - sa[..., None] * (kk_t * a_t)[:, :, None, :]
             + v_t[..., None] * k_t[:, :, None, :])
        return s, (jnp.einsum('bhvk,bhk->bhv', s, r_t) if emit else None)

    s, y = lax.scan(step, s0, seq, reverse=reverse)
    return s, (jnp.moveaxis(y, 0, 1) if emit else None)


def rwkv_bonus(streams, r_k):
    r, _, k, v, _, _ = streams
    return jnp.sum(r * k * r_k, axis=-1, keepdims=True) * v


def rwkv_readout(ys, bonus, lg, g2, ln_w, ln_b):
    y = ys[0] + ys[1]
    mean = jnp.mean(y, axis=-1, keepdims=True)
    var = jnp.mean(jnp.square(y - mean), axis=-1, keepdims=True)
    y = (y - mean) * lax.rsqrt(var + LN_X_EPS) * ln_w.reshape(RWKV_HEADS, HEAD_DIM) + ln_b.reshape(RWKV_HEADS, HEAD_DIM)
    y = y.astype(lg.dtype) + bonus[0] + bonus[1]
    b, t = lg.shape[:2]
    return y.reshape(b, t, RW) * (jax.nn.sigmoid(lg) @ g2)


def mixer_d(zc, zl, mu, w0, w2, a0, a2, k_k, k_a, r_k, g2, ln_w, ln_b, with_ctx):
    b = zl.shape[0]
    ys_l, bon_l, ys_c, bon_c = [], [], [], []
    for d in range(2):
        rev = d == 1
        sc = rwkv_streams(zc, mu[d], w0[d], w2[d], a0[d], a2[d], k_k, k_a, rev)
        sl = rwkv_streams(zl, mu[d], w0[d], w2[d], a0[d], a2[d], k_k, k_a, rev)
        s0 = jnp.zeros((b, RWKV_HEADS, HEAD_DIM, HEAD_DIM), jnp.float32)
        s_ctx, yc = rwkv_scan(s0, *sc, reverse=rev, emit=with_ctx)
        _, yl = rwkv_scan(s_ctx, *sl, reverse=rev, emit=True)
        ys_l.append(yl)
        bon_l.append(rwkv_bonus(sl, r_k))
        if with_ctx:
            ys_c.append(yc)
            bon_c.append(rwkv_bonus(sc, r_k))
    out_l = rwkv_readout(ys_l, bon_l, zl[..., RW_SHIFT:], g2, ln_w, ln_b)
    out_c = rwkv_readout(ys_c, bon_c, zc[..., RW_SHIFT:], g2, ln_w, ln_b) if with_ctx else None
    return out_c, out_l


def merge_branches(h, ys, w_branch, w_gate, b_gate, w_o):
    gates = jax.nn.sigmoid(h @ w_gate + b_gate)
    acc = gates[..., :D_MODEL] * (ys[0] @ w_branch[0])
    for i in range(1, N_BRANCH):
        acc = acc + gates[..., i * D_MODEL:(i + 1) * D_MODEL] * (ys[i] @ w_branch[i])
    return acc @ w_o


def conv_ffn(h, w_up, w_conv, w_down):
    u = dwconv3(h @ w_up, w_conv)
    return (u[..., :D_FF] * jax.nn.silu(u[..., D_FF:])) @ w_down


def setup_inputs(seed: int = 0) -> dict:
    key = jax.random.key(seed)
    ks = iter(jax.random.split(key, 40))
    L = DEPTH
    D = D_MODEL

    def nrm(shape, scale):
        return jax.random.normal(next(ks), shape, jnp.float32) * scale

    def unif(shape, lo, hi):
        return jax.random.uniform(next(ks), shape, jnp.float32, lo, hi)

    return {
        'x': nrm((BATCH, SEQ, D), 1.0),
        'c': nrm((BATCH, D), 1.0),
        'ctx': nrm((BATCH, CTX_LEN, D), 1.0),
        'c_ctx': nrm((D,), 1.0),
        'ada_w': nrm((L, D, 6 * D), 0.5 * D ** -0.5),
        'ada_b': nrm((L, 6 * D), 0.02),
        'norm_mix_pre': 1.0 + nrm((L, D), 0.02),
        'norm_mix_post': 1.0 + nrm((L, D), 0.02),
        'norm_ffn_pre': 1.0 + nrm((L, D), 0.02),
        'norm_ffn_post': 1.0 + nrm((L, D), 0.02),
        'w_in': nrm((L, D, D_IN), D ** -0.5),
        'conv_a': nrm((L, 3, A_WIDTH), 3 ** -0.5),
        'na_bias': nrm((L, NA_HEADS, 2 * NA_WIN_R - 1, 2 * NA_WIN_C - 1), 0.1),
        'q_norm': 1.0 + nrm((L, HEAD_DIM), 0.02),
        'k_norm': 1.0 + nrm((L, HEAD_DIM), 0.02),
        'rw_mu': unif((L, 2, RW_SHIFT), 0.0, 1.0),
        'rw_w0': unif((L, 2, RW), -6.5, -1.5),
        'rw_w2': nrm((L, 2, LORA_W, RW), 0.1 * LORA_W ** -0.5),
        'rw_a0': nrm((L, 2, RW), 0.1),
        'rw_a2': nrm((L, 2, LORA_A, RW), 0.5 * LORA_A ** -0.5),
        'rw_kk': 0.85 + nrm((L, RW), 0.05),
        'rw_ka': 1.0 + nrm((L, RW), 0.05),
        'rw_rk': nrm((L, RWKV_HEADS, HEAD_DIM), 0.1),
        'rw_g2': nrm((L, LORA_G, RW), LORA_G ** -0.5),
        'rw_ln_w': 1.0 + nrm((L, RW), 0.02),
        'rw_ln_b': nrm((L, RW), 0.02),
        'w_branch': nrm((L, N_BRANCH, BRANCH_W, D), BRANCH_W ** -0.5),
        'w_gate': nrm((L, D, N_BRANCH * D), D ** -0.5),
        'b_gate': nrm((L, N_BRANCH * D), 0.02),
        'w_o': nrm((L, D, D), D ** -0.5),
        'ffn_up': nrm((L, D, 2 * D_FF), D ** -0.5),
        'ffn_conv': nrm((L, 3, 2 * D_FF), 3 ** -0.5),
        'ffn_down': nrm((L, D_FF, D), D_FF ** -0.5),
    }


def reference(x, c, ctx, c_ctx, ada_w, ada_b, norm_mix_pre, norm_mix_post, norm_ffn_pre, norm_ffn_post,
              w_in, conv_a, na_bias, q_norm, k_norm, rw_mu, rw_w0, rw_w2, rw_a0, rw_a2, rw_kk, rw_ka, rw_rk,
              rw_g2, rw_ln_w, rw_ln_b, w_branch, w_gate, b_gate, w_o, ffn_up, ffn_conv, ffn_down):
    n_lat = x.shape[1]
    rows = n_lat // GRID_W
    cos, sin = axial_rope_tables(n_lat)
    key_idx, bias_idx = na_tables(rows)
    e1 = COLS_A
    e2 = COLS_A + COLS_NA
    e3 = COLS_A + COLS_NA + COLS_GQA
    xl, xc = x, ctx
    for l in range(DEPTH):
        with_ctx = l < DEPTH - 1
        mod_l = jnp.split((jax.nn.silu(c) @ ada_w[l] + ada_b[l])[:, None, :], 6, axis=-1)
        mod_c = jnp.split(jax.nn.silu(c_ctx) @ ada_w[l] + ada_b[l], 6, axis=-1)
        hl = modulate(rmsnorm(xl, norm_mix_pre[l]), mod_l[0], mod_l[1])
        hc = modulate(rmsnorm(xc, norm_mix_pre[l]), mod_c[0], mod_c[1])
        zl = hl @ w_in[l]
        zc = hc @ w_in[l]
        ya_l = mixer_a(zl[..., :e1], conv_a[l])
        yb_c, yb_l = mixer_b(zc[..., e1:e2], zl[..., e1:e2], na_bias[l], key_idx, bias_idx, with_ctx)
        yc_c, yc_l = mixer_c(zc[..., e2:e3], zl[..., e2:e3], q_norm[l], k_norm[l], cos, sin, with_ctx)
        yd_c, yd_l = mixer_d(zc[..., e3:], zl[..., e3:], rw_mu[l], rw_w0[l], rw_w2[l], rw_a0[l], rw_a2[l],
                             rw_kk[l], rw_ka[l], rw_rk[l], rw_g2[l], rw_ln_w[l], rw_ln_b[l], with_ctx)
        m_l = merge_branches(hl, (ya_l, yb_l, yc_l, yd_l), w_branch[l], w_gate[l], b_gate[l], w_o[l])
        xl = xl + mod_l[2] * rmsnorm(m_l, norm_mix_post[l])
        f_l = conv_ffn(modulate(rmsnorm(xl, norm_ffn_pre[l]), mod_l[3], mod_l[4]), ffn_up[l], ffn_conv[l], ffn_down[l])
        xl = xl + mod_l[5] * rmsnorm(f_l, norm_ffn_post[l])
        if with_ctx:
            ya_c = mixer_a(zc[..., :e1], conv_a[l])
            m_c = merge_branches(hc, (ya_c, yb_c, yc_c, yd_c), w_branch[l], w_gate[l], b_gate[l], w_o[l])
            xc = xc + mod_c[2] * rmsnorm(m_c, norm_mix_post[l])
            f_c = conv_ffn(modulate(rmsnorm(xc, norm_ffn_pre[l]), mod_c[3], mod_c[4]), ffn_up[l], ffn_conv[l], ffn_down[l])
            xc = xc + mod_c[5] * rmsnorm(f_c, norm_ffn_post[l])
    return xl
```

```python
import functools

import numpy as np
import jax
import jax.numpy as jnp
from jax import lax
from jax.experimental import pallas as pl
from jax.experimental.pallas import tpu as pltpu

F32 = jnp.float32
BF16 = jnp.bfloat16
HIGHEST = lax.Precision.HIGHEST

D_MODEL = 1024
GRID_W = 64
HEAD_DIM = 64
NA_HEADS = 4
NA_WIN_R = 8
NA_WIN_C = 16
GQA_Q_HEADS = 4
GQA_KV_HEADS = 2
ROPE_THETA = 10000.0
RWKV_HEADS = 4
RW = RWKV_HEADS * HEAD_DIM
LORA_W = 64
LORA_A = 64
LORA_G = 128
RW_SHIFT = 3 * RW + LORA_W + LORA_A
D_FF = 2816
NORM_EPS = 1e-6
LN_X_EPS = 64e-5
COLS_A = 768
COLS_NA = 768
COLS_GQA = 512
COLS_RW = 1024
D_IN = COLS_A + COLS_NA + COLS_GQA + COLS_RW

TM = 256
CH = 64
HALO = 8
FF_CHUNK = 256
GQA_TQ = 512
GQA_TK = 512
NEG = -1e30
VMEM_LIMIT = 56 * 1024 * 1024


def _cparams(sem):
    return pltpu.CompilerParams(dimension_semantics=sem, vmem_limit_bytes=VMEM_LIMIT)


def _const_spec(shape):
    nd = len(shape)
    return pl.BlockSpec(shape, lambda *_: (0,) * nd, pipeline_mode=pl.Buffered(1))


def _dg(a, b, ca, cb, **kw):
    return lax.dot_general(a, b, (((ca,), (cb,)), ((), ())), preferred_element_type=F32, **kw)


def _split(x):
    hi = x.astype(BF16)
    lo = (x - hi.astype(F32)).astype(BF16)
    return hi, lo


def _mm3(a, b, ca=1, cb=0):
    return _dg(a[0], b[0], ca, cb) + _dg(a[0], b[1], ca, cb) + _dg(a[1], b[0], ca, cb)


def _rms_mod(x, g, shift, scale):
    y = x * lax.rsqrt(jnp.mean(x * x, axis=-1, keepdims=True) + NORM_EPS) * g
    return y * (1.0 + scale) + shift


def _rms(x, g):
    return x * lax.rsqrt(jnp.mean(x * x, axis=-1, keepdims=True) + NORM_EPS) * g


def _sigmoid(x):
    return 1.0 / (1.0 + jnp.exp(-x))


def _ada_kernel(c_ref, w_ref, b_ref, o_ref):
    c = c_ref[...]
    s = c * _sigmoid(c)
    o_ref[0] = jnp.dot(s, w_ref[0], precision=HIGHEST, preferred_element_type=F32) + b_ref[0]


def _ada(cvec, ada_w, ada_b):
    depth = ada_w.shape[0]
    nblk = ada_w.shape[2] // D_MODEL
    return pl.pallas_call(
        _ada_kernel,
        grid=(depth, nblk),
        in_specs=[
            pl.BlockSpec((8, D_MODEL), lambda l, j: (0, 0)),
            pl.BlockSpec((1, D_MODEL, D_MODEL), lambda l, j: (l, 0, j)),
            pl.BlockSpec((1, 1, D_MODEL), lambda l, j: (l, 0, j)),
        ],
        out_specs=pl.BlockSpec((1, 8, D_MODEL), lambda l, j: (l, 0, j)),
        out_shape=jax.ShapeDtypeStruct((depth, 8, ada_w.shape[2]), F32),
        compiler_params=_cparams(("parallel", "parallel")),
        name="ada_mod",
    )(cvec, ada_w, ada_b.reshape(depth, 1, -1))


def _inproj_kernel(x_ref, mod_ref, g_ref, w_ref, cos_ref, sin_ref, gqk_ref, bd_ref,
                   za_ref, qn_ref, kn_ref, vn_ref, qg_ref, kg_ref, vg_ref, zr_ref):
    x = x_ref[0]
    m = mod_ref[0, 0]
    h = _rms_mod(x, g_ref[...], m[0:1], m[1:2])
    z = jnp.dot(h.astype(BF16), w_ref[...], preferred_element_type=F32)
    za_ref[0] = z[:, :COLS_A]
    zr_ref[0] = z[:, COLS_A + COLS_NA + COLS_GQA:]
    na = z[:, COLS_A:COLS_A + COLS_NA]
    scale = HEAD_DIM ** -0.5
    for hd in range(NA_HEADS):
        qn_ref[0, hd] = (na[:, hd * 64:(hd + 1) * 64] * scale).astype(BF16)
        kn_ref[0, hd] = na[:, 256 + hd * 64:256 + (hd + 1) * 64].astype(BF16)
        vn_ref[0, hd] = na[:, 512 + hd * 64:512 + (hd + 1) * 64].astype(BF16)
    g = z[:, COLS_A + COLS_NA:COLS_A + COLS_NA + COLS_GQA]
    qk = g[:, :384]
    ms = jnp.dot(qk * qk, bd_ref[...], precision=HIGHEST, preferred_element_type=F32)
    qkn = qk * lax.rsqrt(ms + NORM_EPS) * gqk_ref[...]
    lane = lax.broadcasted_iota(jnp.int32, (TM, 128), 1)
    even = (lane & 1) == 0
    parts = []
    for j in range(3):
        s = qkn[:, j * 128:(j + 1) * 128]
        sw = jnp.where(even, pltpu.roll(s, 127, 1), pltpu.roll(s, 1, 1))
        parts.append(s * cos_ref[:, j * 128:(j + 1) * 128] + sw * sin_ref[:, j * 128:(j + 1) * 128])
    for hd in range(GQA_Q_HEADS):
        p = parts[hd // 2]
        qg_ref[0, hd] = p[:, (hd % 2) * 64:(hd % 2) * 64 + 64].astype(BF16)
    for hd in range(GQA_KV_HEADS):
        kg_ref[0, hd] = parts[2][:, hd * 64:(hd + 1) * 64].astype(BF16)
        vg_ref[0, hd] = g[:, 384 + hd * 64:384 + (hd + 1) * 64].astype(BF16)


def _inproj(xa, modsel, g_pre, w_in, cos_t, sin_t, gqk, bd384, ct):
    b, tt, _ = xa.shape
    nt = tt // TM
    tile = lambda w: pl.BlockSpec((1, TM, w), lambda bi, i: (bi, i, 0))
    heads = lambda nh: pl.BlockSpec((1, nh, TM, 64), lambda bi, i: (bi, 0, i, 0))
    hs = lambda nh: jax.ShapeDtypeStruct((b, nh, tt, 64), BF16)
    return pl.pallas_call(
        _inproj_kernel,
        grid=(b, nt),
        in_specs=[
            tile(D_MODEL),
            pl.BlockSpec((1, 1, 6, D_MODEL), lambda bi, i: (bi, i // ct, 0, 0)),
            _const_spec((1, D_MODEL)),
            _const_spec((D_MODEL, D_IN)),
            pl.BlockSpec((TM, 384), lambda bi, i: (i, 0)),
            pl.BlockSpec((TM, 384), lambda bi, i: (i, 0)),
            _const_spec((1, 384)),
            _const_spec((384, 384)),
        ],
        out_specs=[tile(COLS_A), heads(4), heads(4), heads(4), heads(4), heads(2), heads(2), tile(COLS_RW)],
        out_shape=[jax.ShapeDtypeStruct((b, tt, COLS_A), F32), hs(4), hs(4), hs(4), hs(4), hs(2), hs(2),
                   jax.ShapeDtypeStruct((b, tt, COLS_RW), F32)],
        compiler_params=_cparams(("parallel", "parallel")),
        name="inproj",
    )(xa, modsel, g_pre, w_in, cos_t, sin_t, gqk, bd384)


def _na_kernel(q_ref, k0_ref, k1_ref, k2_ref, kc_ref, v0_ref, v1_ref, v2_ref, vc_ref, bias_ref, o_ref):
    outs = []
    for hd in range(NA_HEADS):
        kcat = jnp.concatenate([k0_ref[0, hd], k1_ref[0, hd], k2_ref[0, hd], kc_ref[0, hd]], axis=0)
        s = _dg(q_ref[0, hd], kcat, 1, 1)
        s_loc = s[:, :3 * TM] + bias_ref[0, hd]
        s_ctx = s[:, 3 * TM:]
        m = jnp.maximum(jnp.max(s_loc, axis=-1, keepdims=True), jnp.max(s_ctx, axis=-1, keepdims=True))
        p_loc = jnp.exp(s_loc - m)
        p_ctx = jnp.exp(s_ctx - m)
        l = jnp.sum(p_loc, axis=-1, keepdims=True) + jnp.sum(p_ctx, axis=-1, keepdims=True)
        vloc = jnp.concatenate([v0_ref[0, hd], v1_ref[0, hd], v2_ref[0, hd]], axis=0)
        o = _dg(p_loc.astype(BF16), vloc, 1, 0) + _dg(p_ctx.astype(BF16), vc_ref[0, hd], 1, 0)
        outs.append(o / l)
    o_ref[0] = jnp.concatenate(outs, axis=-1).astype(o_ref.dtype)


def _na(qn, kn, vn, bias_tab, ct):
    b, _, tt, _ = qn.shape
    nt = tt // TM

    def kv_spec(j):
        if j is None:
            return pl.BlockSpec((1, 4, TM, 64), lambda bi, i: (bi, 0, ct, 0))
        return pl.BlockSpec((1, 4, TM, 64), lambda bi, i: (bi, 0, jnp.clip(i - 1, 0, ct - 3) + j, 0))

    def pattern(i):
        return jnp.where(i == ct, 3, jnp.where(i == 0, 0, jnp.where(i == ct - 1, 2, 1)))

    return pl.pallas_call(
        _na_kernel,
        grid=(b, nt),
        in_specs=[
            pl.BlockSpec((1, 4, TM, 64), lambda bi, i: (bi, 0, i, 0)),
            kv_spec(0), kv_spec(1), kv_spec(2), kv_spec(None),
            kv_spec(0), kv_spec(1), kv_spec(2), kv_spec(None),
            pl.BlockSpec((1, 4, TM, 3 * TM), lambda bi, i: (pattern(i), 0, 0, 0)),
        ],
        out_specs=pl.BlockSpec((1, TM, 256), lambda bi, i: (bi, i, 0)),
        out_shape=jax.ShapeDtypeStruct((b, tt, 256), BF16),
        compiler_params=_cparams(("parallel", "parallel")),
        name="na_attn",
    )(qn, kn, kn, kn, kn, vn, vn, vn, vn, bias_tab)


def _na_bias_table(na_bias_l, rows):
    ct = rows * GRID_W // TM
    rpt = TM // GRID_W
    wr = min(NA_WIN_R, rows)
    tabs = []
    for tile_i in (0, 1, ct - 1):
        i0 = tile_i * rpt
        kb = int(np.clip(tile_i - 1, 0, ct - 3)) * rpt
        qi = i0 + np.arange(TM) // GRID_W
        qj = np.arange(TM) % GRID_W
        kr = kb + np.arange(3 * TM) // GRID_W
        kc = np.arange(3 * TM) % GRID_W
        rs = np.clip(qi - wr // 2, 0, rows - wr)
        cs = np.clip(qj - NA_WIN_C // 2, 0, GRID_W - NA_WIN_C)
        valid = ((kr[None, :] >= rs[:, None]) & (kr[None, :] < rs[:, None] + wr)
                 & (kc[None, :] >= cs[:, None]) & (kc[None, :] < cs[:, None] + NA_WIN_C))
        dr = np.clip(kr[None, :] - qi[:, None] + (NA_WIN_R - 1), 0, 2 * NA_WIN_R - 2)
        dc = np.clip(kc[None, :] - qj[:, None] + (NA_WIN_C - 1), 0, 2 * NA_WIN_C - 2)
        vals = na_bias_l[:, dr, dc]
        tabs.append(jnp.where(valid[None], vals, NEG))
    tabs.append(jnp.full_like(tabs[0], NEG))
    return jnp.stack(tabs, axis=0)


def _gqa_kernel(q_ref, k_ref, v_ref, o_ref, m_ref, l_ref, acc_ref, *, n_full, tail, tq):
    q2 = jnp.concatenate([q_ref[0, 0], q_ref[0, 1]], axis=0)
    m_ref[...] = jnp.full(m_ref.shape, NEG, F32)
    l_ref[...] = jnp.zeros(l_ref.shape, F32)
    acc_ref[...] = jnp.zeros(acc_ref.shape, F32)

    def chunk(start, size):
        kc = k_ref[0, 0, pl.ds(start, size), :]
        vc = v_ref[0, 0, pl.ds(start, size), :]
        s = _dg(q2, kc, 1, 1)
        m_old = m_ref[...]
        m_new = jnp.maximum(m_old, jnp.max(s, axis=-1, keepdims=True))
        p = jnp.exp(s - m_new)
        alpha = jnp.exp(m_old - m_new)
        l_ref[...] = alpha * l_ref[...] + jnp.sum(p, axis=-1, keepdims=True)
        acc_ref[...] = alpha * acc_ref[...] + _dg(p.astype(BF16), vc, 1, 0)
        m_ref[...] = m_new

    if n_full > 0:
        def body(j, carry):
            chunk(pl.multiple_of(j * GQA_TK, GQA_TK), GQA_TK)
            return carry
        lax.fori_loop(0, n_full, body, 0)
    if tail > 0:
        chunk(n_full * GQA_TK, tail)
    o = acc_ref[...] / l_ref[...]
    o_ref[0] = jnp.concatenate([o[:tq], o[tq:]], axis=-1).astype(o_ref.dtype)


def _gqa(qg, kg, vg, seq, ctx_len):
    b, _, tt, _ = qg.shape
    scratch = lambda tq: [pltpu.VMEM((2 * tq, 1), F32), pltpu.VMEM((2 * tq, 1), F32), pltpu.VMEM((2 * tq, 64), F32)]
    y_lat = pl.pallas_call(
        functools.partial(_gqa_kernel, n_full=seq // GQA_TK, tail=ctx_len, tq=GQA_TQ),
        grid=(b, GQA_KV_HEADS, seq // GQA_TQ),
        in_specs=[
            pl.BlockSpec((1, 2, GQA_TQ, 64), lambda bi, n, i: (bi, n, i, 0)),
            pl.BlockSpec((1, 1, tt, 64), lambda bi, n, i: (bi, n, 0, 0)),
            pl.BlockSpec((1, 1, tt, 64), lambda bi, n, i: (bi, n, 0, 0)),
        ],
        out_specs=pl.BlockSpec((1, GQA_TQ, 128), lambda bi, n, i: (bi, i, n)),
        out_shape=jax.ShapeDtypeStruct((b, seq, 256), BF16),
        scratch_shapes=scratch(GQA_TQ),
        compiler_params=_cparams(("parallel", "parallel", "parallel")),
        name="gqa_latent",
    )(qg, kg, vg)
    cblk = seq // ctx_len
    y_ctx = pl.pallas_call(
        functools.partial(_gqa_kernel, n_full=0, tail=ctx_len, tq=ctx_len),
        grid=(b, GQA_KV_HEADS),
        in_specs=[
            pl.BlockSpec((1, 2, ctx_len, 64), lambda bi, n: (bi, n, cblk, 0)),
            pl.BlockSpec((1, 1, ctx_len, 64), lambda bi, n: (bi, n, cblk, 0)),
            pl.BlockSpec((1, 1, ctx_len, 64), lambda bi, n: (bi, n, cblk, 0)),
        ],
        out_specs=pl.BlockSpec((1, ctx_len, 128), lambda bi, n: (bi, 0, n)),
        out_shape=jax.ShapeDtypeStruct((b, ctx_len, 256), BF16),
        scratch_shapes=scratch(ctx_len),
        compiler_params=_cparams(("parallel", "parallel")),
        name="gqa_context",
    )(qg, kg, vg)
    return jnp.concatenate([y_lat, y_ctx], axis=1)


def _bd(x):
    t = jnp.concatenate([x, x, x, x], axis=0)
    r = lax.broadcasted_iota(jnp.int32, (4 * CH, RW), 0) >> 6
    c = lax.broadcasted_iota(jnp.int32, (4 * CH, RW), 1) >> 6
    return jnp.where(r == c, t, jnp.zeros_like(t))


def _bd2(x):
    return _bd(x[0]), _bd(x[1])


def _fold(x):
    r = lax.broadcasted_iota(jnp.int32, (4 * CH, RW), 0) >> 6
    c = lax.broadcasted_iota(jnp.int32, (4 * CH, RW), 1) >> 6
    xm = jnp.where(r == c, x, 0.0)
    return xm[0:64] + xm[64:128] + xm[128:192] + xm[192:256]


def _chunk_mats(r, logw, k2, v, kk, bv, rev):
    t = lax.broadcasted_iota(jnp.int32, (CH, RW), 0)
    j = lax.broadcasted_iota(jnp.int32, (CH, RW), 1) & 63
    tt = lax.broadcasted_iota(jnp.int32, (CH, CH), 0)
    jj = lax.broadcasted_iota(jnp.int32, (CH, CH), 1)
    if rev:
        before, incl, tri = j > t, j >= t, jj >= tt
    else:
        before, incl, tri = j < t, j <= t, jj <= tt
    eye = j == t
    cum = jnp.dot(tri.astype(F32), logw, precision=HIGHEST, preferred_element_type=F32)
    tot = jnp.sum(logw, axis=0, keepdims=True)
    e_neg = jnp.exp(-cum)
    kap = kk * jnp.exp(cum - logw)
    kt = k2 * e_neg
    bt = bv * e_neg
    rho = r * jnp.exp(cum)
    e_tot = jnp.exp(tot - cum)
    khat = k2 * e_tot
    bhat = bv * e_tot

    a2 = _split(jnp.concatenate([kap, rho], axis=0))
    lm_b = _mm3(a2, _bd2(_split(bt)), 1, 1)
    lm_k = _mm3(a2, _bd2(_split(kt)), 1, 1)
    lb = jnp.where(before, lm_b[:CH], 0.0)
    mb = jnp.where(incl, lm_b[CH:], 0.0)
    lk = jnp.where(before, lm_k[:CH], 0.0)
    mk = jnp.where(incl, lm_k[CH:], 0.0)

    n = -lb
    p = jnp.where(eye, 1.0, 0.0) + n
    mpow = _mm3(_split(n), _bd2(_split(n)))
    for _ in range(4):
        pm = _mm3(_split(jnp.concatenate([p, mpow], axis=0)), _bd2(_split(mpow)))
        p = p + pm[:CH]
        mpow = pm[CH:]
    tinv = p + _mm3(_split(p), _bd2(_split(mpow)))

    bdv = _bd2(_split(v))
    lmv = _mm3(_split(jnp.concatenate([lk, mk], axis=0)), bdv)
    ts = _split(tinv)
    kp = _mm3(ts, _bd2(_split(kap)))
    vp = _mm3(ts, _bd2(_split(lmv[:CH])))
    mbs = _split(mb)
    rp = rho - _mm3(mbs, _bd2(_split(kp)))
    yl = lmv[CH:] - _mm3(mbs, _bd2(_split(vp)))
    g = jnp.where(eye, jnp.exp(tot), 0.0) - _fold(_mm3(_split(kp), _split(bhat), 0, 0))
    hmat = _fold(_mm3(_split(jnp.concatenate([v, -vp], axis=0)),
                      _split(jnp.concatenate([khat, bhat], axis=0)), 0, 0))
    return g, hmat, rp, yl


def _rwkv_chunk_kernel(z_ref, hp_ref, hn_ref, mu_ref, w0_ref, w2_ref, a0_ref, a2_ref, kkw_ref, ka_ref, rk_ref,
                       bd_ref, g_ref, h_ref, rp_ref, yl_ref, bonus_ref, *, ct):
    i = pl.program_id(1)
    z = z_ref[0]
    zs = z[:, :RW_SHIFT]
    valid_prev = jnp.where((i != 0) & (i != ct), 1.0, 0.0)
    valid_next = jnp.where((i != ct - 1) & (i != ct), 1.0, 0.0)
    prev_row = hp_ref[0, HALO - 1:HALO, :RW_SHIFT] * valid_prev
    next_row = hn_ref[0, 0:1, :RW_SHIFT] * valid_next
    row = lax.broadcasted_iota(jnp.int32, (TM, RW_SHIFT), 0)
    bd = bd_ref[...]
    bonus = None
    for d in range(2):
        if d == 0:
            nb = jnp.where(row == 0, prev_row, pltpu.roll(zs, 1, 0))
        else:
            nb = jnp.where(row == TM - 1, next_row, pltpu.roll(zs, TM - 1, 0))
        zd = zs + mu_ref[d:d + 1, :] * (nb - zs)
        r = zd[:, :RW]
        k = zd[:, RW:2 * RW]
        v = zd[:, 2 * RW:3 * RW]
        lw = zd[:, 3 * RW:3 * RW + LORA_W]
        la = zd[:, 3 * RW + LORA_W:]
        w_log = w0_ref[d:d + 1, :] + jnp.dot(jnp.tanh(lw), w2_ref[d], precision=HIGHEST,
                                             preferred_element_type=F32)
        sp = jnp.maximum(-w_log, 0.0) + jnp.log(1.0 + jnp.exp(-jnp.abs(w_log)))
        logw = -jnp.exp(-sp - 0.5)
        a = _sigmoid(a0_ref[d:d + 1, :] + jnp.dot(la, a2_ref[d], precision=HIGHEST, preferred_element_type=F32))
        kkr = k * kkw_ref[...]
        ss = jnp.dot(kkr * kkr, bd, precision=HIGHEST, preferred_element_type=F32)
        kk = kkr * lax.rsqrt(jnp.maximum(ss, 1e-24))
        k2 = k * (1.0 + (a - 1.0) * ka_ref[...])
        bv = kk * a
        bon = jnp.dot(r * k2 * rk_ref[...], bd, precision=HIGHEST, preferred_element_type=F32) * v
        bonus = bon if bonus is None else bonus + bon
        for c in range(TM // CH):
            sl = slice(c * CH, (c + 1) * CH)
            g, hm, rp, yl = _chunk_mats(r[sl], logw[sl], k2[sl], v[sl], kk[sl], bv[sl], rev=(d == 1))
            g_ref[0, d, c] = g
            h_ref[0, d, c] = hm
            rp_ref[0, d, c] = rp
            yl_ref[0, d, c] = yl
    bonus_ref[0] = bonus


def _rwkv_chunks(zr, mu, w0, w2, a0, a2, kkw, ka, rk, bd256, ct):
    b, tt, _ = zr.shape
    nt = tt // TM
    nch = tt // CH
    cpt = TM // CH
    nhb = tt // HALO
    mats = pl.BlockSpec((1, 2, cpt, CH, RW), lambda bi, i: (bi, 0, i, 0, 0))
    mshape = jax.ShapeDtypeStruct((b, 2, nch, CH, RW), F32)
    return pl.pallas_call(
        functools.partial(_rwkv_chunk_kernel, ct=ct),
        grid=(b, nt),
        in_specs=[
            pl.BlockSpec((1, TM, COLS_RW), lambda bi, i: (bi, i, 0)),
            pl.BlockSpec((1, HALO, COLS_RW), lambda bi, i: (bi, jnp.maximum(i * (TM // HALO) - 1, 0), 0)),
            pl.BlockSpec((1, HALO, COLS_RW), lambda bi, i: (bi, jnp.minimum((i + 1) * (TM // HALO), nhb - 1), 0)),
            _const_spec((2, RW_SHIFT)),
            _const_spec((2, RW)),
            _const_spec((2, LORA_W, RW)),
            _const_spec((2, RW)),
            _const_spec((2, LORA_A, RW)),
            _const_spec((1, RW)),
            _const_spec((1, RW)),
            _const_spec((1, RW)),
            _const_spec((RW, RW)),
        ],
        out_specs=[mats, mats, mats, mats, pl.BlockSpec((1, TM, RW), lambda bi, i: (bi, i, 0))],
        out_shape=[mshape, mshape, mshape, mshape, jax.ShapeDtypeStruct((b, tt, RW), F32)],
        compiler_params=_cparams(("parallel", "parallel")),
        name="rwkv_chunks",
    )(zr, zr, zr, mu, w0, w2, a0, a2, kkw, ka, rk, bd256)


def _rwkv_chain_kernel(gf_ref, hf_ref, rf_ref, yf_ref, gr_ref, hr_ref, rr_ref, yr_ref, of_ref, or_ref, s_ref,
                       *, nb):
    @pl.when(pl.program_id(0) == 0)
    def _():
        s_ref[...] = jnp.zeros(s_ref.shape, F32)

    for d, (g_ref, h_ref, rp_ref, yl_ref, o_ref) in enumerate(
            ((gf_ref, hf_ref, rf_ref, yf_ref, of_ref), (gr_ref, hr_ref, rr_ref, yr_ref, or_ref))):
        for bi in range(nb):
            s = s_ref[d, bi]
            ss = _split(s)
            o_ref[bi] = _mm3(_split(rp_ref[bi, 0, 0]), _bd2(ss), 1, 1) + yl_ref[bi, 0, 0]
            s_ref[d, bi] = _mm3(ss, _bd2(_split(g_ref[bi, 0, 0]))) + h_ref[bi, 0, 0]


def _rwkv_chain(g, h, rp, yl, seq, ctx_len):
    b, _, nch, _, _ = g.shape
    n_lat = seq // CH
    n_ctx = ctx_len // CH

    def cf(s):
        return jnp.where(s < n_ctx, n_lat + s, s - n_ctx)

    def cr(s):
        return nch - 1 - s

    fwd = pl.BlockSpec((b, 1, 1, CH, RW), lambda s: (0, 0, cf(s), 0, 0))
    rev = pl.BlockSpec((b, 1, 1, CH, RW), lambda s: (0, 1, cr(s), 0, 0))
    yshape = jax.ShapeDtypeStruct((b, nch * CH, RW), F32)
    return pl.pallas_call(
        functools.partial(_rwkv_chain_kernel, nb=b),
        grid=(nch,),
        in_specs=[fwd, fwd, fwd, fwd, rev, rev, rev, rev],
        out_specs=[pl.BlockSpec((b, CH, RW), lambda s: (0, cf(s), 0)),
                   pl.BlockSpec((b, CH, RW), lambda s: (0, cr(s), 0))],
        out_shape=[yshape, yshape],
        scratch_shapes=[pltpu.VMEM((2, b, CH, RW), F32)],
        compiler_params=_cparams(("arbitrary",)),
        name="rwkv_chain",
    )(g, h, rp, yl, g, h, rp, yl)


def _merge_kernel(x_ref, mod_ref, gpre_ref, gpost_ref, za_ref, hp_ref, hn_ref, ca_ref, yb_ref, yc_ref,
                  yf_ref, yr_ref, bonus_ref, lg_ref, g2_ref, lnw_ref, lnb_ref, bd_ref,
                  wb_ref, wg_ref, bg_ref, wo_ref, o_ref, *, ct):
    i = pl.program_id(1)
    x = x_ref[0]
    m = mod_ref[0, 0]
    hb = _rms_mod(x, gpre_ref[...], m[0:1], m[1:2]).astype(BF16)

    za = za_ref[0]
    u = za[:, 256:512] * za[:, 512:768]
    valid_prev = jnp.where((i != 0) & (i != ct), 1.0, 0.0)
    valid_next = jnp.where((i != ct - 1) & (i != ct), 1.0, 0.0)
    up = hp_ref[0, HALO - 1:HALO, 256:512] * hp_ref[0, HALO - 1:HALO, 512:768] * valid_prev
    un = hn_ref[0, 0:1, 256:512] * hn_ref[0, 0:1, 512:768] * valid_next
    row = lax.broadcasted_iota(jnp.int32, (TM, 256), 0)
    u_prev = jnp.where(row == 0, up, pltpu.roll(u, 1, 0))
    u_next = jnp.where(row == TM - 1, un, pltpu.roll(u, TM - 1, 0))
    ya = za[:, :256] * (u_prev * ca_ref[0:1, :] + u * ca_ref[1:2, :] + u_next * ca_ref[2:3, :])

    bd = bd_ref[...]
    y = yf_ref[0] + yr_ref[0]
    mean = jnp.dot(y, bd, precision=HIGHEST, preferred_element_type=F32) * (1.0 / HEAD_DIM)
    yc0 = y - mean
    var = jnp.dot(yc0 * yc0, bd, precision=HIGHEST, preferred_element_type=F32) * (1.0 / HEAD_DIM)
    yn = yc0 * lax.rsqrt(var + LN_X_EPS) * lnw_ref[...] + lnb_ref[...] + bonus_ref[0]
    yd = yn * jnp.dot(_sigmoid(lg_ref[0]), g2_ref[...], precision=HIGHEST, preferred_element_type=F32)

    acc = None
    for bidx, ys in enumerate((ya.astype(BF16), yb_ref[0], yc_ref[0], yd.astype(BF16))):
        sl = slice(bidx * D_MODEL, (bidx + 1) * D_MODEL)
        gate = _sigmoid(jnp.dot(hb, wg_ref[:, sl], preferred_element_type=F32) + bg_ref[:, sl])
        term = gate * jnp.dot(ys, wb_ref[bidx], preferred_element_type=F32)
        acc = term if acc is None else acc + term
    mo = jnp.dot(acc.astype(BF16), wo_ref[...], preferred_element_type=F32)
    o_ref[0] = x + m[2:3] * _rms(mo, gpost_ref[...])


def _merge(xa, modsel, g_pre, g_post, za, conv_a, yb, yc, yf, yr, bonus, zr, g2, ln_w, ln_b, bd256,
           w_branch, w_gate, b_gate, w_o, ct):
    b, tt, _ = xa.shape
    nt = tt // TM
    nhb = tt // HALO
    tile = lambda w: pl.BlockSpec((1, TM, w), lambda bi, i: (bi, i, 0))
    return pl.pallas_call(
        functools.partial(_merge_kernel, ct=ct),
        grid=(b, nt),
        in_specs=[
            tile(D_MODEL),
            pl.BlockSpec((1, 1, 6, D_MODEL), lambda bi, i: (bi, i // ct, 0, 0)),
            _const_spec((1, D_MODEL)),
            _const_spec((1, D_MODEL)),
            tile(COLS_A),
            pl.BlockSpec((1, HALO, COLS_A), lambda bi, i: (bi, jnp.maximum(i * (TM // HALO) - 1, 0), 0)),
            pl.BlockSpec((1, HALO, COLS_A), lambda bi, i: (bi, jnp.minimum((i + 1) * (TM // HALO), nhb - 1), 0)),
            _const_spec((3, 256)),
            tile(256), tile(256), tile(256), tile(256), tile(256),
            pl.BlockSpec((1, TM, LORA_G), lambda bi, i: (bi, i, RW_SHIFT // LORA_G)),
            _const_spec((LORA_G, RW)),
            _const_spec((1, RW)),
            _const_spec((1, RW)),
            _const_spec((RW, RW)),
            _const_spec((4, 256, D_MODEL)),
            _const_spec((D_MODEL, 4 * D_MODEL)),
            _const_spec((1, 4 * D_MODEL)),
            _const_spec((D_MODEL, D_MODEL)),
        ],
        out_specs=tile(D_MODEL),
        out_shape=jax.ShapeDtypeStruct((b, tt, D_MODEL), F32),
        compiler_params=_cparams(("parallel", "parallel")),
        name="merge",
    )(xa, modsel, g_pre, g_post, za, za, za, conv_a, yb, yc, yf, yr, bonus, zr, g2, ln_w, ln_b, bd256,
      w_branch, w_gate, b_gate, w_o)


def _ffn_kernel(x_ref, hp_ref, hn_ref, mod_ref, gpre_ref, gpost_ref, wu_ref, cw_ref, wd_ref, o_ref, f_ref, *, ct):
    i = pl.program_id(1)
    x = x_ref[0]
    m = mod_ref[0, 0]
    xx = jnp.concatenate([hp_ref[0], x, hn_ref[0]], axis=0)
    hb = _rms_mod(xx, gpre_ref[...], m[3:4], m[4:5]).astype(BF16)
    nrow = TM + 2 * HALO
    row = lax.broadcasted_iota(jnp.int32, (nrow, 1), 0)
    valid_prev = jnp.where((i != 0) & (i != ct), 1.0, 0.0)
    valid_next = jnp.where((i != ct - 1) & (i != ct), 1.0, 0.0)
    rowmask = jnp.where(row < HALO, valid_prev, jnp.where(row >= TM + HALO, valid_next, 1.0))

    def conv(u, col):
        u = u * rowmask
        w = cw_ref[:, col:col + FF_CHUNK]
        c = (pltpu.roll(u, 1, 0) * w[0:1] + u * w[1:2] + pltpu.roll(u, nrow - 1, 0) * w[2:3])
        return c[HALO:HALO + TM]

    for j in range(D_FF // FF_CHUNK):
        ca = conv(jnp.dot(hb, wu_ref[:, j * FF_CHUNK:(j + 1) * FF_CHUNK], preferred_element_type=F32),
                  j * FF_CHUNK)
        cg = conv(jnp.dot(hb, wu_ref[:, D_FF + j * FF_CHUNK:D_FF + (j + 1) * FF_CHUNK],
                          preferred_element_type=F32), D_FF + j * FF_CHUNK)
        act = (ca * (cg * _sigmoid(cg))).astype(BF16)
        part = jnp.dot(act, wd_ref[j * FF_CHUNK:(j + 1) * FF_CHUNK, :], preferred_element_type=F32)
        if j == 0:
            f_ref[...] = part
        else:
            f_ref[...] += part
    o_ref[0] = x + m[5:6] * _rms(f_ref[...], gpost_ref[...])


def _ffn(xa, modsel, g_pre, g_post, w_up, conv_w, w_down, ct):
    b, tt, _ = xa.shape
    nt = tt // TM
    nhb = tt // HALO
    return pl.pallas_call(
        functools.partial(_ffn_kernel, ct=ct),
        grid=(b, nt),
        in_specs=[
            pl.BlockSpec((1, TM, D_MODEL), lambda bi, i: (bi, i, 0)),
            pl.BlockSpec((1, HALO, D_MODEL), lambda bi, i: (bi, jnp.maximum(i * (TM // HALO) - 1, 0), 0)),
            pl.BlockSpec((1, HALO, D_MODEL), lambda bi, i: (bi, jnp.minimum((i + 1) * (TM // HALO), nhb - 1), 0)),
            pl.BlockSpec((1, 1, 6, D_MODEL), lambda bi, i: (bi, i // ct, 0, 0)),
            _const_spec((1, D_MODEL)),
            _const_spec((1, D_MODEL)),
            _const_spec((D_MODEL, 2 * D_FF)),
            _const_spec((3, 2 * D_FF)),
            _const_spec((D_FF, D_MODEL)),
        ],
        out_specs=pl.BlockSpec((1, TM, D_MODEL), lambda bi, i: (bi, i, 0)),
        out_shape=jax.ShapeDtypeStruct((b, tt, D_MODEL), F32),
        scratch_shapes=[pltpu.VMEM((TM, D_MODEL), F32)],
        compiler_params=_cparams(("parallel", "parallel")),
        name="conv_ffn",
    )(xa, xa, xa, modsel, g_pre, g_post, w_up, conv_w, w_down)


def _rope_tables(seq, ctx_len):
    t = np.arange(seq)
    row = (t // GRID_W).astype(np.float32)
    col = (t % GRID_W).astype(np.float32)
    n_freq = HEAD_DIM // 4
    inv_freq = jnp.asarray(ROPE_THETA, F32) ** (-jnp.arange(n_freq, dtype=F32) / n_freq)
    ang = jnp.concatenate([jnp.asarray(row)[:, None] * inv_freq, jnp.asarray(col)[:, None] * inv_freq], axis=-1)
    cos = jnp.repeat(jnp.cos(ang), 2, axis=-1)
    sin = jnp.repeat(jnp.sin(ang), 2, axis=-1) * jnp.tile(jnp.asarray([-1.0, 1.0], F32), HEAD_DIM // 2)
    cos = jnp.concatenate([cos, jnp.ones((ctx_len, HEAD_DIM), F32)], axis=0)
    sin = jnp.concatenate([sin, jnp.zeros((ctx_len, HEAD_DIM), F32)], axis=0)
    nh = GQA_Q_HEADS + GQA_KV_HEADS
    return jnp.tile(cos, (1, nh)), jnp.tile(sin, (1, nh))


def _block_ones(n, scale):
    idx = np.arange(n) // HEAD_DIM
    return jnp.asarray((idx[:, None] == idx[None, :]).astype(np.float32) * scale)


def kernel(x, c, ctx, c_ctx, ada_w, ada_b, norm_mix_pre, norm_mix_post, norm_ffn_pre, norm_ffn_post, w_in, conv_a, na_bias, q_norm, k_norm, rw_mu, rw_w0, rw_w2, rw_a0, rw_a2, rw_kk, rw_ka, rw_rk, rw_g2, rw_ln_w, rw_ln_b, w_branch, w_gate, b_gate, w_o, ffn_up, ffn_conv, ffn_down):
    b, seq, _ = x.shape
    ctx_len = ctx.shape[1]
    depth = ada_w.shape[0]
    assert ctx_len == TM and seq % GQA_TQ == 0 and seq // TM >= 3 and b + 1 <= 8
    ct = seq // TM
    rows = seq // GRID_W

    cvec = jnp.zeros((8, D_MODEL), F32).at[:b].set(c).at[b].set(c_ctx)
    mods = _ada(cvec, ada_w, ada_b)
    cos_t, sin_t = _rope_tables(seq, ctx_len)
    bd384 = _block_ones(384, 1.0 / HEAD_DIM)
    bd256 = _block_ones(256, 1.0)

    xa = jnp.concatenate([x, ctx], axis=1)
    for l in range(depth):
        ml = mods[l].reshape(8, 6, D_MODEL)
        modsel = jnp.stack([ml[:b], jnp.broadcast_to(ml[b][None], (b, 6, D_MODEL))], axis=1)
        row2 = lambda a: a.reshape(1, -1)
        gqk = jnp.concatenate([jnp.tile(q_norm[l], GQA_Q_HEADS) * HEAD_DIM ** -0.5,
                               jnp.tile(k_norm[l], GQA_KV_HEADS)]).reshape(1, -1)
        za, qn, kn, vn, qg, kg, vg, zr = _inproj(xa, modsel, row2(norm_mix_pre[l]), w_in[l].astype(BF16),
                                                 cos_t, sin_t, gqk, bd384, ct)
        yb = _na(qn, kn, vn, _na_bias_table(na_bias[l], rows), ct)
        yc = _gqa(qg, kg, vg, seq, ctx_len)
        g, h, rp, yl, bonus = _rwkv_chunks(zr, rw_mu[l], rw_w0[l], rw_w2[l], rw_a0[l], rw_a2[l],
                                           row2(rw_kk[l]), row2(rw_ka[l]), row2(rw_rk[l]), bd256, ct)
        yf, yr = _rwkv_chain(g, h, rp, yl, seq, ctx_len)
        xa = _merge(xa, modsel, row2(norm_mix_pre[l]), row2(norm_mix_post[l]), za, conv_a[l], yb, yc, yf, yr,
                    bonus, zr, rw_g2[l], row2(rw_ln_w[l]), row2(rw_ln_b[l]), bd256,
                    w_branch[l].astype(BF16), w_gate[l].astype(BF16), row2(b_gate[l]), w_o[l].astype(BF16), ct)
        xa = _ffn(xa, modsel, row2(norm_ffn_pre[l]), row2(norm_ffn_post[l]), ffn_up[l].astype(BF16),
                  ffn_conv[l], ffn_down[l].astype(BF16), ct)
    return xa[:, :seq]
```

```python
import functools

import numpy as np
import jax
import jax.numpy as jnp
from jax import lax
from jax.experimental import pallas as pl
from jax.experimental.pallas import tpu as pltpu

F32 = jnp.float32
BF16 = jnp.bfloat16
HIGHEST = lax.Precision.HIGHEST

D_MODEL = 1024
GRID_W = 64
HEAD_DIM = 64
NA_HEADS = 4
NA_WIN_R = 8
NA_WIN_C = 16
GQA_Q_HEADS = 4
GQA_KV_HEADS = 2
ROPE_THETA = 10000.0
RWKV_HEADS = 4
RW = RWKV_HEADS * HEAD_DIM
LORA_W = 64
LORA_A = 64
LORA_G = 128
RW_SHIFT = 3 * RW + LORA_W + LORA_A
D_FF = 2816
NORM_EPS = 1e-6
LN_X_EPS = 64e-5
COLS_A = 768
COLS_NA = 768
COLS_GQA = 512
COLS_RW = 1024
D_IN = COLS_A + COLS_NA + COLS_GQA + COLS_RW

TM = 256
CH = 64
HALO = 8
FF_CHUNK = 256
GQA_TQ = 512
GQA_TK = 512
NEG = -1e30
VMEM_LIMIT = 56 * 1024 * 1024


def _cparams(sem):
    return pltpu.CompilerParams(dimension_semantics=sem, vmem_limit_bytes=VMEM_LIMIT)


def _const_spec(shape):
    nd = len(shape)
    return pl.BlockSpec(shape, lambda *_: (0,) * nd, pipeline_mode=pl.Buffered(1))


def _dg(a, b, ca, cb, **kw):
    return lax.dot_general(a, b, (((ca,), (cb,)), ((), ())), preferred_element_type=F32, **kw)


def _split(x):
    hi = x.astype(BF16)
    lo = (x - hi.astype(F32)).astype(BF16)
    return hi, lo


def _mm3(a, b, ca=1, cb=0):
    return _dg(a[0], b[0], ca, cb) + _dg(a[0], b[1], ca, cb) + _dg(a[1], b[0], ca, cb)


def _rms_mod(x, g, shift, scale):
    y = x * lax.rsqrt(jnp.mean(x * x, axis=-1, keepdims=True) + NORM_EPS) * g
    return y * (1.0 + scale) + shift


def _rms(x, g):
    return x * lax.rsqrt(jnp.mean(x * x, axis=-1, keepdims=True) + NORM_EPS) * g


def _sigmoid(x):
    return 1.0 / (1.0 + jnp.exp(-x))


def _ada_kernel(c_ref, w_ref, b_ref, o_ref):
    c = c_ref[...]
    s = c * _sigmoid(c)
    o_ref[0] = jnp.dot(s, w_ref[0], precision=HIGHEST, preferred_element_type=F32) + b_ref[0]


def _ada(cvec, ada_w, ada_b):
    depth = ada_w.shape[0]
    nblk = ada_w.shape[2] // D_MODEL
    return pl.pallas_call(
        _ada_kernel,
        grid=(depth, nblk),
        in_specs=[
            pl.BlockSpec((8, D_MODEL), lambda l, j: (0, 0)),
            pl.BlockSpec((1, D_MODEL, D_MODEL), lambda l, j: (l, 0, j)),
            pl.BlockSpec((1, 1, D_MODEL), lambda l, j: (l, 0, j)),
        ],
        out_specs=pl.BlockSpec((1, 8, D_MODEL), lambda l, j: (l, 0, j)),
        out_shape=jax.ShapeDtypeStruct((depth, 8, ada_w.shape[2]), F32),
        compiler_params=_cparams(("parallel", "parallel")),
        name="ada_mod",
    )(cvec, ada_w, ada_b.reshape(depth, 1, -1))


def _inproj_kernel(x_ref, mod_ref, g_ref, w_ref, cos_ref, sin_ref, gqk_ref, bd_ref,
                   za_ref, qn_ref, kn_ref, vn_ref, qg_ref, kg_ref, vg_ref, zr_ref):
    x = x_ref[0]
    m = mod_ref[0, 0]
    h = _rms_mod(x, g_ref[...], m[0:1], m[1:2])
    z = jnp.dot(h.astype(BF16), w_ref[...], preferred_element_type=F32)
    za_ref[0] = z[:, :COLS_A]
    zr_ref[0] = z[:, COLS_A + COLS_NA + COLS_GQA:]
    na = z[:, COLS_A:COLS_A + COLS_NA]
    scale = HEAD_DIM ** -0.5
    for hd in range(NA_HEADS):
        qn_ref[0, hd] = (na[:, hd * 64:(hd + 1) * 64] * scale).astype(BF16)
        kn_ref[0, hd] = na[:, 256 + hd * 64:256 + (hd + 1) * 64].astype(BF16)
        vn_ref[0, hd] = na[:, 512 + hd * 64:512 + (hd + 1) * 64].astype(BF16)
    g = z[:, COLS_A + COLS_NA:COLS_A + COLS_NA + COLS_GQA]
    qk = g[:, :384]
    ms = jnp.dot(qk * qk, bd_ref[...], precision=HIGHEST, preferred_element_type=F32)
    qkn = qk * lax.rsqrt(ms + NORM_EPS) * gqk_ref[...]
    lane = lax.broadcasted_iota(jnp.int32, (TM, 128), 1)
    even = (lane & 1) == 0
    parts = []
    for j in range(3):
        s = qkn[:, j * 128:(j + 1) * 128]
        sw = jnp.where(even, pltpu.roll(s, 127, 1), pltpu.roll(s, 1, 1))
        parts.append(s * cos_ref[:, j * 128:(j + 1) * 128] + sw * sin_ref[:, j * 128:(j + 1) * 128])
    for hd in range(GQA_Q_HEADS):
        p = parts[hd // 2]
        qg_ref[0, hd] = p[:, (hd % 2) * 64:(hd % 2) * 64 + 64].astype(BF16)
    kg_ref[0] = parts[2].T.astype(BF16)
    vt = g[:, 384:512]
    for hd in range(GQA_KV_HEADS):
        vh = vt if hd == 0 else pltpu.roll(vt, 64, 1)
        vg_ref[0, hd] = jnp.where(lane < 64, vh, jnp.where(lane == 64, 1.0, 0.0)).astype(BF16)


def _inproj(xa, modsel, g_pre, w_in, cos_t, sin_t, gqk, bd384, ct):
    b, tt, _ = xa.shape
    nt = tt // TM
    tile = lambda w: pl.BlockSpec((1, TM, w), lambda bi, i: (bi, i, 0))
    heads = lambda nh: pl.BlockSpec((1, nh, TM, 64), lambda bi, i: (bi, 0, i, 0))
    hs = lambda nh: jax.ShapeDtypeStruct((b, nh, tt, 64), BF16)
    return pl.pallas_call(
        _inproj_kernel,
        grid=(b, nt),
        in_specs=[
            tile(D_MODEL),
            pl.BlockSpec((1, 1, 6, D_MODEL), lambda bi, i: (bi, i // ct, 0, 0)),
            _const_spec((1, D_MODEL)),
            _const_spec((D_MODEL, D_IN)),
            pl.BlockSpec((TM, 384), lambda bi, i: (i, 0)),
            pl.BlockSpec((TM, 384), lambda bi, i: (i, 0)),
            _const_spec((1, 384)),
            _const_spec((384, 384)),
        ],
        out_specs=[tile(COLS_A), heads(4), heads(4), heads(4), heads(4),
                   pl.BlockSpec((1, 2 * HEAD_DIM, TM), lambda bi, i: (bi, 0, i)),
                   pl.BlockSpec((1, 2, TM, 128), lambda bi, i: (bi, 0, i, 0)), tile(COLS_RW)],
        out_shape=[jax.ShapeDtypeStruct((b, tt, COLS_A), F32), hs(4), hs(4), hs(4), hs(4),
                   jax.ShapeDtypeStruct((b, 2 * HEAD_DIM, tt), BF16),
                   jax.ShapeDtypeStruct((b, 2, tt, 128), BF16),
                   jax.ShapeDtypeStruct((b, tt, COLS_RW), F32)],
        compiler_params=_cparams(("parallel", "parallel")),
        name="inproj",
    )(xa, modsel, g_pre, w_in, cos_t, sin_t, gqk, bd384)


def _na_kernel(q_ref, k0_ref, k1_ref, k2_ref, kc_ref, v0_ref, v1_ref, v2_ref, vc_ref, bias_ref, o_ref):
    outs = []
    for hd in range(NA_HEADS):
        kcat = jnp.concatenate([k0_ref[0, hd], k1_ref[0, hd], k2_ref[0, hd], kc_ref[0, hd]], axis=0)
        s = _dg(q_ref[0, hd], kcat, 1, 1)
        s_loc = s[:, :3 * TM] + bias_ref[0, hd]
        s_ctx = s[:, 3 * TM:]
        m = jnp.maximum(jnp.max(s_loc, axis=-1, keepdims=True), jnp.max(s_ctx, axis=-1, keepdims=True))
        p_loc = jnp.exp(s_loc - m)
        p_ctx = jnp.exp(s_ctx - m)
        l = jnp.sum(p_loc, axis=-1, keepdims=True) + jnp.sum(p_ctx, axis=-1, keepdims=True)
        vloc = jnp.concatenate([v0_ref[0, hd], v1_ref[0, hd], v2_ref[0, hd]], axis=0)
        o = _dg(p_loc.astype(BF16), vloc, 1, 0) + _dg(p_ctx.astype(BF16), vc_ref[0, hd], 1, 0)
        outs.append(o / l)
    o_ref[0] = jnp.concatenate(outs, axis=-1).astype(o_ref.dtype)


def _na(qn, kn, vn, bias_tab, ct):
    b, _, tt, _ = qn.shape
    nt = tt // TM

    def kv_spec(j):
        if j is None:
            return pl.BlockSpec((1, 4, TM, 64), lambda bi, i: (bi, 0, ct, 0))
        return pl.BlockSpec((1, 4, TM, 64), lambda bi, i: (bi, 0, jnp.clip(i - 1, 0, ct - 3) + j, 0))

    def pattern(i):
        return jnp.where(i == ct, 3, jnp.where(i == 0, 0, jnp.where(i == ct - 1, 2, 1)))

    return pl.pallas_call(
        _na_kernel,
        grid=(b, nt),
        in_specs=[
            pl.BlockSpec((1, 4, TM, 64), lambda bi, i: (bi, 0, i, 0)),
            kv_spec(0), kv_spec(1), kv_spec(2), kv_spec(None),
            kv_spec(0), kv_spec(1), kv_spec(2), kv_spec(None),
            pl.BlockSpec((1, 4, TM, 3 * TM), lambda bi, i: (pattern(i), 0, 0, 0)),
        ],
        out_specs=pl.BlockSpec((1, TM, 256), lambda bi, i: (bi, i, 0)),
        out_shape=jax.ShapeDtypeStruct((b, tt, 256), BF16),
        compiler_params=_cparams(("parallel", "parallel")),
        name="na_attn",
    )(qn, kn, kn, kn, kn, vn, vn, vn, vn, bias_tab)


def _na_bias_table(na_bias_l, rows):
    ct = rows * GRID_W // TM
    rpt = TM // GRID_W
    wr = min(NA_WIN_R, rows)
    qj = np.arange(GRID_W)
    kc = np.arange(GRID_W)
    cs = np.clip(qj - NA_WIN_C // 2, 0, GRID_W - NA_WIN_C)
    colvalid = (kc[None, :] >= cs[:, None]) & (kc[None, :] < cs[:, None] + NA_WIN_C)
    dc = kc[None, :] - qj[:, None] + (NA_WIN_C - 1)
    onehot = (dc.reshape(1, -1) == np.arange(2 * NA_WIN_C - 1)[:, None]) & colvalid.reshape(1, -1)
    toep = jnp.einsum("hrd,dx->hrx", na_bias_l, jnp.asarray(onehot.astype(np.float32)), precision=HIGHEST)
    toep = jnp.where(jnp.asarray(colvalid.reshape(-1)), toep, NEG)
    toep = toep.reshape(NA_HEADS, 2 * NA_WIN_R - 1, GRID_W, GRID_W)
    neg_blk = jnp.full((NA_HEADS, GRID_W, GRID_W), NEG, F32)
    tabs = []
    for tile_i in (0, 1, ct - 1):
        i0 = tile_i * rpt
        kb = int(np.clip(tile_i - 1, 0, ct - 3)) * rpt
        qrows = []
        for ri in range(rpt):
            qi = i0 + ri
            rs = int(np.clip(qi - wr // 2, 0, rows - wr))
            blks = [toep[:, kb + m - qi + NA_WIN_R - 1] if rs <= kb + m < rs + wr else neg_blk
                    for m in range(3 * rpt)]
            qrows.append(jnp.concatenate(blks, axis=-1))
        tabs.append(jnp.concatenate(qrows, axis=-2))
    tabs.append(jnp.full_like(tabs[0], NEG))
    return jnp.stack(tabs, axis=0)


def _gqa_kernel(q_ref, kt_ref, v_ref, o_ref, m_ref, acc_ref, *, n_full, tail, tq):
    q2 = jnp.concatenate([q_ref[0, 0], q_ref[0, 1]], axis=0)
    m_ref[...] = jnp.full(m_ref.shape, NEG, F32)
    acc_ref[...] = jnp.zeros(acc_ref.shape, F32)

    def chunk(start, size):
        kt = kt_ref[0, :, pl.ds(start, size)]
        vc = v_ref[0, 0, pl.ds(start, size), :]
        s = jnp.dot(q2, kt, preferred_element_type=F32)
        nlt = size // 128
        mt = s[:, :128]
        for j in range(1, nlt):
            mt = jnp.maximum(mt, s[:, j * 128:(j + 1) * 128])
        m_old = m_ref[...]
        m_new = jnp.maximum(m_old, jnp.max(mt, axis=-1, keepdims=True))
        p = jnp.concatenate([jnp.exp(s[:, j * 128:(j + 1) * 128] - m_new) for j in range(nlt)], axis=-1)
        acc_ref[...] = jnp.exp(m_old - m_new) * acc_ref[...] + jnp.dot(p.astype(BF16), vc,
                                                                       preferred_element_type=F32)
        m_ref[...] = m_new

    if n_full > 0:
        def body(j, carry):
            chunk(pl.multiple_of(j * GQA_TK, GQA_TK), GQA_TK)
            return carry
        lax.fori_loop(0, n_full, body, 0)
    if tail > 0:
        chunk(n_full * GQA_TK, tail)
    acc = acc_ref[...]
    o = acc[:, :HEAD_DIM] * (1.0 / acc[:, HEAD_DIM:HEAD_DIM + 1])
    o_ref[0] = jnp.concatenate([o[:tq], o[tq:]], axis=-1).astype(o_ref.dtype)


def _gqa(qg, kg, vg, seq, ctx_len):
    b, _, tt, _ = qg.shape
    scratch = lambda tq: [pltpu.VMEM((2 * tq, 128), F32), pltpu.VMEM((2 * tq, 128), F32)]
    y_lat = pl.pallas_call(
        functools.partial(_gqa_kernel, n_full=seq // GQA_TK, tail=ctx_len, tq=GQA_TQ),
        grid=(b, GQA_KV_HEADS, seq // GQA_TQ),
        in_specs=[
            pl.BlockSpec((1, 2, GQA_TQ, 64), lambda bi, n, i: (bi, n, i, 0)),
            pl.BlockSpec((1, HEAD_DIM, tt), lambda bi, n, i: (bi, n, 0)),
            pl.BlockSpec((1, 1, tt, 128), lambda bi, n, i: (bi, n, 0, 0)),
        ],
        out_specs=pl.BlockSpec((1, GQA_TQ, 128), lambda bi, n, i: (bi, i, n)),
        out_shape=jax.ShapeDtypeStruct((b, seq, 256), BF16),
        scratch_shapes=scratch(GQA_TQ),
        compiler_params=_cparams(("parallel", "parallel", "parallel")),
        name="gqa_latent",
    )(qg, kg, vg)
    cblk = seq // ctx_len
    y_ctx = pl.pallas_call(
        functools.partial(_gqa_kernel, n_full=0, tail=ctx_len, tq=ctx_len),
        grid=(b, GQA_KV_HEADS),
        in_specs=[
            pl.BlockSpec((1, 2, ctx_len, 64), lambda bi, n: (bi, n, cblk, 0)),
            pl.BlockSpec((1, HEAD_DIM, ctx_len), lambda bi, n: (bi, n, cblk)),
            pl.BlockSpec((1, 1, ctx_len, 128), lambda bi, n: (bi, n, cblk, 0)),
        ],
        out_specs=pl.BlockSpec((1, ctx_len, 128), lambda bi, n: (bi, 0, n)),
        out_shape=jax.ShapeDtypeStruct((b, ctx_len, 256), BF16),
        scratch_shapes=scratch(ctx_len),
        compiler_params=_cparams(("parallel", "parallel")),
        name="gqa_context",
    )(qg, kg, vg)
    return jnp.concatenate([y_lat, y_ctx], axis=1)


def _bd(x):
    t = jnp.concatenate([x, x, x, x], axis=0)
    r = lax.broadcasted_iota(jnp.int32, (4 * CH, RW), 0) >> 6
    c = lax.broadcasted_iota(jnp.int32, (4 * CH, RW), 1) >> 6
    return jnp.where(r == c, t, jnp.zeros_like(t))


def _bd2(x):
    return _bd(x[0]), _bd(x[1])


def _fold(x):
    r = lax.broadcasted_iota(jnp.int32, (4 * CH, RW), 0) >> 6
    c = lax.broadcasted_iota(jnp.int32, (4 * CH, RW), 1) >> 6
    xm = jnp.where(r == c, x, 0.0)
    return xm[0:64] + xm[64:128] + xm[128:192] + xm[192:256]


def _chunk_mats(r, logw, k2, v, kk, bv, rev):
    t = lax.broadcasted_iota(jnp.int32, (CH, RW), 0)
    j = lax.broadcasted_iota(jnp.int32, (CH, RW), 1) & 63
    tt = lax.broadcasted_iota(jnp.int32, (CH, CH), 0)
    jj = lax.broadcasted_iota(jnp.int32, (CH, CH), 1)
    if rev:
        before, incl, tri = j > t, j >= t, jj >= tt
    else:
        before, incl, tri = j < t, j <= t, jj <= tt
    eye = j == t
    cum = jnp.dot(tri.astype(F32), logw, precision=HIGHEST, preferred_element_type=F32)
    tot = jnp.sum(logw, axis=0, keepdims=True)
    e_neg = jnp.exp(-cum)
    kap = kk * jnp.exp(cum - logw)
    kt = k2 * e_neg
    bt = bv * e_neg
    rho = r * jnp.exp(cum)
    e_tot = jnp.exp(tot - cum)
    khat = k2 * e_tot
    bhat = bv * e_tot

    a2 = _split(jnp.concatenate([kap, rho], axis=0))
    lm_b = _mm3(a2, _bd2(_split(bt)), 1, 1)
    lm_k = _mm3(a2, _bd2(_split(kt)), 1, 1)
    lb = jnp.where(before, lm_b[:CH], 0.0)
    mb = jnp.where(incl, lm_b[CH:], 0.0)
    lk = jnp.where(before, lm_k[:CH], 0.0)
    mk = jnp.where(incl, lm_k[CH:], 0.0)

    n = -lb
    p = jnp.where(eye, 1.0, 0.0) + n
    mpow = _mm3(_split(n), _bd2(_split(n)))
    for _ in range(4):
        pm = _mm3(_split(jnp.concatenate([p, mpow], axis=0)), _bd2(_split(mpow)))
        p = p + pm[:CH]
        mpow = pm[CH:]
    tinv = p + _mm3(_split(p), _bd2(_split(mpow)))

    bdv = _bd2(_split(v))
    lmv = _mm3(_split(jnp.concatenate([lk, mk], axis=0)), bdv)
    ts = _split(tinv)
    kp = _mm3(ts, _bd2(_split(kap)))
    vp = _mm3(ts, _bd2(_split(lmv[:CH])))
    mbs = _split(mb)
    rp = rho - _mm3(mbs, _bd2(_split(kp)))
    yl = lmv[CH:] - _mm3(mbs, _bd2(_split(vp)))
    g = jnp.where(eye, jnp.exp(tot), 0.0) - _fold(_mm3(_split(kp), _split(bhat), 0, 0))
    hmat = _fold(_mm3(_split(jnp.concatenate([v, -vp], axis=0)),
                      _split(jnp.concatenate([khat, bhat], axis=0)), 0, 0))
    return g, hmat, rp, yl


def _rwkv_chunk_kernel(z_ref, hp_ref, hn_ref, mu_ref, w0_ref, w2_ref, a0_ref, a2_ref, kkw_ref, ka_ref, rk_ref,
                       bd_ref, g_ref, h_ref, rp_ref, yl_ref, bonus_ref, *, ct):
    i = pl.program_id(1)
    z = z_ref[0]
    zs = z[:, :RW_SHIFT]
    valid_prev = jnp.where((i != 0) & (i != ct), 1.0, 0.0)
    valid_next = jnp.where((i != ct - 1) & (i != ct), 1.0, 0.0)
    prev_row = hp_ref[0, HALO - 1:HALO, :RW_SHIFT] * valid_prev
    next_row = hn_ref[0, 0:1, :RW_SHIFT] * valid_next
    row = lax.broadcasted_iota(jnp.int32, (TM, RW_SHIFT), 0)
    bd = bd_ref[...]
    bonus = None
    for d in range(2):
        if d == 0:
            nb = jnp.where(row == 0, prev_row, pltpu.roll(zs, 1, 0))
        else:
            nb = jnp.where(row == TM - 1, next_row, pltpu.roll(zs, TM - 1, 0))
        zd = zs + mu_ref[d:d + 1, :] * (nb - zs)
        r = zd[:, :RW]
        k = zd[:, RW:2 * RW]
        v = zd[:, 2 * RW:3 * RW]
        lw = zd[:, 3 * RW:3 * RW + LORA_W]
        la = zd[:, 3 * RW + LORA_W:]
        w_log = w0_ref[d:d + 1, :] + jnp.dot(jnp.tanh(lw), w2_ref[d], precision=HIGHEST,
                                             preferred_element_type=F32)
        sp = jnp.maximum(-w_log, 0.0) + jnp.log(1.0 + jnp.exp(-jnp.abs(w_log)))
        logw = -jnp.exp(-sp - 0.5)
        a = _sigmoid(a0_ref[d:d + 1, :] + jnp.dot(la, a2_ref[d], precision=HIGHEST, preferred_element_type=F32))
        kkr = k * kkw_ref[...]
        ss = jnp.dot(kkr * kkr, bd, precision=HIGHEST, preferred_element_type=F32)
        kk = kkr * lax.rsqrt(jnp.maximum(ss, 1e-24))
        k2 = k * (1.0 + (a - 1.0) * ka_ref[...])
        bv = kk * a
        bon = jnp.dot(r * k2 * rk_ref[...], bd, precision=HIGHEST, preferred_element_type=F32) * v
        bonus = bon if bonus is None else bonus + bon
        for c in range(TM // CH):
            sl = slice(c * CH, (c + 1) * CH)
            g, hm, rp, yl = _chunk_mats(r[sl], logw[sl], k2[sl], v[sl], kk[sl], bv[sl], rev=(d == 1))
            g_ref[0, d, c] = g
            h_ref[0, d, c] = hm
            rp_ref[0, d, c] = rp
            yl_ref[0, d, c] = yl
    bonus_ref[0] = bonus


def _rwkv_chunks(zr, mu, w0, w2, a0, a2, kkw, ka, rk, bd256, ct):
    b, tt, _ = zr.shape
    nt = tt // TM
    nch = tt // CH
    cpt = TM // CH
    nhb = tt // HALO
    mats = pl.BlockSpec((1, 2, cpt, CH, RW), lambda bi, i: (bi, 0, i, 0, 0))
    mshape = jax.ShapeDtypeStruct((b, 2, nch, CH, RW), F32)
    return pl.pallas_call(
        functools.partial(_rwkv_chunk_kernel, ct=ct),
        grid=(b, nt),
        in_specs=[
            pl.BlockSpec((1, TM, COLS_RW), lambda bi, i: (bi, i, 0)),
            pl.BlockSpec((1, HALO, COLS_RW), lambda bi, i: (bi, jnp.maximum(i * (TM // HALO) - 1, 0), 0)),
            pl.BlockSpec((1, HALO, COLS_RW), lambda bi, i: (bi, jnp.minimum((i + 1) * (TM // HALO), nhb - 1), 0)),
            _const_spec((2, RW_SHIFT)),
            _const_spec((2, RW)),
            _const_spec((2, LORA_W, RW)),
            _const_spec((2, RW)),
            _const_spec((2, LORA_A, RW)),
            _const_spec((1, RW)),
            _const_spec((1, RW)),
            _const_spec((1, RW)),
            _const_spec((RW, RW)),
        ],
        out_specs=[mats, mats, mats, mats, pl.BlockSpec((1, TM, RW), lambda bi, i: (bi, i, 0))],
        out_shape=[mshape, mshape, mshape, mshape, jax.ShapeDtypeStruct((b, tt, RW), F32)],
        compiler_params=_cparams(("parallel", "parallel")),
        name="rwkv_chunks",
    )(zr, zr, zr, mu, w0, w2, a0, a2, kkw, ka, rk, bd256)


def _rwkv_chain_kernel(gf_ref, hf_ref, rf_ref, yf_ref, gr_ref, hr_ref, rr_ref, yr_ref, of_ref, or_ref, s_ref,
                       *, nb):
    @pl.when(pl.program_id(0) == 0)
    def _():
        s_ref[...] = jnp.zeros(s_ref.shape, F32)

    for d, (g_ref, h_ref, rp_ref, yl_ref, o_ref) in enumerate(
            ((gf_ref, hf_ref, rf_ref, yf_ref, of_ref), (gr_ref, hr_ref, rr_ref, yr_ref, or_ref))):
        for bi in range(nb):
            s = s_ref[d, bi]
            ss = _split(s)
            o_ref[bi] = _mm3(_split(rp_ref[bi, 0, 0]), _bd2(ss), 1, 1) + yl_ref[bi, 0, 0]
            s_ref[d, bi] = _mm3(ss, _bd2(_split(g_ref[bi, 0, 0]))) + h_ref[bi, 0, 0]


def _rwkv_chain(g, h, rp, yl, seq, ctx_len):
    b, _, nch, _, _ = g.shape
    n_lat = seq // CH
    n_ctx = ctx_len // CH

    def cf(s):
        return jnp.where(s < n_ctx, n_lat + s, s - n_ctx)

    def cr(s):
        return nch - 1 - s

    fwd = pl.BlockSpec((b, 1, 1, CH, RW), lambda s: (0, 0, cf(s), 0, 0))
    rev = pl.BlockSpec((b, 1, 1, CH, RW), lambda s: (0, 1, cr(s), 0, 0))
    yshape = jax.ShapeDtypeStruct((b, nch * CH, RW), F32)
    return pl.pallas_call(
        functools.partial(_rwkv_chain_kernel, nb=b),
        grid=(nch,),
        in_specs=[fwd, fwd, fwd, fwd, rev, rev, rev, rev],
        out_specs=[pl.BlockSpec((b, CH, RW), lambda s: (0, cf(s), 0)),
                   pl.BlockSpec((b, CH, RW), lambda s: (0, cr(s), 0))],
        out_shape=[yshape, yshape],
        scratch_shapes=[pltpu.VMEM((2, b, CH, RW), F32)],
        compiler_params=_cparams(("arbitrary",)),
        name="rwkv_chain",
    )(g, h, rp, yl, g, h, rp, yl)


def _merge_kernel(x_ref, mod_ref, gpre_ref, gpost_ref, za_ref, hp_ref, hn_ref, ca_ref, yb_ref, yc_ref,
                  yf_ref, yr_ref, bonus_ref, lg_ref, g2_ref, lnw_ref, lnb_ref, bd_ref,
                  wb_ref, wg_ref, bg_ref, wo_ref, o_ref, *, ct):
    i = pl.program_id(1)
    x = x_ref[0]
    m = mod_ref[0, 0]
    hb = _rms_mod(x, gpre_ref[...], m[0:1], m[1:2]).astype(BF16)

    za = za_ref[0]
    u = za[:, 256:512] * za[:, 512:768]
    valid_prev = jnp.where((i != 0) & (i != ct), 1.0, 0.0)
    valid_next = jnp.where((i != ct - 1) & (i != ct), 1.0, 0.0)
    up = hp_ref[0, HALO - 1:HALO, 256:512] * hp_ref[0, HALO - 1:HALO, 512:768] * valid_prev
    un = hn_ref[0, 0:1, 256:512] * hn_ref[0, 0:1, 512:768] * valid_next
    row = lax.broadcasted_iota(jnp.int32, (TM, 256), 0)
    u_prev = jnp.where(row == 0, up, pltpu.roll(u, 1, 0))
    u_next = jnp.where(row == TM - 1, un, pltpu.roll(u, TM - 1, 0))
    ya = za[:, :256] * (u_prev * ca_ref[0:1, :] + u * ca_ref[1:2, :] + u_next * ca_ref[2:3, :])

    bd = bd_ref[...]
    y = yf_ref[0] + yr_ref[0]
    mean = jnp.dot(y, bd, precision=HIGHEST, preferred_element_type=F32) * (1.0 / HEAD_DIM)
    yc0 = y - mean
    var = jnp.dot(yc0 * yc0, bd, precision=HIGHEST, preferred_element_type=F32) * (1.0 / HEAD_DIM)
    yn = yc0 * lax.rsqrt(var + LN_X_EPS) * lnw_ref[...] + lnb_ref[...] + bonus_ref[0]
    yd = yn * jnp.dot(_sigmoid(lg_ref[0]), g2_ref[...], precision=HIGHEST, preferred_element_type=F32)

    acc = None
    for bidx, ys in enumerate((ya.astype(BF16), yb_ref[0], yc_ref[0], yd.astype(BF16))):
        sl = slice(bidx * D_MODEL, (bidx + 1) * D_MODEL)
        gate = _sigmoid(jnp.dot(hb, wg_ref[:, sl], preferred_element_type=F32) + bg_ref[:, sl])
        term = gate * jnp.dot(ys, wb_ref[bidx], preferred_element_type=F32)
        acc = term if acc is None else acc + term
    mo = jnp.dot(acc.astype(BF16), wo_ref[...], preferred_element_type=F32)
    o_ref[0] = x + m[2:3] * _rms(mo, gpost_ref[...])


def _merge(xa, modsel, g_pre, g_post, za, conv_a, yb, yc, yf, yr, bonus, zr, g2, ln_w, ln_b, bd256,
           w_branch, w_gate, b_gate, w_o, ct):
    b, tt, _ = xa.shape
    nt = tt // TM
    nhb = tt // HALO
    tile = lambda w: pl.BlockSpec((1, TM, w), lambda bi, i: (bi, i, 0))
    return pl.pallas_call(
        functools.partial(_merge_kernel, ct=ct),
        grid=(b, nt),
        in_specs=[
            tile(D_MODEL),
            pl.BlockSpec((1, 1, 6, D_MODEL), lambda bi, i: (bi, i // ct, 0, 0)),
            _const_spec((1, D_MODEL)),
            _const_spec((1, D_MODEL)),
            tile(COLS_A),
            pl.BlockSpec((1, HALO, COLS_A), lambda bi, i: (bi, jnp.maximum(i * (TM // HALO) - 1, 0), 0)),
            pl.BlockSpec((1, HALO, COLS_A), lambda bi, i: (bi, jnp.minimum((i + 1) * (TM // HALO), nhb - 1), 0)),
            _const_spec((3, 256)),
            tile(256), tile(256), tile(256), tile(256), tile(256),
            pl.BlockSpec((1, TM, LORA_G), lambda bi, i: (bi, i, RW_SHIFT // LORA_G)),
            _const_spec((LORA_G, RW)),
            _const_spec((1, RW)),
            _const_spec((1, RW)),
            _const_spec((RW, RW)),
            _const_spec((4, 256, D_MODEL)),
            _const_spec((D_MODEL, 4 * D_MODEL)),
            _const_spec((1, 4 * D_MODEL)),
            _const_spec((D_MODEL, D_MODEL)),
        ],
        out_specs=tile(D_MODEL),
        out_shape=jax.ShapeDtypeStruct((b, tt, D_MODEL), F32),
        compiler_params=_cparams(("parallel", "parallel")),
        name="merge",
    )(xa, modsel, g_pre, g_post, za, za, za, conv_a, yb, yc, yf, yr, bonus, zr, g2, ln_w, ln_b, bd256,
      w_branch, w_gate, b_gate, w_o)


def _ffn_kernel(x_ref, hp_ref, hn_ref, mod_ref, gpre_ref, gpost_ref, wu_ref, cw_ref, wd_ref, o_ref, f_ref, *, ct):
    i = pl.program_id(1)
    x = x_ref[0]
    m = mod_ref[0, 0]
    xx = jnp.concatenate([hp_ref[0], x, hn_ref[0]], axis=0)
    hb = _rms_mod(xx, gpre_ref[...], m[3:4], m[4:5]).astype(BF16)
    nrow = TM + 2 * HALO
    row = lax.broadcasted_iota(jnp.int32, (nrow, 1), 0)
    valid_prev = jnp.where((i != 0) & (i != ct), 1.0, 0.0)
    valid_next = jnp.where((i != ct - 1) & (i != ct), 1.0, 0.0)
    rowmask = jnp.where(row < HALO, valid_prev, jnp.where(row >= TM + HALO, valid_next, 1.0))

    def conv(u, col):
        u = u * rowmask
        w = cw_ref[:, col:col + FF_CHUNK]
        c = (pltpu.roll(u, 1, 0) * w[0:1] + u * w[1:2] + pltpu.roll(u, nrow - 1, 0) * w[2:3])
        return c[HALO:HALO + TM]

    for j in range(D_FF // FF_CHUNK):
        ca = conv(jnp.dot(hb, wu_ref[:, j * FF_CHUNK:(j + 1) * FF_CHUNK], preferred_element_type=F32),
                  j * FF_CHUNK)
        cg = conv(jnp.dot(hb, wu_ref[:, D_FF + j * FF_CHUNK:D_FF + (j + 1) * FF_CHUNK],
                          preferred_element_type=F32), D_FF + j * FF_CHUNK)
        act = (ca * (cg * _sigmoid(cg))).astype(BF16)
        part = jnp.dot(act, wd_ref[j * FF_CHUNK:(j + 1) * FF_CHUNK, :], preferred_element_type=F32)
        if j == 0:
            f_ref[...] = part
        else:
            f_ref[...] += part
    o_ref[0] = x + m[5:6] * _rms(f_ref[...], gpost_ref[...])


def _ffn(xa, modsel, g_pre, g_post, w_up, conv_w, w_down, ct):
    b, tt, _ = xa.shape
    nt = tt // TM
    nhb = tt // HALO
    return pl.pallas_call(
        functools.partial(_ffn_kernel, ct=ct),
        grid=(b, nt),
        in_specs=[
            pl.BlockSpec((1, TM, D_MODEL), lambda bi, i: (bi, i, 0)),
            pl.BlockSpec((1, HALO, D_MODEL), lambda bi, i: (bi, jnp.maximum(i * (TM // HALO) - 1, 0), 0)),
            pl.BlockSpec((1, HALO, D_MODEL), lambda bi, i: (bi, jnp.minimum((i + 1) * (TM // HALO), nhb - 1), 0)),
            pl.BlockSpec((1, 1, 6, D_MODEL), lambda bi, i: (bi, i // ct, 0, 0)),
            _const_spec((1, D_MODEL)),
            _const_spec((1, D_MODEL)),
            _const_spec((D_MODEL, 2 * D_FF)),
            _const_spec((3, 2 * D_FF)),
            _const_spec((D_FF, D_MODEL)),
        ],
        out_specs=pl.BlockSpec((1, TM, D_MODEL), lambda bi, i: (bi, i, 0)),
        out_shape=jax.ShapeDtypeStruct((b, tt, D_MODEL), F32),
        scratch_shapes=[pltpu.VMEM((TM, D_MODEL), F32)],
        compiler_params=_cparams(("parallel", "parallel")),
        name="conv_ffn",
    )(xa, xa, xa, modsel, g_pre, g_post, w_up, conv_w, w_down)


def _rope_tables(seq, ctx_len):
    t = np.arange(seq)
    row = (t // GRID_W).astype(np.float32)
    col = (t % GRID_W).astype(np.float32)
    n_freq = HEAD_DIM // 4
    inv_freq = jnp.asarray(ROPE_THETA, F32) ** (-jnp.arange(n_freq, dtype=F32) / n_freq)
    ang = jnp.concatenate([jnp.asarray(row)[:, None] * inv_freq, jnp.asarray(col)[:, None] * inv_freq], axis=-1)
    cos = jnp.repeat(jnp.cos(ang), 2, axis=-1)
    sin = jnp.repeat(jnp.sin(ang), 2, axis=-1) * jnp.tile(jnp.asarray([-1.0, 1.0], F32), HEAD_DIM // 2)
    cos = jnp.concatenate([cos, jnp.ones((ctx_len, HEAD_DIM), F32)], axis=0)
    sin = jnp.concatenate([sin, jnp.zeros((ctx_len, HEAD_DIM), F32)], axis=0)
    nh = GQA_Q_HEADS + GQA_KV_HEADS
    return jnp.tile(cos, (1, nh)), jnp.tile(sin, (1, nh))


def _block_ones(n, scale):
    idx = np.arange(n) // HEAD_DIM
    return jnp.asarray((idx[:, None] == idx[None, :]).astype(np.float32) * scale)


def kernel(x, c, ctx, c_ctx, ada_w, ada_b, norm_mix_pre, norm_mix_post, norm_ffn_pre, norm_ffn_post, w_in, conv_a, na_bias, q_norm, k_norm, rw_mu, rw_w0, rw_w2, rw_a0, rw_a2, rw_kk, rw_ka, rw_rk, rw_g2, rw_ln_w, rw_ln_b, w_branch, w_gate, b_gate, w_o, ffn_up, ffn_conv, ffn_down):
    b, seq, _ = x.shape
    ctx_len = ctx.shape[1]
    depth = ada_w.shape[0]
    assert ctx_len == TM and seq % GQA_TQ == 0 and seq // TM >= 3 and b + 1 <= 8
    ct = seq // TM
    rows = seq // GRID_W

    cvec = jnp.zeros((8, D_MODEL), F32).at[:b].set(c).at[b].set(c_ctx)
    mods = _ada(cvec, ada_w, ada_b)
    cos_t, sin_t = _rope_tables(seq, ctx_len)
    bd384 = _block_ones(384, 1.0 / HEAD_DIM)
    bd256 = _block_ones(256, 1.0)

    xa = jnp.concatenate([x, ctx], axis=1)
    for l in range(depth):
        ml = mods[l].reshape(8, 6, D_MODEL)
        modsel = jnp.stack([ml[:b], jnp.broadcast_to(ml[b][None], (b, 6, D_MODEL))], axis=1)
        row2 = lambda a: a.reshape(1, -1)
        gqk = jnp.concatenate([jnp.tile(q_norm[l], GQA_Q_HEADS) * HEAD_DIM ** -0.5,
                               jnp.tile(k_norm[l], GQA_KV_HEADS)]).reshape(1, -1)
        za, qn, kn, vn, qg, kg, vg, zr = _inproj(xa, modsel, row2(norm_mix_pre[l]), w_in[l].astype(BF16),
                                                 cos_t, sin_t, gqk, bd384, ct)
        yb = _na(qn, kn, vn, _na_bias_table(na_bias[l], rows), ct)
        yc = _gqa(qg, kg, vg, seq, ctx_len)
        g, h, rp, yl, bonus = _rwkv_chunks(zr, rw_mu[l], rw_w0[l], rw_w2[l], rw_a0[l], rw_a2[l],
                                           row2(rw_kk[l]), row2(rw_ka[l]), row2(rw_rk[l]), bd256, ct)
        yf, yr = _rwkv_chain(g, h, rp, yl, seq, ctx_len)
        xa = _merge(xa, modsel, row2(norm_mix_pre[l]), row2(norm_mix_post[l]), za, conv_a[l], yb, yc, yf, yr,
                    bonus, zr, rw_g2[l], row2(rw_ln_w[l]), row2(rw_ln_b[l]), bd256,
                    w_branch[l].astype(BF16), w_gate[l].astype(BF16), row2(b_gate[l]), w_o[l].astype(BF16), ct)
        xa = _ffn(xa, modsel, row2(norm_ffn_pre[l]), row2(norm_ffn_post[l]), ffn_up[l].astype(BF16),
                  ffn_conv[l], ffn_down[l].astype(BF16), ct)
    return xa[:, :seq]
```

```python
import functools

import numpy as np
import jax
import jax.numpy as jnp
from jax import lax
from jax.experimental import pallas as pl
from jax.experimental.pallas import tpu as pltpu

F32 = jnp.float32
BF16 = jnp.bfloat16
HIGHEST = lax.Precision.HIGHEST

D_MODEL = 1024
GRID_W = 64
HEAD_DIM = 64
NA_HEADS = 4
NA_WIN_R = 8
NA_WIN_C = 16
GQA_Q_HEADS = 4
GQA_KV_HEADS = 2
ROPE_THETA = 10000.0
RWKV_HEADS = 4
RW = RWKV_HEADS * HEAD_DIM
LORA_W = 64
LORA_A = 64
LORA_G = 128
RW_SHIFT = 3 * RW + LORA_W + LORA_A
D_FF = 2816
NORM_EPS = 1e-6
LN_X_EPS = 64e-5
COLS_A = 768
COLS_NA = 768
COLS_GQA = 512
COLS_RW = 1024
D_IN = COLS_A + COLS_NA + COLS_GQA + COLS_RW

TM = 256
CH = 64
HALO = 8
FF_CHUNK = 256
GQA_TQ = 512
GQA_TK = 512
NEG = -1e30
VMEM_LIMIT = 56 * 1024 * 1024


def _cparams(sem):
    return pltpu.CompilerParams(dimension_semantics=sem, vmem_limit_bytes=VMEM_LIMIT)


def _const_spec(shape):
    nd = len(shape)
    return pl.BlockSpec(shape, lambda *_: (0,) * nd, pipeline_mode=pl.Buffered(1))


def _dg(a, b, ca, cb, **kw):
    return lax.dot_general(a, b, (((ca,), (cb,)), ((), ())), preferred_element_type=F32, **kw)


def _rms_mod(x, g, shift, scale):
    y = x * lax.rsqrt(jnp.mean(x * x, axis=-1, keepdims=True) + NORM_EPS) * g
    return y * (1.0 + scale) + shift


def _rms(x, g):
    return x * lax.rsqrt(jnp.mean(x * x, axis=-1, keepdims=True) + NORM_EPS) * g


def _sigmoid(x):
    return 1.0 / (1.0 + jnp.exp(-x))


def _ada_kernel(c_ref, w_ref, b_ref, o_ref):
    c = c_ref[...]
    s = c * _sigmoid(c)
    o_ref[0] = jnp.dot(s, w_ref[0], precision=HIGHEST, preferred_element_type=F32) + b_ref[0]


def _ada(cvec, ada_w, ada_b):
    depth = ada_w.shape[0]
    nblk = ada_w.shape[2] // D_MODEL
    return pl.pallas_call(
        _ada_kernel,
        grid=(depth, nblk),
        in_specs=[
            pl.BlockSpec((8, D_MODEL), lambda l, j: (0, 0)),
            pl.BlockSpec((1, D_MODEL, D_MODEL), lambda l, j: (l, 0, j)),
            pl.BlockSpec((1, 1, D_MODEL), lambda l, j: (l, 0, j)),
        ],
        out_specs=pl.BlockSpec((1, 8, D_MODEL), lambda l, j: (l, 0, j)),
        out_shape=jax.ShapeDtypeStruct((depth, 8, ada_w.shape[2]), F32),
        compiler_params=_cparams(("parallel", "parallel")),
        name="ada_mod",
    )(cvec, ada_w, ada_b.reshape(depth, 1, -1))


def _inproj_kernel(x_ref, mod_ref, g_ref, w_ref, cos_ref, sin_ref, gqk_ref, bd_ref,
                   za_ref, qn_ref, kn_ref, vn_ref, qg_ref, kg_ref, vg_ref, zr_ref):
    x = x_ref[0]
    m = mod_ref[0, 0]
    h = _rms_mod(x, g_ref[...], m[0:1], m[1:2])
    z = jnp.dot(h.astype(BF16), w_ref[...], preferred_element_type=F32)
    za_ref[0] = z[:, :COLS_A]
    zr_ref[0] = z[:, COLS_A + COLS_NA + COLS_GQA:]
    na = z[:, COLS_A:COLS_A + COLS_NA]
    scale = HEAD_DIM ** -0.5
    for hd in range(NA_HEADS):
        qn_ref[0, hd] = (na[:, hd * 64:(hd + 1) * 64] * scale).astype(BF16)
        kn_ref[0, hd] = na[:, 256 + hd * 64:256 + (hd + 1) * 64].astype(BF16)
        vn_ref[0, hd] = na[:, 512 + hd * 64:512 + (hd + 1) * 64].astype(BF16)
    g = z[:, COLS_A + COLS_NA:COLS_A + COLS_NA + COLS_GQA]
    qk = g[:, :384]
    ms = jnp.dot(qk * qk, bd_ref[...], precision=HIGHEST, preferred_element_type=F32)
    qkn = qk * lax.rsqrt(ms + NORM_EPS) * gqk_ref[...]
    lane = lax.broadcasted_iota(jnp.int32, (TM, 128), 1)
    even = (lane & 1) == 0
    parts = []
    for j in range(3):
        s = qkn[:, j * 128:(j + 1) * 128]
        sw = jnp.where(even, pltpu.roll(s, 127, 1), pltpu.roll(s, 1, 1))
        parts.append(s * cos_ref[:, j * 128:(j + 1) * 128] + sw * sin_ref[:, j * 128:(j + 1) * 128])
    for hd in range(GQA_Q_HEADS):
        p = parts[hd // 2]
        qg_ref[0, hd] = p[:, (hd % 2) * 64:(hd % 2) * 64 + 64].astype(BF16)
    kg_ref[0] = parts[2].T.astype(BF16)
    vt = g[:, 384:512]
    for hd in range(GQA_KV_HEADS):
        vh = vt if hd == 0 else pltpu.roll(vt, 64, 1)
        vg_ref[0, hd] = jnp.where(lane < 64, vh, jnp.where(lane == 64, 1.0, 0.0)).astype(BF16)


def _inproj(xa, modsel, g_pre, w_in, cos_t, sin_t, gqk, bd384, ct):
    b, tt, _ = xa.shape
    nt = tt // TM
    tile = lambda w: pl.BlockSpec((1, TM, w), lambda bi, i: (bi, i, 0))
    heads = lambda nh: pl.BlockSpec((1, nh, TM, 64), lambda bi, i: (bi, 0, i, 0))
    hs = lambda nh: jax.ShapeDtypeStruct((b, nh, tt, 64), BF16)
    return pl.pallas_call(
        _inproj_kernel,
        grid=(b, nt),
        in_specs=[
            tile(D_MODEL),
            pl.BlockSpec((1, 1, 6, D_MODEL), lambda bi, i: (bi, i // ct, 0, 0)),
            _const_spec((1, D_MODEL)),
            _const_spec((D_MODEL, D_IN)),
            pl.BlockSpec((TM, 384), lambda bi, i: (i, 0)),
            pl.BlockSpec((TM, 384), lambda bi, i: (i, 0)),
            _const_spec((1, 384)),
            _const_spec((384, 384)),
        ],
        out_specs=[tile(COLS_A), heads(4), heads(4), heads(4), heads(4),
                   pl.BlockSpec((1, 2 * HEAD_DIM, TM), lambda bi, i: (bi, 0, i)),
                   pl.BlockSpec((1, 2, TM, 128), lambda bi, i: (bi, 0, i, 0)), tile(COLS_RW)],
        out_shape=[jax.ShapeDtypeStruct((b, tt, COLS_A), F32), hs(4), hs(4), hs(4), hs(4),
                   jax.ShapeDtypeStruct((b, 2 * HEAD_DIM, tt), BF16),
                   jax.ShapeDtypeStruct((b, 2, tt, 128), BF16),
                   jax.ShapeDtypeStruct((b, tt, COLS_RW), F32)],
        compiler_params=_cparams(("parallel", "parallel")),
        name="inproj",
    )(xa, modsel, g_pre, w_in, cos_t, sin_t, gqk, bd384)


def _na_kernel(q_ref, k0_ref, k1_ref, k2_ref, kc_ref, v0_ref, v1_ref, v2_ref, vc_ref, bias_ref, o_ref):
    outs = []
    for hd in range(NA_HEADS):
        kcat = jnp.concatenate([k0_ref[0, hd], k1_ref[0, hd], k2_ref[0, hd], kc_ref[0, hd]], axis=0)
        s = _dg(q_ref[0, hd], kcat, 1, 1)
        s_loc = s[:, :3 * TM] + bias_ref[0, hd]
        s_ctx = s[:, 3 * TM:]
        m = jnp.maximum(jnp.max(s_loc, axis=-1, keepdims=True), jnp.max(s_ctx, axis=-1, keepdims=True))
        p_loc = jnp.exp(s_loc - m)
        p_ctx = jnp.exp(s_ctx - m)
        l = jnp.sum(p_loc, axis=-1, keepdims=True) + jnp.sum(p_ctx, axis=-1, keepdims=True)
        vloc = jnp.concatenate([v0_ref[0, hd], v1_ref[0, hd], v2_ref[0, hd]], axis=0)
        o = _dg(p_loc.astype(BF16), vloc, 1, 0) + _dg(p_ctx.astype(BF16), vc_ref[0, hd], 1, 0)
        outs.append(o / l)
    o_ref[0] = jnp.concatenate(outs, axis=-1).astype(o_ref.dtype)


def _na(qn, kn, vn, bias_tab, ct):
    b, _, tt, _ = qn.shape
    nt = tt // TM

    def kv_spec(j):
        if j is None:
            return pl.BlockSpec((1, 4, TM, 64), lambda bi, i: (bi, 0, ct, 0))
        return pl.BlockSpec((1, 4, TM, 64), lambda bi, i: (bi, 0, jnp.clip(i - 1, 0, ct - 3) + j, 0))

    def pattern(i):
        return jnp.where(i == ct, 3, jnp.where(i == 0, 0, jnp.where(i == ct - 1, 2, 1)))

    return pl.pallas_call(
        _na_kernel,
        grid=(b, nt),
        in_specs=[
            pl.BlockSpec((1, 4, TM, 64), lambda bi, i: (bi, 0, i, 0)),
            kv_spec(0), kv_spec(1), kv_spec(2), kv_spec(None),
            kv_spec(0), kv_spec(1), kv_spec(2), kv_spec(None),
            pl.BlockSpec((1, 4, TM, 3 * TM), lambda bi, i: (pattern(i), 0, 0, 0)),
        ],
        out_specs=pl.BlockSpec((1, TM, 256), lambda bi, i: (bi, i, 0)),
        out_shape=jax.ShapeDtypeStruct((b, tt, 256), BF16),
        compiler_params=_cparams(("parallel", "parallel")),
        name="na_attn",
    )(qn, kn, kn, kn, kn, vn, vn, vn, vn, bias_tab)


def _na_bias_table(na_bias_l, rows):
    ct = rows * GRID_W // TM
    rpt = TM // GRID_W
    wr = min(NA_WIN_R, rows)
    qj = np.arange(GRID_W)
    kc = np.arange(GRID_W)
    cs = np.clip(qj - NA_WIN_C // 2, 0, GRID_W - NA_WIN_C)
    colvalid = (kc[None, :] >= cs[:, None]) & (kc[None, :] < cs[:, None] + NA_WIN_C)
    dc = kc[None, :] - qj[:, None] + (NA_WIN_C - 1)
    onehot = (dc.reshape(1, -1) == np.arange(2 * NA_WIN_C - 1)[:, None]) & colvalid.reshape(1, -1)
    toep = jnp.einsum("hrd,dx->hrx", na_bias_l, jnp.asarray(onehot.astype(np.float32)), precision=HIGHEST)
    toep = jnp.where(jnp.asarray(colvalid.reshape(-1)), toep, NEG)
    toep = toep.reshape(NA_HEADS, 2 * NA_WIN_R - 1, GRID_W, GRID_W)
    neg_blk = jnp.full((NA_HEADS, GRID_W, GRID_W), NEG, F32)
    tabs = []
    for tile_i in (0, 1, ct - 1):
        i0 = tile_i * rpt
        kb = int(np.clip(tile_i - 1, 0, ct - 3)) * rpt
        qrows = []
        for ri in range(rpt):
            qi = i0 + ri
            rs = int(np.clip(qi - wr // 2, 0, rows - wr))
            blks = [toep[:, kb + m - qi + NA_WIN_R - 1] if rs <= kb + m < rs + wr else neg_blk
                    for m in range(3 * rpt)]
            qrows.append(jnp.concatenate(blks, axis=-1))
        tabs.append(jnp.concatenate(qrows, axis=-2))
    tabs.append(jnp.full_like(tabs[0], NEG))
    return jnp.stack(tabs, axis=0)


def _gqa_kernel(q_ref, kt_ref, v_ref, o_ref, m_ref, acc_ref, *, n_full, tail, tq):
    q2 = jnp.concatenate([q_ref[0, 0], q_ref[0, 1]], axis=0)
    m_ref[...] = jnp.full(m_ref.shape, NEG, F32)
    acc_ref[...] = jnp.zeros(acc_ref.shape, F32)

    def chunk(start, size):
        kt = kt_ref[0, :, pl.ds(start, size)]
        vc = v_ref[0, 0, pl.ds(start, size), :]
        s = jnp.dot(q2, kt, preferred_element_type=F32)
        nlt = size // 128
        mt = s[:, :128]
        for j in range(1, nlt):
            mt = jnp.maximum(mt, s[:, j * 128:(j + 1) * 128])
        m_old = m_ref[...]
        m_new = jnp.maximum(m_old, jnp.max(mt, axis=-1, keepdims=True))
        p = jnp.concatenate([jnp.exp(s[:, j * 128:(j + 1) * 128] - m_new) for j in range(nlt)], axis=-1)
        acc_ref[...] = jnp.exp(m_old - m_new) * acc_ref[...] + jnp.dot(p.astype(BF16), vc,
                                                                       preferred_element_type=F32)
        m_ref[...] = m_new

    if n_full > 0:
        def body(j, carry):
            chunk(pl.multiple_of(j * GQA_TK, GQA_TK), GQA_TK)
            return carry
        lax.fori_loop(0, n_full, body, 0)
    if tail > 0:
        chunk(n_full * GQA_TK, tail)
    acc = acc_ref[...]
    o = acc[:, :HEAD_DIM] * (1.0 / acc[:, HEAD_DIM:HEAD_DIM + 1])
    o_ref[0] = jnp.concatenate([o[:tq], o[tq:]], axis=-1).astype(o_ref.dtype)


def _gqa(qg, kg, vg, seq, ctx_len):
    b, _, tt, _ = qg.shape
    scratch = lambda tq: [pltpu.VMEM((2 * tq, 128), F32), pltpu.VMEM((2 * tq, 128), F32)]
    y_lat = pl.pallas_call(
        functools.partial(_gqa_kernel, n_full=seq // GQA_TK, tail=ctx_len, tq=GQA_TQ),
        grid=(b, GQA_KV_HEADS, seq // GQA_TQ),
        in_specs=[
            pl.BlockSpec((1, 2, GQA_TQ, 64), lambda bi, n, i: (bi, n, i, 0)),
            pl.BlockSpec((1, HEAD_DIM, tt), lambda bi, n, i: (bi, n, 0)),
            pl.BlockSpec((1, 1, tt, 128), lambda bi, n, i: (bi, n, 0, 0)),
        ],
        out_specs=pl.BlockSpec((1, GQA_TQ, 128), lambda bi, n, i: (bi, i, n)),
        out_shape=jax.ShapeDtypeStruct((b, seq, 256), BF16),
        scratch_shapes=scratch(GQA_TQ),
        compiler_params=_cparams(("parallel", "parallel", "parallel")),
        name="gqa_latent",
    )(qg, kg, vg)
    cblk = seq // ctx_len
    y_ctx = pl.pallas_call(
        functools.partial(_gqa_kernel, n_full=0, tail=ctx_len, tq=ctx_len),
        grid=(b, GQA_KV_HEADS),
        in_specs=[
            pl.BlockSpec((1, 2, ctx_len, 64), lambda bi, n: (bi, n, cblk, 0)),
            pl.BlockSpec((1, HEAD_DIM, ctx_len), lambda bi, n: (bi, n, cblk)),
            pl.BlockSpec((1, 1, ctx_len, 128), lambda bi, n: (bi, n, cblk, 0)),
        ],
        out_specs=pl.BlockSpec((1, ctx_len, 128), lambda bi, n: (bi, 0, n)),
        out_shape=jax.ShapeDtypeStruct((b, ctx_len, 256), BF16),
        scratch_shapes=scratch(ctx_len),
        compiler_params=_cparams(("parallel", "parallel")),
        name="gqa_context",
    )(qg, kg, vg)
    return jnp.concatenate([y_lat, y_ctx], axis=1)


def _bd(x):
    t = jnp.concatenate([x, x, x, x], axis=0)
    r = lax.broadcasted_iota(jnp.int32, (4 * CH, RW), 0) >> 6
    c = lax.broadcasted_iota(jnp.int32, (4 * CH, RW), 1) >> 6
    return jnp.where(r == c, t, jnp.zeros_like(t))


def _fold(x):
    r = lax.broadcasted_iota(jnp.int32, (4 * CH, RW), 0) >> 6
    c = lax.broadcasted_iota(jnp.int32, (4 * CH, RW), 1) >> 6
    xm = jnp.where(r == c, x, 0.0)
    return xm[0:64] + xm[64:128] + xm[128:192] + xm[192:256]


def _chunk_mats(items):
    t = lax.broadcasted_iota(jnp.int32, (CH, RW), 0)
    j = lax.broadcasted_iota(jnp.int32, (CH, RW), 1) & 63
    eye = j == t
    before = {False: j < t, True: j > t}
    incl = {False: j <= t, True: j >= t}
    bf = lambda x: x.astype(BF16)
    n_items = range(len(items))

    lb, mb, lk, mk = [], [], [], []
    for it in items:
        a2 = bf(jnp.concatenate([it["kap"], it["rho"]], axis=0))
        lm_b = _dg(a2, _bd(bf(it["bt"])), 1, 1)
        lm_k = _dg(a2, _bd(bf(it["kt"])), 1, 1)
        lb.append(jnp.where(before[it["rev"]], lm_b[:CH], 0.0))
        mb.append(jnp.where(incl[it["rev"]], lm_b[CH:], 0.0))
        lk.append(jnp.where(before[it["rev"]], lm_k[:CH], 0.0))
        mk.append(jnp.where(incl[it["rev"]], lm_k[CH:], 0.0))

    p, mpow = [], []
    for i in n_items:
        nb = bf(-lb[i])
        p.append(jnp.where(eye, 1.0, 0.0) - lb[i])
        mpow.append(_dg(nb, _bd(nb), 1, 0))
    for _ in range(4):
        for i in n_items:
            mbf = bf(mpow[i])
            pm = _dg(jnp.concatenate([bf(p[i]), mbf], axis=0), _bd(mbf), 1, 0)
            p[i] = p[i] + pm[:CH]
            mpow[i] = pm[CH:]
    tinv = [bf(p[i] + _dg(bf(p[i]), _bd(bf(mpow[i])), 1, 0)) for i in n_items]

    lmv = [_dg(bf(jnp.concatenate([lk[i], mk[i]], axis=0)), _bd(bf(items[i]["v"])), 1, 0) for i in n_items]
    kp = [_dg(tinv[i], _bd(bf(items[i]["kap"])), 1, 0) for i in n_items]
    vp = [_dg(tinv[i], _bd(bf(lmv[i][:CH])), 1, 0) for i in n_items]
    out = []
    for i in n_items:
        it = items[i]
        mbb = bf(mb[i])
        rp = it["rho"] - _dg(mbb, _bd(bf(kp[i])), 1, 0)
        yl = lmv[i][CH:] - _dg(mbb, _bd(bf(vp[i])), 1, 0)
        g = jnp.where(eye, it["gdiag"], 0.0) - _fold(_dg(bf(kp[i]), bf(it["bhat"]), 0, 0))
        hmat = _fold(_dg(bf(jnp.concatenate([it["v"], -vp[i]], axis=0)),
                         bf(jnp.concatenate([it["khat"], it["bhat"]], axis=0)), 0, 0))
        out.append((g, hmat, rp, yl))
    return out


def _rwkv_chunk_kernel(z_ref, hp_ref, hn_ref, mu_ref, w0_ref, w2_ref, a0_ref, a2_ref, kkw_ref, ka_ref, rk_ref,
                       bd_ref, g_ref, h_ref, rp_ref, yl_ref, bonus_ref, *, ct):
    i = pl.program_id(1)
    z = z_ref[0]
    zs = z[:, :RW_SHIFT]
    valid_prev = jnp.where((i != 0) & (i != ct), 1.0, 0.0)
    valid_next = jnp.where((i != ct - 1) & (i != ct), 1.0, 0.0)
    prev_row = hp_ref[0, HALO - 1:HALO, :RW_SHIFT] * valid_prev
    next_row = hn_ref[0, 0:1, :RW_SHIFT] * valid_next
    row = lax.broadcasted_iota(jnp.int32, (TM, RW_SHIFT), 0)
    tt = lax.broadcasted_iota(jnp.int32, (TM, TM), 0)
    jj = lax.broadcasted_iota(jnp.int32, (TM, TM), 1)
    same_chunk = (tt >> 6) == (jj >> 6)
    bd = bd_ref[...]
    bonus = None
    items = []
    for d in range(2):
        if d == 0:
            nb = jnp.where(row == 0, prev_row, pltpu.roll(zs, 1, 0))
        else:
            nb = jnp.where(row == TM - 1, next_row, pltpu.roll(zs, TM - 1, 0))
        zd = zs + mu_ref[d:d + 1, :] * (nb - zs)
        r = zd[:, :RW]
        k = zd[:, RW:2 * RW]
        v = zd[:, 2 * RW:3 * RW]
        lw = zd[:, 3 * RW:3 * RW + LORA_W]
        la = zd[:, 3 * RW + LORA_W:]
        w_log = w0_ref[d:d + 1, :] + jnp.dot(jnp.tanh(lw), w2_ref[d], precision=HIGHEST,
                                             preferred_element_type=F32)
        sp = jnp.maximum(-w_log, 0.0) + jnp.log(1.0 + jnp.exp(-jnp.abs(w_log)))
        logw = -jnp.exp(-sp - 0.5)
        a = _sigmoid(a0_ref[d:d + 1, :] + jnp.dot(la, a2_ref[d], precision=HIGHEST, preferred_element_type=F32))
        kkr = k * kkw_ref[...]
        ss = jnp.dot(kkr * kkr, bd, precision=HIGHEST, preferred_element_type=F32)
        kk = kkr * lax.rsqrt(jnp.maximum(ss, 1e-24))
        k2 = k * (1.0 + (a - 1.0) * ka_ref[...])
        bv = kk * a
        bon = jnp.dot(r * k2 * rk_ref[...], bd, precision=HIGHEST, preferred_element_type=F32) * v
        bonus = bon if bonus is None else bonus + bon
        tri = same_chunk & ((jj >= tt) if d == 1 else (jj <= tt))
        cum = jnp.dot(jnp.where(tri, 1.0, 0.0), logw, precision=HIGHEST, preferred_element_type=F32)
        tot = jnp.dot(jnp.where(same_chunk, 1.0, 0.0), logw, precision=HIGHEST, preferred_element_type=F32)
        e_neg = jnp.exp(-cum)
        e_tot = jnp.exp(tot - cum)
        streams = dict(kap=kk * jnp.exp(cum - logw), kt=k2 * e_neg, bt=bv * e_neg, rho=r * jnp.exp(cum),
                       khat=k2 * e_tot, bhat=bv * e_tot, v=v, gdiag=jnp.exp(tot))
        for c in range(TM // CH):
            it = {name: val[c * CH:(c + 1) * CH] for name, val in streams.items()}
            it["gdiag"] = it["gdiag"][0:1]
            it["rev"] = d == 1
            items.append(it)
    for idx, (g, hm, rp, yl) in enumerate(_chunk_mats(items)):
        d, c = divmod(idx, TM // CH)
        g_ref[0, d, c] = g
        h_ref[0, d, c] = hm
        rp_ref[0, d, c] = rp
        yl_ref[0, d, c] = yl
    bonus_ref[0] = bonus


def _rwkv_chunks(zr, mu, w0, w2, a0, a2, kkw, ka, rk, bd256, ct):
    b, tt, _ = zr.shape
    nt = tt // TM
    nch = tt // CH
    cpt = TM // CH
    nhb = tt // HALO
    mats = pl.BlockSpec((1, 2, cpt, CH, RW), lambda bi, i: (bi, 0, i, 0, 0))
    mshape = jax.ShapeDtypeStruct((b, 2, nch, CH, RW), F32)
    return pl.pallas_call(
        functools.partial(_rwkv_chunk_kernel, ct=ct),
        grid=(b, nt),
        in_specs=[
            pl.BlockSpec((1, TM, COLS_RW), lambda bi, i: (bi, i, 0)),
            pl.BlockSpec((1, HALO, COLS_RW), lambda bi, i: (bi, jnp.maximum(i * (TM // HALO) - 1, 0), 0)),
            pl.BlockSpec((1, HALO, COLS_RW), lambda bi, i: (bi, jnp.minimum((i + 1) * (TM // HALO), nhb - 1), 0)),
            _const_spec((2, RW_SHIFT)),
            _const_spec((2, RW)),
            _const_spec((2, LORA_W, RW)),
            _const_spec((2, RW)),
            _const_spec((2, LORA_A, RW)),
            _const_spec((1, RW)),
            _const_spec((1, RW)),
            _const_spec((1, RW)),
            _const_spec((RW, RW)),
        ],
        out_specs=[mats, mats, mats, mats, pl.BlockSpec((1, TM, RW), lambda bi, i: (bi, i, 0))],
        out_shape=[mshape, mshape, mshape, mshape, jax.ShapeDtypeStruct((b, tt, RW), F32)],
        compiler_params=_cparams(("parallel", "parallel")),
        name="rwkv_chunks",
    )(zr, zr, zr, mu, w0, w2, a0, a2, kkw, ka, rk, bd256)


def _rwkv_chain_kernel(gf_ref, hf_ref, rf_ref, yf_ref, gr_ref, hr_ref, rr_ref, yr_ref, of_ref, or_ref, s_ref,
                       *, nb):
    @pl.when(pl.program_id(0) == 0)
    def _():
        s_ref[...] = jnp.zeros(s_ref.shape, F32)

    for d, (g_ref, h_ref, rp_ref, yl_ref, o_ref) in enumerate(
            ((gf_ref, hf_ref, rf_ref, yf_ref, of_ref), (gr_ref, hr_ref, rr_ref, yr_ref, or_ref))):
        for bi in range(nb):
            s = s_ref[d, bi]
            sb = s.astype(BF16)
            o_ref[bi] = _dg(rp_ref[bi, 0, 0].astype(BF16), _bd(sb), 1, 1) + yl_ref[bi, 0, 0]
            s_ref[d, bi] = _dg(sb, _bd(g_ref[bi, 0, 0].astype(BF16)), 1, 0) + h_ref[bi, 0, 0]


def _rwkv_chain(g, h, rp, yl, seq, ctx_len):
    b, _, nch, _, _ = g.shape
    n_lat = seq // CH
    n_ctx = ctx_len // CH

    def cf(s):
        return jnp.where(s < n_ctx, n_lat + s, s - n_ctx)

    def cr(s):
        return nch - 1 - s

    fwd = pl.BlockSpec((b, 1, 1, CH, RW), lambda s: (0, 0, cf(s), 0, 0))
    rev = pl.BlockSpec((b, 1, 1, CH, RW), lambda s: (0, 1, cr(s), 0, 0))
    yshape = jax.ShapeDtypeStruct((b, nch * CH, RW), F32)
    return pl.pallas_call(
        functools.partial(_rwkv_chain_kernel, nb=b),
        grid=(nch,),
        in_specs=[fwd, fwd, fwd, fwd, rev, rev, rev, rev],
        out_specs=[pl.BlockSpec((b, CH, RW), lambda s: (0, cf(s), 0)),
                   pl.BlockSpec((b, CH, RW), lambda s: (0, cr(s), 0))],
        out_shape=[yshape, yshape],
        scratch_shapes=[pltpu.VMEM((2, b, CH, RW), F32)],
        compiler_params=_cparams(("arbitrary",)),
        name="rwkv_chain",
    )(g, h, rp, yl, g, h, rp, yl)


def _merge_kernel(x_ref, mod_ref, gpre_ref, gpost_ref, za_ref, hp_ref, hn_ref, ca_ref, yb_ref, yc_ref,
                  yf_ref, yr_ref, bonus_ref, lg_ref, g2_ref, lnw_ref, lnb_ref, bd_ref,
                  wb_ref, wg_ref, bg_ref, wo_ref, o_ref, *, ct):
    i = pl.program_id(1)
    x = x_ref[0]
    m = mod_ref[0, 0]
    hb = _rms_mod(x, gpre_ref[...], m[0:1], m[1:2]).astype(BF16)

    za = za_ref[0]
    u = za[:, 256:512] * za[:, 512:768]
    valid_prev = jnp.where((i != 0) & (i != ct), 1.0, 0.0)
    valid_next = jnp.where((i != ct - 1) & (i != ct), 1.0, 0.0)
    up = hp_ref[0, HALO - 1:HALO, 256:512] * hp_ref[0, HALO - 1:HALO, 512:768] * valid_prev
    un = hn_ref[0, 0:1, 256:512] * hn_ref[0, 0:1, 512:768] * valid_next
    row = lax.broadcasted_iota(jnp.int32, (TM, 256), 0)
    u_prev = jnp.where(row == 0, up, pltpu.roll(u, 1, 0))
    u_next = jnp.where(row == TM - 1, un, pltpu.roll(u, TM - 1, 0))
    ya = za[:, :256] * (u_prev * ca_ref[0:1, :] + u * ca_ref[1:2, :] + u_next * ca_ref[2:3, :])

    bd = bd_ref[...]
    y = yf_ref[0] + yr_ref[0]
    mean = jnp.dot(y, bd, precision=HIGHEST, preferred_element_type=F32) * (1.0 / HEAD_DIM)
    yc0 = y - mean
    var = jnp.dot(yc0 * yc0, bd, precision=HIGHEST, preferred_element_type=F32) * (1.0 / HEAD_DIM)
    yn = yc0 * lax.rsqrt(var + LN_X_EPS) * lnw_ref[...] + lnb_ref[...] + bonus_ref[0]
    yd = yn * jnp.dot(_sigmoid(lg_ref[0]), g2_ref[...], precision=HIGHEST, preferred_element_type=F32)

    acc = None
    for bidx, ys in enumerate((ya.astype(BF16), yb_ref[0], yc_ref[0], yd.astype(BF16))):
        sl = slice(bidx * D_MODEL, (bidx + 1) * D_MODEL)
        gate = _sigmoid(jnp.dot(hb, wg_ref[:, sl], preferred_element_type=F32) + bg_ref[:, sl])
        term = gate * jnp.dot(ys, wb_ref[bidx], preferred_element_type=F32)
        acc = term if acc is None else acc + term
    mo = jnp.dot(acc.astype(BF16), wo_ref[...], preferred_element_type=F32)
    o_ref[0] = x + m[2:3] * _rms(mo, gpost_ref[...])


def _merge(xa, modsel, g_pre, g_post, za, conv_a, yb, yc, yf, yr, bonus, zr, g2, ln_w, ln_b, bd256,
           w_branch, w_gate, b_gate, w_o, ct):
    b, tt, _ = xa.shape
    nt = tt // TM
    nhb = tt // HALO
    tile = lambda w: pl.BlockSpec((1, TM, w), lambda bi, i: (bi, i, 0))
    return pl.pallas_call(
        functools.partial(_merge_kernel, ct=ct),
        grid=(b, nt),
        in_specs=[
            tile(D_MODEL),
            pl.BlockSpec((1, 1, 6, D_MODEL), lambda bi, i: (bi, i // ct, 0, 0)),
            _const_spec((1, D_MODEL)),
            _const_spec((1, D_MODEL)),
            tile(COLS_A),
            pl.BlockSpec((1, HALO, COLS_A), lambda bi, i: (bi, jnp.maximum(i * (TM // HALO) - 1, 0), 0)),
            pl.BlockSpec((1, HALO, COLS_A), lambda bi, i: (bi, jnp.minimum((i + 1) * (TM // HALO), nhb - 1), 0)),
            _const_spec((3, 256)),
            tile(256), tile(256), tile(256), tile(256), tile(256),
            pl.BlockSpec((1, TM, LORA_G), lambda bi, i: (bi, i, RW_SHIFT // LORA_G)),
            _const_spec((LORA_G, RW)),
            _const_spec((1, RW)),
            _const_spec((1, RW)),
            _const_spec((RW, RW)),
            _const_spec((4, 256, D_MODEL)),
            _const_spec((D_MODEL, 4 * D_MODEL)),
            _const_spec((1, 4 * D_MODEL)),
            _const_spec((D_MODEL, D_MODEL)),
        ],
        out_specs=tile(D_MODEL),
        out_shape=jax.ShapeDtypeStruct((b, tt, D_MODEL), F32),
        compiler_params=_cparams(("parallel", "parallel")),
        name="merge",
    )(xa, modsel, g_pre, g_post, za, za, za, conv_a, yb, yc, yf, yr, bonus, zr, g2, ln_w, ln_b, bd256,
      w_branch, w_gate, b_gate, w_o)


def _ffn_kernel(x_ref, hp_ref, hn_ref, mod_ref, gpre_ref, gpost_ref, wu_ref, cw_ref, wd_ref, o_ref, f_ref, *, ct):
    i = pl.program_id(1)
    x = x_ref[0]
    m = mod_ref[0, 0]
    xx = jnp.concatenate([hp_ref[0], x, hn_ref[0]], axis=0)
    hb = _rms_mod(xx, gpre_ref[...], m[3:4], m[4:5]).astype(BF16)
    nrow = TM + 2 * HALO
    row = lax.broadcasted_iota(jnp.int32, (nrow, 1), 0)
    valid_prev = jnp.where((i != 0) & (i != ct), 1.0, 0.0)
    valid_next = jnp.where((i != ct - 1) & (i != ct), 1.0, 0.0)
    rowmask = jnp.where(row < HALO, valid_prev, jnp.where(row >= TM + HALO, valid_next, 1.0))

    def conv(u, col):
        u = u * rowmask
        w = cw_ref[:, col:col + FF_CHUNK]
        c = (pltpu.roll(u, 1, 0) * w[0:1] + u * w[1:2] + pltpu.roll(u, nrow - 1, 0) * w[2:3])
        return c[HALO:HALO + TM]

    for j in range(D_FF // FF_CHUNK):
        ca = conv(jnp.dot(hb, wu_ref[:, j * FF_CHUNK:(j + 1) * FF_CHUNK], preferred_element_type=F32),
                  j * FF_CHUNK)
        cg = conv(jnp.dot(hb, wu_ref[:, D_FF + j * FF_CHUNK:D_FF + (j + 1) * FF_CHUNK],
                          preferred_element_type=F32), D_FF + j * FF_CHUNK)
        act = (ca * (cg * _sigmoid(cg))).astype(BF16)
        part = jnp.dot(act, wd_ref[j * FF_CHUNK:(j + 1) * FF_CHUNK, :], preferred_element_type=F32)
        if j == 0:
            f_ref[...] = part
        else:
            f_ref[...] += part
    o_ref[0] = x + m[5:6] * _rms(f_ref[...], gpost_ref[...])


def _ffn(xa, modsel, g_pre, g_post, w_up, conv_w, w_down, ct):
    b, tt, _ = xa.shape
    nt = tt // TM
    nhb = tt // HALO
    return pl.pallas_call(
        functools.partial(_ffn_kernel, ct=ct),
        grid=(b, nt),
        in_specs=[
            pl.BlockSpec((1, TM, D_MODEL), lambda bi, i: (bi, i, 0)),
            pl.BlockSpec((1, HALO, D_MODEL), lambda bi, i: (bi, jnp.maximum(i * (TM // HALO) - 1, 0), 0)),
            pl.BlockSpec((1, HALO, D_MODEL), lambda bi, i: (bi, jnp.minimum((i + 1) * (TM // HALO), nhb - 1), 0)),
            pl.BlockSpec((1, 1, 6, D_MODEL), lambda bi, i: (bi, i // ct, 0, 0)),
            _const_spec((1, D_MODEL)),
            _const_spec((1, D_MODEL)),
            _const_spec((D_MODEL, 2 * D_FF)),
            _const_spec((3, 2 * D_FF)),
            _const_spec((D_FF, D_MODEL)),
        ],
        out_specs=pl.BlockSpec((1, TM, D_MODEL), lambda bi, i: (bi, i, 0)),
        out_shape=jax.ShapeDtypeStruct((b, tt, D_MODEL), F32),
        scratch_shapes=[pltpu.VMEM((TM, D_MODEL), F32)],
        compiler_params=_cparams(("parallel", "parallel")),
        name="conv_ffn",
    )(xa, xa, xa, modsel, g_pre, g_post, w_up, conv_w, w_down)


def _rope_tables(seq, ctx_len):
    t = np.arange(seq)
    row = (t // GRID_W).astype(np.float32)
    col = (t % GRID_W).astype(np.float32)
    n_freq = HEAD_DIM // 4
    inv_freq = jnp.asarray(ROPE_THETA, F32) ** (-jnp.arange(n_freq, dtype=F32) / n_freq)
    ang = jnp.concatenate([jnp.asarray(row)[:, None] * inv_freq, jnp.asarray(col)[:, None] * inv_freq], axis=-1)
    cos = jnp.repeat(jnp.cos(ang), 2, axis=-1)
    sin = jnp.repeat(jnp.sin(ang), 2, axis=-1) * jnp.tile(jnp.asarray([-1.0, 1.0], F32), HEAD_DIM // 2)
    cos = jnp.concatenate([cos, jnp.ones((ctx_len, HEAD_DIM), F32)], axis=0)
    sin = jnp.concatenate([sin, jnp.zeros((ctx_len, HEAD_DIM), F32)], axis=0)
    nh = GQA_Q_HEADS + GQA_KV_HEADS
    return jnp.tile(cos, (1, nh)), jnp.tile(sin, (1, nh))


def _block_ones(n, scale):
    idx = np.arange(n) // HEAD_DIM
    return jnp.asarray((idx[:, None] == idx[None, :]).astype(np.float32) * scale)


def kernel(x, c, ctx, c_ctx, ada_w, ada_b, norm_mix_pre, norm_mix_post, norm_ffn_pre, norm_ffn_post, w_in, conv_a, na_bias, q_norm, k_norm, rw_mu, rw_w0, rw_w2, rw_a0, rw_a2, rw_kk, rw_ka, rw_rk, rw_g2, rw_ln_w, rw_ln_b, w_branch, w_gate, b_gate, w_o, ffn_up, ffn_conv, ffn_down):
    b, seq, _ = x.shape
    ctx_len = ctx.shape[1]
    depth = ada_w.shape[0]
    assert ctx_len == TM and seq % GQA_TQ == 0 and seq // TM >= 3 and b + 1 <= 8
    ct = seq // TM
    rows = seq // GRID_W

    cvec = jnp.zeros((8, D_MODEL), F32).at[:b].set(c).at[b].set(c_ctx)
    mods = _ada(cvec, ada_w, ada_b)
    cos_t, sin_t = _rope_tables(seq, ctx_len)
    bd384 = _block_ones(384, 1.0 / HEAD_DIM)
    bd256 = _block_ones(256, 1.0)

    xa = jnp.concatenate([x, ctx], axis=1)
    for l in range(depth):
        ml = mods[l].reshape(8, 6, D_MODEL)
        modsel = jnp.stack([ml[:b], jnp.broadcast_to(ml[b][None], (b, 6, D_MODEL))], axis=1)
        row2 = lambda a: a.reshape(1, -1)
        gqk = jnp.concatenate([jnp.tile(q_norm[l], GQA_Q_HEADS) * HEAD_DIM ** -0.5,
                               jnp.tile(k_norm[l], GQA_KV_HEADS)]).reshape(1, -1)
        za, qn, kn, vn, qg, kg, vg, zr = _inproj(xa, modsel, row2(norm_mix_pre[l]), w_in[l].astype(BF16),
                                                 cos_t, sin_t, gqk, bd384, ct)
        yb = _na(qn, kn, vn, _na_bias_table(na_bias[l], rows), ct)
        yc = _gqa(qg, kg, vg, seq, ctx_len)
        g, h, rp, yl, bonus = _rwkv_chunks(zr, rw_mu[l], rw_w0[l], rw_w2[l], rw_a0[l], rw_a2[l],
                                           row2(rw_kk[l]), row2(rw_ka[l]), row2(rw_rk[l]), bd256, ct)
        yf, yr = _rwkv_chain(g, h, rp, yl, seq, ctx_len)
        xa = _merge(xa, modsel, row2(norm_mix_pre[l]), row2(norm_mix_post[l]), za, conv_a[l], yb, yc, yf, yr,
                    bonus, zr, rw_g2[l], row2(rw_ln_w[l]), row2(rw_ln_b[l]), bd256,
                    w_branch[l].astype(BF16), w_gate[l].astype(BF16), row2(b_gate[l]), w_o[l].astype(BF16), ct)
        xa = _ffn(xa, modsel, row2(norm_ffn_pre[l]), row2(norm_ffn_post[l]), ffn_up[l].astype(BF16),
                  ffn_conv[l], ffn_down[l].astype(BF16), ct)
    return xa[:, :seq]
```

```python
import functools

import numpy as np
import jax
import jax.numpy as jnp
from jax import lax
from jax.experimental import pallas as pl
from jax.experimental.pallas import tpu as pltpu

F32 = jnp.float32
BF16 = jnp.bfloat16
HIGHEST = lax.Precision.HIGHEST

D_MODEL = 1024
GRID_W = 64
HEAD_DIM = 64
NA_HEADS = 4
NA_WIN_R = 8
NA_WIN_C = 16
GQA_Q_HEADS = 4
GQA_KV_HEADS = 2
ROPE_THETA = 10000.0
RWKV_HEADS = 4
RW = RWKV_HEADS * HEAD_DIM
LORA_W = 64
LORA_A = 64
LORA_G = 128
RW_SHIFT = 3 * RW + LORA_W + LORA_A
D_FF = 2816
NORM_EPS = 1e-6
LN_X_EPS = 64e-5
COLS_A = 768
COLS_NA = 768
COLS_GQA = 512
COLS_RW = 1024
D_IN = COLS_A + COLS_NA + COLS_GQA + COLS_RW

TM = 256
TML = 512
CH = 64
HALO = 8
FF_CHUNK = 256
GQA_TQ = 512
GQA_TK = 512
NEG = -1e30
VMEM_LIMIT = 56 * 1024 * 1024


def _cparams(sem):
    return pltpu.CompilerParams(dimension_semantics=sem, vmem_limit_bytes=VMEM_LIMIT)


def _const_spec(shape):
    nd = len(shape)
    return pl.BlockSpec(shape, lambda *_: (0,) * nd, pipeline_mode=pl.Buffered(1))


def _token_tiling(seq, tm, ctx_mode, n_out):
    nblk = seq // tm
    blk = (lambda i: i * 0 + nblk) if ctx_mode else (lambda i: i)
    alias_specs = [pl.BlockSpec(memory_space=pl.ANY)] * n_out if ctx_mode else []
    return nblk, blk, (1 if ctx_mode else nblk), alias_specs


def _halo_specs(width, tm, blk, nhb):
    per = tm // HALO
    return [pl.BlockSpec((1, HALO, width), lambda bi, i: (bi, jnp.maximum(blk(i) * per - 1, 0), 0)),
            pl.BlockSpec((1, HALO, width), lambda bi, i: (bi, jnp.minimum((blk(i) + 1) * per, nhb - 1), 0))]


def _edge_valid(i, nblk, ctx_mode):
    if ctx_mode:
        return 0.0, 0.0
    return jnp.where(i != 0, 1.0, 0.0), jnp.where(i != nblk - 1, 1.0, 0.0)


def _dg(a, b, ca, cb, **kw):
    return lax.dot_general(a, b, (((ca,), (cb,)), ((), ())), preferred_element_type=F32, **kw)


def _rms_mod(x, g, shift, scale):
    y = x * lax.rsqrt(jnp.mean(x * x, axis=-1, keepdims=True) + NORM_EPS) * g
    return y * (1.0 + scale) + shift


def _rms(x, g):
    return x * lax.rsqrt(jnp.mean(x * x, axis=-1, keepdims=True) + NORM_EPS) * g


def _sigmoid(x):
    return 1.0 / (1.0 + jnp.exp(-x))


def _ada_kernel(c_ref, w_ref, b_ref, o_ref):
    c = c_ref[...]
    s = c * _sigmoid(c)
    o_ref[0] = jnp.dot(s, w_ref[0], precision=HIGHEST, preferred_element_type=F32) + b_ref[0]


def _ada(cvec, ada_w, ada_b):
    depth = ada_w.shape[0]
    nblk = ada_w.shape[2] // D_MODEL
    return pl.pallas_call(
        _ada_kernel,
        grid=(depth, nblk),
        in_specs=[
            pl.BlockSpec((8, D_MODEL), lambda l, j: (0, 0)),
            pl.BlockSpec((1, D_MODEL, D_MODEL), lambda l, j: (l, 0, j)),
            pl.BlockSpec((1, 1, D_MODEL), lambda l, j: (l, 0, j)),
        ],
        out_specs=pl.BlockSpec((1, 8, D_MODEL), lambda l, j: (l, 0, j)),
        out_shape=jax.ShapeDtypeStruct((depth, 8, ada_w.shape[2]), F32),
        compiler_params=_cparams(("parallel", "parallel")),
        name="ada_mod",
    )(cvec, ada_w, ada_b.reshape(depth, 1, -1))


def _inproj_kernel(*refs):
    x_ref, mod_ref, g_ref, w_ref, cos_ref, sin_ref, gqk_ref, bd_ref = refs[:8]
    za_ref, qn_ref, kn_ref, vn_ref, qg_ref, kg_ref, vg_ref, zr_ref = refs[-8:]
    tm = x_ref.shape[1]
    x = x_ref[0]
    m = mod_ref[0, 0]
    h = _rms_mod(x, g_ref[...], m[0:1], m[1:2])
    z = jnp.dot(h.astype(BF16), w_ref[...], preferred_element_type=F32)
    za_ref[0] = z[:, :COLS_A]
    zr_ref[0] = z[:, COLS_A + COLS_NA + COLS_GQA:]
    na = z[:, COLS_A:COLS_A + COLS_NA]
    scale = HEAD_DIM ** -0.5
    for hd in range(NA_HEADS):
        qn_ref[0, hd] = (na[:, hd * 64:(hd + 1) * 64] * scale).astype(BF16)
        kn_ref[0, hd] = na[:, 256 + hd * 64:256 + (hd + 1) * 64].astype(BF16)
        vn_ref[0, hd] = na[:, 512 + hd * 64:512 + (hd + 1) * 64].astype(BF16)
    g = z[:, COLS_A + COLS_NA:COLS_A + COLS_NA + COLS_GQA]
    qk = g[:, :384]
    ms = jnp.dot(qk * qk, bd_ref[...], precision=HIGHEST, preferred_element_type=F32)
    qkn = qk * lax.rsqrt(ms + NORM_EPS) * gqk_ref[...]
    lane = lax.broadcasted_iota(jnp.int32, (tm, 128), 1)
    even = (lane & 1) == 0
    parts = []
    for j in range(3):
        s = qkn[:, j * 128:(j + 1) * 128]
        sw = jnp.where(even, pltpu.roll(s, 127, 1), pltpu.roll(s, 1, 1))
        parts.append(s * cos_ref[:, j * 128:(j + 1) * 128] + sw * sin_ref[:, j * 128:(j + 1) * 128])
    for hd in range(GQA_Q_HEADS):
        p = parts[hd // 2]
        qg_ref[0, hd] = p[:, (hd % 2) * 64:(hd % 2) * 64 + 64].astype(BF16)
    kg_ref[0] = parts[2].T.astype(BF16)
    vt = g[:, 384:512]
    for hd in range(GQA_KV_HEADS):
        vh = vt if hd == 0 else pltpu.roll(vt, 64, 1)
        vg_ref[0, hd] = jnp.where(lane < 64, vh, jnp.where(lane == 64, 1.0, 0.0)).astype(BF16)


def _inproj(xa, modsel, g_pre, w_in, cos_t, sin_t, gqk, bd384, seq, tm, prev=None):
    b, tt, _ = xa.shape
    ctx_mode = prev is not None
    nblk, blk, steps, alias_specs = _token_tiling(seq, tm, ctx_mode, 8)
    tile = lambda w: pl.BlockSpec((1, tm, w), lambda bi, i: (bi, blk(i), 0))
    heads = lambda nh: pl.BlockSpec((1, nh, tm, 64), lambda bi, i: (bi, 0, blk(i), 0))
    hs = lambda nh: jax.ShapeDtypeStruct((b, nh, tt, 64), BF16)
    return pl.pallas_call(
        _inproj_kernel,
        grid=(b, steps),
        in_specs=[
            tile(D_MODEL),
            pl.BlockSpec((1, 1, 6, D_MODEL), lambda bi, i: (bi, int(ctx_mode), 0, 0)),
            _const_spec((1, D_MODEL)),
            _const_spec((D_MODEL, D_IN)),
            pl.BlockSpec((tm, 384), lambda bi, i: (blk(i), 0)),
            pl.BlockSpec((tm, 384), lambda bi, i: (blk(i), 0)),
            _const_spec((1, 384)),
            _const_spec((384, 384)),
        ] + alias_specs,
        out_specs=[tile(COLS_A), heads(4), heads(4), heads(4), heads(4),
                   pl.BlockSpec((1, 2 * HEAD_DIM, tm), lambda bi, i: (bi, 0, blk(i))),
                   pl.BlockSpec((1, 2, tm, 128), lambda bi, i: (bi, 0, blk(i), 0)), tile(COLS_RW)],
        out_shape=[jax.ShapeDtypeStruct((b, tt, COLS_A), F32), hs(4), hs(4), hs(4), hs(4),
                   jax.ShapeDtypeStruct((b, 2 * HEAD_DIM, tt), BF16),
                   jax.ShapeDtypeStruct((b, 2, tt, 128), BF16),
                   jax.ShapeDtypeStruct((b, tt, COLS_RW), F32)],
        input_output_aliases={8 + k: k for k in range(8)} if ctx_mode else {},
        compiler_params=_cparams(("parallel", "parallel")),
        name="inproj_ctx" if ctx_mode else "inproj",
    )(xa, modsel, g_pre, w_in, cos_t, sin_t, gqk, bd384, *(prev or ()))


def _na_kernel(q_ref, k0_ref, k1_ref, k2_ref, kc_ref, v0_ref, v1_ref, v2_ref, vc_ref, bias_ref, o_ref):
    outs = []
    for hd in range(NA_HEADS):
        kcat = jnp.concatenate([k0_ref[0, hd], k1_ref[0, hd], k2_ref[0, hd], kc_ref[0, hd]], axis=0)
        s = _dg(q_ref[0, hd], kcat, 1, 1)
        s_loc = s[:, :3 * TM] + bias_ref[0, hd]
        s_ctx = s[:, 3 * TM:]
        m = jnp.maximum(jnp.max(s_loc, axis=-1, keepdims=True), jnp.max(s_ctx, axis=-1, keepdims=True))
        p_loc = jnp.exp(s_loc - m)
        p_ctx = jnp.exp(s_ctx - m)
        l = jnp.sum(p_loc, axis=-1, keepdims=True) + jnp.sum(p_ctx, axis=-1, keepdims=True)
        vloc = jnp.concatenate([v0_ref[0, hd], v1_ref[0, hd], v2_ref[0, hd]], axis=0)
        o = _dg(p_loc.astype(BF16), vloc, 1, 0) + _dg(p_ctx.astype(BF16), vc_ref[0, hd], 1, 0)
        outs.append(o / l)
    o_ref[0] = jnp.concatenate(outs, axis=-1).astype(o_ref.dtype)


def _na(qn, kn, vn, bias_tab, ct):
    b, _, tt, _ = qn.shape
    nt = tt // TM

    def kv_spec(j):
        if j is None:
            return pl.BlockSpec((1, 4, TM, 64), lambda bi, i: (bi, 0, ct, 0))
        return pl.BlockSpec((1, 4, TM, 64), lambda bi, i: (bi, 0, jnp.clip(i - 1, 0, ct - 3) + j, 0))

    def pattern(i):
        return jnp.where(i == ct, 3, jnp.where(i == 0, 0, jnp.where(i == ct - 1, 2, 1)))

    return pl.pallas_call(
        _na_kernel,
        grid=(b, nt),
        in_specs=[
            pl.BlockSpec((1, 4, TM, 64), lambda bi, i: (bi, 0, i, 0)),
            kv_spec(0), kv_spec(1), kv_spec(2), kv_spec(None),
            kv_spec(0), kv_spec(1), kv_spec(2), kv_spec(None),
            pl.BlockSpec((1, 4, TM, 3 * TM), lambda bi, i: (pattern(i), 0, 0, 0)),
        ],
        out_specs=pl.BlockSpec((1, TM, 256), lambda bi, i: (bi, i, 0)),
        out_shape=jax.ShapeDtypeStruct((b, tt, 256), BF16),
        compiler_params=_cparams(("parallel", "parallel")),
        name="na_attn",
    )(qn, kn, kn, kn, kn, vn, vn, vn, vn, bias_tab)


def _na_bias_table(na_bias_l, rows):
    ct = rows * GRID_W // TM
    rpt = TM // GRID_W
    wr = min(NA_WIN_R, rows)
    qj = np.arange(GRID_W)
    kc = np.arange(GRID_W)
    cs = np.clip(qj - NA_WIN_C // 2, 0, GRID_W - NA_WIN_C)
    colvalid = (kc[None, :] >= cs[:, None]) & (kc[None, :] < cs[:, None] + NA_WIN_C)
    dc = kc[None, :] - qj[:, None] + (NA_WIN_C - 1)
    onehot = (dc.reshape(1, -1) == np.arange(2 * NA_WIN_C - 1)[:, None]) & colvalid.reshape(1, -1)
    toep = jnp.einsum("hrd,dx->hrx", na_bias_l, jnp.asarray(onehot.astype(np.float32)), precision=HIGHEST)
    toep = jnp.where(jnp.asarray(colvalid.reshape(-1)), toep, NEG)
    toep = toep.reshape(NA_HEADS, 2 * NA_WIN_R - 1, GRID_W, GRID_W)
    neg_blk = jnp.full((NA_HEADS, GRID_W, GRID_W), NEG, F32)
    tabs = []
    for tile_i in (0, 1, ct - 1):
        i0 = tile_i * rpt
        kb = int(np.clip(tile_i - 1, 0, ct - 3)) * rpt
        qrows = []
        for ri in range(rpt):
            qi = i0 + ri
            rs = int(np.clip(qi - wr // 2, 0, rows - wr))
            blks = [toep[:, kb + m - qi + NA_WIN_R - 1] if rs <= kb + m < rs + wr else neg_blk
                    for m in range(3 * rpt)]
            qrows.append(jnp.concatenate(blks, axis=-1))
        tabs.append(jnp.concatenate(qrows, axis=-2))
    tabs.append(jnp.full_like(tabs[0], NEG))
    return jnp.stack(tabs, axis=0)


def _gqa_kernel(q_ref, kt_ref, v_ref, o_ref, m_ref, acc_ref, *, n_full, tail, tq):
    q2 = jnp.concatenate([q_ref[0, 0], q_ref[0, 1]], axis=0)
    m_ref[...] = jnp.full(m_ref.shape, NEG, F32)
    acc_ref[...] = jnp.zeros(acc_ref.shape, F32)

    def chunk(start, size):
        kt = kt_ref[0, :, pl.ds(start, size)]
        vc = v_ref[0, 0, pl.ds(start, size), :]
        s = jnp.dot(q2, kt, preferred_element_type=F32)
        nlt = size // 128
        mt = s[:, :128]
        for j in range(1, nlt):
            mt = jnp.maximum(mt, s[:, j * 128:(j + 1) * 128])
        m_old = m_ref[...]
        m_new = jnp.maximum(m_old, jnp.max(mt, axis=-1, keepdims=True))
        p = jnp.concatenate([jnp.exp(s[:, j * 128:(j + 1) * 128] - m_new) for j in range(nlt)], axis=-1)
        acc_ref[...] = jnp.exp(m_old - m_new) * acc_ref[...] + jnp.dot(p.astype(BF16), vc,
                                                                       preferred_element_type=F32)
        m_ref[...] = m_new

    if n_full > 0:
        def body(j, carry):
            chunk(pl.multiple_of(j * GQA_TK, GQA_TK), GQA_TK)
            return carry
        lax.fori_loop(0, n_full, body, 0, unroll=4)
    if tail > 0:
        chunk(n_full * GQA_TK, tail)
    acc = acc_ref[...]
    o = acc[:, :HEAD_DIM] * (1.0 / acc[:, HEAD_DIM:HEAD_DIM + 1])
    o_ref[0] = jnp.concatenate([o[:tq], o[tq:]], axis=-1).astype(o_ref.dtype)


def _gqa(qg, kg, vg, seq, ctx_len):
    b, _, tt, _ = qg.shape
    scratch = lambda tq: [pltpu.VMEM((2 * tq, 128), F32), pltpu.VMEM((2 * tq, 128), F32)]
    y_lat = pl.pallas_call(
        functools.partial(_gqa_kernel, n_full=seq // GQA_TK, tail=ctx_len, tq=GQA_TQ),
        grid=(b, GQA_KV_HEADS, seq // GQA_TQ),
        in_specs=[
            pl.BlockSpec((1, 2, GQA_TQ, 64), lambda bi, n, i: (bi, n, i, 0)),
            pl.BlockSpec((1, HEAD_DIM, tt), lambda bi, n, i: (bi, n, 0)),
            pl.BlockSpec((1, 1, tt, 128), lambda bi, n, i: (bi, n, 0, 0)),
        ],
        out_specs=pl.BlockSpec((1, GQA_TQ, 128), lambda bi, n, i: (bi, i, n)),
        out_shape=jax.ShapeDtypeStruct((b, seq, 256), BF16),
        scratch_shapes=scratch(GQA_TQ),
        compiler_params=_cparams(("parallel", "parallel", "parallel")),
        name="gqa_latent",
    )(qg, kg, vg)
    cblk = seq // ctx_len
    y_ctx = pl.pallas_call(
        functools.partial(_gqa_kernel, n_full=0, tail=ctx_len, tq=ctx_len),
        grid=(b, GQA_KV_HEADS),
        in_specs=[
            pl.BlockSpec((1, 2, ctx_len, 64), lambda bi, n: (bi, n, cblk, 0)),
            pl.BlockSpec((1, HEAD_DIM, ctx_len), lambda bi, n: (bi, n, cblk)),
            pl.BlockSpec((1, 1, ctx_len, 128), lambda bi, n: (bi, n, cblk, 0)),
        ],
        out_specs=pl.BlockSpec((1, ctx_len, 128), lambda bi, n: (bi, 0, n)),
        out_shape=jax.ShapeDtypeStruct((b, ctx_len, 256), BF16),
        scratch_shapes=scratch(ctx_len),
        compiler_params=_cparams(("parallel", "parallel")),
        name="gqa_context",
    )(qg, kg, vg)
    return jnp.concatenate([y_lat, y_ctx], axis=1)


def _bd(x):
    t = jnp.concatenate([x, x, x, x], axis=0)
    r = lax.broadcasted_iota(jnp.int32, (4 * CH, RW), 0) >> 6
    c = lax.broadcasted_iota(jnp.int32, (4 * CH, RW), 1) >> 6
    return jnp.where(r == c, t, jnp.zeros_like(t))


def _fold(x):
    r = lax.broadcasted_iota(jnp.int32, (4 * CH, RW), 0) >> 6
    c = lax.broadcasted_iota(jnp.int32, (4 * CH, RW), 1) >> 6
    xm = jnp.where(r == c, x, 0.0)
    return xm[0:64] + xm[64:128] + xm[128:192] + xm[192:256]


def _chunk_mats(items):
    t = lax.broadcasted_iota(jnp.int32, (CH, RW), 0)
    j = lax.broadcasted_iota(jnp.int32, (CH, RW), 1) & 63
    eye = j == t
    before = {False: j < t, True: j > t}
    incl = {False: j <= t, True: j >= t}
    bf = lambda x: x.astype(BF16)
    n_items = range(len(items))

    lb, mb, lk, mk = [], [], [], []
    for it in items:
        a2 = bf(jnp.concatenate([it["kap"], it["rho"]], axis=0))
        lm_b = _dg(a2, _bd(bf(it["bt"])), 1, 1)
        lm_k = _dg(a2, _bd(bf(it["kt"])), 1, 1)
        lb.append(jnp.where(before[it["rev"]], lm_b[:CH], 0.0))
        mb.append(jnp.where(incl[it["rev"]], lm_b[CH:], 0.0))
        lk.append(jnp.where(before[it["rev"]], lm_k[:CH], 0.0))
        mk.append(jnp.where(incl[it["rev"]], lm_k[CH:], 0.0))

    p, mpow = [], []
    for i in n_items:
        nb = bf(-lb[i])
        p.append(jnp.where(eye, 1.0, 0.0) - lb[i])
        mpow.append(_dg(nb, _bd(nb), 1, 0))
    for _ in range(4):
        for i in n_items:
            mbf = bf(mpow[i])
            pm = _dg(jnp.concatenate([bf(p[i]), mbf], axis=0), _bd(mbf), 1, 0)
            p[i] = p[i] + pm[:CH]
            mpow[i] = pm[CH:]
    tinv = [bf(p[i] + _dg(bf(p[i]), _bd(bf(mpow[i])), 1, 0)) for i in n_items]

    lmv = [_dg(bf(jnp.concatenate([lk[i], mk[i]], axis=0)), _bd(bf(items[i]["v"])), 1, 0) for i in n_items]
    kp = [_dg(tinv[i], _bd(bf(items[i]["kap"])), 1, 0) for i in n_items]
    vp = [_dg(tinv[i], _bd(bf(lmv[i][:CH])), 1, 0) for i in n_items]
    out = []
    for i in n_items:
        it = items[i]
        mbb = bf(mb[i])
        rp = it["rho"] - _dg(mbb, _bd(bf(kp[i])), 1, 0)
        yl = lmv[i][CH:] - _dg(mbb, _bd(bf(vp[i])), 1, 0)
        g = jnp.where(eye, it["gdiag"], 0.0) - _fold(_dg(bf(kp[i]), bf(it["bhat"]), 0, 0))
        hmat = _fold(_dg(bf(jnp.concatenate([it["v"], -vp[i]], axis=0)),
                         bf(jnp.concatenate([it["khat"], it["bhat"]], axis=0)), 0, 0))
        out.append((g, hmat, rp, yl))
    return out


def _rwkv_chunk_kernel(z_ref, hp_ref, hn_ref, mu_ref, w0_ref, w2_ref, a0_ref, a2_ref, kkw_ref, ka_ref, rk_ref,
                       bd_ref, g_ref, h_ref, rp_ref, yl_ref, bonus_ref, *, ct):
    i = pl.program_id(1)
    z = z_ref[0]
    zs = z[:, :RW_SHIFT]
    valid_prev = jnp.where((i != 0) & (i != ct), 1.0, 0.0)
    valid_next = jnp.where((i != ct - 1) & (i != ct), 1.0, 0.0)
    prev_row = hp_ref[0, HALO - 1:HALO, :RW_SHIFT] * valid_prev
    next_row = hn_ref[0, 0:1, :RW_SHIFT] * valid_next
    row = lax.broadcasted_iota(jnp.int32, (TM, RW_SHIFT), 0)
    tt = lax.broadcasted_iota(jnp.int32, (TM, TM), 0)
    jj = lax.broadcasted_iota(jnp.int32, (TM, TM), 1)
    same_chunk = (tt >> 6) == (jj >> 6)
    bd = bd_ref[...]
    bonus = None
    items = []
    for d in range(2):
        if d == 0:
            nb = jnp.where(row == 0, prev_row, pltpu.roll(zs, 1, 0))
        else:
            nb = jnp.where(row == TM - 1, next_row, pltpu.roll(zs, TM - 1, 0))
        zd = zs + mu_ref[d:d + 1, :] * (nb - zs)
        r = zd[:, :RW]
        k = zd[:, RW:2 * RW]
        v = zd[:, 2 * RW:3 * RW]
        lw = zd[:, 3 * RW:3 * RW + LORA_W]
        la = zd[:, 3 * RW + LORA_W:]
        w_log = w0_ref[d:d + 1, :] + jnp.dot(jnp.tanh(lw), w2_ref[d], precision=HIGHEST,
                                             preferred_element_type=F32)
        sp = jnp.maximum(-w_log, 0.0) + jnp.log(1.0 + jnp.exp(-jnp.abs(w_log)))
        logw = -jnp.exp(-sp - 0.5)
        a = _sigmoid(a0_ref[d:d + 1, :] + jnp.dot(la, a2_ref[d], precision=HIGHEST, preferred_element_type=F32))
        kkr = k * kkw_ref[...]
        ss = jnp.dot(kkr * kkr, bd, precision=HIGHEST, preferred_element_type=F32)
        kk = kkr * lax.rsqrt(jnp.maximum(ss, 1e-24))
        k2 = k * (1.0 + (a - 1.0) * ka_ref[...])
        bv = kk * a
        bon = jnp.dot(r * k2 * rk_ref[...], bd, precision=HIGHEST, preferred_element_type=F32) * v
        bonus = bon if bonus is None else bonus + bon
        tri = same_chunk & ((jj >= tt) if d == 1 else (jj <= tt))
        cum = jnp.dot(jnp.where(tri, 1.0, 0.0), logw, precision=HIGHEST, preferred_element_type=F32)
        tot = jnp.dot(jnp.where(same_chunk, 1.0, 0.0), logw, precision=HIGHEST, preferred_element_type=F32)
        e_neg = jnp.exp(-cum)
        e_tot = jnp.exp(tot - cum)
        streams = dict(kap=kk * jnp.exp(cum - logw), kt=k2 * e_neg, bt=bv * e_neg, rho=r * jnp.exp(cum),
                       khat=k2 * e_tot, bhat=bv * e_tot, v=v, gdiag=jnp.exp(tot))
        for c in range(TM // CH):
            it = {name: val[c * CH:(c + 1) * CH] for name, val in streams.items()}
            it["gdiag"] = it["gdiag"][0:1]
            it["rev"] = d == 1
            items.append(it)
    for idx, (g, hm, rp, yl) in enumerate(_chunk_mats(items)):
        d, c = divmod(idx, TM // CH)
        g_ref[0, d, c] = g
        h_ref[0, d, c] = hm
        rp_ref[0, d, c] = rp
        yl_ref[0, d, c] = yl
    bonus_ref[0] = bonus


def _rwkv_chunks(zr, mu, w0, w2, a0, a2, kkw, ka, rk, bd256, ct):
    b, tt, _ = zr.shape
    nt = tt // TM
    nch = tt // CH
    cpt = TM // CH
    nhb = tt // HALO
    mats = pl.BlockSpec((1, 2, cpt, CH, RW), lambda bi, i: (bi, 0, i, 0, 0))
    mshape = jax.ShapeDtypeStruct((b, 2, nch, CH, RW), F32)
    return pl.pallas_call(
        functools.partial(_rwkv_chunk_kernel, ct=ct),
        grid=(b, nt),
        in_specs=[
            pl.BlockSpec((1, TM, COLS_RW), lambda bi, i: (bi, i, 0)),
            pl.BlockSpec((1, HALO, COLS_RW), lambda bi, i: (bi, jnp.maximum(i * (TM // HALO) - 1, 0), 0)),
            pl.BlockSpec((1, HALO, COLS_RW), lambda bi, i: (bi, jnp.minimum((i + 1) * (TM // HALO), nhb - 1), 0)),
            _const_spec((2, RW_SHIFT)),
            _const_spec((2, RW)),
            _const_spec((2, LORA_W, RW)),
            _const_spec((2, RW)),
            _const_spec((2, LORA_A, RW)),
            _const_spec((1, RW)),
            _const_spec((1, RW)),
            _const_spec((1, RW)),
            _const_spec((RW, RW)),
        ],
        out_specs=[mats, mats, mats, mats, pl.BlockSpec((1, TM, RW), lambda bi, i: (bi, i, 0))],
        out_shape=[mshape, mshape, mshape, mshape, jax.ShapeDtypeStruct((b, tt, RW), F32)],
        compiler_params=_cparams(("parallel", "parallel")),
        name="rwkv_chunks",
    )(zr, zr, zr, mu, w0, w2, a0, a2, kkw, ka, rk, bd256)


def _rwkv_chain_kernel(gf_ref, hf_ref, rf_ref, yf_ref, gr_ref, hr_ref, rr_ref, yr_ref, of_ref, or_ref, s_ref,
                       *, nb):
    @pl.when(pl.program_id(0) == 0)
    def _():
        s_ref[...] = jnp.zeros(s_ref.shape, F32)

    for d, (g_ref, h_ref, rp_ref, yl_ref, o_ref) in enumerate(
            ((gf_ref, hf_ref, rf_ref, yf_ref, of_ref), (gr_ref, hr_ref, rr_ref, yr_ref, or_ref))):
        for bi in range(nb):
            s = s_ref[d, bi]
            sb = s.astype(BF16)
            o_ref[bi] = _dg(rp_ref[bi, 0, 0].astype(BF16), _bd(sb), 1, 1) + yl_ref[bi, 0, 0]
            s_ref[d, bi] = _dg(sb, _bd(g_ref[bi, 0, 0].astype(BF16)), 1, 0) + h_ref[bi, 0, 0]


def _rwkv_chain(g, h, rp, yl, seq, ctx_len):
    b, _, nch, _, _ = g.shape
    n_lat = seq // CH
    n_ctx = ctx_len // CH

    def cf(s):
        return jnp.where(s < n_ctx, n_lat + s, s - n_ctx)

    def cr(s):
        return nch - 1 - s

    fwd = pl.BlockSpec((b, 1, 1, CH, RW), lambda s: (0, 0, cf(s), 0, 0))
    rev = pl.BlockSpec((b, 1, 1, CH, RW), lambda s: (0, 1, cr(s), 0, 0))
    yshape = jax.ShapeDtypeStruct((b, nch * CH, RW), F32)
    return pl.pallas_call(
        functools.partial(_rwkv_chain_kernel, nb=b),
        grid=(nch,),
        in_specs=[fwd, fwd, fwd, fwd, rev, rev, rev, rev],
        out_specs=[pl.BlockSpec((b, CH, RW), lambda s: (0, cf(s), 0)),
                   pl.BlockSpec((b, CH, RW), lambda s: (0, cr(s), 0))],
        out_shape=[yshape, yshape],
        scratch_shapes=[pltpu.VMEM((2, b, CH, RW), F32)],
        compiler_params=_cparams(("arbitrary",)),
        name="rwkv_chain",
    )(g, h, rp, yl, g, h, rp, yl)


def _merge_kernel(*refs, nblk, ctx_mode):
    (x_ref, mod_ref, gpre_ref, gpost_ref, za_ref, hp_ref, hn_ref, ca_ref, yb_ref, yc_ref, yf_ref, yr_ref,
     bonus_ref, lg_ref, g2_ref, lnw_ref, lnb_ref, bd_ref, wb_ref, wg_ref, bg_ref, wo_ref) = refs[:22]
    o_ref = refs[-1]
    tm = x_ref.shape[1]
    x = x_ref[0]
    m = mod_ref[0, 0]
    hb = _rms_mod(x, gpre_ref[...], m[0:1], m[1:2]).astype(BF16)

    za = za_ref[0]
    u = za[:, 256:512] * za[:, 512:768]
    valid_prev, valid_next = _edge_valid(pl.program_id(1), nblk, ctx_mode)
    up = hp_ref[0, HALO - 1:HALO, 256:512] * hp_ref[0, HALO - 1:HALO, 512:768] * valid_prev
    un = hn_ref[0, 0:1, 256:512] * hn_ref[0, 0:1, 512:768] * valid_next
    row = lax.broadcasted_iota(jnp.int32, (tm, 256), 0)
    u_prev = jnp.where(row == 0, up, pltpu.roll(u, 1, 0))
    u_next = jnp.where(row == tm - 1, un, pltpu.roll(u, tm - 1, 0))
    ya = za[:, :256] * (u_prev * ca_ref[0:1, :] + u * ca_ref[1:2, :] + u_next * ca_ref[2:3, :])

    bd = bd_ref[...]
    y = yf_ref[0] + yr_ref[0]
    mean = jnp.dot(y, bd, precision=HIGHEST, preferred_element_type=F32) * (1.0 / HEAD_DIM)
    yc0 = y - mean
    var = jnp.dot(yc0 * yc0, bd, precision=HIGHEST, preferred_element_type=F32) * (1.0 / HEAD_DIM)
    yn = yc0 * lax.rsqrt(var + LN_X_EPS) * lnw_ref[...] + lnb_ref[...] + bonus_ref[0]
    yd = yn * jnp.dot(_sigmoid(lg_ref[0]), g2_ref[...], precision=HIGHEST, preferred_element_type=F32)

    acc = None
    for bidx, ys in enumerate((ya.astype(BF16), yb_ref[0], yc_ref[0], yd.astype(BF16))):
        sl = slice(bidx * D_MODEL, (bidx + 1) * D_MODEL)
        gate = _sigmoid(jnp.dot(hb, wg_ref[:, sl], preferred_element_type=F32) + bg_ref[:, sl])
        term = gate * jnp.dot(ys, wb_ref[bidx], preferred_element_type=F32)
        acc = term if acc is None else acc + term
    mo = jnp.dot(acc.astype(BF16), wo_ref[...], preferred_element_type=F32)
    o_ref[0] = x + m[2:3] * _rms(mo, gpost_ref[...])


def _merge(xa, modsel, g_pre, g_post, za, conv_a, yb, yc, yf, yr, bonus, zr, g2, ln_w, ln_b, bd256,
           w_branch, w_gate, b_gate, w_o, seq, tm, prev=None):
    b, tt, _ = xa.shape
    ctx_mode = prev is not None
    nblk, blk, steps, alias_specs = _token_tiling(seq, tm, ctx_mode, 1)
    tile = lambda w: pl.BlockSpec((1, tm, w), lambda bi, i: (bi, blk(i), 0))
    return pl.pallas_call(
        functools.partial(_merge_kernel, nblk=nblk, ctx_mode=ctx_mode),
        grid=(b, steps),
        in_specs=[
            tile(D_MODEL),
            pl.BlockSpec((1, 1, 6, D_MODEL), lambda bi, i: (bi, int(ctx_mode), 0, 0)),
            _const_spec((1, D_MODEL)),
            _const_spec((1, D_MODEL)),
            tile(COLS_A),
            *_halo_specs(COLS_A, tm, blk, tt // HALO),
            _const_spec((3, 256)),
            tile(256), tile(256), tile(256), tile(256), tile(256),
            pl.BlockSpec((1, tm, LORA_G), lambda bi, i: (bi, blk(i), RW_SHIFT // LORA_G)),
            _const_spec((LORA_G, RW)),
            _const_spec((1, RW)),
            _const_spec((1, RW)),
            _const_spec((RW, RW)),
            _const_spec((4, 256, D_MODEL)),
            _const_spec((D_MODEL, 4 * D_MODEL)),
            _const_spec((1, 4 * D_MODEL)),
            _const_spec((D_MODEL, D_MODEL)),
        ] + alias_specs,
        out_specs=tile(D_MODEL),
        out_shape=jax.ShapeDtypeStruct((b, tt, D_MODEL), F32),
        input_output_aliases={22: 0} if ctx_mode else {},
        compiler_params=_cparams(("parallel", "parallel")),
        name="merge_ctx" if ctx_mode else "merge",
    )(xa, modsel, g_pre, g_post, za, za, za, conv_a, yb, yc, yf, yr, bonus, zr, g2, ln_w, ln_b, bd256,
      w_branch, w_gate, b_gate, w_o, *((prev,) if ctx_mode else ()))


def _ffn_kernel(*refs, nblk, ctx_mode):
    x_ref, hp_ref, hn_ref, mod_ref, gpre_ref, gpost_ref, wu_ref, cw_ref, wd_ref = refs[:9]
    o_ref, act_ref = refs[-2:]
    tm = x_ref.shape[1]
    x = x_ref[0]
    m = mod_ref[0, 0]
    xx = jnp.concatenate([hp_ref[0], x, hn_ref[0]], axis=0)
    nrow = tm + 2 * HALO
    row = lax.broadcasted_iota(jnp.int32, (nrow, 1), 0)
    valid_prev, valid_next = _edge_valid(pl.program_id(1), nblk, ctx_mode)
    rowmask = jnp.where(row < HALO, valid_prev, jnp.where(row >= tm + HALO, valid_next, 1.0))
    hb = (_rms_mod(xx, gpre_ref[...], m[3:4], m[4:5]) * rowmask).astype(BF16)

    def conv(u, col):
        w = cw_ref[:, col:col + FF_CHUNK]
        c = (pltpu.roll(u, 1, 0) * w[0:1] + u * w[1:2] + pltpu.roll(u, nrow - 1, 0) * w[2:3])
        return c[HALO:HALO + tm]

    for j in range(D_FF // FF_CHUNK):
        ca = conv(jnp.dot(hb, wu_ref[:, j * FF_CHUNK:(j + 1) * FF_CHUNK], preferred_element_type=F32),
                  j * FF_CHUNK)
        cg = conv(jnp.dot(hb, wu_ref[:, D_FF + j * FF_CHUNK:D_FF + (j + 1) * FF_CHUNK],
                          preferred_element_type=F32), D_FF + j * FF_CHUNK)
        act_ref[:, j * FF_CHUNK:(j + 1) * FF_CHUNK] = (ca * (cg * _sigmoid(cg))).astype(BF16)
    f = jnp.dot(act_ref[...], wd_ref[...], preferred_element_type=F32)
    o_ref[0] = x + m[5:6] * _rms(f, gpost_ref[...])


def _ffn(xa, modsel, g_pre, g_post, w_up, conv_w, w_down, seq, tm, prev=None):
    b, tt, _ = xa.shape
    ctx_mode = prev is not None
    nblk, blk, steps, alias_specs = _token_tiling(seq, tm, ctx_mode, 1)
    return pl.pallas_call(
        functools.partial(_ffn_kernel, nblk=nblk, ctx_mode=ctx_mode),
        grid=(b, steps),
        in_specs=[
            pl.BlockSpec((1, tm, D_MODEL), lambda bi, i: (bi, blk(i), 0)),
            *_halo_specs(D_MODEL, tm, blk, tt // HALO),
            pl.BlockSpec((1, 1, 6, D_MODEL), lambda bi, i: (bi, int(ctx_mode), 0, 0)),
            _const_spec((1, D_MODEL)),
            _const_spec((1, D_MODEL)),
            _const_spec((D_MODEL, 2 * D_FF)),
            _const_spec((3, 2 * D_FF)),
            _const_spec((D_FF, D_MODEL)),
        ] + alias_specs,
        out_specs=pl.BlockSpec((1, tm, D_MODEL), lambda bi, i: (bi, blk(i), 0)),
        out_shape=jax.ShapeDtypeStruct((b, tt, D_MODEL), F32),
        scratch_shapes=[pltpu.VMEM((tm, D_FF), BF16)],
        input_output_aliases={9: 0} if ctx_mode else {},
        compiler_params=_cparams(("parallel", "parallel")),
        name="conv_ffn_ctx" if ctx_mode else "conv_ffn",
    )(xa, xa, xa, modsel, g_pre, g_post, w_up, conv_w, w_down, *((prev,) if ctx_mode else ()))


def _rope_tables(seq, ctx_len):
    t = np.arange(seq)
    row = (t // GRID_W).astype(np.float32)
    col = (t % GRID_W).astype(np.float32)
    n_freq = HEAD_DIM // 4
    inv_freq = jnp.asarray(ROPE_THETA, F32) ** (-jnp.arange(n_freq, dtype=F32) / n_freq)
    ang = jnp.concatenate([jnp.asarray(row)[:, None] * inv_freq, jnp.asarray(col)[:, None] * inv_freq], axis=-1)
    cos = jnp.repeat(jnp.cos(ang), 2, axis=-1)
    sin = jnp.repeat(jnp.sin(ang), 2, axis=-1) * jnp.tile(jnp.asarray([-1.0, 1.0], F32), HEAD_DIM // 2)
    cos = jnp.concatenate([cos, jnp.ones((ctx_len, HEAD_DIM), F32)], axis=0)
    sin = jnp.concatenate([sin, jnp.zeros((ctx_len, HEAD_DIM), F32)], axis=0)
    nh = GQA_Q_HEADS + GQA_KV_HEADS
    return jnp.tile(cos, (1, nh)), jnp.tile(sin, (1, nh))


def _block_ones(n, scale):
    idx = np.arange(n) // HEAD_DIM
    return jnp.asarray((idx[:, None] == idx[None, :]).astype(np.float32) * scale)


def kernel(x, c, ctx, c_ctx, ada_w, ada_b, norm_mix_pre, norm_mix_post, norm_ffn_pre, norm_ffn_post, w_in, conv_a, na_bias, q_norm, k_norm, rw_mu, rw_w0, rw_w2, rw_a0, rw_a2, rw_kk, rw_ka, rw_rk, rw_g2, rw_ln_w, rw_ln_b, w_branch, w_gate, b_gate, w_o, ffn_up, ffn_conv, ffn_down):
    b, seq, _ = x.shape
    ctx_len = ctx.shape[1]
    depth = ada_w.shape[0]
    assert ctx_len == TM and seq % GQA_TQ == 0 and seq % TML == 0 and seq // TM >= 3 and b + 1 <= 8
    ct = seq // TM
    rows = seq // GRID_W

    cvec = jnp.zeros((8, D_MODEL), F32).at[:b].set(c).at[b].set(c_ctx)
    mods = _ada(cvec, ada_w, ada_b)
    cos_t, sin_t = _rope_tables(seq, ctx_len)
    bd384 = _block_ones(384, 1.0 / HEAD_DIM)
    bd256 = _block_ones(256, 1.0)

    xa = jnp.concatenate([x, ctx], axis=1)
    for l in range(depth):
        ml = mods[l].reshape(8, 6, D_MODEL)
        modsel = jnp.stack([ml[:b], jnp.broadcast_to(ml[b][None], (b, 6, D_MODEL))], axis=1)
        row2 = lambda a: a.reshape(1, -1)
        gqk = jnp.concatenate([jnp.tile(q_norm[l], GQA_Q_HEADS) * HEAD_DIM ** -0.5,
                               jnp.tile(k_norm[l], GQA_KV_HEADS)]).reshape(1, -1)
        ip_args = (xa, modsel, row2(norm_mix_pre[l]), w_in[l].astype(BF16), cos_t, sin_t, gqk, bd384, seq)
        za, qn, kn, vn, qg, kg, vg, zr = _inproj(*ip_args, ctx_len, prev=_inproj(*ip_args, TML))
        yb = _na(qn, kn, vn, _na_bias_table(na_bias[l], rows), ct)
        yc = _gqa(qg, kg, vg, seq, ctx_len)
        g, h, rp, yl, bonus = _rwkv_chunks(zr, rw_mu[l], rw_w0[l], rw_w2[l], rw_a0[l], rw_a2[l],
                                           row2(rw_kk[l]), row2(rw_ka[l]), row2(rw_rk[l]), bd256, ct)
        yf, yr = _rwkv_chain(g, h, rp, yl, seq, ctx_len)
        mg_args = (xa, modsel, row2(norm_mix_pre[l]), row2(norm_mix_post[l]), za, conv_a[l], yb, yc, yf, yr,
                   bonus, zr, rw_g2[l], row2(rw_ln_w[l]), row2(rw_ln_b[l]), bd256,
                   w_branch[l].astype(BF16), w_gate[l].astype(BF16), row2(b_gate[l]), w_o[l].astype(BF16), seq)
        xm = _merge(*mg_args, ctx_len, prev=_merge(*mg_args, TML))
        ff_args = (xm, modsel, row2(norm_ffn_pre[l]), row2(norm_ffn_post[l]), ffn_up[l].astype(BF16),
                   ffn_conv[l], ffn_down[l].astype(BF16), seq)
        xa = _ffn(*ff_args, ctx_len, prev=_ffn(*ff_args, TML))
    return xa[:, :seq]
```

```python
import functools

import numpy as np
import jax
import jax.numpy as jnp
from jax import lax
from jax.experimental import pallas as pl
from jax.experimental.pallas import tpu as pltpu

F32 = jnp.float32
BF16 = jnp.bfloat16
HIGHEST = lax.Precision.HIGHEST

D_MODEL = 1024
GRID_W = 64
HEAD_DIM = 64
NA_HEADS = 4
NA_WIN_R = 8
NA_WIN_C = 16
GQA_Q_HEADS = 4
GQA_KV_HEADS = 2
ROPE_THETA = 10000.0
RWKV_HEADS = 4
RW = RWKV_HEADS * HEAD_DIM
LORA_W = 64
LORA_A = 64
LORA_G = 128
RW_SHIFT = 3 * RW + LORA_W + LORA_A
D_FF = 2816
NORM_EPS = 1e-6
LN_X_EPS = 64e-5
COLS_A = 768
COLS_NA = 768
COLS_GQA = 512
COLS_RW = 1024
D_IN = COLS_A + COLS_NA + COLS_GQA + COLS_RW

TM = 256
TML = 512
CH = 64
HALO = 8
FF_CHUNK = 256
GQA_TQ = 512
GQA_TK = 512
NEG = -1e30
VMEM_LIMIT = 56 * 1024 * 1024


def _cparams(sem):
    return pltpu.CompilerParams(dimension_semantics=sem, vmem_limit_bytes=VMEM_LIMIT)


def _const_spec(shape):
    nd = len(shape)
    return pl.BlockSpec(shape, lambda *_: (0,) * nd, pipeline_mode=pl.Buffered(1))


def _token_tiling(seq, tm, ctx_mode, n_out):
    nblk = seq // tm
    blk = (lambda i: i * 0 + nblk) if ctx_mode else (lambda i: i)
    alias_specs = [pl.BlockSpec(memory_space=pl.ANY)] * n_out if ctx_mode else []
    return nblk, blk, (1 if ctx_mode else nblk), alias_specs


def _halo_specs(width, tm, blk, nhb):
    per = tm // HALO
    return [pl.BlockSpec((1, HALO, width), lambda bi, i: (bi, jnp.maximum(blk(i) * per - 1, 0), 0)),
            pl.BlockSpec((1, HALO, width), lambda bi, i: (bi, jnp.minimum((blk(i) + 1) * per, nhb - 1), 0))]


def _edge_valid(i, nblk, ctx_mode):
    if ctx_mode:
        return 0.0, 0.0
    return jnp.where(i != 0, 1.0, 0.0), jnp.where(i != nblk - 1, 1.0, 0.0)


def _dg(a, b, ca, cb, **kw):
    return lax.dot_general(a, b, (((ca,), (cb,)), ((), ())), preferred_element_type=F32, **kw)


def _sum_dot(x, w, terms):
    acc = None
    for _ in range(terms):
        piece = x.astype(BF16)
        part = jnp.dot(piece, w, preferred_element_type=F32)
        acc = part if acc is None else acc + part
        x = x - piece.astype(F32)
    return acc


def _dot3(x, w_hi, w_lo):
    x_hi = x.astype(BF16)
    x_lo = (x - x_hi.astype(F32)).astype(BF16)
    return (jnp.dot(x_hi, w_hi, preferred_element_type=F32) + jnp.dot(x_lo, w_hi, preferred_element_type=F32)
            + jnp.dot(x_hi, w_lo, preferred_element_type=F32))


def _rms_mod(x, g, shift, scale):
    y = x * lax.rsqrt(jnp.mean(x * x, axis=-1, keepdims=True) + NORM_EPS) * g
    return y * (1.0 + scale) + shift


def _rms(x, g):
    return x * lax.rsqrt(jnp.mean(x * x, axis=-1, keepdims=True) + NORM_EPS) * g


def _sigmoid(x):
    return 0.5 * jnp.tanh(0.5 * x) + 0.5


def _ada_kernel(c_ref, w_ref, b_ref, o_ref):
    c = c_ref[...]
    s = c * _sigmoid(c)
    o_ref[0] = jnp.dot(s, w_ref[0], precision=HIGHEST, preferred_element_type=F32) + b_ref[0]


def _ada(cvec, ada_w, ada_b):
    depth = ada_w.shape[0]
    nblk = ada_w.shape[2] // D_MODEL
    return pl.pallas_call(
        _ada_kernel,
        grid=(depth, nblk),
        in_specs=[
            pl.BlockSpec((8, D_MODEL), lambda l, j: (0, 0)),
            pl.BlockSpec((1, D_MODEL, D_MODEL), lambda l, j: (l, 0, j)),
            pl.BlockSpec((1, 1, D_MODEL), lambda l, j: (l, 0, j)),
        ],
        out_specs=pl.BlockSpec((1, 8, D_MODEL), lambda l, j: (l, 0, j)),
        out_shape=jax.ShapeDtypeStruct((depth, 8, ada_w.shape[2]), F32),
        compiler_params=_cparams(("parallel", "parallel")),
        name="ada_mod",
    )(cvec, ada_w, ada_b.reshape(depth, 1, -1))


def _inproj_kernel(*refs):
    x_ref, mod_ref, g_ref, w_ref, cos_ref, sin_ref, gqk_ref, bd_ref = refs[:8]
    za_ref, qn_ref, kn_ref, vn_ref, qg_ref, kg_ref, vg_ref, zr_ref = refs[-8:]
    tm = x_ref.shape[1]
    x = x_ref[0]
    m = mod_ref[0, 0]
    h = _rms_mod(x, g_ref[...], m[0:1], m[1:2])
    z = jnp.dot(h.astype(BF16), w_ref[...], preferred_element_type=F32)
    za_ref[0] = z[:, :COLS_A]
    zr_ref[0] = z[:, COLS_A + COLS_NA + COLS_GQA:]
    na = z[:, COLS_A:COLS_A + COLS_NA]
    scale = HEAD_DIM ** -0.5
    for hd in range(NA_HEADS):
        qn_ref[0, hd] = (na[:, hd * 64:(hd + 1) * 64] * scale).astype(BF16)
        kn_ref[0, hd] = na[:, 256 + hd * 64:256 + (hd + 1) * 64].astype(BF16)
        vn_ref[0, hd] = na[:, 512 + hd * 64:512 + (hd + 1) * 64].astype(BF16)
    g = z[:, COLS_A + COLS_NA:COLS_A + COLS_NA + COLS_GQA]
    qk = g[:, :384]
    ms = _sum_dot(qk * qk, bd_ref[...], 2)
    qkn = qk * lax.rsqrt(ms + NORM_EPS) * gqk_ref[...]
    lane = lax.broadcasted_iota(jnp.int32, (tm, 128), 1)
    even = (lane & 1) == 0
    parts = []
    for j in range(3):
        s = qkn[:, j * 128:(j + 1) * 128]
        sw = jnp.where(even, pltpu.roll(s, 127, 1), pltpu.roll(s, 1, 1))
        parts.append(s * cos_ref[:, j * 128:(j + 1) * 128] + sw * sin_ref[:, j * 128:(j + 1) * 128])
    for hd in range(GQA_Q_HEADS):
        p = parts[hd // 2]
        qg_ref[0, hd] = p[:, (hd % 2) * 64:(hd % 2) * 64 + 64].astype(BF16)
    kg_ref[0] = parts[2].T.astype(BF16)
    vt = g[:, 384:512]
    for hd in range(GQA_KV_HEADS):
        vh = vt if hd == 0 else pltpu.roll(vt, 64, 1)
        vg_ref[0, hd] = jnp.where(lane < 64, vh, jnp.where(lane == 64, 1.0, 0.0)).astype(BF16)


def _inproj(xa, modsel, g_pre, w_in, cos_t, sin_t, gqk, bd384, seq, tm, prev=None):
    b, tt, _ = xa.shape
    ctx_mode = prev is not None
    nblk, blk, steps, alias_specs = _token_tiling(seq, tm, ctx_mode, 8)
    tile = lambda w: pl.BlockSpec((1, tm, w), lambda bi, i: (bi, blk(i), 0))
    heads = lambda nh: pl.BlockSpec((1, nh, tm, 64), lambda bi, i: (bi, 0, blk(i), 0))
    hs = lambda nh: jax.ShapeDtypeStruct((b, nh, tt, 64), BF16)
    return pl.pallas_call(
        _inproj_kernel,
        grid=(b, steps),
        in_specs=[
            tile(D_MODEL),
            pl.BlockSpec((1, 1, 6, D_MODEL), lambda bi, i: (bi, int(ctx_mode), 0, 0)),
            _const_spec((1, D_MODEL)),
            _const_spec((D_MODEL, D_IN)),
            pl.BlockSpec((tm, 384), lambda bi, i: (blk(i), 0)),
            pl.BlockSpec((tm, 384), lambda bi, i: (blk(i), 0)),
            _const_spec((1, 384)),
            _const_spec((384, 384)),
        ] + alias_specs,
        out_specs=[tile(COLS_A), heads(4), heads(4), heads(4), heads(4),
                   pl.BlockSpec((1, 2 * HEAD_DIM, tm), lambda bi, i: (bi, 0, blk(i))),
                   pl.BlockSpec((1, 2, tm, 128), lambda bi, i: (bi, 0, blk(i), 0)), tile(COLS_RW)],
        out_shape=[jax.ShapeDtypeStruct((b, tt, COLS_A), F32), hs(4), hs(4), hs(4), hs(4),
                   jax.ShapeDtypeStruct((b, 2 * HEAD_DIM, tt), BF16),
                   jax.ShapeDtypeStruct((b, 2, tt, 128), BF16),
                   jax.ShapeDtypeStruct((b, tt, COLS_RW), F32)],
        input_output_aliases={8 + k: k for k in range(8)} if ctx_mode else {},
        compiler_params=_cparams(("parallel", "parallel")),
        name="inproj_ctx" if ctx_mode else "inproj",
    )(xa, modsel, g_pre, w_in, cos_t, sin_t, gqk, bd384, *(prev or ()))


def _na_kernel(q_ref, k0_ref, k1_ref, k2_ref, kc_ref, v0_ref, v1_ref, v2_ref, vc_ref, bias_ref, o_ref):
    outs = []
    for hd in range(NA_HEADS):
        kcat = jnp.concatenate([k0_ref[0, hd], k1_ref[0, hd], k2_ref[0, hd], kc_ref[0, hd]], axis=0)
        s = _dg(q_ref[0, hd], kcat, 1, 1)
        s_loc = s[:, :3 * TM] + bias_ref[0, hd]
        s_ctx = s[:, 3 * TM:]
        m = jnp.maximum(jnp.max(s_loc, axis=-1, keepdims=True), jnp.max(s_ctx, axis=-1, keepdims=True))
        p_loc = jnp.exp(s_loc - m)
        p_ctx = jnp.exp(s_ctx - m)
        l = jnp.sum(p_loc, axis=-1, keepdims=True) + jnp.sum(p_ctx, axis=-1, keepdims=True)
        vloc = jnp.concatenate([v0_ref[0, hd], v1_ref[0, hd], v2_ref[0, hd]], axis=0)
        o = _dg(p_loc.astype(BF16), vloc, 1, 0) + _dg(p_ctx.astype(BF16), vc_ref[0, hd], 1, 0)
        outs.append(o / l)
    o_ref[0] = jnp.concatenate(outs, axis=-1).astype(o_ref.dtype)


def _na(qn, kn, vn, bias_tab, ct):
    b, _, tt, _ = qn.shape
    nt = tt // TM

    def kv_spec(j):
        if j is None:
            return pl.BlockSpec((1, 4, TM, 64), lambda bi, i: (bi, 0, ct, 0))
        return pl.BlockSpec((1, 4, TM, 64), lambda bi, i: (bi, 0, jnp.clip(i - 1, 0, ct - 3) + j, 0))

    def pattern(i):
        return jnp.where(i == ct, 3, jnp.where(i == 0, 0, jnp.where(i == ct - 1, 2, 1)))

    return pl.pallas_call(
        _na_kernel,
        grid=(b, nt),
        in_specs=[
            pl.BlockSpec((1, 4, TM, 64), lambda bi, i: (bi, 0, i, 0)),
            kv_spec(0), kv_spec(1), kv_spec(2), kv_spec(None),
            kv_spec(0), kv_spec(1), kv_spec(2), kv_spec(None),
            pl.BlockSpec((1, 4, TM, 3 * TM), lambda bi, i: (pattern(i), 0, 0, 0)),
        ],
        out_specs=pl.BlockSpec((1, TM, 256), lambda bi, i: (bi, i, 0)),
        out_shape=jax.ShapeDtypeStruct((b, tt, 256), BF16),
        compiler_params=_cparams(("parallel", "parallel")),
        name="na_attn",
    )(qn, kn, kn, kn, kn, vn, vn, vn, vn, bias_tab)


def _na_bias_table(na_bias_l, rows):
    ct = rows * GRID_W // TM
    rpt = TM // GRID_W
    wr = min(NA_WIN_R, rows)
    qj = np.arange(GRID_W)
    kc = np.arange(GRID_W)
    cs = np.clip(qj - NA_WIN_C // 2, 0, GRID_W - NA_WIN_C)
    colvalid = (kc[None, :] >= cs[:, None]) & (kc[None, :] < cs[:, None] + NA_WIN_C)
    dc = kc[None, :] - qj[:, None] + (NA_WIN_C - 1)
    onehot = (dc.reshape(1, -1) == np.arange(2 * NA_WIN_C - 1)[:, None]) & colvalid.reshape(1, -1)
    toep = jnp.einsum("hrd,dx->hrx", na_bias_l, jnp.asarray(onehot.astype(np.float32)), precision=HIGHEST)
    toep = jnp.where(jnp.asarray(colvalid.reshape(-1)), toep, NEG)
    toep = toep.reshape(NA_HEADS, 2 * NA_WIN_R - 1, GRID_W, GRID_W)
    neg_blk = jnp.full((NA_HEADS, GRID_W, GRID_W), NEG, F32)
    tabs = []
    for tile_i in (0, 1, ct - 1):
        i0 = tile_i * rpt
        kb = int(np.clip(tile_i - 1, 0, ct - 3)) * rpt
        qrows = []
        for ri in range(rpt):
            qi = i0 + ri
            rs = int(np.clip(qi - wr // 2, 0, rows - wr))
            blks = [toep[:, kb + m - qi + NA_WIN_R - 1] if rs <= kb + m < rs + wr else neg_blk
                    for m in range(3 * rpt)]
            qrows.append(jnp.concatenate(blks, axis=-1))
        tabs.append(jnp.concatenate(qrows, axis=-2))
    tabs.append(jnp.full_like(tabs[0], NEG))
    return jnp.stack(tabs, axis=0)


def _gqa_kernel(q_ref, kt_ref, v_ref, o_ref, m_ref, acc_ref, *, n_full, tail, tq):
    q2 = jnp.concatenate([q_ref[0, 0], q_ref[0, 1]], axis=0)
    m_ref[...] = jnp.full(m_ref.shape, NEG, F32)
    acc_ref[...] = jnp.zeros(acc_ref.shape, F32)

    def chunk(start, size):
        kt = kt_ref[0, :, pl.ds(start, size)]
        vc = v_ref[0, 0, pl.ds(start, size), :]
        s = jnp.dot(q2, kt, preferred_element_type=F32)
        nlt = size // 128
        mt = s[:, :128]
        for j in range(1, nlt):
            mt = jnp.maximum(mt, s[:, j * 128:(j + 1) * 128])
        m_old = m_ref[...]
        m_new = jnp.maximum(m_old, jnp.max(mt, axis=-1, keepdims=True))
        p = jnp.concatenate([jnp.exp(s[:, j * 128:(j + 1) * 128] - m_new) for j in range(nlt)], axis=-1)
        acc_ref[...] = jnp.exp(m_old - m_new) * acc_ref[...] + jnp.dot(p.astype(BF16), vc,
                                                                       preferred_element_type=F32)
        m_ref[...] = m_new

    if n_full > 0:
        def body(j, carry):
            chunk(pl.multiple_of(j * GQA_TK, GQA_TK), GQA_TK)
            return carry
        lax.fori_loop(0, n_full, body, 0, unroll=4)
    if tail > 0:
        chunk(n_full * GQA_TK, tail)
    acc = acc_ref[...]
    o = acc[:, :HEAD_DIM] * (1.0 / acc[:, HEAD_DIM:HEAD_DIM + 1])
    o_ref[0] = jnp.concatenate([o[:tq], o[tq:]], axis=-1).astype(o_ref.dtype)


def _gqa(qg, kg, vg, seq, ctx_len):
    b, _, tt, _ = qg.shape
    scratch = lambda tq: [pltpu.VMEM((2 * tq, 128), F32), pltpu.VMEM((2 * tq, 128), F32)]
    y_lat = pl.pallas_call(
        functools.partial(_gqa_kernel, n_full=seq // GQA_TK, tail=ctx_len, tq=GQA_TQ),
        grid=(b, GQA_KV_HEADS, seq // GQA_TQ),
        in_specs=[
            pl.BlockSpec((1, 2, GQA_TQ, 64), lambda bi, n, i: (bi, n, i, 0)),
            pl.BlockSpec((1, HEAD_DIM, tt), lambda bi, n, i: (bi, n, 0)),
            pl.BlockSpec((1, 1, tt, 128), lambda bi, n, i: (bi, n, 0, 0)),
        ],
        out_specs=pl.BlockSpec((1, GQA_TQ, 128), lambda bi, n, i: (bi, i, n)),
        out_shape=jax.ShapeDtypeStruct((b, seq, 256), BF16),
        scratch_shapes=scratch(GQA_TQ),
        compiler_params=_cparams(("parallel", "parallel", "parallel")),
        name="gqa_latent",
    )(qg, kg, vg)
    cblk = seq // ctx_len
    y_ctx = pl.pallas_call(
        functools.partial(_gqa_kernel, n_full=0, tail=ctx_len, tq=ctx_len),
        grid=(b, GQA_KV_HEADS),
        in_specs=[
            pl.BlockSpec((1, 2, ctx_len, 64), lambda bi, n: (bi, n, cblk, 0)),
            pl.BlockSpec((1, HEAD_DIM, ctx_len), lambda bi, n: (bi, n, cblk)),
            pl.BlockSpec((1, 1, ctx_len, 128), lambda bi, n: (bi, n, cblk, 0)),
        ],
        out_specs=pl.BlockSpec((1, ctx_len, 128), lambda bi, n: (bi, 0, n)),
        out_shape=jax.ShapeDtypeStruct((b, ctx_len, 256), BF16),
        scratch_shapes=scratch(ctx_len),
        compiler_params=_cparams(("parallel", "parallel")),
        name="gqa_context",
    )(qg, kg, vg)
    return jnp.concatenate([y_lat, y_ctx], axis=1)


def _bd(x):
    left = lax.broadcasted_iota(jnp.int32, (CH, 128), 1) < HEAD_DIM
    x0, x1 = x[:, :128], x[:, 128:]
    z = jnp.zeros_like(x0)
    keep_l = lambda a: jnp.where(left, a, z)
    keep_r = lambda a: jnp.where(left, z, a)
    return jnp.concatenate([jnp.concatenate([keep_l(x0), z], axis=1), jnp.concatenate([keep_r(x0), z], axis=1),
                            jnp.concatenate([z, keep_l(x1)], axis=1), jnp.concatenate([z, keep_r(x1)], axis=1)],
                           axis=0)


def _fold(x):
    left = lax.broadcasted_iota(jnp.int32, (CH, 128), 1) < HEAD_DIM
    return jnp.concatenate([jnp.where(left, x[0:64, :128], x[64:128, :128]),
                            jnp.where(left, x[128:192, 128:], x[192:256, 128:])], axis=1)


def _chunk_mats(items):
    t = lax.broadcasted_iota(jnp.int32, (CH, RW), 0)
    j = lax.broadcasted_iota(jnp.int32, (CH, RW), 1) & 63
    eye = j == t
    before = {False: j < t, True: j > t}
    incl = {False: j <= t, True: j >= t}
    bf = lambda x: x.astype(BF16)
    n_items = range(len(items))

    lb, mb, lk, mk = [], [], [], []
    for it in items:
        a2 = bf(jnp.concatenate([it["kap"], it["rho"]], axis=0))
        lm_b = _dg(a2, _bd(bf(it["bt"])), 1, 1)
        lm_k = _dg(a2, _bd(bf(it["kt"])), 1, 1)
        lb.append(jnp.where(before[it["rev"]], lm_b[:CH], 0.0))
        mb.append(jnp.where(incl[it["rev"]], lm_b[CH:], 0.0))
        lk.append(jnp.where(before[it["rev"]], lm_k[:CH], 0.0))
        mk.append(jnp.where(incl[it["rev"]], lm_k[CH:], 0.0))

    p, mpow = [], []
    for i in n_items:
        nb = bf(-lb[i])
        p.append(jnp.where(eye, 1.0, 0.0) - lb[i])
        mpow.append(_dg(nb, _bd(nb), 1, 0))
    for _ in range(4):
        for i in n_items:
            mbf = bf(mpow[i])
            pm = _dg(jnp.concatenate([bf(p[i]), mbf], axis=0), _bd(mbf), 1, 0)
            p[i] = p[i] + pm[:CH]
            mpow[i] = pm[CH:]
    tinv = [bf(p[i] + _dg(bf(p[i]), _bd(bf(mpow[i])), 1, 0)) for i in n_items]

    lmv = [_dg(bf(jnp.concatenate([lk[i], mk[i]], axis=0)), _bd(bf(items[i]["v"])), 1, 0) for i in n_items]
    kp = [_dg(tinv[i], _bd(bf(items[i]["kap"])), 1, 0) for i in n_items]
    vp = [_dg(tinv[i], _bd(bf(lmv[i][:CH])), 1, 0) for i in n_items]
    out = []
    for i in n_items:
        it = items[i]
        mbb = bf(mb[i])
        rp = it["rho"] - _dg(mbb, _bd(bf(kp[i])), 1, 0)
        yl = lmv[i][CH:] - _dg(mbb, _bd(bf(vp[i])), 1, 0)
        g = jnp.where(eye, it["gdiag"], 0.0) - _fold(_dg(bf(kp[i]), bf(it["bhat"]), 0, 0))
        hmat = _fold(_dg(bf(jnp.concatenate([it["v"], -vp[i]], axis=0)),
                         bf(jnp.concatenate([it["khat"], it["bhat"]], axis=0)), 0, 0))
        out.append((g, hmat, rp, yl))
    return out


def _rwkv_chunk_kernel(z_ref, hp_ref, hn_ref, mu_ref, w0_ref, w2_ref, a0_ref, a2_ref, kkw_ref, ka_ref, rk_ref,
                       bd_ref, g_ref, h_ref, rp_ref, yl_ref, bonus_ref, *, ct):
    i = pl.program_id(1)
    z = z_ref[0]
    zs = z[:, :RW_SHIFT]
    valid_prev = jnp.where((i != 0) & (i != ct), 1.0, 0.0)
    valid_next = jnp.where((i != ct - 1) & (i != ct), 1.0, 0.0)
    prev_row = hp_ref[0, HALO - 1:HALO, :RW_SHIFT] * valid_prev
    next_row = hn_ref[0, 0:1, :RW_SHIFT] * valid_next
    row = lax.broadcasted_iota(jnp.int32, (TM, RW_SHIFT), 0)
    tt = lax.broadcasted_iota(jnp.int32, (TM, TM), 0)
    jj = lax.broadcasted_iota(jnp.int32, (TM, TM), 1)
    same_chunk = (tt >> 6) == (jj >> 6)
    bd = bd_ref[...]
    bonus = None
    items = []
    for d in range(2):
        if d == 0:
            nb = jnp.where(row == 0, prev_row, pltpu.roll(zs, 1, 0))
        else:
            nb = jnp.where(row == TM - 1, next_row, pltpu.roll(zs, TM - 1, 0))
        zd = zs + mu_ref[d:d + 1, :] * (nb - zs)
        r = zd[:, :RW]
        k = zd[:, RW:2 * RW]
        v = zd[:, 2 * RW:3 * RW]
        lw = zd[:, 3 * RW:3 * RW + LORA_W]
        la = zd[:, 3 * RW + LORA_W:]
        w_log = w0_ref[d:d + 1, :] + _dot3(jnp.tanh(lw), w2_ref[d, 0], w2_ref[d, 1])
        sp = jnp.maximum(-w_log, 0.0) + jnp.log(1.0 + jnp.exp(-jnp.abs(w_log)))
        logw = -jnp.exp(-sp - 0.5)
        a = _sigmoid(a0_ref[d:d + 1, :] + _dot3(la, a2_ref[d, 0], a2_ref[d, 1]))
        kkr = k * kkw_ref[...]
        ss = _sum_dot(kkr * kkr, bd, 2)
        kk = kkr * lax.rsqrt(jnp.maximum(ss, 1e-24))
        k2 = k * (1.0 + (a - 1.0) * ka_ref[...])
        bv = kk * a
        bon = _sum_dot(r * k2 * rk_ref[...], bd, 2) * v
        bonus = bon if bonus is None else bonus + bon
        tri = jnp.where(same_chunk & ((jj >= tt) if d == 1 else (jj <= tt)), 1.0, 0.0).astype(BF16)
        cum, rest = None, logw
        for _ in range(3):
            piece = rest.astype(BF16)
            part = jnp.dot(tri, piece, preferred_element_type=F32)
            cum = part if cum is None else cum + part
            rest = rest - piece.astype(F32)
        e_neg = jnp.exp(-cum)
        streams = dict(kap=kk * jnp.exp(cum - logw), kt=k2 * e_neg, bt=bv * e_neg, rho=r * jnp.exp(cum), v=v)
        for c in range(TM // CH):
            sl = slice(c * CH, (c + 1) * CH)
            it = {name: val[sl] for name, val in streams.items()}
            last = c * CH if d == 1 else (c + 1) * CH - 1
            tot = cum[last:last + 1]
            e_tot = jnp.exp(tot - cum[sl])
            it.update(khat=k2[sl] * e_tot, bhat=bv[sl] * e_tot, gdiag=jnp.exp(tot), rev=d == 1)
            items.append(it)
    for idx, (g, hm, rp, yl) in enumerate(_chunk_mats(items)):
        d, c = divmod(idx, TM // CH)
        g_ref[0, d, c] = g.astype(BF16)
        h_ref[0, d, c] = hm
        rp_ref[0, d, c] = rp.astype(BF16)
        yl_ref[0, d, c] = yl
    bonus_ref[0] = bonus


def _rwkv_chunks(zr, mu, w0, w2, a0, a2, kkw, ka, rk, bd256, ct):
    b, tt, _ = zr.shape
    nt = tt // TM
    nch = tt // CH
    cpt = TM // CH
    nhb = tt // HALO
    mats = pl.BlockSpec((1, 2, cpt, CH, RW), lambda bi, i: (bi, 0, i, 0, 0))
    mshape = lambda dt: jax.ShapeDtypeStruct((b, 2, nch, CH, RW), dt)
    return pl.pallas_call(
        functools.partial(_rwkv_chunk_kernel, ct=ct),
        grid=(b, nt),
        in_specs=[
            pl.BlockSpec((1, TM, COLS_RW), lambda bi, i: (bi, i, 0)),
            pl.BlockSpec((1, HALO, COLS_RW), lambda bi, i: (bi, jnp.maximum(i * (TM // HALO) - 1, 0), 0)),
            pl.BlockSpec((1, HALO, COLS_RW), lambda bi, i: (bi, jnp.minimum((i + 1) * (TM // HALO), nhb - 1), 0)),
            _const_spec((2, RW_SHIFT)),
            _const_spec((2, RW)),
            _const_spec((2, 2, LORA_W, RW)),
            _const_spec((2, RW)),
            _const_spec((2, 2, LORA_A, RW)),
            _const_spec((1, RW)),
            _const_spec((1, RW)),
            _const_spec((1, RW)),
            _const_spec((RW, RW)),
        ],
        out_specs=[mats, mats, mats, mats, pl.BlockSpec((1, TM, RW), lambda bi, i: (bi, i, 0))],
        out_shape=[mshape(BF16), mshape(F32), mshape(BF16), mshape(F32), jax.ShapeDtypeStruct((b, tt, RW), F32)],
        compiler_params=_cparams(("parallel", "parallel")),
        name="rwkv_chunks",
    )(zr, zr, zr, mu, w0, w2, a0, a2, kkw, ka, rk, bd256)


def _rwkv_chain_kernel(gf_ref, hf_ref, rf_ref, yf_ref, gr_ref, hr_ref, rr_ref, yr_ref, of_ref, or_ref, s_ref,
                       *, nb):
    @pl.when(pl.program_id(0) == 0)
    def _():
        s_ref[...] = jnp.zeros(s_ref.shape, F32)

    for d, (g_ref, h_ref, rp_ref, yl_ref, o_ref) in enumerate(
            ((gf_ref, hf_ref, rf_ref, yf_ref, of_ref), (gr_ref, hr_ref, rr_ref, yr_ref, or_ref))):
        for bi in range(nb):
            s = s_ref[d, bi]
            sb = s.astype(BF16)
            o_ref[bi] = _dg(rp_ref[bi, 0, 0], _bd(sb), 1, 1) + yl_ref[bi, 0, 0]
            s_ref[d, bi] = _dg(sb, _bd(g_ref[bi, 0, 0]), 1, 0) + h_ref[bi, 0, 0]


def _rwkv_chain(g, h, rp, yl, seq, ctx_len):
    b, _, nch, _, _ = g.shape
    n_lat = seq // CH
    n_ctx = ctx_len // CH

    def cf(s):
        return jnp.where(s < n_ctx, n_lat + s, s - n_ctx)

    def cr(s):
        return nch - 1 - s

    fwd = pl.BlockSpec((b, 1, 1, CH, RW), lambda s: (0, 0, cf(s), 0, 0))
    rev = pl.BlockSpec((b, 1, 1, CH, RW), lambda s: (0, 1, cr(s), 0, 0))
    yshape = jax.ShapeDtypeStruct((b, nch * CH, RW), F32)
    return pl.pallas_call(
        functools.partial(_rwkv_chain_kernel, nb=b),
        grid=(nch,),
        in_specs=[fwd, fwd, fwd, fwd, rev, rev, rev, rev],
        out_specs=[pl.BlockSpec((b, CH, RW), lambda s: (0, cf(s), 0)),
                   pl.BlockSpec((b, CH, RW), lambda s: (0, cr(s), 0))],
        out_shape=[yshape, yshape],
        scratch_shapes=[pltpu.VMEM((2, b, CH, RW), F32)],
        compiler_params=_cparams(("arbitrary",)),
        name="rwkv_chain",
    )(g, h, rp, yl, g, h, rp, yl)


def _merge_kernel(*refs, nblk, ctx_mode):
    (x_ref, mod_ref, gpre_ref, gpost_ref, za_ref, hp_ref, hn_ref, ca_ref, yb_ref, yc_ref, yf_ref, yr_ref,
     bonus_ref, lg_ref, g2_ref, lnw_ref, lnb_ref, bd_ref, wb_ref, wg_ref, bg_ref, wo_ref) = refs[:22]
    o_ref = refs[-1]
    tm = x_ref.shape[1]
    x = x_ref[0]
    m = mod_ref[0, 0]
    hb = _rms_mod(x, gpre_ref[...], m[0:1], m[1:2]).astype(BF16)

    za = za_ref[0]
    u = za[:, 256:512] * za[:, 512:768]
    valid_prev, valid_next = _edge_valid(pl.program_id(1), nblk, ctx_mode)
    up = hp_ref[0, HALO - 1:HALO, 256:512] * hp_ref[0, HALO - 1:HALO, 512:768] * valid_prev
    un = hn_ref[0, 0:1, 256:512] * hn_ref[0, 0:1, 512:768] * valid_next
    row = lax.broadcasted_iota(jnp.int32, (tm, 256), 0)
    u_prev = jnp.where(row == 0, up, pltpu.roll(u, 1, 0))
    u_next = jnp.where(row == tm - 1, un, pltpu.roll(u, tm - 1, 0))
    ya = za[:, :256] * (u_prev * ca_ref[0:1, :] + u * ca_ref[1:2, :] + u_next * ca_ref[2:3, :])

    bd = bd_ref[...]
    y = yf_ref[0] + yr_ref[0]
    mean = _sum_dot(y, bd, 3) * (1.0 / HEAD_DIM)
    yc0 = y - mean
    var = _sum_dot(yc0 * yc0, bd, 2) * (1.0 / HEAD_DIM)
    yn = yc0 * lax.rsqrt(var + LN_X_EPS) * lnw_ref[...] + lnb_ref[...] + bonus_ref[0]
    yd = yn * _dot3(_sigmoid(lg_ref[0]), g2_ref[0], g2_ref[1])

    acc = None
    for bidx, ys in enumerate((ya.astype(BF16), yb_ref[0], yc_ref[0], yd.astype(BF16))):
        sl = slice(bidx * D_MODEL, (bidx + 1) * D_MODEL)
        gate2 = jnp.tanh(jnp.dot(hb, wg_ref[:, sl], preferred_element_type=F32) + bg_ref[:, sl]) + 1.0
        term = gate2 * jnp.dot(ys, wb_ref[bidx], preferred_element_type=F32)
        acc = term if acc is None else acc + term
    mo = jnp.dot(acc.astype(BF16), wo_ref[...], preferred_element_type=F32)
    o_ref[0] = x + m[2:3] * _rms(mo, gpost_ref[...])


def _merge(xa, modsel, g_pre, g_post, za, conv_a, yb, yc, yf, yr, bonus, zr, g2, ln_w, ln_b, bd256,
           w_branch, w_gate, b_gate, w_o, seq, tm, prev=None):
    b, tt, _ = xa.shape
    ctx_mode = prev is not None
    nblk, blk, steps, alias_specs = _token_tiling(seq, tm, ctx_mode, 1)
    tile = lambda w: pl.BlockSpec((1, tm, w), lambda bi, i: (bi, blk(i), 0))
    return pl.pallas_call(
        functools.partial(_merge_kernel, nblk=nblk, ctx_mode=ctx_mode),
        grid=(b, steps),
        in_specs=[
            tile(D_MODEL),
            pl.BlockSpec((1, 1, 6, D_MODEL), lambda bi, i: (bi, int(ctx_mode), 0, 0)),
            _const_spec((1, D_MODEL)),
            _const_spec((1, D_MODEL)),
            tile(COLS_A),
            *_halo_specs(COLS_A, tm, blk, tt // HALO),
            _const_spec((3, 256)),
            tile(256), tile(256), tile(256), tile(256), tile(256),
            pl.BlockSpec((1, tm, LORA_G), lambda bi, i: (bi, blk(i), RW_SHIFT // LORA_G)),
            _const_spec((2, LORA_G, RW)),
            _const_spec((1, RW)),
            _const_spec((1, RW)),
            _const_spec((RW, RW)),
            _const_spec((4, 256, D_MODEL)),
            _const_spec((D_MODEL, 4 * D_MODEL)),
            _const_spec((1, 4 * D_MODEL)),
            _const_spec((D_MODEL, D_MODEL)),
        ] + alias_specs,
        out_specs=tile(D_MODEL),
        out_shape=jax.ShapeDtypeStruct((b, tt, D_MODEL), F32),
        input_output_aliases={22: 0} if ctx_mode else {},
        compiler_params=_cparams(("parallel", "parallel")),
        name="merge_ctx" if ctx_mode else "merge",
    )(xa, modsel, g_pre, g_post, za, za, za, conv_a, yb, yc, yf, yr, bonus, zr, g2, ln_w, ln_b, bd256,
      w_branch, w_gate, b_gate, w_o, *((prev,) if ctx_mode else ()))


def _ffn_kernel(*refs, nblk, ctx_mode):
    x_ref, hp_ref, hn_ref, mod_ref, gpre_ref, gpost_ref, wu_ref, cw_ref, wd_ref = refs[:9]
    o_ref, act_ref = refs[-2:]
    tm = x_ref.shape[1]
    x = x_ref[0]
    m = mod_ref[0, 0]
    xx = jnp.concatenate([hp_ref[0], x, hn_ref[0]], axis=0)
    nrow = tm + 2 * HALO
    row = lax.broadcasted_iota(jnp.int32, (nrow, 1), 0)
    valid_prev, valid_next = _edge_valid(pl.program_id(1), nblk, ctx_mode)
    rowmask = jnp.where(row < HALO, valid_prev, jnp.where(row >= tm + HALO, valid_next, 1.0))
    hb = (_rms_mod(xx, gpre_ref[...], m[3:4], m[4:5]) * rowmask).astype(BF16)

    def conv(u, col):
        w = cw_ref[:, col:col + FF_CHUNK]
        c = (pltpu.roll(u, 1, 0) * w[0:1] + u * w[1:2] + pltpu.roll(u, nrow - 1, 0) * w[2:3])
        return c[HALO:HALO + tm]

    for j in range(D_FF // FF_CHUNK):
        ca = conv(jnp.dot(hb, wu_ref[:, j * FF_CHUNK:(j + 1) * FF_CHUNK], preferred_element_type=F32),
                  j * FF_CHUNK)
        cg = conv(jnp.dot(hb, wu_ref[:, D_FF + j * FF_CHUNK:D_FF + (j + 1) * FF_CHUNK],
                          preferred_element_type=F32), D_FF + j * FF_CHUNK)
        act_ref[:, j * FF_CHUNK:(j + 1) * FF_CHUNK] = (ca * (cg * (1.0 + jnp.tanh(cg)))).astype(BF16)
    f = jnp.dot(act_ref[...], wd_ref[...], preferred_element_type=F32)
    o_ref[0] = x + m[5:6] * _rms(f, gpost_ref[...])


def _ffn(xa, modsel, g_pre, g_post, w_up, conv_w, w_down, seq, tm, prev=None, out_rows=None):
    b, tt, _ = xa.shape
    ctx_mode = prev is not None
    nblk, blk, steps, alias_specs = _token_tiling(seq, tm, ctx_mode, 1)
    return pl.pallas_call(
        functools.partial(_ffn_kernel, nblk=nblk, ctx_mode=ctx_mode),
        grid=(b, steps),
        in_specs=[
            pl.BlockSpec((1, tm, D_MODEL), lambda bi, i: (bi, blk(i), 0)),
            *_halo_specs(D_MODEL, tm, blk, tt // HALO),
            pl.BlockSpec((1, 1, 6, D_MODEL), lambda bi, i: (bi, int(ctx_mode), 0, 0)),
            _const_spec((1, D_MODEL)),
            _const_spec((1, D_MODEL)),
            _const_spec((D_MODEL, 2 * D_FF)),
            _const_spec((3, 2 * D_FF)),
            _const_spec((D_FF, D_MODEL)),
        ] + alias_specs,
        out_specs=pl.BlockSpec((1, tm, D_MODEL), lambda bi, i: (bi, blk(i), 0)),
        out_shape=jax.ShapeDtypeStruct((b, out_rows or tt, D_MODEL), F32),
        scratch_shapes=[pltpu.VMEM((tm, D_FF), BF16)],
        input_output_aliases={9: 0} if ctx_mode else {},
        compiler_params=_cparams(("parallel", "parallel")),
        name="conv_ffn_ctx" if ctx_mode else "conv_ffn",
    )(xa, xa, xa, modsel, g_pre, g_post, w_up, conv_w, w_down, *((prev,) if ctx_mode else ()))


def _rope_tables(seq, ctx_len):
    t = np.arange(seq)
    row = (t // GRID_W).astype(np.float32)
    col = (t % GRID_W).astype(np.float32)
    n_freq = HEAD_DIM // 4
    inv_freq = jnp.asarray(ROPE_THETA, F32) ** (-jnp.arange(n_freq, dtype=F32) / n_freq)
    ang = jnp.concatenate([jnp.asarray(row)[:, None] * inv_freq, jnp.asarray(col)[:, None] * inv_freq], axis=-1)
    cos = jnp.repeat(jnp.cos(ang), 2, axis=-1)
    sin = jnp.repeat(jnp.sin(ang), 2, axis=-1) * jnp.tile(jnp.asarray([-1.0, 1.0], F32), HEAD_DIM // 2)
    cos = jnp.concatenate([cos, jnp.ones((ctx_len, HEAD_DIM), F32)], axis=0)
    sin = jnp.concatenate([sin, jnp.zeros((ctx_len, HEAD_DIM), F32)], axis=0)
    nh = GQA_Q_HEADS + GQA_KV_HEADS
    return jnp.tile(cos, (1, nh)), jnp.tile(sin, (1, nh))


def _block_ones(n, scale):
    idx = np.arange(n) // HEAD_DIM
    return jnp.asarray((idx[:, None] == idx[None, :]).astype(np.float32) * scale)


def kernel(x, c, ctx, c_ctx, ada_w, ada_b, norm_mix_pre, norm_mix_post, norm_ffn_pre, norm_ffn_post, w_in, conv_a, na_bias, q_norm, k_norm, rw_mu, rw_w0, rw_w2, rw_a0, rw_a2, rw_kk, rw_ka, rw_rk, rw_g2, rw_ln_w, rw_ln_b, w_branch, w_gate, b_gate, w_o, ffn_up, ffn_conv, ffn_down):
    b, seq, _ = x.shape
    ctx_len = ctx.shape[1]
    depth = ada_w.shape[0]
    assert ctx_len == TM and seq % GQA_TQ == 0 and seq % TML == 0 and seq // TM >= 3 and b + 1 <= 8
    ct = seq // TM
    rows = seq // GRID_W

    cvec = jnp.zeros((8, D_MODEL), F32).at[:b].set(c).at[b].set(c_ctx)
    mods = _ada(cvec, ada_w, ada_b)
    cos_t, sin_t = _rope_tables(seq, ctx_len)
    bd384 = _block_ones(384, 1.0 / HEAD_DIM).astype(BF16)
    bd256 = _block_ones(256, 1.0).astype(BF16)

    def hi_lo(w):
        hi = w.astype(BF16)
        return jnp.stack([hi, (w - hi.astype(F32)).astype(BF16)], axis=-3)

    xa = jnp.concatenate([x, ctx], axis=1)
    for l in range(depth):
        ml = mods[l].reshape(8, 6, D_MODEL)
        modsel = jnp.stack([ml[:b], jnp.broadcast_to(ml[b][None], (b, 6, D_MODEL))], axis=1)
        row2 = lambda a: a.reshape(1, -1)
        gqk = jnp.concatenate([jnp.tile(q_norm[l], GQA_Q_HEADS) * HEAD_DIM ** -0.5,
                               jnp.tile(k_norm[l], GQA_KV_HEADS)]).reshape(1, -1)
        ip_args = (xa, modsel, row2(norm_mix_pre[l]), w_in[l].astype(BF16), cos_t, sin_t, gqk, bd384, seq)
        za, qn, kn, vn, qg, kg, vg, zr = _inproj(*ip_args, ctx_len, prev=_inproj(*ip_args, TML))
        yb = _na(qn, kn, vn, _na_bias_table(na_bias[l], rows), ct)
        yc = _gqa(qg, kg, vg, seq, ctx_len)
        g, h, rp, yl, bonus = _rwkv_chunks(zr, rw_mu[l], rw_w0[l], hi_lo(rw_w2[l]), rw_a0[l], hi_lo(rw_a2[l]),
                                           row2(rw_kk[l]), row2(rw_ka[l]), row2(rw_rk[l]), bd256, ct)
        yf, yr = _rwkv_chain(g, h, rp, yl, seq, ctx_len)
        mg_args = (xa, modsel, row2(norm_mix_pre[l]), row2(norm_mix_post[l]), za, conv_a[l], yb, yc, yf, yr,
                   bonus, zr, hi_lo(rw_g2[l]), row2(rw_ln_w[l]), row2(rw_ln_b[l]), bd256,
                   w_branch[l].astype(BF16), (0.5 * w_gate[l]).astype(BF16), row2(0.5 * b_gate[l]),
                   (0.5 * w_o[l]).astype(BF16), seq)
        xm = _merge(*mg_args, ctx_len, prev=_merge(*mg_args, TML))
        conv_w = jnp.concatenate([ffn_conv[l][:, :D_FF], 0.5 * ffn_conv[l][:, D_FF:]], axis=1)
        ff_args = (xm, modsel, row2(norm_ffn_pre[l]), row2(norm_ffn_post[l]), ffn_up[l].astype(BF16),
                   conv_w, ffn_down[l].astype(BF16), seq)
        if l < depth - 1:
            xa = _ffn(*ff_args, ctx_len, prev=_ffn(*ff_args, TML))
        else:
            xa = _ffn(*ff_args, TML, out_rows=seq)
    return xa
```

```python
import functools

import numpy as np
import jax
import jax.numpy as jnp
from jax import lax
from jax.experimental import pallas as pl
from jax.experimental.pallas import tpu as pltpu

F32 = jnp.float32
BF16 = jnp.bfloat16
HIGHEST = lax.Precision.HIGHEST

D_MODEL = 1024
GRID_W = 64
HEAD_DIM = 64
NA_HEADS = 4
NA_WIN_R = 8
NA_WIN_C = 16
GQA_Q_HEADS = 4
GQA_KV_HEADS = 2
ROPE_THETA = 10000.0
RWKV_HEADS = 4
RW = RWKV_HEADS * HEAD_DIM
LORA_W = 64
LORA_A = 64
LORA_G = 128
RW_SHIFT = 3 * RW + LORA_W + LORA_A
D_FF = 2816
NORM_EPS = 1e-6
LN_X_EPS = 64e-5
COLS_A = 768
COLS_NA = 768
COLS_GQA = 512
COLS_RW = 1024
D_IN = COLS_A + COLS_NA + COLS_GQA + COLS_RW

TM = 256
TML = 512
CH = 64
HALO = 8
FF_CHUNK = 256
GQA_TQ = 512
GQA_TK = 512
NEG = -1e30
VMEM_LIMIT = 56 * 1024 * 1024


def _cparams(sem):
    return pltpu.CompilerParams(dimension_semantics=sem, vmem_limit_bytes=VMEM_LIMIT)


def _const_spec(shape):
    nd = len(shape)
    return pl.BlockSpec(shape, lambda *_: (0,) * nd, pipeline_mode=pl.Buffered(1))


def _token_tiling(seq, tm, ctx_mode, n_out):
    nblk = seq // tm
    blk = (lambda i: i * 0 + nblk) if ctx_mode else (lambda i: i)
    alias_specs = [pl.BlockSpec(memory_space=pl.ANY)] * n_out if ctx_mode else []
    return nblk, blk, (1 if ctx_mode else nblk), alias_specs


def _halo_specs(width, tm, blk, nhb):
    per = tm // HALO
    return [pl.BlockSpec((1, HALO, width), lambda bi, i: (bi, jnp.maximum(blk(i) * per - 1, 0), 0)),
            pl.BlockSpec((1, HALO, width), lambda bi, i: (bi, jnp.minimum((blk(i) + 1) * per, nhb - 1), 0))]


def _edge_valid(i, nblk, ctx_mode):
    if ctx_mode:
        return 0.0, 0.0
    return jnp.where(i != 0, 1.0, 0.0), jnp.where(i != nblk - 1, 1.0, 0.0)


def _dg(a, b, ca, cb, **kw):
    return lax.dot_general(a, b, (((ca,), (cb,)), ((), ())), preferred_element_type=F32, **kw)


def _sum_dot(x, w, terms):
    acc = None
    for _ in range(terms):
        piece = x.astype(BF16)
        part = jnp.dot(piece, w, preferred_element_type=F32)
        acc = part if acc is None else acc + part
        x = x - piece.astype(F32)
    return acc


def _dot3(x, w_hi, w_lo):
    x_hi = x.astype(BF16)
    x_lo = (x - x_hi.astype(F32)).astype(BF16)
    return (jnp.dot(x_hi, w_hi, preferred_element_type=F32) + jnp.dot(x_lo, w_hi, preferred_element_type=F32)
            + jnp.dot(x_hi, w_lo, preferred_element_type=F32))


def _rms_mod(x, g, shift, scale):
    y = x * lax.rsqrt(jnp.mean(x * x, axis=-1, keepdims=True) + NORM_EPS) * g
    return y * (1.0 + scale) + shift


def _rms(x, g):
    return x * lax.rsqrt(jnp.mean(x * x, axis=-1, keepdims=True) + NORM_EPS) * g


def _sigmoid(x):
    return 0.5 * jnp.tanh(0.5 * x) + 0.5


def _ada_kernel(c_ref, w_ref, b_ref, o_ref):
    c = c_ref[...]
    s = c * _sigmoid(c)
    o_ref[0] = jnp.dot(s, w_ref[0], precision=HIGHEST, preferred_element_type=F32) + b_ref[0]


def _ada(cvec, ada_w, ada_b):
    depth = ada_w.shape[0]
    nblk = ada_w.shape[2] // D_MODEL
    return pl.pallas_call(
        _ada_kernel,
        grid=(depth, nblk),
        in_specs=[
            pl.BlockSpec((8, D_MODEL), lambda l, j: (0, 0)),
            pl.BlockSpec((1, D_MODEL, D_MODEL), lambda l, j: (l, 0, j)),
            pl.BlockSpec((1, 1, D_MODEL), lambda l, j: (l, 0, j)),
        ],
        out_specs=pl.BlockSpec((1, 8, D_MODEL), lambda l, j: (l, 0, j)),
        out_shape=jax.ShapeDtypeStruct((depth, 8, ada_w.shape[2]), F32),
        compiler_params=_cparams(("parallel", "parallel")),
        name="ada_mod",
    )(cvec, ada_w, ada_b.reshape(depth, 1, -1))


def _inproj_kernel(*refs):
    x_ref, mod_ref, g_ref, w_ref, cos_ref, sin_ref, gqk_ref, bd_ref = refs[:8]
    za_ref, qn_ref, kn_ref, vn_ref, qg_ref, kg_ref, vg_ref, zr_ref = refs[-8:]
    tm = x_ref.shape[1]
    x = x_ref[0]
    m = mod_ref[0, 0]
    h = _rms_mod(x, g_ref[...], m[0:1], m[1:2])
    z = jnp.dot(h.astype(BF16), w_ref[...], preferred_element_type=F32)
    za_ref[0] = z[:, :COLS_A]
    zr_ref[0] = z[:, COLS_A + COLS_NA + COLS_GQA:]
    na = z[:, COLS_A:COLS_A + COLS_NA]
    scale = HEAD_DIM ** -0.5
    for hd in range(NA_HEADS):
        qn_ref[0, hd] = (na[:, hd * 64:(hd + 1) * 64] * scale).astype(BF16)
        kn_ref[0, hd] = na[:, 256 + hd * 64:256 + (hd + 1) * 64].astype(BF16)
        vn_ref[0, hd] = na[:, 512 + hd * 64:512 + (hd + 1) * 64].astype(BF16)
    g = z[:, COLS_A + COLS_NA:COLS_A + COLS_NA + COLS_GQA]
    qk = g[:, :384]
    ms = _sum_dot(qk * qk, bd_ref[...], 2)
    qkn = qk * lax.rsqrt(ms + NORM_EPS) * gqk_ref[...]
    lane = lax.broadcasted_iota(jnp.int32, (tm, 128), 1)
    even = (lane & 1) == 0
    parts = []
    for j in range(3):
        s = qkn[:, j * 128:(j + 1) * 128]
        sw = jnp.where(even, pltpu.roll(s, 127, 1), pltpu.roll(s, 1, 1))
        parts.append(s * cos_ref[:, j * 128:(j + 1) * 128] + sw * sin_ref[:, j * 128:(j + 1) * 128])
    qg_ref[0] = jnp.concatenate([parts[0].T, parts[1].T], axis=0).astype(BF16)
    for hd in range(GQA_KV_HEADS):
        kg_ref[0, hd] = parts[2][:, hd * 64:(hd + 1) * 64].astype(BF16)
    vt = g[:, 384:512]
    for hd in range(GQA_KV_HEADS):
        vh = vt if hd == 0 else pltpu.roll(vt, 64, 1)
        vext = jnp.where(lane < 64, vh, jnp.where(lane == 64, 1.0, 0.0))
        vg_ref[0, hd] = vext.T.astype(BF16)


def _inproj(xa, modsel, g_pre, w_in, cos_t, sin_t, gqk, bd384, seq, tm, prev=None):
    b, tt, _ = xa.shape
    ctx_mode = prev is not None
    nblk, blk, steps, alias_specs = _token_tiling(seq, tm, ctx_mode, 8)
    tile = lambda w: pl.BlockSpec((1, tm, w), lambda bi, i: (bi, blk(i), 0))
    heads = lambda nh: pl.BlockSpec((1, nh, tm, 64), lambda bi, i: (bi, 0, blk(i), 0))
    hs = lambda nh: jax.ShapeDtypeStruct((b, nh, tt, 64), BF16)
    return pl.pallas_call(
        _inproj_kernel,
        grid=(b, steps),
        in_specs=[
            tile(D_MODEL),
            pl.BlockSpec((1, 1, 6, D_MODEL), lambda bi, i: (bi, int(ctx_mode), 0, 0)),
            _const_spec((1, D_MODEL)),
            _const_spec((D_MODEL, D_IN)),
            pl.BlockSpec((tm, 384), lambda bi, i: (blk(i), 0)),
            pl.BlockSpec((tm, 384), lambda bi, i: (blk(i), 0)),
            _const_spec((1, 384)),
            _const_spec((384, 384)),
        ] + alias_specs,
        out_specs=[tile(COLS_A), heads(4), heads(4), heads(4),
                   pl.BlockSpec((1, GQA_Q_HEADS * HEAD_DIM, tm), lambda bi, i: (bi, 0, blk(i))),
                   heads(2),
                   pl.BlockSpec((1, 2, 128, tm), lambda bi, i: (bi, 0, 0, blk(i))), tile(COLS_RW)],
        out_shape=[jax.ShapeDtypeStruct((b, tt, COLS_A), F32), hs(4), hs(4), hs(4),
                   jax.ShapeDtypeStruct((b, GQA_Q_HEADS * HEAD_DIM, tt), BF16),
                   hs(2),
                   jax.ShapeDtypeStruct((b, 2, 128, tt), BF16),
                   jax.ShapeDtypeStruct((b, tt, COLS_RW), F32)],
        input_output_aliases={8 + k: k for k in range(8)} if ctx_mode else {},
        compiler_params=_cparams(("parallel", "parallel")),
        name="inproj_ctx" if ctx_mode else "inproj",
    )(xa, modsel, g_pre, w_in, cos_t, sin_t, gqk, bd384, *(prev or ()))


def _na_kernel(q_ref, k0_ref, k1_ref, k2_ref, kc_ref, v0_ref, v1_ref, v2_ref, vc_ref, bias_ref, o_ref):
    scores = []
    for hd in range(NA_HEADS):
        kcat = jnp.concatenate([k0_ref[0, hd], k1_ref[0, hd], k2_ref[0, hd], kc_ref[0, hd]], axis=0)
        scores.append(_dg(q_ref[0, hd], kcat, 1, 1))
    outs = []
    for hd in range(NA_HEADS):
        s = scores[hd]
        s_loc = s[:, :3 * TM] + bias_ref[0, hd]
        s_ctx = s[:, 3 * TM:]
        m = jnp.maximum(jnp.max(s_loc, axis=-1, keepdims=True), jnp.max(s_ctx, axis=-1, keepdims=True))
        p_loc = jnp.exp(s_loc - m)
        p_ctx = jnp.exp(s_ctx - m)
        l = jnp.sum(p_loc, axis=-1, keepdims=True) + jnp.sum(p_ctx, axis=-1, keepdims=True)
        vloc = jnp.concatenate([v0_ref[0, hd], v1_ref[0, hd], v2_ref[0, hd]], axis=0)
        o = _dg(p_loc.astype(BF16), vloc, 1, 0) + _dg(p_ctx.astype(BF16), vc_ref[0, hd], 1, 0)
        outs.append(o / l)
    o_ref[0] = jnp.concatenate(outs, axis=-1).astype(o_ref.dtype)


def _na(qn, kn, vn, bias_tab, ct):
    b, _, tt, _ = qn.shape
    nt = tt // TM

    def kv_spec(j):
        if j is None:
            return pl.BlockSpec((1, 4, TM, 64), lambda bi, i: (bi, 0, ct, 0))
        return pl.BlockSpec((1, 4, TM, 64), lambda bi, i: (bi, 0, jnp.clip(i - 1, 0, ct - 3) + j, 0))

    def pattern(i):
        return jnp.where(i == ct, 3, jnp.where(i == 0, 0, jnp.where(i == ct - 1, 2, 1)))

    return pl.pallas_call(
        _na_kernel,
        grid=(b, nt),
        in_specs=[
            pl.BlockSpec((1, 4, TM, 64), lambda bi, i: (bi, 0, i, 0)),
            kv_spec(0), kv_spec(1), kv_spec(2), kv_spec(None),
            kv_spec(0), kv_spec(1), kv_spec(2), kv_spec(None),
            pl.BlockSpec((1, 4, TM, 3 * TM), lambda bi, i: (pattern(i), 0, 0, 0)),
        ],
        out_specs=pl.BlockSpec((1, TM, 256), lambda bi, i: (bi, i, 0)),
        out_shape=jax.ShapeDtypeStruct((b, tt, 256), BF16),
        compiler_params=_cparams(("parallel", "parallel")),
        name="na_attn",
    )(qn, kn, kn, kn, kn, vn, vn, vn, vn, bias_tab)


def _na_bias_table(na_bias_l, rows):
    ct = rows * GRID_W // TM
    rpt = TM // GRID_W
    wr = min(NA_WIN_R, rows)
    qj = np.arange(GRID_W)
    kc = np.arange(GRID_W)
    cs = np.clip(qj - NA_WIN_C // 2, 0, GRID_W - NA_WIN_C)
    colvalid = (kc[None, :] >= cs[:, None]) & (kc[None, :] < cs[:, None] + NA_WIN_C)
    dc = kc[None, :] - qj[:, None] + (NA_WIN_C - 1)
    onehot = (dc.reshape(1, -1) == np.arange(2 * NA_WIN_C - 1)[:, None]) & colvalid.reshape(1, -1)
    toep = jnp.einsum("hrd,dx->hrx", na_bias_l, jnp.asarray(onehot.astype(np.float32)), precision=HIGHEST)
    toep = jnp.where(jnp.asarray(colvalid.reshape(-1)), toep, NEG)
    toep = toep.reshape(NA_HEADS, 2 * NA_WIN_R - 1, GRID_W, GRID_W)
    neg_blk = jnp.full((NA_HEADS, GRID_W, GRID_W), NEG, F32)
    tabs = []
    for tile_i in (0, 1, ct - 1):
        i0 = tile_i * rpt
        kb = int(np.clip(tile_i - 1, 0, ct - 3)) * rpt
        qrows = []
        for ri in range(rpt):
            qi = i0 + ri
            rs = int(np.clip(qi - wr // 2, 0, rows - wr))
            blks = [toep[:, kb + m - qi + NA_WIN_R - 1] if rs <= kb + m < rs + wr else neg_blk
                    for m in range(3 * rpt)]
            qrows.append(jnp.concatenate(blks, axis=-1))
        tabs.append(jnp.concatenate(qrows, axis=-2))
    tabs.append(jnp.full_like(tabs[0], NEG))
    return jnp.stack(tabs, axis=0)


GQA_VROWS = 80
GQA_CB = 256
GQA_UNROLL = 8
GQA_AHEAD = 4


def _gqa_kernel(q_ref, k_ref, v_ref, o_ref, *scratch, n_full, tail, tq):
    nblk = 2 * tq // GQA_CB
    m_refs, acc_refs = scratch[:nblk], scratch[nblk:]
    qt = jnp.concatenate([q_ref[0, :HEAD_DIM, :], q_ref[0, HEAD_DIM:, :]], axis=1)
    for n in range(nblk):
        m_refs[n][...] = jnp.full(m_refs[n].shape, NEG, F32)
        acc_refs[n][...] = jnp.zeros(acc_refs[n].shape, F32)

    def chunks(spans):
        kcs = [k_ref[0, 0, pl.ds(start, size), :] for start, size in spans]
        vts = [v_ref[0, 0, :GQA_VROWS, pl.ds(start, size)] for start, size in spans]
        items = [(c, n) for c in range(len(spans)) for n in range(nblk)]
        score = lambda c, n: jnp.dot(kcs[c], qt[:, n * GQA_CB:(n + 1) * GQA_CB], preferred_element_type=F32)
        ahead = {i: score(*items[i]) for i in range(min(GQA_AHEAD, len(items)))}
        for i, (c, n) in enumerate(items):
            if i + GQA_AHEAD < len(items):
                ahead[i + GQA_AHEAD] = score(*items[i + GQA_AHEAD])
            st = ahead.pop(i)
            m_old = m_refs[n][...]
            m_new = jnp.maximum(m_old, jnp.max(st, axis=0, keepdims=True))
            pt = jnp.exp2(st - m_new).astype(BF16)
            acc_refs[n][...] = (jnp.exp2(m_old - m_new) * acc_refs[n][...]
                                + jnp.dot(vts[c], pt, preferred_element_type=F32))
            m_refs[n][...] = m_new

    n_trips = n_full // GQA_UNROLL
    if n_trips > 0:
        def body(j, carry):
            base = j * (GQA_UNROLL * GQA_TK)
            chunks([(pl.multiple_of(base + u * GQA_TK, GQA_TK), GQA_TK) for u in range(GQA_UNROLL)])
            return carry
        lax.fori_loop(0, n_trips, body, 0)
    rest = [(c * GQA_TK, GQA_TK) for c in range(n_trips * GQA_UNROLL, n_full)]
    if tail > 0:
        rest.append((n_full * GQA_TK, tail))
    if rest:
        chunks(rest)
    acc = jnp.concatenate([r[...] for r in acc_refs], axis=1)
    ot = acc[:HEAD_DIM] * (1.0 / acc[HEAD_DIM:HEAD_DIM + 1])
    ot = jnp.concatenate([ot, jnp.zeros_like(ot)], axis=0)
    o = ot.T
    o_ref[0] = jnp.concatenate([o[:tq, :HEAD_DIM], o[tq:, :HEAD_DIM]], axis=-1).astype(o_ref.dtype)


def _gqa(qg, kg, vg, seq, ctx_len):
    b, _, tt = qg.shape
    scratch = lambda tq: ([pltpu.VMEM((1, GQA_CB), F32)] * (2 * tq // GQA_CB)
                          + [pltpu.VMEM((GQA_VROWS, GQA_CB), F32)] * (2 * tq // GQA_CB))
    y_lat = pl.pallas_call(
        functools.partial(_gqa_kernel, n_full=seq // GQA_TK, tail=ctx_len, tq=GQA_TQ),
        grid=(b, GQA_KV_HEADS, seq // GQA_TQ),
        in_specs=[
            pl.BlockSpec((1, 2 * HEAD_DIM, GQA_TQ), lambda bi, n, i: (bi, n, i)),
            pl.BlockSpec((1, 1, tt, HEAD_DIM), lambda bi, n, i: (bi, n, 0, 0)),
            pl.BlockSpec((1, 1, 128, tt), lambda bi, n, i: (bi, n, 0, 0)),
        ],
        out_specs=pl.BlockSpec((1, GQA_TQ, 128), lambda bi, n, i: (bi, i, n)),
        out_shape=jax.ShapeDtypeStruct((b, seq, 256), BF16),
        scratch_shapes=scratch(GQA_TQ),
        compiler_params=_cparams(("parallel", "parallel", "parallel")),
        name="gqa_latent",
    )(qg, kg, vg)
    cblk = seq // ctx_len
    y_ctx = pl.pallas_call(
        functools.partial(_gqa_kernel, n_full=0, tail=ctx_len, tq=ctx_len),
        grid=(b, GQA_KV_HEADS),
        in_specs=[
            pl.BlockSpec((1, 2 * HEAD_DIM, ctx_len), lambda bi, n: (bi, n, cblk)),
            pl.BlockSpec((1, 1, ctx_len, HEAD_DIM), lambda bi, n: (bi, n, cblk, 0)),
            pl.BlockSpec((1, 1, 128, ctx_len), lambda bi, n: (bi, n, 0, cblk)),
        ],
        out_specs=pl.BlockSpec((1, ctx_len, 128), lambda bi, n: (bi, 0, n)),
        out_shape=jax.ShapeDtypeStruct((b, ctx_len, 256), BF16),
        scratch_shapes=scratch(ctx_len),
        compiler_params=_cparams(("parallel", "parallel")),
        name="gqa_context",
    )(qg, kg, vg)
    return jnp.concatenate([y_lat, y_ctx], axis=1)


def _bd(x):
    left = lax.broadcasted_iota(jnp.int32, (CH, 128), 1) < HEAD_DIM
    x0, x1 = x[:, :128], x[:, 128:]
    z = jnp.zeros_like(x0)
    keep_l = lambda a: jnp.where(left, a, z)
    keep_r = lambda a: jnp.where(left, z, a)
    return jnp.concatenate([jnp.concatenate([keep_l(x0), z], axis=1), jnp.concatenate([keep_r(x0), z], axis=1),
                            jnp.concatenate([z, keep_l(x1)], axis=1), jnp.concatenate([z, keep_r(x1)], axis=1)],
                           axis=0)


def _fold(x):
    left = lax.broadcasted_iota(jnp.int32, (CH, 128), 1) < HEAD_DIM
    return jnp.concatenate([jnp.where(left, x[0:64, :128], x[64:128, :128]),
                            jnp.where(left, x[128:192, 128:], x[192:256, 128:])], axis=1)


def _chunk_mats(items):
    t = lax.broadcasted_iota(jnp.int32, (CH, RW), 0)
    j = lax.broadcasted_iota(jnp.int32, (CH, RW), 1) & 63
    eye = j == t
    before = {False: j < t, True: j > t}
    incl = {False: j <= t, True: j >= t}
    bf = lambda x: x.astype(BF16)
    n_items = range(len(items))

    lb, mb, lk, mk = [], [], [], []
    for it in items:
        a2 = bf(jnp.concatenate([it["kap"], it["rho"]], axis=0))
        lm_b = _dg(a2, _bd(bf(it["bt"])), 1, 1)
        lm_k = _dg(a2, _bd(bf(it["kt"])), 1, 1)
        lb.append(jnp.where(before[it["rev"]], lm_b[:CH], 0.0))
        mb.append(jnp.where(incl[it["rev"]], lm_b[CH:], 0.0))
        lk.append(jnp.where(before[it["rev"]], lm_k[:CH], 0.0))
        mk.append(jnp.where(incl[it["rev"]], lm_k[CH:], 0.0))

    p, mpow = [], []
    for i in n_items:
        nb = bf(-lb[i])
        p.append(jnp.where(eye, 1.0, 0.0) - lb[i])
        mpow.append(_dg(nb, _bd(nb), 1, 0))
    for _ in range(4):
        for i in n_items:
            mbf = bf(mpow[i])
            pm = _dg(jnp.concatenate([bf(p[i]), mbf], axis=0), _bd(mbf), 1, 0)
            p[i] = p[i] + pm[:CH]
            mpow[i] = pm[CH:]
    tinv = [bf(p[i] + _dg(bf(p[i]), _bd(bf(mpow[i])), 1, 0)) for i in n_items]

    lmv = [_dg(bf(jnp.concatenate([lk[i], mk[i]], axis=0)), _bd(bf(items[i]["v"])), 1, 0) for i in n_items]
    kp = [_dg(tinv[i], _bd(bf(items[i]["kap"])), 1, 0) for i in n_items]
    vp = [_dg(tinv[i], _bd(bf(lmv[i][:CH])), 1, 0) for i in n_items]
    out = []
    for i in n_items:
        it = items[i]
        mbb = bf(mb[i])
        rp = it["rho"] - _dg(mbb, _bd(bf(kp[i])), 1, 0)
        yl = lmv[i][CH:] - _dg(mbb, _bd(bf(vp[i])), 1, 0)
        g = jnp.where(eye, it["gdiag"], 0.0) - _fold(_dg(bf(kp[i]), bf(it["bhat"]), 0, 0))
        hmat = _fold(_dg(bf(jnp.concatenate([it["v"], -vp[i]], axis=0)),
                         bf(jnp.concatenate([it["khat"], it["bhat"]], axis=0)), 0, 0))
        out.append((g, hmat, rp, yl))
    return out


def _rwkv_chunk_kernel(z_ref, hp_ref, hn_ref, mu_ref, w0_ref, w2_ref, a0_ref, a2_ref, kkw_ref, ka_ref, rk_ref,
                       bd_ref, g_ref, h_ref, rp_ref, yl_ref, bonus_ref, *, ct):
    i = pl.program_id(1)
    z = z_ref[0]
    zs = z[:, :RW_SHIFT]
    valid_prev = jnp.where((i != 0) & (i != ct), 1.0, 0.0)
    valid_next = jnp.where((i != ct - 1) & (i != ct), 1.0, 0.0)
    prev_row = hp_ref[0, HALO - 1:HALO, :RW_SHIFT] * valid_prev
    next_row = hn_ref[0, 0:1, :RW_SHIFT] * valid_next
    row = lax.broadcasted_iota(jnp.int32, (TM, RW_SHIFT), 0)
    tt = lax.broadcasted_iota(jnp.int32, (TM, TM), 0)
    jj = lax.broadcasted_iota(jnp.int32, (TM, TM), 1)
    same_chunk = (tt >> 6) == (jj >> 6)
    bd = bd_ref[...]
    bonus = None
    items = []
    for d in range(2):
        if d == 0:
            nb = jnp.where(row == 0, prev_row, pltpu.roll(zs, 1, 0))
        else:
            nb = jnp.where(row == TM - 1, next_row, pltpu.roll(zs, TM - 1, 0))
        zd = zs + mu_ref[d:d + 1, :] * (nb - zs)
        r = zd[:, :RW]
        k = zd[:, RW:2 * RW]
        v = zd[:, 2 * RW:3 * RW]
        lw = zd[:, 3 * RW:3 * RW + LORA_W]
        la = zd[:, 3 * RW + LORA_W:]
        w_log = w0_ref[d:d + 1, :] + _dot3(jnp.tanh(lw), w2_ref[d, 0], w2_ref[d, 1])
        sp = jnp.maximum(-w_log, 0.0) + jnp.log(1.0 + jnp.exp(-jnp.abs(w_log)))
        logw = -jnp.exp(-sp - 0.5)
        a = _sigmoid(a0_ref[d:d + 1, :] + _dot3(la, a2_ref[d, 0], a2_ref[d, 1]))
        kkr = k * kkw_ref[...]
        ss = _sum_dot(kkr * kkr, bd, 2)
        kk = kkr * lax.rsqrt(jnp.maximum(ss, 1e-24))
        k2 = k * (1.0 + (a - 1.0) * ka_ref[...])
        bv = kk * a
        bon = _sum_dot(r * k2 * rk_ref[...], bd, 2) * v
        bonus = bon if bonus is None else bonus + bon
        tri = jnp.where(same_chunk & ((jj >= tt) if d == 1 else (jj <= tt)), 1.0, 0.0).astype(BF16)
        cum, rest = None, logw
        for _ in range(3):
            piece = rest.astype(BF16)
            part = jnp.dot(tri, piece, preferred_element_type=F32)
            cum = part if cum is None else cum + part
            rest = rest - piece.astype(F32)
        e_neg = jnp.exp(-cum)
        streams = dict(kap=kk * jnp.exp(cum - logw), kt=k2 * e_neg, bt=bv * e_neg, rho=r * jnp.exp(cum), v=v)
        for c in range(TM // CH):
            sl = slice(c * CH, (c + 1) * CH)
            it = {name: val[sl] for name, val in streams.items()}
            last = c * CH if d == 1 else (c + 1) * CH - 1
            tot = cum[last:last + 1]
            e_tot = jnp.exp(tot - cum[sl])
            it.update(khat=k2[sl] * e_tot, bhat=bv[sl] * e_tot, gdiag=jnp.exp(tot), rev=d == 1)
            items.append(it)
    for idx, (g, hm, rp, yl) in enumerate(_chunk_mats(items)):
        d, c = divmod(idx, TM // CH)
        g_ref[0, d, c] = g.astype(BF16)
        h_ref[0, d, c] = hm
        rp_ref[0, d, c] = rp.astype(BF16)
        yl_ref[0, d, c] = yl
    bonus_ref[0] = bonus


def _rwkv_chunks(zr, mu, w0, w2, a0, a2, kkw, ka, rk, bd256, ct):
    b, tt, _ = zr.shape
    nt = tt // TM
    nch = tt // CH
    cpt = TM // CH
    nhb = tt // HALO
    mats = pl.BlockSpec((1, 2, cpt, CH, RW), lambda bi, i: (bi, 0, i, 0, 0))
    mshape = lambda dt: jax.ShapeDtypeStruct((b, 2, nch, CH, RW), dt)
    return pl.pallas_call(
        functools.partial(_rwkv_chunk_kernel, ct=ct),
        grid=(b, nt),
        in_specs=[
            pl.BlockSpec((1, TM, COLS_RW), lambda bi, i: (bi, i, 0)),
            pl.BlockSpec((1, HALO, COLS_RW), lambda bi, i: (bi, jnp.maximum(i * (TM // HALO) - 1, 0), 0)),
            pl.BlockSpec((1, HALO, COLS_RW), lambda bi, i: (bi, jnp.minimum((i + 1) * (TM // HALO), nhb - 1), 0)),
            _const_spec((2, RW_SHIFT)),
            _const_spec((2, RW)),
            _const_spec((2, 2, LORA_W, RW)),
            _const_spec((2, RW)),
            _const_spec((2, 2, LORA_A, RW)),
            _const_spec((1, RW)),
            _const_spec((1, RW)),
            _const_spec((1, RW)),
            _const_spec((RW, RW)),
        ],
        out_specs=[mats, mats, mats, mats, pl.BlockSpec((1, TM, RW), lambda bi, i: (bi, i, 0))],
        out_shape=[mshape(BF16), mshape(F32), mshape(BF16), mshape(F32), jax.ShapeDtypeStruct((b, tt, RW), F32)],
        compiler_params=_cparams(("parallel", "parallel")),
        name="rwkv_chunks",
    )(zr, zr, zr, mu, w0, w2, a0, a2, kkw, ka, rk, bd256)


def _rwkv_chain_kernel(gf_ref, hf_ref, rf_ref, yf_ref, gr_ref, hr_ref, rr_ref, yr_ref, of_ref, or_ref, s_ref,
                       *, nb):
    @pl.when(pl.program_id(0) == 0)
    def _():
        s_ref[...] = jnp.zeros(s_ref.shape, F32)

    for d, (g_ref, h_ref, rp_ref, yl_ref, o_ref) in enumerate(
            ((gf_ref, hf_ref, rf_ref, yf_ref, of_ref), (gr_ref, hr_ref, rr_ref, yr_ref, or_ref))):
        for bi in range(nb):
            s = s_ref[d, bi]
            sb = s.astype(BF16)
            o_ref[bi] = _dg(rp_ref[bi, 0, 0], _bd(sb), 1, 1) + yl_ref[bi, 0, 0]
            s_ref[d, bi] = _dg(sb, _bd(g_ref[bi, 0, 0]), 1, 0) + h_ref[bi, 0, 0]


def _rwkv_chain(g, h, rp, yl, seq, ctx_len):
    b, _, nch, _, _ = g.shape
    n_lat = seq // CH
    n_ctx = ctx_len // CH

    def cf(s):
        return jnp.where(s < n_ctx, n_lat + s, s - n_ctx)

    def cr(s):
        return nch - 1 - s

    fwd = pl.BlockSpec((b, 1, 1, CH, RW), lambda s: (0, 0, cf(s), 0, 0))
    rev = pl.BlockSpec((b, 1, 1, CH, RW), lambda s: (0, 1, cr(s), 0, 0))
    yshape = jax.ShapeDtypeStruct((b, nch * CH, RW), F32)
    return pl.pallas_call(
        functools.partial(_rwkv_chain_kernel, nb=b),
        grid=(nch,),
        in_specs=[fwd, fwd, fwd, fwd, rev, rev, rev, rev],
        out_specs=[pl.BlockSpec((b, CH, RW), lambda s: (0, cf(s), 0)),
                   pl.BlockSpec((b, CH, RW), lambda s: (0, cr(s), 0))],
        out_shape=[yshape, yshape],
        scratch_shapes=[pltpu.VMEM((2, b, CH, RW), F32)],
        compiler_params=_cparams(("arbitrary",)),
        name="rwkv_chain",
    )(g, h, rp, yl, g, h, rp, yl)


def _merge_kernel(*refs, nblk, ctx_mode):
    (x_ref, mod_ref, gpre_ref, gpost_ref, za_ref, hp_ref, hn_ref, ca_ref, yb_ref, yc_ref, yf_ref, yr_ref,
     bonus_ref, lg_ref, g2_ref, lnw_ref, lnb_ref, bd_ref, wb_ref, wg_ref, bg_ref, wo_ref) = refs[:22]
    o_ref = refs[-1]
    tm = x_ref.shape[1]
    x = x_ref[0]
    m = mod_ref[0, 0]
    hb = _rms_mod(x, gpre_ref[...], m[0:1], m[1:2]).astype(BF16)

    za = za_ref[0]
    u = za[:, 256:512] * za[:, 512:768]
    valid_prev, valid_next = _edge_valid(pl.program_id(1), nblk, ctx_mode)
    up = hp_ref[0, HALO - 1:HALO, 256:512] * hp_ref[0, HALO - 1:HALO, 512:768] * valid_prev
    un = hn_ref[0, 0:1, 256:512] * hn_ref[0, 0:1, 512:768] * valid_next
    row = lax.broadcasted_iota(jnp.int32, (tm, 256), 0)
    u_prev = jnp.where(row == 0, up, pltpu.roll(u, 1, 0))
    u_next = jnp.where(row == tm - 1, un, pltpu.roll(u, tm - 1, 0))
    ya = za[:, :256] * (u_prev * ca_ref[0:1, :] + u * ca_ref[1:2, :] + u_next * ca_ref[2:3, :])

    bd = bd_ref[...]
    y = yf_ref[0] + yr_ref[0]
    mean = _sum_dot(y, bd, 3) * (1.0 / HEAD_DIM)
    yc0 = y - mean
    var = _sum_dot(yc0 * yc0, bd, 2) * (1.0 / HEAD_DIM)
    yn = yc0 * lax.rsqrt(var + LN_X_EPS) * lnw_ref[...] + lnb_ref[...] + bonus_ref[0]
    yd = yn * _dot3(_sigmoid(lg_ref[0]), g2_ref[0], g2_ref[1])

    acc = None
    for bidx, ys in enumerate((ya.astype(BF16), yb_ref[0], yc_ref[0], yd.astype(BF16))):
        sl = slice(bidx * D_MODEL, (bidx + 1) * D_MODEL)
        gate2 = jnp.tanh(jnp.dot(hb, wg_ref[:, sl], preferred_element_type=F32) + bg_ref[:, sl]) + 1.0
        term = gate2 * jnp.dot(ys, wb_ref[bidx], preferred_element_type=F32)
        acc = term if acc is None else acc + term
    mo = jnp.dot(acc.astype(BF16), wo_ref[...], preferred_element_type=F32)
    o_ref[0] = x + m[2:3] * _rms(mo, gpost_ref[...])


def _merge(xa, modsel, g_pre, g_post, za, conv_a, yb, yc, yf, yr, bonus, zr, g2, ln_w, ln_b, bd256,
           w_branch, w_gate, b_gate, w_o, seq, tm, prev=None):
    b, tt, _ = xa.shape
    ctx_mode = prev is not None
    nblk, blk, steps, alias_specs = _token_tiling(seq, tm, ctx_mode, 1)
    tile = lambda w: pl.BlockSpec((1, tm, w), lambda bi, i: (bi, blk(i), 0))
    return pl.pallas_call(
        functools.partial(_merge_kernel, nblk=nblk, ctx_mode=ctx_mode),
        grid=(b, steps),
        in_specs=[
            tile(D_MODEL),
            pl.BlockSpec((1, 1, 6, D_MODEL), lambda bi, i: (bi, int(ctx_mode), 0, 0)),
            _const_spec((1, D_MODEL)),
            _const_spec((1, D_MODEL)),
            tile(COLS_A),
            *_halo_specs(COLS_A, tm, blk, tt // HALO),
            _const_spec((3, 256)),
            tile(256), tile(256), tile(256), tile(256), tile(256),
            pl.BlockSpec((1, tm, LORA_G), lambda bi, i: (bi, blk(i), RW_SHIFT // LORA_G)),
            _const_spec((2, LORA_G, RW)),
            _const_spec((1, RW)),
            _const_spec((1, RW)),
            _const_spec((RW, RW)),
            _const_spec((4, 256, D_MODEL)),
            _const_spec((D_MODEL, 4 * D_MODEL)),
            _const_spec((1, 4 * D_MODEL)),
            _const_spec((D_MODEL, D_MODEL)),
        ] + alias_specs,
        out_specs=tile(D_MODEL),
        out_shape=jax.ShapeDtypeStruct((b, tt, D_MODEL), F32),
        input_output_aliases={22: 0} if ctx_mode else {},
        compiler_params=_cparams(("parallel", "parallel")),
        name="merge_ctx" if ctx_mode else "merge",
    )(xa, modsel, g_pre, g_post, za, za, za, conv_a, yb, yc, yf, yr, bonus, zr, g2, ln_w, ln_b, bd256,
      w_branch, w_gate, b_gate, w_o, *((prev,) if ctx_mode else ()))


def _ffn_kernel(*refs, nblk, ctx_mode):
    x_ref, hp_ref, hn_ref, mod_ref, gpre_ref, gpost_ref, wu_ref, cw_ref, wd_ref = refs[:9]
    o_ref, act_ref = refs[-2:]
    tm = x_ref.shape[1]
    x = x_ref[0]
    m = mod_ref[0, 0]
    xx = jnp.concatenate([hp_ref[0], x, hn_ref[0]], axis=0)
    nrow = tm + 2 * HALO
    row = lax.broadcasted_iota(jnp.int32, (nrow, 1), 0)
    valid_prev, valid_next = _edge_valid(pl.program_id(1), nblk, ctx_mode)
    rowmask = jnp.where(row < HALO, valid_prev, jnp.where(row >= tm + HALO, valid_next, 1.0))
    hb = (_rms_mod(xx, gpre_ref[...], m[3:4], m[4:5]) * rowmask).astype(BF16)

    def conv(u, col):
        w = cw_ref[:, col:col + FF_CHUNK]
        c = (pltpu.roll(u, 1, 0) * w[0:1] + u * w[1:2] + pltpu.roll(u, nrow - 1, 0) * w[2:3])
        return c[HALO:HALO + tm]

    for j in range(D_FF // FF_CHUNK):
        ca = conv(jnp.dot(hb, wu_ref[:, j * FF_CHUNK:(j + 1) * FF_CHUNK], preferred_element_type=F32),
                  j * FF_CHUNK)
        cg = conv(jnp.dot(hb, wu_ref[:, D_FF + j * FF_CHUNK:D_FF + (j + 1) * FF_CHUNK],
                          preferred_element_type=F32), D_FF + j * FF_CHUNK)
        act_ref[:, j * FF_CHUNK:(j + 1) * FF_CHUNK] = (ca * (cg * (1.0 + jnp.tanh(cg)))).astype(BF16)
    f = jnp.dot(act_ref[...], wd_ref[...], preferred_element_type=F32)
    o_ref[0] = x + m[5:6] * _rms(f, gpost_ref[...])


def _ffn(xa, modsel, g_pre, g_post, w_up, conv_w, w_down, seq, tm, prev=None, out_rows=None):
    b, tt, _ = xa.shape
    ctx_mode = prev is not None
    nblk, blk, steps, alias_specs = _token_tiling(seq, tm, ctx_mode, 1)
    return pl.pallas_call(
        functools.partial(_ffn_kernel, nblk=nblk, ctx_mode=ctx_mode),
        grid=(b, steps),
        in_specs=[
            pl.BlockSpec((1, tm, D_MODEL), lambda bi, i: (bi, blk(i), 0)),
            *_halo_specs(D_MODEL, tm, blk, tt // HALO),
            pl.BlockSpec((1, 1, 6, D_MODEL), lambda bi, i: (bi, int(ctx_mode), 0, 0)),
            _const_spec((1, D_MODEL)),
            _const_spec((1, D_MODEL)),
            _const_spec((D_MODEL, 2 * D_FF)),
            _const_spec((3, 2 * D_FF)),
            _const_spec((D_FF, D_MODEL)),
        ] + alias_specs,
        out_specs=pl.BlockSpec((1, tm, D_MODEL), lambda bi, i: (bi, blk(i), 0)),
        out_shape=jax.ShapeDtypeStruct((b, out_rows or tt, D_MODEL), F32),
        scratch_shapes=[pltpu.VMEM((tm, D_FF), BF16)],
        input_output_aliases={9: 0} if ctx_mode else {},
        compiler_params=_cparams(("parallel", "parallel")),
        name="conv_ffn_ctx" if ctx_mode else "conv_ffn",
    )(xa, xa, xa, modsel, g_pre, g_post, w_up, conv_w, w_down, *((prev,) if ctx_mode else ()))


def _rope_tables(seq, ctx_len):
    t = np.arange(seq)
    row = (t // GRID_W).astype(np.float32)
    col = (t % GRID_W).astype(np.float32)
    n_freq = HEAD_DIM // 4
    inv_freq = jnp.asarray(ROPE_THETA, F32) ** (-jnp.arange(n_freq, dtype=F32) / n_freq)
    ang = jnp.concatenate([jnp.asarray(row)[:, None] * inv_freq, jnp.asarray(col)[:, None] * inv_freq], axis=-1)
    cos = jnp.repeat(jnp.cos(ang), 2, axis=-1)
    sin = jnp.repeat(jnp.sin(ang), 2, axis=-1) * jnp.tile(jnp.asarray([-1.0, 1.0], F32), HEAD_DIM // 2)
    cos = jnp.concatenate([cos, jnp.ones((ctx_len, HEAD_DIM), F32)], axis=0)
    sin = jnp.concatenate([sin, jnp.zeros((ctx_len, HEAD_DIM), F32)], axis=0)
    nh = GQA_Q_HEADS + GQA_KV_HEADS
    return jnp.tile(cos, (1, nh)), jnp.tile(sin, (1, nh))


def _block_ones(n, scale):
    idx = np.arange(n) // HEAD_DIM
    return jnp.asarray((idx[:, None] == idx[None, :]).astype(np.float32) * scale)


def kernel(x, c, ctx, c_ctx, ada_w, ada_b, norm_mix_pre, norm_mix_post, norm_ffn_pre, norm_ffn_post, w_in, conv_a, na_bias, q_norm, k_norm, rw_mu, rw_w0, rw_w2, rw_a0, rw_a2, rw_kk, rw_ka, rw_rk, rw_g2, rw_ln_w, rw_ln_b, w_branch, w_gate, b_gate, w_o, ffn_up, ffn_conv, ffn_down):
    b, seq, _ = x.shape
    ctx_len = ctx.shape[1]
    depth = ada_w.shape[0]
    assert ctx_len == TM and seq % GQA_TQ == 0 and seq % TML == 0 and seq // TM >= 3 and b + 1 <= 8
    ct = seq // TM
    rows = seq // GRID_W

    cvec = jnp.zeros((8, D_MODEL), F32).at[:b].set(c).at[b].set(c_ctx)
    mods = _ada(cvec, ada_w, ada_b)
    cos_t, sin_t = _rope_tables(seq, ctx_len)
    bd384 = _block_ones(384, 1.0 / HEAD_DIM).astype(BF16)
    bd256 = _block_ones(256, 1.0).astype(BF16)

    def hi_lo(w):
        hi = w.astype(BF16)
        return jnp.stack([hi, (w - hi.astype(F32)).astype(BF16)], axis=-3)

    xa = jnp.concatenate([x, ctx], axis=1)
    for l in range(depth):
        ml = mods[l].reshape(8, 6, D_MODEL)
        modsel = jnp.stack([ml[:b], jnp.broadcast_to(ml[b][None], (b, 6, D_MODEL))], axis=1)
        row2 = lambda a: a.reshape(1, -1)
        gqk = jnp.concatenate([jnp.tile(q_norm[l], GQA_Q_HEADS) * (HEAD_DIM ** -0.5 * np.log2(np.e)),
                               jnp.tile(k_norm[l], GQA_KV_HEADS)]).reshape(1, -1)
        ip_args = (xa, modsel, row2(norm_mix_pre[l]), w_in[l].astype(BF16), cos_t, sin_t, gqk, bd384, seq)
        za, qn, kn, vn, qg, kg, vg, zr = _inproj(*ip_args, ctx_len, prev=_inproj(*ip_args, TML))
        yb = _na(qn, kn, vn, _na_bias_table(na_bias[l], rows), ct)
        yc = _gqa(qg, kg, vg, seq, ctx_len)
        g, h, rp, yl, bonus = _rwkv_chunks(zr, rw_mu[l], rw_w0[l], hi_lo(rw_w2[l]), rw_a0[l], hi_lo(rw_a2[l]),
                                           row2(rw_kk[l]), row2(rw_ka[l]), row2(rw_rk[l]), bd256, ct)
        yf, yr = _rwkv_chain(g, h, rp, yl, seq, ctx_len)
        mg_args = (xa, modsel, row2(norm_mix_pre[l]), row2(norm_mix_post[l]), za, conv_a[l], yb, yc, yf, yr,
                   bonus, zr, hi_lo(rw_g2[l]), row2(rw_ln_w[l]), row2(rw_ln_b[l]), bd256,
                   w_branch[l].astype(BF16), (0.5 * w_gate[l]).astype(BF16), row2(0.5 * b_gate[l]),
                   (0.5 * w_o[l]).astype(BF16), seq)
        xm = _merge(*mg_args, ctx_len, prev=_merge(*mg_args, TML))
        conv_w = jnp.concatenate([ffn_conv[l][:, :D_FF], 0.5 * ffn_conv[l][:, D_FF:]], axis=1)
        ff_args = (xm, modsel, row2(norm_ffn_pre[l]), row2(norm_ffn_post[l]), ffn_up[l].astype(BF16),
                   conv_w, ffn_down[l].astype(BF16), seq)
        if l < depth - 1:
            xa = _ffn(*ff_args, ctx_len, prev=_ffn(*ff_args, TML))
        else:
            xa = _ffn(*ff_args, TML, out_rows=seq)
    return xa
```

```python
import functools

import numpy as np
import jax
import jax.numpy as jnp
from jax import lax
from jax.experimental import pallas as pl
from jax.experimental.pallas import tpu as pltpu

F32 = jnp.float32
BF16 = jnp.bfloat16
HIGHEST = lax.Precision.HIGHEST

D_MODEL = 1024
GRID_W = 64
HEAD_DIM = 64
NA_HEADS = 4
NA_WIN_R = 8
NA_WIN_C = 16
GQA_Q_HEADS = 4
GQA_KV_HEADS = 2
ROPE_THETA = 10000.0
RWKV_HEADS = 4
RW = RWKV_HEADS * HEAD_DIM
LORA_W = 64
LORA_A = 64
LORA_G = 128
RW_SHIFT = 3 * RW + LORA_W + LORA_A
D_FF = 2816
NORM_EPS = 1e-6
LN_X_EPS = 64e-5
COLS_A = 768
COLS_NA = 768
COLS_GQA = 512
COLS_RW = 1024
D_IN = COLS_A + COLS_NA + COLS_GQA + COLS_RW

TM = 256
TML = 512
CH = 64
HALO = 8
FF_CHUNK = 256
GQA_TQ = 512
GQA_TK = 512
NEG = -1e30
VMEM_LIMIT = 56 * 1024 * 1024


def _cparams(sem):
    return pltpu.CompilerParams(dimension_semantics=sem, vmem_limit_bytes=VMEM_LIMIT)


def _const_spec(shape):
    nd = len(shape)
    return pl.BlockSpec(shape, lambda *_: (0,) * nd, pipeline_mode=pl.Buffered(1))


def _token_tiling(seq, tm, ctx_mode, n_out):
    nblk = seq // tm
    blk = (lambda i: i * 0 + nblk) if ctx_mode else (lambda i: i)
    alias_specs = [pl.BlockSpec(memory_space=pl.ANY)] * n_out if ctx_mode else []
    return nblk, blk, (1 if ctx_mode else nblk), alias_specs


def _halo_specs(width, tm, blk, nhb):
    per = tm // HALO
    return [pl.BlockSpec((1, HALO, width), lambda bi, i: (bi, jnp.maximum(blk(i) * per - 1, 0), 0)),
            pl.BlockSpec((1, HALO, width), lambda bi, i: (bi, jnp.minimum((blk(i) + 1) * per, nhb - 1), 0))]


def _edge_valid(i, nblk, ctx_mode):
    if ctx_mode:
        return 0.0, 0.0
    return jnp.where(i != 0, 1.0, 0.0), jnp.where(i != nblk - 1, 1.0, 0.0)


def _dg(a, b, ca, cb, **kw):
    return lax.dot_general(a, b, (((ca,), (cb,)), ((), ())), preferred_element_type=F32, **kw)


def _sum_dot(x, w, terms):
    acc = None
    for _ in range(terms):
        piece = x.astype(BF16)
        part = jnp.dot(piece, w, preferred_element_type=F32)
        acc = part if acc is None else acc + part
        x = x - piece.astype(F32)
    return acc


def _dot3(x, w_hi, w_lo):
    x_hi = x.astype(BF16)
    x_lo = (x - x_hi.astype(F32)).astype(BF16)
    return (jnp.dot(x_hi, w_hi, preferred_element_type=F32) + jnp.dot(x_lo, w_hi, preferred_element_type=F32)
            + jnp.dot(x_hi, w_lo, preferred_element_type=F32))


def _rms_mod(x, g, shift, scale):
    y = x * lax.rsqrt(jnp.mean(x * x, axis=-1, keepdims=True) + NORM_EPS) * g
    return y * (1.0 + scale) + shift


def _rms(x, g):
    return x * lax.rsqrt(jnp.mean(x * x, axis=-1, keepdims=True) + NORM_EPS) * g


def _sigmoid(x):
    return 0.5 * jnp.tanh(0.5 * x) + 0.5


def _ada_kernel(c_ref, w_ref, b_ref, o_ref):
    c = c_ref[...]
    s = c * _sigmoid(c)
    o_ref[0] = jnp.dot(s, w_ref[0], precision=HIGHEST, preferred_element_type=F32) + b_ref[0]


def _ada(cvec, ada_w, ada_b):
    depth = ada_w.shape[0]
    nblk = ada_w.shape[2] // D_MODEL
    return pl.pallas_call(
        _ada_kernel,
        grid=(depth, nblk),
        in_specs=[
            pl.BlockSpec((8, D_MODEL), lambda l, j: (0, 0)),
            pl.BlockSpec((1, D_MODEL, D_MODEL), lambda l, j: (l, 0, j)),
            pl.BlockSpec((1, 1, D_MODEL), lambda l, j: (l, 0, j)),
        ],
        out_specs=pl.BlockSpec((1, 8, D_MODEL), lambda l, j: (l, 0, j)),
        out_shape=jax.ShapeDtypeStruct((depth, 8, ada_w.shape[2]), F32),
        compiler_params=_cparams(("parallel", "parallel")),
        name="ada_mod",
    )(cvec, ada_w, ada_b.reshape(depth, 1, -1))


def _inproj_kernel(*refs):
    x_ref, mod_ref, g_ref, w_ref, cos_ref, sin_ref, gqk_ref, bd_ref = refs[:8]
    za_ref, qn_ref, kn_ref, vn_ref, qg_ref, kg_ref, vg_ref, zr_ref = refs[-8:]
    tm = x_ref.shape[1]
    x = x_ref[0]
    m = mod_ref[0, 0]
    h = _rms_mod(x, g_ref[...], m[0:1], m[1:2])
    z = jnp.dot(h.astype(BF16), w_ref[...], preferred_element_type=F32)
    za_ref[0] = z[:, :COLS_A]
    zr_ref[0] = z[:, COLS_A + COLS_NA + COLS_GQA:]
    na = z[:, COLS_A:COLS_A + COLS_NA]
    scale = HEAD_DIM ** -0.5
    for hd in range(NA_HEADS):
        qn_ref[0, hd] = (na[:, hd * 64:(hd + 1) * 64] * scale).astype(BF16)
        kn_ref[0, hd] = na[:, 256 + hd * 64:256 + (hd + 1) * 64].astype(BF16)
        vn_ref[0, hd] = na[:, 512 + hd * 64:512 + (hd + 1) * 64].astype(BF16)
    g = z[:, COLS_A + COLS_NA:COLS_A + COLS_NA + COLS_GQA]
    qk = g[:, :384]
    ms = _sum_dot(qk * qk, bd_ref[...], 2)
    qkn = qk * lax.rsqrt(ms + NORM_EPS) * gqk_ref[...]
    lane = lax.broadcasted_iota(jnp.int32, (tm, 128), 1)
    even = (lane & 1) == 0
    parts = []
    for j in range(3):
        s = qkn[:, j * 128:(j + 1) * 128]
        sw = jnp.where(even, pltpu.roll(s, 127, 1), pltpu.roll(s, 1, 1))
        parts.append(s * cos_ref[:, j * 128:(j + 1) * 128] + sw * sin_ref[:, j * 128:(j + 1) * 128])
    qg_ref[0] = jnp.concatenate([parts[0].T, parts[1].T], axis=0).astype(BF16)
    for hd in range(GQA_KV_HEADS):
        kg_ref[0, hd] = parts[2][:, hd * 64:(hd + 1) * 64].astype(BF16)
    vt = g[:, 384:512]
    for hd in range(GQA_KV_HEADS):
        vh = vt if hd == 0 else pltpu.roll(vt, 64, 1)
        vext = jnp.where(lane < 64, vh, jnp.where(lane == 64, 1.0, 0.0))
        vg_ref[0, hd] = vext.T.astype(BF16)


def _inproj(xa, modsel, g_pre, w_in, cos_t, sin_t, gqk, bd384, seq, tm, prev=None):
    b, tt, _ = xa.shape
    ctx_mode = prev is not None
    nblk, blk, steps, alias_specs = _token_tiling(seq, tm, ctx_mode, 8)
    tile = lambda w: pl.BlockSpec((1, tm, w), lambda bi, i: (bi, blk(i), 0))
    heads = lambda nh: pl.BlockSpec((1, nh, tm, 64), lambda bi, i: (bi, 0, blk(i), 0))
    hs = lambda nh: jax.ShapeDtypeStruct((b, nh, tt, 64), BF16)
    return pl.pallas_call(
        _inproj_kernel,
        grid=(b, steps),
        in_specs=[
            tile(D_MODEL),
            pl.BlockSpec((1, 1, 6, D_MODEL), lambda bi, i: (bi, int(ctx_mode), 0, 0)),
            _const_spec((1, D_MODEL)),
            _const_spec((D_MODEL, D_IN)),
            pl.BlockSpec((tm, 384), lambda bi, i: (blk(i), 0)),
            pl.BlockSpec((tm, 384), lambda bi, i: (blk(i), 0)),
            _const_spec((1, 384)),
            _const_spec((384, 384)),
        ] + alias_specs,
        out_specs=[tile(COLS_A), heads(4), heads(4), heads(4),
                   pl.BlockSpec((1, GQA_Q_HEADS * HEAD_DIM, tm), lambda bi, i: (bi, 0, blk(i))),
                   heads(2),
                   pl.BlockSpec((1, 2, 128, tm), lambda bi, i: (bi, 0, 0, blk(i))), tile(COLS_RW)],
        out_shape=[jax.ShapeDtypeStruct((b, tt, COLS_A), F32), hs(4), hs(4), hs(4),
                   jax.ShapeDtypeStruct((b, GQA_Q_HEADS * HEAD_DIM, tt), BF16),
                   hs(2),
                   jax.ShapeDtypeStruct((b, 2, 128, tt), BF16),
                   jax.ShapeDtypeStruct((b, tt, COLS_RW), F32)],
        input_output_aliases={8 + k: k for k in range(8)} if ctx_mode else {},
        compiler_params=_cparams(("parallel", "parallel")),
        name="inproj_ctx" if ctx_mode else "inproj",
    )(xa, modsel, g_pre, w_in, cos_t, sin_t, gqk, bd384, *(prev or ()))


def _na_kernel(q_ref, k0_ref, k1_ref, k2_ref, kc_ref, v0_ref, v1_ref, v2_ref, vc_ref, bias_ref, o_ref):
    scores = []
    for hd in range(NA_HEADS):
        kcat = jnp.concatenate([k0_ref[0, hd], k1_ref[0, hd], k2_ref[0, hd], kc_ref[0, hd]], axis=0)
        scores.append(_dg(q_ref[0, hd], kcat, 1, 1))
    outs = []
    for hd in range(NA_HEADS):
        s = scores[hd]
        s_loc = s[:, :3 * TM] + bias_ref[0, hd]
        s_ctx = s[:, 3 * TM:]
        m = jnp.maximum(jnp.max(s_loc, axis=-1, keepdims=True), jnp.max(s_ctx, axis=-1, keepdims=True))
        p_loc = jnp.exp(s_loc - m)
        p_ctx = jnp.exp(s_ctx - m)
        l = jnp.sum(p_loc, axis=-1, keepdims=True) + jnp.sum(p_ctx, axis=-1, keepdims=True)
        vloc = jnp.concatenate([v0_ref[0, hd], v1_ref[0, hd], v2_ref[0, hd]], axis=0)
        o = _dg(p_loc.astype(BF16), vloc, 1, 0) + _dg(p_ctx.astype(BF16), vc_ref[0, hd], 1, 0)
        outs.append(o / l)
    o_ref[0] = jnp.concatenate(outs, axis=-1).astype(o_ref.dtype)


def _na(qn, kn, vn, bias_tab, ct):
    b, _, tt, _ = qn.shape
    nt = tt // TM

    def kv_spec(j):
        if j is None:
            return pl.BlockSpec((1, 4, TM, 64), lambda bi, i: (bi, 0, ct, 0))
        return pl.BlockSpec((1, 4, TM, 64), lambda bi, i: (bi, 0, jnp.clip(i - 1, 0, ct - 3) + j, 0))

    def pattern(i):
        return jnp.where(i == ct, 3, jnp.where(i == 0, 0, jnp.where(i == ct - 1, 2, 1)))

    return pl.pallas_call(
        _na_kernel,
        grid=(b, nt),
        in_specs=[
            pl.BlockSpec((1, 4, TM, 64), lambda bi, i: (bi, 0, i, 0)),
            kv_spec(0), kv_spec(1), kv_spec(2), kv_spec(None),
            kv_spec(0), kv_spec(1), kv_spec(2), kv_spec(None),
            pl.BlockSpec((1, 4, TM, 3 * TM), lambda bi, i: (pattern(i), 0, 0, 0)),
        ],
        out_specs=pl.BlockSpec((1, TM, 256), lambda bi, i: (bi, i, 0)),
        out_shape=jax.ShapeDtypeStruct((b, tt, 256), BF16),
        compiler_params=_cparams(("parallel", "parallel")),
        name="na_attn",
    )(qn, kn, kn, kn, kn, vn, vn, vn, vn, bias_tab)


def _na_bias_table(na_bias_l, rows):
    ct = rows * GRID_W // TM
    rpt = TM // GRID_W
    wr = min(NA_WIN_R, rows)
    qj = np.arange(GRID_W)
    kc = np.arange(GRID_W)
    cs = np.clip(qj - NA_WIN_C // 2, 0, GRID_W - NA_WIN_C)
    colvalid = (kc[None, :] >= cs[:, None]) & (kc[None, :] < cs[:, None] + NA_WIN_C)
    dc = kc[None, :] - qj[:, None] + (NA_WIN_C - 1)
    onehot = (dc.reshape(1, -1) == np.arange(2 * NA_WIN_C - 1)[:, None]) & colvalid.reshape(1, -1)
    toep = jnp.einsum("hrd,dx->hrx", na_bias_l, jnp.asarray(onehot.astype(np.float32)), precision=HIGHEST)
    toep = jnp.where(jnp.asarray(colvalid.reshape(-1)), toep, NEG)
    toep = toep.reshape(NA_HEADS, 2 * NA_WIN_R - 1, GRID_W, GRID_W)
    neg_blk = jnp.full((NA_HEADS, GRID_W, GRID_W), NEG, F32)
    tabs = []
    for tile_i in (0, 1, ct - 1):
        i0 = tile_i * rpt
        kb = int(np.clip(tile_i - 1, 0, ct - 3)) * rpt
        qrows = []
        for ri in range(rpt):
            qi = i0 + ri
            rs = int(np.clip(qi - wr // 2, 0, rows - wr))
            blks = [toep[:, kb + m - qi + NA_WIN_R - 1] if rs <= kb + m < rs + wr else neg_blk
                    for m in range(3 * rpt)]
            qrows.append(jnp.concatenate(blks, axis=-1))
        tabs.append(jnp.concatenate(qrows, axis=-2))
    tabs.append(jnp.full_like(tabs[0], NEG))
    return jnp.stack(tabs, axis=0)


GQA_VROWS = 80
GQA_CB = 256
GQA_UNROLL = 8
GQA_AHEAD = 4


def _gqa_kernel(q_ref, k_ref, v_ref, o_ref, *scratch, n_full, tail, tq):
    nblk = 2 * tq // GQA_CB
    m_refs, acc_refs = scratch[:nblk], scratch[nblk:]
    qt = jnp.concatenate([q_ref[0, :HEAD_DIM, :], q_ref[0, HEAD_DIM:, :]], axis=1)
    for n in range(nblk):
        m_refs[n][...] = jnp.full(m_refs[n].shape, NEG, F32)
        acc_refs[n][...] = jnp.zeros(acc_refs[n].shape, F32)

    def chunks(spans):
        kcs = [k_ref[0, 0, pl.ds(start, size), :] for start, size in spans]
        vts = [v_ref[0, 0, :GQA_VROWS, pl.ds(start, size)] for start, size in spans]
        items = [(c, n) for c in range(len(spans)) for n in range(nblk)]
        score = lambda c, n: jnp.dot(kcs[c], qt[:, n * GQA_CB:(n + 1) * GQA_CB], preferred_element_type=F32)
        ahead = {i: score(*items[i]) for i in range(min(GQA_AHEAD, len(items)))}
        for i, (c, n) in enumerate(items):
            if i + GQA_AHEAD < len(items):
                ahead[i + GQA_AHEAD] = score(*items[i + GQA_AHEAD])
            st = ahead.pop(i)
            m_old = m_refs[n][...]
            m_new = jnp.maximum(m_old, jnp.max(st, axis=0, keepdims=True))
            pt = jnp.exp2(st - m_new).astype(BF16)
            acc_refs[n][...] = (jnp.exp2(m_old - m_new) * acc_refs[n][...]
                                + jnp.dot(vts[c], pt, preferred_element_type=F32))
            m_refs[n][...] = m_new

    n_trips = n_full // GQA_UNROLL
    if n_trips > 0:
        def body(j, carry):
            base = j * (GQA_UNROLL * GQA_TK)
            chunks([(pl.multiple_of(base + u * GQA_TK, GQA_TK), GQA_TK) for u in range(GQA_UNROLL)])
            return carry
        lax.fori_loop(0, n_trips, body, 0)
    rest = [(c * GQA_TK, GQA_TK) for c in range(n_trips * GQA_UNROLL, n_full)]
    if tail > 0:
        rest.append((n_full * GQA_TK, tail))
    if rest:
        chunks(rest)
    acc = jnp.concatenate([r[...] for r in acc_refs], axis=1)
    ot = acc[:HEAD_DIM] * (1.0 / acc[HEAD_DIM:HEAD_DIM + 1])
    ot = jnp.concatenate([ot, jnp.zeros_like(ot)], axis=0)
    o = ot.T
    o_ref[0] = jnp.concatenate([o[:tq, :HEAD_DIM], o[tq:, :HEAD_DIM]], axis=-1).astype(o_ref.dtype)


def _gqa(qg, kg, vg, seq, ctx_len):
    b, _, tt = qg.shape
    scratch = lambda tq: ([pltpu.VMEM((1, GQA_CB), F32)] * (2 * tq // GQA_CB)
                          + [pltpu.VMEM((GQA_VROWS, GQA_CB), F32)] * (2 * tq // GQA_CB))
    y_lat = pl.pallas_call(
        functools.partial(_gqa_kernel, n_full=seq // GQA_TK, tail=ctx_len, tq=GQA_TQ),
        grid=(b, GQA_KV_HEADS, seq // GQA_TQ),
        in_specs=[
            pl.BlockSpec((1, 2 * HEAD_DIM, GQA_TQ), lambda bi, n, i: (bi, n, i)),
            pl.BlockSpec((1, 1, tt, HEAD_DIM), lambda bi, n, i: (bi, n, 0, 0)),
            pl.BlockSpec((1, 1, 128, tt), lambda bi, n, i: (bi, n, 0, 0)),
        ],
        out_specs=pl.BlockSpec((1, GQA_TQ, 128), lambda bi, n, i: (bi, i, n)),
        out_shape=jax.ShapeDtypeStruct((b, seq, 256), BF16),
        scratch_shapes=scratch(GQA_TQ),
        compiler_params=_cparams(("parallel", "parallel", "parallel")),
        name="gqa_latent",
    )(qg, kg, vg)
    cblk = seq // ctx_len
    y_ctx = pl.pallas_call(
        functools.partial(_gqa_kernel, n_full=0, tail=ctx_len, tq=ctx_len),
        grid=(b, GQA_KV_HEADS),
        in_specs=[
            pl.BlockSpec((1, 2 * HEAD_DIM, ctx_len), lambda bi, n: (bi, n, cblk)),
            pl.BlockSpec((1, 1, ctx_len, HEAD_DIM), lambda bi, n: (bi, n, cblk, 0)),
            pl.BlockSpec((1, 1, 128, ctx_len), lambda bi, n: (bi, n, 0, cblk)),
        ],
        out_specs=pl.BlockSpec((1, ctx_len, 128), lambda bi, n: (bi, 0, n)),
        out_shape=jax.ShapeDtypeStruct((b, ctx_len, 256), BF16),
        scratch_shapes=scratch(ctx_len),
        compiler_params=_cparams(("parallel", "parallel")),
        name="gqa_context",
    )(qg, kg, vg)
    return jnp.concatenate([y_lat, y_ctx], axis=1)


def _bd(x):
    left = lax.broadcasted_iota(jnp.int32, (CH, 128), 1) < HEAD_DIM
    x0, x1 = x[:, :128], x[:, 128:]
    z = jnp.zeros_like(x0)
    keep_l = lambda a: jnp.where(left, a, z)
    keep_r = lambda a: jnp.where(left, z, a)
    return jnp.concatenate([jnp.concatenate([keep_l(x0), z], axis=1), jnp.concatenate([keep_r(x0), z], axis=1),
                            jnp.concatenate([z, keep_l(x1)], axis=1), jnp.concatenate([z, keep_r(x1)], axis=1)],
                           axis=0)


def _fold(x):
    left = lax.broadcasted_iota(jnp.int32, (CH, 128), 1) < HEAD_DIM
    return jnp.concatenate([jnp.where(left, x[0:64, :128], x[64:128, :128]),
                            jnp.where(left, x[128:192, 128:], x[192:256, 128:])], axis=1)


def _chunk_mats(items):
    t = lax.broadcasted_iota(jnp.int32, (CH, RW), 0)
    j = lax.broadcasted_iota(jnp.int32, (CH, RW), 1) & 63
    eye = j == t
    before = {False: j < t, True: j > t}
    incl = {False: j <= t, True: j >= t}
    bf = lambda x: x.astype(BF16)
    n_items = range(len(items))

    lb, mb, lk, mk = [], [], [], []
    for it in items:
        a2 = bf(jnp.concatenate([it["kap"], it["rho"]], axis=0))
        lm_b = _dg(a2, _bd(bf(it["bt"])), 1, 1)
        lm_k = _dg(a2, _bd(bf(it["kt"])), 1, 1)
        lb.append(jnp.where(before[it["rev"]], lm_b[:CH], 0.0))
        mb.append(jnp.where(incl[it["rev"]], lm_b[CH:], 0.0))
        lk.append(jnp.where(before[it["rev"]], lm_k[:CH], 0.0))
        mk.append(jnp.where(incl[it["rev"]], lm_k[CH:], 0.0))

    p, mpow = [], []
    for i in n_items:
        nb = bf(-lb[i])
        p.append(jnp.where(eye, 1.0, 0.0) - lb[i])
        mpow.append(_dg(nb, _bd(nb), 1, 0))
    for _ in range(4):
        for i in n_items:
            mbf = bf(mpow[i])
            pm = _dg(jnp.concatenate([bf(p[i]), mbf], axis=0), _bd(mbf), 1, 0)
            p[i] = p[i] + pm[:CH]
            mpow[i] = pm[CH:]
    tinv = [bf(p[i] + _dg(bf(p[i]), _bd(bf(mpow[i])), 1, 0)) for i in n_items]

    lmv = [_dg(bf(jnp.concatenate([lk[i], mk[i]], axis=0)), _bd(bf(items[i]["v"])), 1, 0) for i in n_items]
    kp = [_dg(tinv[i], _bd(bf(items[i]["kap"])), 1, 0) for i in n_items]
    vp = [_dg(tinv[i], _bd(bf(lmv[i][:CH])), 1, 0) for i in n_items]
    out = []
    for i in n_items:
        it = items[i]
        mbb = bf(mb[i])
        rp = it["rho"] - _dg(mbb, _bd(bf(kp[i])), 1, 0)
        yl = lmv[i][CH:] - _dg(mbb, _bd(bf(vp[i])), 1, 0)
        g = jnp.where(eye, it["gdiag"], 0.0) - _fold(_dg(bf(kp[i]), bf(it["bhat"]), 0, 0))
        hmat = _fold(_dg(bf(jnp.concatenate([it["v"], -vp[i]], axis=0)),
                         bf(jnp.concatenate([it["khat"], it["bhat"]], axis=0)), 0, 0))
        out.append((g, hmat, rp, yl))
    return out


def _rwkv_chunk_kernel(z_ref, hp_ref, hn_ref, mu_ref, w0_ref, w2_ref, a0_ref, a2_ref, kkw_ref, ka_ref, rk_ref,
                       bd_ref, g_ref, h_ref, rp_ref, yl_ref, bonus_ref, *, ct):
    i = pl.program_id(1)
    z = z_ref[0]
    zs = z[:, :RW_SHIFT]
    valid_prev = jnp.where((i != 0) & (i != ct), 1.0, 0.0)
    valid_next = jnp.where((i != ct - 1) & (i != ct), 1.0, 0.0)
    prev_row = hp_ref[0, HALO - 1:HALO, :RW_SHIFT] * valid_prev
    next_row = hn_ref[0, 0:1, :RW_SHIFT] * valid_next
    row = lax.broadcasted_iota(jnp.int32, (TM, RW_SHIFT), 0)
    tt = lax.broadcasted_iota(jnp.int32, (TM, TM), 0)
    jj = lax.broadcasted_iota(jnp.int32, (TM, TM), 1)
    same_chunk = (tt >> 6) == (jj >> 6)
    bd = bd_ref[...]
    bonus = None
    items = []
    for d in range(2):
        if d == 0:
            nb = jnp.where(row == 0, prev_row, pltpu.roll(zs, 1, 0))
        else:
            nb = jnp.where(row == TM - 1, next_row, pltpu.roll(zs, TM - 1, 0))
        zd = zs + mu_ref[d:d + 1, :] * (nb - zs)
        r = zd[:, :RW]
        k = zd[:, RW:2 * RW]
        v = zd[:, 2 * RW:3 * RW]
        lw = zd[:, 3 * RW:3 * RW + LORA_W]
        la = zd[:, 3 * RW + LORA_W:]
        w_log = w0_ref[d:d + 1, :] + _dot3(jnp.tanh(lw), w2_ref[d, 0], w2_ref[d, 1])
        sp = jnp.maximum(-w_log, 0.0) + jnp.log(1.0 + jnp.exp(-jnp.abs(w_log)))
        logw = -jnp.exp(-sp - 0.5)
        a = _sigmoid(a0_ref[d:d + 1, :] + _dot3(la, a2_ref[d, 0], a2_ref[d, 1]))
        kkr = k * kkw_ref[...]
        ss = _sum_dot(kkr * kkr, bd, 2)
        kk = kkr * lax.rsqrt(jnp.maximum(ss, 1e-24))
        k2 = k * (1.0 + (a - 1.0) * ka_ref[...])
        bv = kk * a
        bon = _sum_dot(r * k2 * rk_ref[...], bd, 2) * v
        bonus = bon if bonus is None else bonus + bon
        tri = jnp.where(same_chunk & ((jj >= tt) if d == 1 else (jj <= tt)), 1.0, 0.0).astype(BF16)
        cum, rest = None, logw
        for _ in range(3):
            piece = rest.astype(BF16)
            part = jnp.dot(tri, piece, preferred_element_type=F32)
            cum = part if cum is None else cum + part
            rest = rest - piece.astype(F32)
        e_neg = jnp.exp(-cum)
        streams = dict(kap=kk * jnp.exp(cum - logw), kt=k2 * e_neg, bt=bv * e_neg, rho=r * jnp.exp(cum), v=v)
        for c in range(TM // CH):
            sl = slice(c * CH, (c + 1) * CH)
            it = {name: val[sl] for name, val in streams.items()}
            last = c * CH if d == 1 else (c + 1) * CH - 1
            tot = cum[last:last + 1]
            e_tot = jnp.exp(tot - cum[sl])
            it.update(khat=k2[sl] * e_tot, bhat=bv[sl] * e_tot, gdiag=jnp.exp(tot), rev=d == 1)
            items.append(it)
    for idx, (g, hm, rp, yl) in enumerate(_chunk_mats(items)):
        d, c = divmod(idx, TM // CH)
        g_ref[0, d, c] = g.astype(BF16)
        h_ref[0, d, c] = hm
        rp_ref[0, d, c] = rp.astype(BF16)
        yl_ref[0, d, c] = yl
    bonus_ref[0] = bonus


def _rwkv_chunks(zr, mu, w0, w2, a0, a2, kkw, ka, rk, bd256, ct):
    b, tt, _ = zr.shape
    nt = tt // TM
    nch = tt // CH
    cpt = TM // CH
    nhb = tt // HALO
    mats = pl.BlockSpec((1, 2, cpt, CH, RW), lambda bi, i: (bi, 0, i, 0, 0))
    mshape = lambda dt: jax.ShapeDtypeStruct((b, 2, nch, CH, RW), dt)
    return pl.pallas_call(
        functools.partial(_rwkv_chunk_kernel, ct=ct),
        grid=(b, nt),
        in_specs=[
            pl.BlockSpec((1, TM, COLS_RW), lambda bi, i: (bi, i, 0)),
            pl.BlockSpec((1, HALO, COLS_RW), lambda bi, i: (bi, jnp.maximum(i * (TM // HALO) - 1, 0), 0)),
            pl.BlockSpec((1, HALO, COLS_RW), lambda bi, i: (bi, jnp.minimum((i + 1) * (TM // HALO), nhb - 1), 0)),
            _const_spec((2, RW_SHIFT)),
            _const_spec((2, RW)),
            _const_spec((2, 2, LORA_W, RW)),
            _const_spec((2, RW)),
            _const_spec((2, 2, LORA_A, RW)),
            _const_spec((1, RW)),
            _const_spec((1, RW)),
            _const_spec((1, RW)),
            _const_spec((RW, RW)),
        ],
        out_specs=[mats, mats, mats, mats, pl.BlockSpec((1, TM, RW), lambda bi, i: (bi, i, 0))],
        out_shape=[mshape(BF16), mshape(F32), mshape(BF16), mshape(F32), jax.ShapeDtypeStruct((b, tt, RW), F32)],
        compiler_params=_cparams(("parallel", "parallel")),
        name="rwkv_chunks",
    )(zr, zr, zr, mu, w0, w2, a0, a2, kkw, ka, rk, bd256)


CHAIN_GROUP = 4


def _rwkv_chain_kernel(gf_ref, hf_ref, rf_ref, yf_ref, gr_ref, hr_ref, rr_ref, yr_ref, of_ref, or_ref, s_ref,
                       *, nb):
    @pl.when(pl.program_id(0) == 0)
    def _():
        s_ref[...] = jnp.zeros(s_ref.shape, F32)

    dirs = ((gf_ref, hf_ref, rf_ref, yf_ref, of_ref), (gr_ref, hr_ref, rr_ref, yr_ref, or_ref))
    state = [[s_ref[d, bi] for bi in range(nb)] for d in range(2)]
    for step in range(CHAIN_GROUP):
        for d, (g_ref, h_ref, rp_ref, yl_ref, o_ref) in enumerate(dirs):
            c = step if d == 0 else CHAIN_GROUP - 1 - step
            for bi in range(nb):
                sb = state[d][bi].astype(BF16)
                o_ref[bi, c * CH:(c + 1) * CH, :] = _dg(rp_ref[bi, 0, c], _bd(sb), 1, 1) + yl_ref[bi, 0, c]
                state[d][bi] = _dg(sb, _bd(g_ref[bi, 0, c]), 1, 0) + h_ref[bi, 0, c]
    for d in range(2):
        for bi in range(nb):
            s_ref[d, bi] = state[d][bi]


def _rwkv_chain(g, h, rp, yl, seq, ctx_len):
    b, _, nch, _, _ = g.shape
    assert ctx_len == CHAIN_GROUP * CH and seq % (CHAIN_GROUP * CH) == 0
    ngrp = nch // CHAIN_GROUP
    n_lat = seq // (CHAIN_GROUP * CH)

    def gf(s):
        return jnp.where(s == 0, n_lat, s - 1)

    def gr(s):
        return ngrp - 1 - s

    fwd = pl.BlockSpec((b, 1, CHAIN_GROUP, CH, RW), lambda s: (0, 0, gf(s), 0, 0))
    rev = pl.BlockSpec((b, 1, CHAIN_GROUP, CH, RW), lambda s: (0, 1, gr(s), 0, 0))
    yshape = jax.ShapeDtypeStruct((b, nch * CH, RW), F32)
    return pl.pallas_call(
        functools.partial(_rwkv_chain_kernel, nb=b),
        grid=(ngrp,),
        in_specs=[fwd, fwd, fwd, fwd, rev, rev, rev, rev],
        out_specs=[pl.BlockSpec((b, CHAIN_GROUP * CH, RW), lambda s: (0, gf(s), 0)),
                   pl.BlockSpec((b, CHAIN_GROUP * CH, RW), lambda s: (0, gr(s), 0))],
        out_shape=[yshape, yshape],
        scratch_shapes=[pltpu.VMEM((2, b, CH, RW), F32)],
        compiler_params=_cparams(("arbitrary",)),
        name="rwkv_chain",
    )(g, h, rp, yl, g, h, rp, yl)


def _merge_kernel(*refs, nblk, ctx_mode):
    (x_ref, mod_ref, gpre_ref, gpost_ref, za_ref, hp_ref, hn_ref, ca_ref, yb_ref, yc_ref, yf_ref, yr_ref,
     bonus_ref, lg_ref, g2_ref, lnw_ref, lnb_ref, bd_ref, wb_ref, wg_ref, bg_ref, wo_ref) = refs[:22]
    o_ref = refs[-1]
    tm = x_ref.shape[1]
    x = x_ref[0]
    m = mod_ref[0, 0]
    hb = _rms_mod(x, gpre_ref[...], m[0:1], m[1:2]).astype(BF16)

    def branch(bidx, ys):
        sl = slice(bidx * D_MODEL, (bidx + 1) * D_MODEL)
        gate2 = jnp.tanh(jnp.dot(hb, wg_ref[:, sl], preferred_element_type=F32) + bg_ref[:, sl]) + 1.0
        return gate2 * jnp.dot(ys, wb_ref[bidx], preferred_element_type=F32)

    za = za_ref[0]
    u = za[:, 256:512] * za[:, 512:768]
    valid_prev, valid_next = _edge_valid(pl.program_id(1), nblk, ctx_mode)
    up = hp_ref[0, HALO - 1:HALO, 256:512] * hp_ref[0, HALO - 1:HALO, 512:768] * valid_prev
    un = hn_ref[0, 0:1, 256:512] * hn_ref[0, 0:1, 512:768] * valid_next
    row = lax.broadcasted_iota(jnp.int32, (tm, 256), 0)
    u_prev = jnp.where(row == 0, up, pltpu.roll(u, 1, 0))
    u_next = jnp.where(row == tm - 1, un, pltpu.roll(u, tm - 1, 0))
    ya = za[:, :256] * (u_prev * ca_ref[0:1, :] + u * ca_ref[1:2, :] + u_next * ca_ref[2:3, :])

    bd = bd_ref[...]
    y = yf_ref[0] + yr_ref[0]
    mean = _sum_dot(y, bd, 3) * (1.0 / HEAD_DIM)
    yc0 = y - mean
    var = _sum_dot(yc0 * yc0, bd, 2) * (1.0 / HEAD_DIM)
    yn = yc0 * lax.rsqrt(var + LN_X_EPS) * lnw_ref[...] + lnb_ref[...] + bonus_ref[0]
    yd = yn * _dot3(_sigmoid(lg_ref[0]), g2_ref[0], g2_ref[1])
    acc = branch(0, ya.astype(BF16)) + branch(1, yb_ref[0]) + branch(2, yc_ref[0]) + branch(3, yd.astype(BF16))
    mo =jnp.dot(acc.astype(BF16), wo_ref[...], preferred_element_type=F32)
    o_ref[0] = x + m[2:3] * _rms(mo, gpost_ref[...])


def _merge(xa, modsel, g_pre, g_post, za, conv_a, yb, yc, yf, yr, bonus, zr, g2, ln_w, ln_b, bd256,
           w_branch, w_gate, b_gate, w_o, seq, tm, prev=None):
    b, tt, _ = xa.shape
    ctx_mode = prev is not None
    nblk, blk, steps, alias_specs = _token_tiling(seq, tm, ctx_mode, 1)
    tile = lambda w: pl.BlockSpec((1, tm, w), lambda bi, i: (bi, blk(i), 0))
    return pl.pallas_call(
        functools.partial(_merge_kernel, nblk=nblk, ctx_mode=ctx_mode),
        grid=(b, steps),
        in_specs=[
            tile(D_MODEL),
            pl.BlockSpec((1, 1, 6, D_MODEL), lambda bi, i: (bi, int(ctx_mode), 0, 0)),
            _const_spec((1, D_MODEL)),
            _const_spec((1, D_MODEL)),
            tile(COLS_A),
            *_halo_specs(COLS_A, tm, blk, tt // HALO),
            _const_spec((3, 256)),
            tile(256), tile(256), tile(256), tile(256), tile(256),
            pl.BlockSpec((1, tm, LORA_G), lambda bi, i: (bi, blk(i), RW_SHIFT // LORA_G)),
            _const_spec((2, LORA_G, RW)),
            _const_spec((1, RW)),
            _const_spec((1, RW)),
            _const_spec((RW, RW)),
            _const_spec((4, 256, D_MODEL)),
            _const_spec((D_MODEL, 4 * D_MODEL)),
            _const_spec((1, 4 * D_MODEL)),
            _const_spec((D_MODEL, D_MODEL)),
        ] + alias_specs,
        out_specs=tile(D_MODEL),
        out_shape=jax.ShapeDtypeStruct((b, tt, D_MODEL), F32),
        input_output_aliases={22: 0} if ctx_mode else {},
        compiler_params=_cparams(("parallel", "parallel")),
        name="merge_ctx" if ctx_mode else "merge",
    )(xa, modsel, g_pre, g_post, za, za, za, conv_a, yb, yc, yf, yr, bonus, zr, g2, ln_w, ln_b, bd256,
      w_branch, w_gate, b_gate, w_o, *((prev,) if ctx_mode else ()))


def _ffn_kernel(*refs, nblk, ctx_mode):
    x_ref, hp_ref, hn_ref, mod_ref, gpre_ref, gpost_ref, wu_ref, cw_ref, wd_ref = refs[:9]
    o_ref, act_ref = refs[-2:]
    tm = x_ref.shape[1]
    x = x_ref[0]
    m = mod_ref[0, 0]
    xx = jnp.concatenate([hp_ref[0], x, hn_ref[0]], axis=0)
    nrow = tm + 2 * HALO
    row = lax.broadcasted_iota(jnp.int32, (nrow, 1), 0)
    valid_prev, valid_next = _edge_valid(pl.program_id(1), nblk, ctx_mode)
    rowmask = jnp.where(row < HALO, valid_prev, jnp.where(row >= tm + HALO, valid_next, 1.0))
    hb = (_rms_mod(xx, gpre_ref[...], m[3:4], m[4:5]) * rowmask).astype(BF16)

    def conv(u, col):
        w = cw_ref[:, col:col + FF_CHUNK]
        c = (pltpu.roll(u, 1, 0) * w[0:1] + u * w[1:2] + pltpu.roll(u, nrow - 1, 0) * w[2:3])
        return c[HALO:HALO + tm]

    for j in range(D_FF // FF_CHUNK):
        ca = conv(jnp.dot(hb, wu_ref[:, j * FF_CHUNK:(j + 1) * FF_CHUNK], preferred_element_type=F32),
                  j * FF_CHUNK)
        cg = conv(jnp.dot(hb, wu_ref[:, D_FF + j * FF_CHUNK:D_FF + (j + 1) * FF_CHUNK],
                          preferred_element_type=F32), D_FF + j * FF_CHUNK)
        act_ref[:, j * FF_CHUNK:(j + 1) * FF_CHUNK] = (ca * (cg * (1.0 + jnp.tanh(cg)))).astype(BF16)
    f = jnp.dot(act_ref[...], wd_ref[...], preferred_element_type=F32)
    o_ref[0] = x + m[5:6] * _rms(f, gpost_ref[...])


def _ffn(xa, modsel, g_pre, g_post, w_up, conv_w, w_down, seq, tm, prev=None, out_rows=None):
    b, tt, _ = xa.shape
    ctx_mode = prev is not None
    nblk, blk, steps, alias_specs = _token_tiling(seq, tm, ctx_mode, 1)
    return pl.pallas_call(
        functools.partial(_ffn_kernel, nblk=nblk, ctx_mode=ctx_mode),
        grid=(b, steps),
        in_specs=[
            pl.BlockSpec((1, tm, D_MODEL), lambda bi, i: (bi, blk(i), 0)),
            *_halo_specs(D_MODEL, tm, blk, tt // HALO),
            pl.BlockSpec((1, 1, 6, D_MODEL), lambda bi, i: (bi, int(ctx_mode), 0, 0)),
            _const_spec((1, D_MODEL)),
            _const_spec((1, D_MODEL)),
            _const_spec((D_MODEL, 2 * D_FF)),
            _const_spec((3, 2 * D_FF)),
            _const_spec((D_FF, D_MODEL)),
        ] + alias_specs,
        out_specs=pl.BlockSpec((1, tm, D_MODEL), lambda bi, i: (bi, blk(i), 0)),
        out_shape=jax.ShapeDtypeStruct((b, out_rows or tt, D_MODEL), F32),
        scratch_shapes=[pltpu.VMEM((tm, D_FF), BF16)],
        input_output_aliases={9: 0} if ctx_mode else {},
        compiler_params=_cparams(("parallel", "parallel")),
        name="conv_ffn_ctx" if ctx_mode else "conv_ffn",
    )(xa, xa, xa, modsel, g_pre, g_post, w_up, conv_w, w_down, *((prev,) if ctx_mode else ()))


def _rope_tables(seq, ctx_len):
    t = np.arange(seq)
    row = (t // GRID_W).astype(np.float32)
    col = (t % GRID_W).astype(np.float32)
    n_freq = HEAD_DIM // 4
    inv_freq = jnp.asarray(ROPE_THETA, F32) ** (-jnp.arange(n_freq, dtype=F32) / n_freq)
    ang = jnp.concatenate([jnp.asarray(row)[:, None] * inv_freq, jnp.asarray(col)[:, None] * inv_freq], axis=-1)
    cos = jnp.repeat(jnp.cos(ang), 2, axis=-1)
    sin = jnp.repeat(jnp.sin(ang), 2, axis=-1) * jnp.tile(jnp.asarray([-1.0, 1.0], F32), HEAD_DIM // 2)
    cos = jnp.concatenate([cos, jnp.ones((ctx_len, HEAD_DIM), F32)], axis=0)
    sin = jnp.concatenate([sin, jnp.zeros((ctx_len, HEAD_DIM), F32)], axis=0)
    nh = GQA_Q_HEADS + GQA_KV_HEADS
    return jnp.tile(cos, (1, nh)), jnp.tile(sin, (1, nh))


def _block_ones(n, scale):
    idx = np.arange(n) // HEAD_DIM
    return jnp.asarray((idx[:, None] == idx[None, :]).astype(np.float32) * scale)


def kernel(x, c, ctx, c_ctx, ada_w, ada_b, norm_mix_pre, norm_mix_post, norm_ffn_pre, norm_ffn_post, w_in, conv_a, na_bias, q_norm, k_norm, rw_mu, rw_w0, rw_w2, rw_a0, rw_a2, rw_kk, rw_ka, rw_rk, rw_g2, rw_ln_w, rw_ln_b, w_branch, w_gate, b_gate, w_o, ffn_up, ffn_conv, ffn_down):
    b, seq, _ = x.shape
    ctx_len = ctx.shape[1]
    depth = ada_w.shape[0]
    assert ctx_len == TM and seq % GQA_TQ == 0 and seq % TML == 0 and seq // TM >= 3 and b + 1 <= 8
    ct = seq // TM
    rows = seq // GRID_W

    cvec = jnp.zeros((8, D_MODEL), F32).at[:b].set(c).at[b].set(c_ctx)
    mods = _ada(cvec, ada_w, ada_b)
    cos_t, sin_t = _rope_tables(seq, ctx_len)
    bd384 = _block_ones(384, 1.0 / HEAD_DIM).astype(BF16)
    bd256 = _block_ones(256, 1.0).astype(BF16)

    def hi_lo(w):
        hi = w.astype(BF16)
        return jnp.stack([hi, (w - hi.astype(F32)).astype(BF16)], axis=-3)

    xa = jnp.concatenate([x, ctx], axis=1)
    for l in range(depth):
        ml = mods[l].reshape(8, 6, D_MODEL)
        modsel = jnp.stack([ml[:b], jnp.broadcast_to(ml[b][None], (b, 6, D_MODEL))], axis=1)
        row2 = lambda a: a.reshape(1, -1)
        gqk = jnp.concatenate([jnp.tile(q_norm[l], GQA_Q_HEADS) * (HEAD_DIM ** -0.5 * np.log2(np.e)),
                               jnp.tile(k_norm[l], GQA_KV_HEADS)]).reshape(1, -1)
        ip_args = (xa, modsel, row2(norm_mix_pre[l]), w_in[l].astype(BF16), cos_t, sin_t, gqk, bd384, seq)
        za, qn, kn, vn, qg, kg, vg, zr = _inproj(*ip_args, ctx_len, prev=_inproj(*ip_args, TML))
        yb = _na(qn, kn, vn, _na_bias_table(na_bias[l], rows), ct)
        yc = _gqa(qg, kg, vg, seq, ctx_len)
        g, h, rp, yl, bonus = _rwkv_chunks(zr, rw_mu[l], rw_w0[l], hi_lo(rw_w2[l]), rw_a0[l], hi_lo(rw_a2[l]),
                                           row2(rw_kk[l]), row2(rw_ka[l]), row2(rw_rk[l]), bd256, ct)
        yf, yr = _rwkv_chain(g, h, rp, yl, seq, ctx_len)
        mg_args = (xa, modsel, row2(norm_mix_pre[l]), row2(norm_mix_post[l]), za, conv_a[l], yb, yc, yf, yr,
                   bonus, zr, hi_lo(rw_g2[l]), row2(rw_ln_w[l]), row2(rw_ln_b[l]), bd256,
                   w_branch[l].astype(BF16), (0.5 * w_gate[l]).astype(BF16), row2(0.5 * b_gate[l]),
                   (0.5 * w_o[l]).astype(BF16), seq)
        xm = _merge(*mg_args, ctx_len, prev=_merge(*mg_args, TML))
        conv_w = jnp.concatenate([ffn_conv[l][:, :D_FF], 0.5 * ffn_conv[l][:, D_FF:]], axis=1)
        ff_args = (xm, modsel, row2(norm_ffn_pre[l]), row2(norm_ffn_post[l]), ffn_up[l].astype(BF16),
                   conv_w, ffn_down[l].astype(BF16), seq)
        if l < depth - 1:
            xa = _ffn(*ff_args, ctx_len, prev=_ffn(*ff_args, TML))
        else:
            xa = _ffn(*ff_args, TML, out_rows=seq)
    return xa
```

```python
import functools

import numpy as np
import jax
import jax.numpy as jnp
from jax import lax
from jax.experimental import pallas as pl
from jax.experimental.pallas import tpu as pltpu

F32 = jnp.float32
BF16 = jnp.bfloat16
HIGHEST = lax.Precision.HIGHEST

D_MODEL = 1024
GRID_W = 64
HEAD_DIM = 64
NA_HEADS = 4
NA_WIN_R = 8
NA_WIN_C = 16
GQA_Q_HEADS = 4
GQA_KV_HEADS = 2
ROPE_THETA = 10000.0
RWKV_HEADS = 4
RW = RWKV_HEADS * HEAD_DIM
LORA_W = 64
LORA_A = 64
LORA_G = 128
RW_SHIFT = 3 * RW + LORA_W + LORA_A
D_FF = 2816
NORM_EPS = 1e-6
LN_X_EPS = 64e-5
COLS_A = 768
COLS_NA = 768
COLS_GQA = 512
COLS_RW = 1024
D_IN = COLS_A + COLS_NA + COLS_GQA + COLS_RW

TM = 256
TML = 512
CH = 64
HALO = 8
FF_CHUNK = 256
GQA_TQ = 512
GQA_TK = 512
NEG = -1e30
VMEM_LIMIT = 56 * 1024 * 1024


def _cparams(sem):
    return pltpu.CompilerParams(dimension_semantics=sem, vmem_limit_bytes=VMEM_LIMIT)


def _const_spec(shape):
    nd = len(shape)
    return pl.BlockSpec(shape, lambda *_: (0,) * nd, pipeline_mode=pl.Buffered(1))


def _token_tiling(seq, tm, ctx_mode, n_out):
    nblk = seq // tm
    blk = (lambda i: i * 0 + nblk) if ctx_mode else (lambda i: i)
    alias_specs = [pl.BlockSpec(memory_space=pl.ANY)] * n_out if ctx_mode else []
    return nblk, blk, (1 if ctx_mode else nblk), alias_specs


def _halo_specs(width, tm, blk, nhb):
    per = tm // HALO
    return [pl.BlockSpec((1, HALO, width), lambda bi, i: (bi, jnp.maximum(blk(i) * per - 1, 0), 0)),
            pl.BlockSpec((1, HALO, width), lambda bi, i: (bi, jnp.minimum((blk(i) + 1) * per, nhb - 1), 0))]


def _edge_valid(i, nblk, ctx_mode):
    if ctx_mode:
        return 0.0, 0.0
    return jnp.where(i != 0, 1.0, 0.0), jnp.where(i != nblk - 1, 1.0, 0.0)


def _dg(a, b, ca, cb, **kw):
    return lax.dot_general(a, b, (((ca,), (cb,)), ((), ())), preferred_element_type=F32, **kw)


def _sum_dot(x, w, terms):
    acc = None
    for _ in range(terms):
        piece = x.astype(BF16)
        part = jnp.dot(piece, w, preferred_element_type=F32)
        acc = part if acc is None else acc + part
        x = x - piece.astype(F32)
    return acc


def _dot3(x, w_hi, w_lo):
    x_hi = x.astype(BF16)
    x_lo = (x - x_hi.astype(F32)).astype(BF16)
    return (jnp.dot(x_hi, w_hi, preferred_element_type=F32) + jnp.dot(x_lo, w_hi, preferred_element_type=F32)
            + jnp.dot(x_hi, w_lo, preferred_element_type=F32))


def _rms_mod(x, g, shift, scale):
    y = x * lax.rsqrt(jnp.mean(x * x, axis=-1, keepdims=True) + NORM_EPS) * g
    return y * (1.0 + scale) + shift


def _rms(x, g):
    return x * lax.rsqrt(jnp.mean(x * x, axis=-1, keepdims=True) + NORM_EPS) * g


def _sigmoid(x):
    return 0.5 * jnp.tanh(0.5 * x) + 0.5


def _ada_kernel(c_ref, w_ref, b_ref, o_ref):
    c = c_ref[...]
    s = c * _sigmoid(c)
    o_ref[0] = jnp.dot(s, w_ref[0], precision=HIGHEST, preferred_element_type=F32) + b_ref[0]


def _ada(cvec, ada_w, ada_b):
    depth = ada_w.shape[0]
    nblk = ada_w.shape[2] // D_MODEL
    return pl.pallas_call(
        _ada_kernel,
        grid=(depth, nblk),
        in_specs=[
            pl.BlockSpec((8, D_MODEL), lambda l, j: (0, 0)),
            pl.BlockSpec((1, D_MODEL, D_MODEL), lambda l, j: (l, 0, j)),
            pl.BlockSpec((1, 1, D_MODEL), lambda l, j: (l, 0, j)),
        ],
        out_specs=pl.BlockSpec((1, 8, D_MODEL), lambda l, j: (l, 0, j)),
        out_shape=jax.ShapeDtypeStruct((depth, 8, ada_w.shape[2]), F32),
        compiler_params=_cparams(("parallel", "parallel")),
        name="ada_mod",
    )(cvec, ada_w, ada_b.reshape(depth, 1, -1))


def _inproj_kernel(*refs):
    x_ref, mod_ref, g_ref, w_ref, cos_ref, sin_ref, gqk_ref, bd_ref = refs[:8]
    za_ref, qn_ref, kn_ref, vn_ref, qg_ref, kg_ref, vg_ref, zr_ref = refs[-8:]
    tm = x_ref.shape[1]
    x = x_ref[0]
    m = mod_ref[0, 0]
    hb = _rms_mod(x, g_ref[...], m[0:1], m[1:2]).astype(BF16)
    e1, e2, e3 = COLS_A, COLS_A + COLS_NA, COLS_A + COLS_NA + COLS_GQA
    g = jnp.dot(hb, w_ref[:, e2:e3], preferred_element_type=F32)
    na = jnp.dot(hb, w_ref[:, e1:e2], preferred_element_type=F32)
    scale = HEAD_DIM ** -0.5
    for hd in range(NA_HEADS):
        qn_ref[0, hd] = (na[:, hd * 64:(hd + 1) * 64] * scale).astype(BF16)
        kn_ref[0, hd] = na[:, 256 + hd * 64:256 + (hd + 1) * 64].astype(BF16)
        vn_ref[0, hd] = na[:, 512 + hd * 64:512 + (hd + 1) * 64].astype(BF16)
    qk = g[:, :384]
    ms = _sum_dot(qk * qk, bd_ref[...], 2)
    za_ref[0] = jnp.dot(hb, w_ref[:, :e1], preferred_element_type=F32)
    zr_ref[0] = jnp.dot(hb, w_ref[:, e3:], preferred_element_type=F32)
    qkn = qk * lax.rsqrt(ms + NORM_EPS) * gqk_ref[...]
    lane = lax.broadcasted_iota(jnp.int32, (tm, 128), 1)
    even = (lane & 1) == 0
    parts = []
    for j in range(3):
        s = qkn[:, j * 128:(j + 1) * 128]
        sw = jnp.where(even, pltpu.roll(s, 127, 1), pltpu.roll(s, 1, 1))
        parts.append(s * cos_ref[:, j * 128:(j + 1) * 128] + sw * sin_ref[:, j * 128:(j + 1) * 128])
    qg_ref[0] = jnp.concatenate([parts[0].T, parts[1].T], axis=0).astype(BF16)
    for hd in range(GQA_KV_HEADS):
        kg_ref[0, hd] = parts[2][:, hd * 64:(hd + 1) * 64].astype(BF16)
    vt = g[:, 384:512]
    for hd in range(GQA_KV_HEADS):
        vh = vt if hd == 0 else pltpu.roll(vt, 64, 1)
        vext = jnp.where(lane < 64, vh, jnp.where(lane == 64, 1.0, 0.0))
        vg_ref[0, hd] = vext.T.astype(BF16)


def _inproj(xa, modsel, g_pre, w_in, cos_t, sin_t, gqk, bd384, seq, tm, prev=None):
    b, tt, _ = xa.shape
    ctx_mode = prev is not None
    nblk, blk, steps, alias_specs = _token_tiling(seq, tm, ctx_mode, 8)
    tile = lambda w: pl.BlockSpec((1, tm, w), lambda bi, i: (bi, blk(i), 0))
    heads = lambda nh: pl.BlockSpec((1, nh, tm, 64), lambda bi, i: (bi, 0, blk(i), 0))
    hs = lambda nh: jax.ShapeDtypeStruct((b, nh, tt, 64), BF16)
    return pl.pallas_call(
        _inproj_kernel,
        grid=(b, steps),
        in_specs=[
            tile(D_MODEL),
            pl.BlockSpec((1, 1, 6, D_MODEL), lambda bi, i: (bi, int(ctx_mode), 0, 0)),
            _const_spec((1, D_MODEL)),
            _const_spec((D_MODEL, D_IN)),
            pl.BlockSpec((tm, 384), lambda bi, i: (blk(i), 0)),
            pl.BlockSpec((tm, 384), lambda bi, i: (blk(i), 0)),
            _const_spec((1, 384)),
            _const_spec((384, 384)),
        ] + alias_specs,
        out_specs=[tile(COLS_A), heads(4), heads(4), heads(4),
                   pl.BlockSpec((1, GQA_Q_HEADS * HEAD_DIM, tm), lambda bi, i: (bi, 0, blk(i))),
                   heads(2),
                   pl.BlockSpec((1, 2, 128, tm), lambda bi, i: (bi, 0, 0, blk(i))), tile(COLS_RW)],
        out_shape=[jax.ShapeDtypeStruct((b, tt, COLS_A), F32), hs(4), hs(4), hs(4),
                   jax.ShapeDtypeStruct((b, GQA_Q_HEADS * HEAD_DIM, tt), BF16),
                   hs(2),
                   jax.ShapeDtypeStruct((b, 2, 128, tt), BF16),
                   jax.ShapeDtypeStruct((b, tt, COLS_RW), F32)],
        input_output_aliases={8 + k: k for k in range(8)} if ctx_mode else {},
        compiler_params=_cparams(("parallel", "parallel")),
        name="inproj_ctx" if ctx_mode else "inproj",
    )(xa, modsel, g_pre, w_in, cos_t, sin_t, gqk, bd384, *(prev or ()))


def _na_kernel(q_ref, k0_ref, k1_ref, k2_ref, kc_ref, v0_ref, v1_ref, v2_ref, vc_ref, bias_ref, o_ref):
    scores = []
    for hd in range(NA_HEADS):
        kcat = jnp.concatenate([k0_ref[0, hd], k1_ref[0, hd], k2_ref[0, hd], kc_ref[0, hd]], axis=0)
        scores.append(_dg(q_ref[0, hd], kcat, 1, 1))
    outs = []
    for hd in range(NA_HEADS):
        s = scores[hd]
        s_loc = s[:, :3 * TM] + bias_ref[0, hd]
        s_ctx = s[:, 3 * TM:]
        m = jnp.maximum(jnp.max(s_loc, axis=-1, keepdims=True), jnp.max(s_ctx, axis=-1, keepdims=True))
        p_loc = jnp.exp(s_loc - m)
        p_ctx = jnp.exp(s_ctx - m)
        l = jnp.sum(p_loc, axis=-1, keepdims=True) + jnp.sum(p_ctx, axis=-1, keepdims=True)
        vloc = jnp.concatenate([v0_ref[0, hd], v1_ref[0, hd], v2_ref[0, hd]], axis=0)
        o = _dg(p_loc.astype(BF16), vloc, 1, 0) + _dg(p_ctx.astype(BF16), vc_ref[0, hd], 1, 0)
        outs.append(o / l)
    o_ref[0] = jnp.concatenate(outs, axis=-1).astype(o_ref.dtype)


def _na(qn, kn, vn, bias_tab, ct):
    b, _, tt, _ = qn.shape
    nt = tt // TM

    def kv_spec(j):
        if j is None:
            return pl.BlockSpec((1, 4, TM, 64), lambda bi, i: (bi, 0, ct, 0))
        return pl.BlockSpec((1, 4, TM, 64), lambda bi, i: (bi, 0, jnp.clip(i - 1, 0, ct - 3) + j, 0))

    def pattern(i):
        return jnp.where(i == ct, 3, jnp.where(i == 0, 0, jnp.where(i == ct - 1, 2, 1)))

    return pl.pallas_call(
        _na_kernel,
        grid=(b, nt),
        in_specs=[
            pl.BlockSpec((1, 4, TM, 64), lambda bi, i: (bi, 0, i, 0)),
            kv_spec(0), kv_spec(1), kv_spec(2), kv_spec(None),
            kv_spec(0), kv_spec(1), kv_spec(2), kv_spec(None),
            pl.BlockSpec((1, 4, TM, 3 * TM), lambda bi, i: (pattern(i), 0, 0, 0)),
        ],
        out_specs=pl.BlockSpec((1, TM, 256), lambda bi, i: (bi, i, 0)),
        out_shape=jax.ShapeDtypeStruct((b, tt, 256), BF16),
        compiler_params=_cparams(("parallel", "parallel")),
        name="na_attn",
    )(qn, kn, kn, kn, kn, vn, vn, vn, vn, bias_tab)


def _na_bias_table(na_bias_l, rows):
    ct = rows * GRID_W // TM
    rpt = TM // GRID_W
    wr = min(NA_WIN_R, rows)
    qj = np.arange(GRID_W)
    kc = np.arange(GRID_W)
    cs = np.clip(qj - NA_WIN_C // 2, 0, GRID_W - NA_WIN_C)
    colvalid = (kc[None, :] >= cs[:, None]) & (kc[None, :] < cs[:, None] + NA_WIN_C)
    dc = kc[None, :] - qj[:, None] + (NA_WIN_C - 1)
    onehot = (dc.reshape(1, -1) == np.arange(2 * NA_WIN_C - 1)[:, None]) & colvalid.reshape(1, -1)
    toep = jnp.einsum("hrd,dx->hrx", na_bias_l, jnp.asarray(onehot.astype(np.float32)), precision=HIGHEST)
    toep = jnp.where(jnp.asarray(colvalid.reshape(-1)), toep, NEG)
    toep = toep.reshape(NA_HEADS, 2 * NA_WIN_R - 1, GRID_W, GRID_W)
    neg_blk = jnp.full((NA_HEADS, GRID_W, GRID_W), NEG, F32)
    tabs = []
    for tile_i in (0, 1, ct - 1):
        i0 = tile_i * rpt
        kb = int(np.clip(tile_i - 1, 0, ct - 3)) * rpt
        qrows = []
        for ri in range(rpt):
            qi = i0 + ri
            rs = int(np.clip(qi - wr // 2, 0, rows - wr))
            blks = [toep[:, kb + m - qi + NA_WIN_R - 1] if rs <= kb + m < rs + wr else neg_blk
                    for m in range(3 * rpt)]
            qrows.append(jnp.concatenate(blks, axis=-1))
        tabs.append(jnp.concatenate(qrows, axis=-2))
    tabs.append(jnp.full_like(tabs[0], NEG))
    return jnp.stack(tabs, axis=0)


GQA_VROWS = 80
GQA_CB = 256
GQA_UNROLL = 8
GQA_AHEAD = 3


def _gqa_kernel(q_ref, k_ref, v_ref, o_ref, *scratch, n_full, tail, tq):
    nblk = 2 * tq // GQA_CB
    m_refs, acc_refs = scratch[:nblk], scratch[nblk:]
    qt = jnp.concatenate([q_ref[0, :HEAD_DIM, :], q_ref[0, HEAD_DIM:, :]], axis=1)
    for n in range(nblk):
        m_refs[n][...] = jnp.full(m_refs[n].shape, NEG, F32)
        acc_refs[n][...] = jnp.zeros(acc_refs[n].shape, F32)

    def chunks(spans):
        kcs = [k_ref[0, 0, pl.ds(start, size), :] for start, size in spans]
        vts = [v_ref[0, 0, :GQA_VROWS, pl.ds(start, size)] for start, size in spans]
        items = [(c, n) for c in range(len(spans)) for n in range(nblk)]
        score = lambda c, n: jnp.dot(kcs[c], qt[:, n * GQA_CB:(n + 1) * GQA_CB], preferred_element_type=F32)
        ahead = {i: score(*items[i]) for i in range(min(GQA_AHEAD, len(items)))}
        for i, (c, n) in enumerate(items):
            if i + GQA_AHEAD < len(items):
                ahead[i + GQA_AHEAD] = score(*items[i + GQA_AHEAD])
            st = ahead.pop(i)
            m_old = m_refs[n][...]
            m_new = jnp.maximum(m_old, jnp.max(st, axis=0, keepdims=True))
            pt = jnp.exp2(st - m_new).astype(BF16)
            acc_refs[n][...] = (jnp.exp2(m_old - m_new) * acc_refs[n][...]
                                + jnp.dot(vts[c], pt, preferred_element_type=F32))
            m_refs[n][...] = m_new

    n_trips = n_full // GQA_UNROLL
    if n_trips > 0:
        def body(j, carry):
            base = j * (GQA_UNROLL * GQA_TK)
            chunks([(pl.multiple_of(base + u * GQA_TK, GQA_TK), GQA_TK) for u in range(GQA_UNROLL)])
            return carry
        lax.fori_loop(0, n_trips, body, 0)
    rest = [(c * GQA_TK, GQA_TK) for c in range(n_trips * GQA_UNROLL, n_full)]
    if tail > 0:
        rest.append((n_full * GQA_TK, tail))
    if rest:
        chunks(rest)
    acc = jnp.concatenate([r[...] for r in acc_refs], axis=1)
    ot = acc[:HEAD_DIM] * (1.0 / acc[HEAD_DIM:HEAD_DIM + 1])
    ot = jnp.concatenate([ot, jnp.zeros_like(ot)], axis=0)
    o = ot.T
    o_ref[0] = jnp.concatenate([o[:tq, :HEAD_DIM], o[tq:, :HEAD_DIM]], axis=-1).astype(o_ref.dtype)


def _gqa(qg, kg, vg, seq, ctx_len):
    b, _, tt = qg.shape
    scratch = lambda tq: ([pltpu.VMEM((1, GQA_CB), F32)] * (2 * tq // GQA_CB)
                          + [pltpu.VMEM((GQA_VROWS, GQA_CB), F32)] * (2 * tq // GQA_CB))
    y_lat = pl.pallas_call(
        functools.partial(_gqa_kernel, n_full=seq // GQA_TK, tail=ctx_len, tq=GQA_TQ),
        grid=(b, GQA_KV_HEADS, seq // GQA_TQ),
        in_specs=[
            pl.BlockSpec((1, 2 * HEAD_DIM, GQA_TQ), lambda bi, n, i: (bi, n, i)),
            pl.BlockSpec((1, 1, tt, HEAD_DIM), lambda bi, n, i: (bi, n, 0, 0)),
            pl.BlockSpec((1, 1, 128, tt), lambda bi, n, i: (bi, n, 0, 0)),
        ],
        out_specs=pl.BlockSpec((1, GQA_TQ, 128), lambda bi, n, i: (bi, i, n)),
        out_shape=jax.ShapeDtypeStruct((b, seq, 256), BF16),
        scratch_shapes=scratch(GQA_TQ),
        compiler_params=_cparams(("parallel", "parallel", "parallel")),
        name="gqa_latent",
    )(qg, kg, vg)
    cblk = seq // ctx_len
    y_ctx = pl.pallas_call(
        functools.partial(_gqa_kernel, n_full=0, tail=ctx_len, tq=ctx_len),
        grid=(b, GQA_KV_HEADS),
        in_specs=[
            pl.BlockSpec((1, 2 * HEAD_DIM, ctx_len), lambda bi, n: (bi, n, cblk)),
            pl.BlockSpec((1, 1, ctx_len, HEAD_DIM), lambda bi, n: (bi, n, cblk, 0)),
            pl.BlockSpec((1, 1, 128, ctx_len), lambda bi, n: (bi, n, 0, cblk)),
        ],
        out_specs=pl.BlockSpec((1, ctx_len, 128), lambda bi, n: (bi, 0, n)),
        out_shape=jax.ShapeDtypeStruct((b, ctx_len, 256), BF16),
        scratch_shapes=scratch(ctx_len),
        compiler_params=_cparams(("parallel", "parallel")),
        name="gqa_context",
    )(qg, kg, vg)
    return jnp.concatenate([y_lat, y_ctx], axis=1)


def _bd(x):
    left = lax.broadcasted_iota(jnp.int32, (CH, 128), 1) < HEAD_DIM
    x0, x1 = x[:, :128], x[:, 128:]
    z = jnp.zeros_like(x0)
    keep_l = lambda a: jnp.where(left, a, z)
    keep_r = lambda a: jnp.where(left, z, a)
    return jnp.concatenate([jnp.concatenate([keep_l(x0), z], axis=1), jnp.concatenate([keep_r(x0), z], axis=1),
                            jnp.concatenate([z, keep_l(x1)], axis=1), jnp.concatenate([z, keep_r(x1)], axis=1)],
                           axis=0)


def _fold(x):
    left = lax.broadcasted_iota(jnp.int32, (CH, 128), 1) < HEAD_DIM
    return jnp.concatenate([jnp.where(left, x[0:64, :128], x[64:128, :128]),
                            jnp.where(left, x[128:192, 128:], x[192:256, 128:])], axis=1)


def _chunk_mats(items):
    t = lax.broadcasted_iota(jnp.int32, (CH, RW), 0)
    j = lax.broadcasted_iota(jnp.int32, (CH, RW), 1) & 63
    eye = j == t
    before = {False: j < t, True: j > t}
    incl = {False: j <= t, True: j >= t}
    bf = lambda x: x.astype(BF16)
    n_items = range(len(items))

    lb, mb, lk, mk = [], [], [], []
    for it in items:
        a2 = bf(jnp.concatenate([it["kap"], it["rho"]], axis=0))
        lm_b = _dg(a2, _bd(bf(it["bt"])), 1, 1)
        lm_k = _dg(a2, _bd(bf(it["kt"])), 1, 1)
        lb.append(jnp.where(before[it["rev"]], lm_b[:CH], 0.0))
        mb.append(jnp.where(incl[it["rev"]], lm_b[CH:], 0.0))
        lk.append(jnp.where(before[it["rev"]], lm_k[:CH], 0.0))
        mk.append(jnp.where(incl[it["rev"]], lm_k[CH:], 0.0))

    p, mpow = [], []
    for i in n_items:
        nb = bf(-lb[i])
        p.append(jnp.where(eye, 1.0, 0.0) - lb[i])
        mpow.append(_dg(nb, _bd(nb), 1, 0))
    for _ in range(4):
        for i in n_items:
            mbf = bf(mpow[i])
            pm = _dg(jnp.concatenate([bf(p[i]), mbf], axis=0), _bd(mbf), 1, 0)
            p[i] = p[i] + pm[:CH]
            mpow[i] = pm[CH:]
    tinv = [bf(p[i] + _dg(bf(p[i]), _bd(bf(mpow[i])), 1, 0)) for i in n_items]

    lmv = [_dg(bf(jnp.concatenate([lk[i], mk[i]], axis=0)), _bd(bf(items[i]["v"])), 1, 0) for i in n_items]
    kp = [_dg(tinv[i], _bd(bf(items[i]["kap"])), 1, 0) for i in n_items]
    vp = [_dg(tinv[i], _bd(bf(lmv[i][:CH])), 1, 0) for i in n_items]
    out = []
    for i in n_items:
        it = items[i]
        mbb = bf(mb[i])
        rp = it["rho"] - _dg(mbb, _bd(bf(kp[i])), 1, 0)
        yl = lmv[i][CH:] - _dg(mbb, _bd(bf(vp[i])), 1, 0)
        g = jnp.where(eye, it["gdiag"], 0.0) - _fold(_dg(bf(kp[i]), bf(it["bhat"]), 0, 0))
        hmat = _fold(_dg(bf(jnp.concatenate([it["v"], -vp[i]], axis=0)),
                         bf(jnp.concatenate([it["khat"], it["bhat"]], axis=0)), 0, 0))
        out.append((g, hmat, rp, yl))
    return out


def _rwkv_chunk_kernel(z_ref, hp_ref, hn_ref, mu_ref, w0_ref, w2_ref, a0_ref, a2_ref, kkw_ref, ka_ref, rk_ref,
                       bd_ref, g_ref, h_ref, rp_ref, yl_ref, bonus_ref, *, ct):
    i = pl.program_id(1)
    z = z_ref[0]
    zs = z[:, :RW_SHIFT]
    valid_prev = jnp.where((i != 0) & (i != ct), 1.0, 0.0)
    valid_next = jnp.where((i != ct - 1) & (i != ct), 1.0, 0.0)
    prev_row = hp_ref[0, HALO - 1:HALO, :RW_SHIFT] * valid_prev
    next_row = hn_ref[0, 0:1, :RW_SHIFT] * valid_next
    row = lax.broadcasted_iota(jnp.int32, (TM, RW_SHIFT), 0)
    tt = lax.broadcasted_iota(jnp.int32, (TM, TM), 0)
    jj = lax.broadcasted_iota(jnp.int32, (TM, TM), 1)
    same_chunk = (tt >> 6) == (jj >> 6)
    bd = bd_ref[...]
    bonus = None
    items = []
    for d in range(2):
        if d == 0:
            nb = jnp.where(row == 0, prev_row, pltpu.roll(zs, 1, 0))
        else:
            nb = jnp.where(row == TM - 1, next_row, pltpu.roll(zs, TM - 1, 0))
        zd = zs + mu_ref[d:d + 1, :] * (nb - zs)
        r = zd[:, :RW]
        k = zd[:, RW:2 * RW]
        v = zd[:, 2 * RW:3 * RW]
        lw = zd[:, 3 * RW:3 * RW + LORA_W]
        la = zd[:, 3 * RW + LORA_W:]
        w_log = w0_ref[d:d + 1, :] + _dot3(jnp.tanh(lw), w2_ref[d, 0], w2_ref[d, 1])
        sp = jnp.maximum(-w_log, 0.0) + jnp.log(1.0 + jnp.exp(-jnp.abs(w_log)))
        logw = -jnp.exp(-sp - 0.5)
        a = _sigmoid(a0_ref[d:d + 1, :] + _dot3(la, a2_ref[d, 0], a2_ref[d, 1]))
        kkr = k * kkw_ref[...]
        ss = _sum_dot(kkr * kkr, bd, 2)
        kk = kkr * lax.rsqrt(jnp.maximum(ss, 1e-24))
        k2 = k * (1.0 + (a - 1.0) * ka_ref[...])
        bv = kk * a
        bon = _sum_dot(r * k2 * rk_ref[...], bd, 2) * v
        bonus = bon if bonus is None else bonus + bon
        tri = jnp.where(same_chunk & ((jj >= tt) if d == 1 else (jj <= tt)), 1.0, 0.0).astype(BF16)
        cum, rest = None, logw
        for _ in range(3):
            piece = rest.astype(BF16)
            part = jnp.dot(tri, piece, preferred_element_type=F32)
            cum = part if cum is None else cum + part
            rest = rest - piece.astype(F32)
        e_neg = jnp.exp(-cum)
        streams = dict(kap=kk * jnp.exp(cum - logw), kt=k2 * e_neg, bt=bv * e_neg, rho=r * jnp.exp(cum), v=v)
        for c in range(TM // CH):
            sl = slice(c * CH, (c + 1) * CH)
            it = {name: val[sl] for name, val in streams.items()}
            last = c * CH if d == 1 else (c + 1) * CH - 1
            tot = cum[last:last + 1]
            e_tot = jnp.exp(tot - cum[sl])
            it.update(khat=k2[sl] * e_tot, bhat=bv[sl] * e_tot, gdiag=jnp.exp(tot), rev=d == 1)
            items.append(it)
    for idx, (g, hm, rp, yl) in enumerate(_chunk_mats(items)):
        d, c = divmod(idx, TM // CH)
        g_ref[0, d, c] = g.astype(BF16)
        h_ref[0, d, c] = hm
        rp_ref[0, d, c] = rp.astype(BF16)
        yl_ref[0, d, c] = yl
    bonus_ref[0] = bonus


def _rwkv_chunks(zr, mu, w0, w2, a0, a2, kkw, ka, rk, bd256, ct):
    b, tt, _ = zr.shape
    nt = tt // TM
    nch = tt // CH
    cpt = TM // CH
    nhb = tt // HALO
    mats = pl.BlockSpec((1, 2, cpt, CH, RW), lambda bi, i: (bi, 0, i, 0, 0))
    mshape = lambda dt: jax.ShapeDtypeStruct((b, 2, nch, CH, RW), dt)
    return pl.pallas_call(
        functools.partial(_rwkv_chunk_kernel, ct=ct),
        grid=(b, nt),
        in_specs=[
            pl.BlockSpec((1, TM, COLS_RW), lambda bi, i: (bi, i, 0)),
            pl.BlockSpec((1, HALO, COLS_RW), lambda bi, i: (bi, jnp.maximum(i * (TM // HALO) - 1, 0), 0)),
            pl.BlockSpec((1, HALO, COLS_RW), lambda bi, i: (bi, jnp.minimum((i + 1) * (TM // HALO), nhb - 1), 0)),
            _const_spec((2, RW_SHIFT)),
            _const_spec((2, RW)),
            _const_spec((2, 2, LORA_W, RW)),
            _const_spec((2, RW)),
            _const_spec((2, 2, LORA_A, RW)),
            _const_spec((1, RW)),
            _const_spec((1, RW)),
            _const_spec((1, RW)),
            _const_spec((RW, RW)),
        ],
        out_specs=[mats, mats, mats, mats, pl.BlockSpec((1, TM, RW), lambda bi, i: (bi, i, 0))],
        out_shape=[mshape(BF16), mshape(F32), mshape(BF16), mshape(F32), jax.ShapeDtypeStruct((b, tt, RW), F32)],
        compiler_params=_cparams(("parallel", "parallel")),
        name="rwkv_chunks",
    )(zr, zr, zr, mu, w0, w2, a0, a2, kkw, ka, rk, bd256)


CHAIN_GROUP = 4


def _rwkv_chain_kernel(gf_ref, hf_ref, rf_ref, yf_ref, gr_ref, hr_ref, rr_ref, yr_ref, of_ref, or_ref, s_ref,
                       *, nb):
    @pl.when(pl.program_id(0) == 0)
    def _():
        s_ref[...] = jnp.zeros(s_ref.shape, F32)

    dirs = ((gf_ref, hf_ref, rf_ref, yf_ref, of_ref), (gr_ref, hr_ref, rr_ref, yr_ref, or_ref))
    state = [[s_ref[d, bi] for bi in range(nb)] for d in range(2)]
    for step in range(CHAIN_GROUP):
        for d, (g_ref, h_ref, rp_ref, yl_ref, o_ref) in enumerate(dirs):
            c = step if d == 0 else CHAIN_GROUP - 1 - step
            for bi in range(nb):
                sb = state[d][bi].astype(BF16)
                o_ref[bi, c * CH:(c + 1) * CH, :] = _dg(rp_ref[bi, 0, c], _bd(sb), 1, 1) + yl_ref[bi, 0, c]
                state[d][bi] = _dg(sb, _bd(g_ref[bi, 0, c]), 1, 0) + h_ref[bi, 0, c]
    for d in range(2):
        for bi in range(nb):
            s_ref[d, bi] = state[d][bi]


def _rwkv_chain(g, h, rp, yl, seq, ctx_len):
    b, _, nch, _, _ = g.shape
    assert ctx_len == CHAIN_GROUP * CH and seq % (CHAIN_GROUP * CH) == 0
    ngrp = nch // CHAIN_GROUP
    n_lat = seq // (CHAIN_GROUP * CH)

    def gf(s):
        return jnp.where(s == 0, n_lat, s - 1)

    def gr(s):
        return ngrp - 1 - s

    fwd = pl.BlockSpec((b, 1, CHAIN_GROUP, CH, RW), lambda s: (0, 0, gf(s), 0, 0))
    rev = pl.BlockSpec((b, 1, CHAIN_GROUP, CH, RW), lambda s: (0, 1, gr(s), 0, 0))
    yshape = jax.ShapeDtypeStruct((b, nch * CH, RW), F32)
    return pl.pallas_call(
        functools.partial(_rwkv_chain_kernel, nb=b),
        grid=(ngrp,),
        in_specs=[fwd, fwd, fwd, fwd, rev, rev, rev, rev],
        out_specs=[pl.BlockSpec((b, CHAIN_GROUP * CH, RW), lambda s: (0, gf(s), 0)),
                   pl.BlockSpec((b, CHAIN_GROUP * CH, RW), lambda s: (0, gr(s), 0))],
        out_shape=[yshape, yshape],
        scratch_shapes=[pltpu.VMEM((2, b, CH, RW), F32)],
        compiler_params=_cparams(("arbitrary",)),
        name="rwkv_chain",
    )(g, h, rp, yl, g, h, rp, yl)


def _merge_kernel(*refs, nblk, ctx_mode):
    (x_ref, mod_ref, gpre_ref, gpost_ref, za_ref, hp_ref, hn_ref, ca_ref, yb_ref, yc_ref, yf_ref, yr_ref,
     bonus_ref, lg_ref, g2_ref, lnw_ref, lnb_ref, bd_ref, wb_ref, wg_ref, bg_ref, wo_ref) = refs[:22]
    o_ref = refs[-1]
    tm = x_ref.shape[1]
    x = x_ref[0]
    m = mod_ref[0, 0]
    hb = _rms_mod(x, gpre_ref[...], m[0:1], m[1:2]).astype(BF16)

    def gate_mm(bidx):
        return jnp.dot(hb, wg_ref[:, bidx * D_MODEL:(bidx + 1) * D_MODEL], preferred_element_type=F32)

    def gated(bidx, pre, ys):
        gate2 = jnp.tanh(pre + bg_ref[:, bidx * D_MODEL:(bidx + 1) * D_MODEL]) + 1.0
        return gate2 * jnp.dot(ys, wb_ref[bidx], preferred_element_type=F32)

    za = za_ref[0]
    u = za[:, 256:512] * za[:, 512:768]
    valid_prev, valid_next = _edge_valid(pl.program_id(1), nblk, ctx_mode)
    up = hp_ref[0, HALO - 1:HALO, 256:512] * hp_ref[0, HALO - 1:HALO, 512:768] * valid_prev
    un = hn_ref[0, 0:1, 256:512] * hn_ref[0, 0:1, 512:768] * valid_next
    row = lax.broadcasted_iota(jnp.int32, (tm, 256), 0)
    u_prev = jnp.where(row == 0, up, pltpu.roll(u, 1, 0))
    u_next = jnp.where(row == tm - 1, un, pltpu.roll(u, tm - 1, 0))
    ya = za[:, :256] * (u_prev * ca_ref[0:1, :] + u * ca_ref[1:2, :] + u_next * ca_ref[2:3, :])

    bd = bd_ref[...]
    ones_dot = lambda piece: jnp.dot(piece, bd, preferred_element_type=F32)
    y = yf_ref[0] + yr_ref[0]
    p1 = y.astype(BF16)
    s1 = ones_dot(p1)
    pre1 = gate_mm(1)
    r1 = y - p1.astype(F32)
    p2 = r1.astype(BF16)
    s2 = ones_dot(p2)
    acc = gated(1, pre1, yb_ref[0])
    s3 = ones_dot((r1 - p2.astype(F32)).astype(BF16))
    pre2 = gate_mm(2)
    yc0 = y - (s1 + s2 + s3) * (1.0 / HEAD_DIM)
    sq = yc0 * yc0
    q1 = sq.astype(BF16)
    v1 = ones_dot(q1)
    acc = acc + gated(2, pre2, yc_ref[0])
    v2 = ones_dot((sq - q1.astype(F32)).astype(BF16))
    pre0 = gate_mm(0)
    lgate = _dot3(_sigmoid(lg_ref[0]), g2_ref[0], g2_ref[1])
    acc = acc + gated(0, pre0, ya.astype(BF16))
    pre3 = gate_mm(3)
    var = (v1 + v2) * (1.0 / HEAD_DIM)
    yn = yc0 * lax.rsqrt(var + LN_X_EPS) * lnw_ref[...] + lnb_ref[...] + bonus_ref[0]
    acc = acc + gated(3, pre3, (yn * lgate).astype(BF16))
    mo = jnp.dot(acc.astype(BF16), wo_ref[...], preferred_element_type=F32)
    o_ref[0] = x + m[2:3] * _rms(mo, gpost_ref[...])


def _merge(xa, modsel, g_pre, g_post, za, conv_a, yb, yc, yf, yr, bonus, zr, g2, ln_w, ln_b, bd256,
           w_branch, w_gate, b_gate, w_o, seq, tm, prev=None):
    b, tt, _ = xa.shape
    ctx_mode = prev is not None
    nblk, blk, steps, alias_specs = _token_tiling(seq, tm, ctx_mode, 1)
    tile = lambda w: pl.BlockSpec((1, tm, w), lambda bi, i: (bi, blk(i), 0))
    return pl.pallas_call(
        functools.partial(_merge_kernel, nblk=nblk, ctx_mode=ctx_mode),
        grid=(b, steps),
        in_specs=[
            tile(D_MODEL),
            pl.BlockSpec((1, 1, 6, D_MODEL), lambda bi, i: (bi, int(ctx_mode), 0, 0)),
            _const_spec((1, D_MODEL)),
            _const_spec((1, D_MODEL)),
            tile(COLS_A),
            *_halo_specs(COLS_A, tm, blk, tt // HALO),
            _const_spec((3, 256)),
            tile(256), tile(256), tile(256), tile(256), tile(256),
            pl.BlockSpec((1, tm, LORA_G), lambda bi, i: (bi, blk(i), RW_SHIFT // LORA_G)),
            _const_spec((2, LORA_G, RW)),
            _const_spec((1, RW)),
            _const_spec((1, RW)),
            _const_spec((RW, RW)),
            _const_spec((4, 256, D_MODEL)),
            _const_spec((D_MODEL, 4 * D_MODEL)),
            _const_spec((1, 4 * D_MODEL)),
            _const_spec((D_MODEL, D_MODEL)),
        ] + alias_specs,
        out_specs=tile(D_MODEL),
        out_shape=jax.ShapeDtypeStruct((b, tt, D_MODEL), F32),
        input_output_aliases={22: 0} if ctx_mode else {},
        compiler_params=_cparams(("parallel", "parallel")),
        name="merge_ctx" if ctx_mode else "merge",
    )(xa, modsel, g_pre, g_post, za, za, za, conv_a, yb, yc, yf, yr, bonus, zr, g2, ln_w, ln_b, bd256,
      w_branch, w_gate, b_gate, w_o, *((prev,) if ctx_mode else ()))


def _ffn_kernel(*refs, nblk, ctx_mode):
    x_ref, hp_ref, hn_ref, mod_ref, gpre_ref, gpost_ref, wu_ref, cw_ref, wd_ref = refs[:9]
    o_ref, act_ref = refs[-2:]
    tm = x_ref.shape[1]
    x = x_ref[0]
    m = mod_ref[0, 0]
    xx = jnp.concatenate([hp_ref[0], x, hn_ref[0]], axis=0)
    nrow = tm + 2 * HALO
    row = lax.broadcasted_iota(jnp.int32, (nrow, 1), 0)
    valid_prev, valid_next = _edge_valid(pl.program_id(1), nblk, ctx_mode)
    rowmask = jnp.where(row < HALO, valid_prev, jnp.where(row >= tm + HALO, valid_next, 1.0))
    hb = (_rms_mod(xx, gpre_ref[...], m[3:4], m[4:5]) * rowmask).astype(BF16)

    def conv(u, col):
        w = cw_ref[:, col:col + FF_CHUNK]
        c = (pltpu.roll(u, 1, 0) * w[0:1] + u * w[1:2] + pltpu.roll(u, nrow - 1, 0) * w[2:3])
        return c[HALO:HALO + tm]

    for j in range(D_FF // FF_CHUNK):
        ca = conv(jnp.dot(hb, wu_ref[:, j * FF_CHUNK:(j + 1) * FF_CHUNK], preferred_element_type=F32),
                  j * FF_CHUNK)
        cg = conv(jnp.dot(hb, wu_ref[:, D_FF + j * FF_CHUNK:D_FF + (j + 1) * FF_CHUNK],
                          preferred_element_type=F32), D_FF + j * FF_CHUNK)
        act_ref[:, j * FF_CHUNK:(j + 1) * FF_CHUNK] = (ca * (cg * (1.0 + jnp.tanh(cg)))).astype(BF16)
    f = jnp.dot(act_ref[...], wd_ref[...], preferred_element_type=F32)
    o_ref[0] = x + m[5:6] * _rms(f, gpost_ref[...])


def _ffn(xa, modsel, g_pre, g_post, w_up, conv_w, w_down, seq, tm, prev=None, out_rows=None):
    b, tt, _ = xa.shape
    ctx_mode = prev is not None
    nblk, blk, steps, alias_specs = _token_tiling(seq, tm, ctx_mode, 1)
    return pl.pallas_call(
        functools.partial(_ffn_kernel, nblk=nblk, ctx_mode=ctx_mode),
        grid=(b, steps),
        in_specs=[
            pl.BlockSpec((1, tm, D_MODEL), lambda bi, i: (bi, blk(i), 0)),
            *_halo_specs(D_MODEL, tm, blk, tt // HALO),
            pl.BlockSpec((1, 1, 6, D_MODEL), lambda bi, i: (bi, int(ctx_mode), 0, 0)),
            _const_spec((1, D_MODEL)),
            _const_spec((1, D_MODEL)),
            _const_spec((D_MODEL, 2 * D_FF)),
            _const_spec((3, 2 * D_FF)),
            _const_spec((D_FF, D_MODEL)),
        ] + alias_specs,
        out_specs=pl.BlockSpec((1, tm, D_MODEL), lambda bi, i: (bi, blk(i), 0)),
        out_shape=jax.ShapeDtypeStruct((b, out_rows or tt, D_MODEL), F32),
        scratch_shapes=[pltpu.VMEM((tm, D_FF), BF16)],
        input_output_aliases={9: 0} if ctx_mode else {},
        compiler_params=_cparams(("parallel", "parallel")),
        name="conv_ffn_ctx" if ctx_mode else "conv_ffn",
    )(xa, xa, xa, modsel, g_pre, g_post, w_up, conv_w, w_down, *((prev,) if ctx_mode else ()))


def _rope_tables(seq, ctx_len):
    t = np.arange(seq)
    row = (t // GRID_W).astype(np.float32)
    col = (t % GRID_W).astype(np.float32)
    n_freq = HEAD_DIM // 4
    inv_freq = jnp.asarray(ROPE_THETA, F32) ** (-jnp.arange(n_freq, dtype=F32) / n_freq)
    ang = jnp.concatenate([jnp.asarray(row)[:, None] * inv_freq, jnp.asarray(col)[:, None] * inv_freq], axis=-1)
    cos = jnp.repeat(jnp.cos(ang), 2, axis=-1)
    sin = jnp.repeat(jnp.sin(ang), 2, axis=-1) * jnp.tile(jnp.asarray([-1.0, 1.0], F32), HEAD_DIM // 2)
    cos = jnp.concatenate([cos, jnp.ones((ctx_len, HEAD_DIM), F32)], axis=0)
    sin = jnp.concatenate([sin, jnp.zeros((ctx_len, HEAD_DIM), F32)], axis=0)
    nh = GQA_Q_HEADS + GQA_KV_HEADS
    return jnp.tile(cos, (1, nh)), jnp.tile(sin, (1, nh))


def _block_ones(n, scale):
    idx = np.arange(n) // HEAD_DIM
    return jnp.asarray((idx[:, None] == idx[None, :]).astype(np.float32) * scale)


def kernel(x, c, ctx, c_ctx, ada_w, ada_b, norm_mix_pre, norm_mix_post, norm_ffn_pre, norm_ffn_post, w_in, conv_a, na_bias, q_norm, k_norm, rw_mu, rw_w0, rw_w2, rw_a0, rw_a2, rw_kk, rw_ka, rw_rk, rw_g2, rw_ln_w, rw_ln_b, w_branch, w_gate, b_gate, w_o, ffn_up, ffn_conv, ffn_down):
    b, seq, _ = x.shape
    ctx_len = ctx.shape[1]
    depth = ada_w.shape[0]
    assert ctx_len == TM and seq % GQA_TQ == 0 and seq % TML == 0 and seq // TM >= 3 and b + 1 <= 8
    ct = seq // TM
    rows = seq // GRID_W

    cvec = jnp.zeros((8, D_MODEL), F32).at[:b].set(c).at[b].set(c_ctx)
    mods = _ada(cvec, ada_w, ada_b)
    cos_t, sin_t = _rope_tables(seq, ctx_len)
    bd384 = _block_ones(384, 1.0 / HEAD_DIM).astype(BF16)
    bd256 = _block_ones(256, 1.0).astype(BF16)

    def hi_lo(w):
        hi = w.astype(BF16)
        return jnp.stack([hi, (w - hi.astype(F32)).astype(BF16)], axis=-3)

    xa = jnp.concatenate([x, ctx], axis=1)
    for l in range(depth):
        ml = mods[l].reshape(8, 6, D_MODEL)
        modsel = jnp.stack([ml[:b], jnp.broadcast_to(ml[b][None], (b, 6, D_MODEL))], axis=1)
        row2 = lambda a: a.reshape(1, -1)
        gqk = jnp.concatenate([jnp.tile(q_norm[l], GQA_Q_HEADS) * (HEAD_DIM ** -0.5 * np.log2(np.e)),
                               jnp.tile(k_norm[l], GQA_KV_HEADS)]).reshape(1, -1)
        ip_args = (xa, modsel, row2(norm_mix_pre[l]), w_in[l].astype(BF16), cos_t, sin_t, gqk, bd384, seq)
        za, qn, kn, vn, qg, kg, vg, zr = _inproj(*ip_args, ctx_len, prev=_inproj(*ip_args, TML))
        yb = _na(qn, kn, vn, _na_bias_table(na_bias[l], rows), ct)
        yc = _gqa(qg, kg, vg, seq, ctx_len)
        g, h, rp, yl, bonus = _rwkv_chunks(zr, rw_mu[l], rw_w0[l], hi_lo(rw_w2[l]), rw_a0[l], hi_lo(rw_a2[l]),
                                           row2(rw_kk[l]), row2(rw_ka[l]), row2(rw_rk[l]), bd256, ct)
        yf, yr = _rwkv_chain(g, h, rp, yl, seq, ctx_len)
        mg_args = (xa, modsel, row2(norm_mix_pre[l]), row2(norm_mix_post[l]), za, conv_a[l], yb, yc, yf, yr,
                   bonus, zr, hi_lo(rw_g2[l]), row2(rw_ln_w[l]), row2(rw_ln_b[l]), bd256,
                   w_branch[l].astype(BF16), (0.5 * w_gate[l]).astype(BF16), row2(0.5 * b_gate[l]),
                   (0.5 * w_o[l]).astype(BF16), seq)
        xm = _merge(*mg_args, ctx_len, prev=_merge(*mg_args, TML))
        conv_w = jnp.concatenate([ffn_conv[l][:, :D_FF], 0.5 * ffn_conv[l][:, D_FF:]], axis=1)
        ff_args = (xm, modsel, row2(norm_ffn_pre[l]), row2(norm_ffn_post[l]), ffn_up[l].astype(BF16),
                   conv_w, ffn_down[l].astype(BF16), seq)
        if l < depth - 1:
            xa = _ffn(*ff_args, ctx_len, prev=_ffn(*ff_args, TML))
        else:
            xa = _ffn(*ff_args, TML, out_rows=seq)
    return xa
```

```python
import functools

import numpy as np
import jax
import jax.numpy as jnp
from jax import lax
from jax.experimental import pallas as pl
from jax.experimental.pallas import tpu as pltpu

F32 = jnp.float32
BF16 = jnp.bfloat16
HIGHEST = lax.Precision.HIGHEST

D_MODEL = 1024
GRID_W = 64
HEAD_DIM = 64
NA_HEADS = 4
NA_WIN_R = 8
NA_WIN_C = 16
GQA_Q_HEADS = 4
GQA_KV_HEADS = 2
ROPE_THETA = 10000.0
RWKV_HEADS = 4
RW = RWKV_HEADS * HEAD_DIM
LORA_W = 64
LORA_A = 64
LORA_G = 128
RW_SHIFT = 3 * RW + LORA_W + LORA_A
D_FF = 2816
NORM_EPS = 1e-6
LN_X_EPS = 64e-5
COLS_A = 768
COLS_NA = 768
COLS_GQA = 512
COLS_RW = 1024
D_IN = COLS_A + COLS_NA + COLS_GQA + COLS_RW

TM = 256
TML = 512
CH = 64
HALO = 8
FF_CHUNK = 256
GQA_VROWS = 80
GQA_TQ = 512
GQA_TK = 512
NEG = -1e30
VMEM_LIMIT = 56 * 1024 * 1024


def _cparams(sem):
    return pltpu.CompilerParams(dimension_semantics=sem, vmem_limit_bytes=VMEM_LIMIT)


def _const_spec(shape):
    nd = len(shape)
    return pl.BlockSpec(shape, lambda *_: (0,) * nd, pipeline_mode=pl.Buffered(1))


def _token_tiling(seq, tm, ctx_mode, n_out):
    nblk = seq // tm
    blk = (lambda i: i * 0 + nblk) if ctx_mode else (lambda i: i)
    alias_specs = [pl.BlockSpec(memory_space=pl.ANY)] * n_out if ctx_mode else []
    return nblk, blk, (1 if ctx_mode else nblk), alias_specs


def _halo_specs(width, tm, blk, nhb):
    per = tm // HALO
    return [pl.BlockSpec((1, HALO, width), lambda bi, i: (bi, jnp.maximum(blk(i) * per - 1, 0), 0)),
            pl.BlockSpec((1, HALO, width), lambda bi, i: (bi, jnp.minimum((blk(i) + 1) * per, nhb - 1), 0))]


def _edge_valid(i, nblk, ctx_mode):
    if ctx_mode:
        return 0.0, 0.0
    return jnp.where(i != 0, 1.0, 0.0), jnp.where(i != nblk - 1, 1.0, 0.0)


def _dg(a, b, ca, cb, **kw):
    return lax.dot_general(a, b, (((ca,), (cb,)), ((), ())), preferred_element_type=F32, **kw)


def _sum_dot(x, w, terms):
    acc = None
    for _ in range(terms):
        piece = x.astype(BF16)
        part = jnp.dot(piece, w, preferred_element_type=F32)
        acc = part if acc is None else acc + part
        x = x - piece.astype(F32)
    return acc


def _dot3(x, w_hi, w_lo):
    x_hi = x.astype(BF16)
    x_lo = (x - x_hi.astype(F32)).astype(BF16)
    return (jnp.dot(x_hi, w_hi, preferred_element_type=F32) + jnp.dot(x_lo, w_hi, preferred_element_type=F32)
            + jnp.dot(x_hi, w_lo, preferred_element_type=F32))


def _rms_mod(x, g, shift, scale):
    y = x * lax.rsqrt(jnp.mean(x * x, axis=-1, keepdims=True) + NORM_EPS) * g
    return y * (1.0 + scale) + shift


def _rms(x, g):
    return x * lax.rsqrt(jnp.mean(x * x, axis=-1, keepdims=True) + NORM_EPS) * g


def _sigmoid(x):
    return 0.5 * jnp.tanh(0.5 * x) + 0.5


def _ada_kernel(c_ref, w_ref, b_ref, o_ref):
    c = c_ref[...]
    s = c * _sigmoid(c)
    o_ref[0] = jnp.dot(s, w_ref[0], precision=HIGHEST, preferred_element_type=F32) + b_ref[0]


def _ada(cvec, ada_w, ada_b):
    depth = ada_w.shape[0]
    nblk = ada_w.shape[2] // D_MODEL
    return pl.pallas_call(
        _ada_kernel,
        grid=(depth, nblk),
        in_specs=[
            pl.BlockSpec((8, D_MODEL), lambda l, j: (0, 0)),
            pl.BlockSpec((1, D_MODEL, D_MODEL), lambda l, j: (l, 0, j)),
            pl.BlockSpec((1, 1, D_MODEL), lambda l, j: (l, 0, j)),
        ],
        out_specs=pl.BlockSpec((1, 8, D_MODEL), lambda l, j: (l, 0, j)),
        out_shape=jax.ShapeDtypeStruct((depth, 8, ada_w.shape[2]), F32),
        compiler_params=_cparams(("parallel", "parallel")),
        name="ada_mod",
    )(cvec, ada_w, ada_b.reshape(depth, 1, -1))


def _inproj_kernel(*refs):
    x_ref, mod_ref, g_ref, w_ref, cos_ref, sin_ref, gqk_ref, bd_ref = refs[:8]
    za_ref, qn_ref, kn_ref, vn_ref, qg_ref, kg_ref, vg_ref, zr_ref = refs[-8:]
    tm = x_ref.shape[1]
    x = x_ref[0]
    m = mod_ref[0, 0]
    hb = _rms_mod(x, g_ref[...], m[0:1], m[1:2]).astype(BF16)
    e1, e2, e3 = COLS_A, COLS_A + COLS_NA, COLS_A + COLS_NA + COLS_GQA
    g = jnp.dot(hb, w_ref[:, e2:e3], preferred_element_type=F32)
    na = jnp.dot(hb, w_ref[:, e1:e2], preferred_element_type=F32)
    lane = lax.broadcasted_iota(jnp.int32, (tm, 128), 1)
    scale = HEAD_DIM ** -0.5 * np.log2(np.e)
    qn_ref[0] = jnp.concatenate([(na[:, :128] * scale).T, (na[:, 128:256] * scale).T], axis=0).astype(BF16)
    for hd in range(NA_HEADS):
        kn_ref[0, hd] = na[:, 256 + hd * 64:256 + (hd + 1) * 64].astype(BF16)
        vpair = na[:, 512 + (hd // 2) * 128:512 + (hd // 2 + 1) * 128]
        vh = vpair if hd % 2 == 0 else pltpu.roll(vpair, 64, 1)
        vn_ref[0, hd] = jnp.where(lane < 64, vh, jnp.where(lane == 64, 1.0, 0.0)).T.astype(BF16)
    qk = g[:, :384]
    ms = _sum_dot(qk * qk, bd_ref[...], 2)
    za_ref[0] = jnp.dot(hb, w_ref[:, :e1], preferred_element_type=F32)
    zr_ref[0] = jnp.dot(hb, w_ref[:, e3:], preferred_element_type=F32)
    qkn = qk * lax.rsqrt(ms + NORM_EPS) * gqk_ref[...]
    even = (lane & 1) == 0
    parts = []
    for j in range(3):
        s = qkn[:, j * 128:(j + 1) * 128]
        sw = jnp.where(even, pltpu.roll(s, 127, 1), pltpu.roll(s, 1, 1))
        parts.append(s * cos_ref[:, j * 128:(j + 1) * 128] + sw * sin_ref[:, j * 128:(j + 1) * 128])
    qg_ref[0] = jnp.concatenate([parts[0].T, parts[1].T], axis=0).astype(BF16)
    for hd in range(GQA_KV_HEADS):
        kg_ref[0, hd] = parts[2][:, hd * 64:(hd + 1) * 64].astype(BF16)
    vt = g[:, 384:512]
    for hd in range(GQA_KV_HEADS):
        vh = vt if hd == 0 else pltpu.roll(vt, 64, 1)
        vext = jnp.where(lane < 64, vh, jnp.where(lane == 64, 1.0, 0.0))
        vg_ref[0, hd] = vext.T.astype(BF16)


def _inproj(xa, modsel, g_pre, w_in, cos_t, sin_t, gqk, bd384, seq, tm, prev=None):
    b, tt, _ = xa.shape
    ctx_mode = prev is not None
    nblk, blk, steps, alias_specs = _token_tiling(seq, tm, ctx_mode, 8)
    tile = lambda w: pl.BlockSpec((1, tm, w), lambda bi, i: (bi, blk(i), 0))
    heads = lambda nh: pl.BlockSpec((1, nh, tm, 64), lambda bi, i: (bi, 0, blk(i), 0))
    hs = lambda nh: jax.ShapeDtypeStruct((b, nh, tt, 64), BF16)
    return pl.pallas_call(
        _inproj_kernel,
        grid=(b, steps),
        in_specs=[
            tile(D_MODEL),
            pl.BlockSpec((1, 1, 6, D_MODEL), lambda bi, i: (bi, int(ctx_mode), 0, 0)),
            _const_spec((1, D_MODEL)),
            _const_spec((D_MODEL, D_IN)),
            pl.BlockSpec((tm, 384), lambda bi, i: (blk(i), 0)),
            pl.BlockSpec((tm, 384), lambda bi, i: (blk(i), 0)),
            _const_spec((1, 384)),
            _const_spec((384, 384)),
        ] + alias_specs,
        out_specs=[tile(COLS_A),
                   pl.BlockSpec((1, NA_HEADS * HEAD_DIM, tm), lambda bi, i: (bi, 0, blk(i))),
                   heads(4),
                   pl.BlockSpec((1, NA_HEADS, 128, tm), lambda bi, i: (bi, 0, 0, blk(i))),
                   pl.BlockSpec((1, GQA_Q_HEADS * HEAD_DIM, tm), lambda bi, i: (bi, 0, blk(i))),
                   heads(2),
                   pl.BlockSpec((1, 2, 128, tm), lambda bi, i: (bi, 0, 0, blk(i))), tile(COLS_RW)],
        out_shape=[jax.ShapeDtypeStruct((b, tt, COLS_A), F32),
                   jax.ShapeDtypeStruct((b, NA_HEADS * HEAD_DIM, tt), BF16),
                   hs(4),
                   jax.ShapeDtypeStruct((b, NA_HEADS, 128, tt), BF16),
                   jax.ShapeDtypeStruct((b, GQA_Q_HEADS * HEAD_DIM, tt), BF16),
                   hs(2),
                   jax.ShapeDtypeStruct((b, 2, 128, tt), BF16),
                   jax.ShapeDtypeStruct((b, tt, COLS_RW), F32)],
        input_output_aliases={8 + k: k for k in range(8)} if ctx_mode else {},
        compiler_params=_cparams(("parallel", "parallel")),
        name="inproj_ctx" if ctx_mode else "inproj",
    )(xa, modsel, g_pre, w_in, cos_t, sin_t, gqk, bd384, *(prev or ()))


def _na_kernel(q_ref, k0_ref, k1_ref, k2_ref, kc_ref, v0_ref, v1_ref, v2_ref, vc_ref, bias_ref, o_ref):
    scores = []
    for hd in range(NA_HEADS):
        kcat = jnp.concatenate([k0_ref[0, hd], k1_ref[0, hd], k2_ref[0, hd], kc_ref[0, hd]], axis=0)
        scores.append(jnp.dot(kcat, q_ref[0, hd * HEAD_DIM:(hd + 1) * HEAD_DIM, :],
                              preferred_element_type=F32))
    outs = []
    for hd in range(NA_HEADS):
        st = scores[hd]
        st_loc = st[:3 * TM] + bias_ref[0, hd]
        st_ctx = st[3 * TM:]
        m = jnp.maximum(jnp.max(st_loc, axis=0, keepdims=True), jnp.max(st_ctx, axis=0, keepdims=True))
        pt = jnp.concatenate([jnp.exp2(st_loc - m), jnp.exp2(st_ctx - m)], axis=0).astype(BF16)
        vt = jnp.concatenate([r[0, hd, :GQA_VROWS, :] for r in (v0_ref, v1_ref, v2_ref, vc_ref)], axis=1)
        acc = jnp.dot(vt, pt, preferred_element_type=F32)
        outs.append(acc[:HEAD_DIM] * (1.0 / acc[HEAD_DIM:HEAD_DIM + 1]))
    o_ref[0] = jnp.concatenate(outs, axis=0).T.astype(o_ref.dtype)


def _na(qn, kn, vn, bias_tab, ct):
    b, _, tt = qn.shape
    nt = tt // TM

    def key_tile(i, j):
        return ct if j is None else jnp.clip(i - 1, 0, ct - 3) + j

    k_spec = lambda j: pl.BlockSpec((1, 4, TM, 64), lambda bi, i: (bi, 0, key_tile(i, j), 0))
    v_spec = lambda j: pl.BlockSpec((1, 4, 128, TM), lambda bi, i: (bi, 0, 0, key_tile(i, j)))

    def pattern(i):
        return jnp.where(i == ct, 3, jnp.where(i == 0, 0, jnp.where(i == ct - 1, 2, 1)))

    return pl.pallas_call(
        _na_kernel,
        grid=(b, nt),
        in_specs=[
            pl.BlockSpec((1, NA_HEADS * HEAD_DIM, TM), lambda bi, i: (bi, 0, i)),
            k_spec(0), k_spec(1), k_spec(2), k_spec(None),
            v_spec(0), v_spec(1), v_spec(2), v_spec(None),
            pl.BlockSpec((1, 4, 3 * TM, TM), lambda bi, i: (pattern(i), 0, 0, 0)),
        ],
        out_specs=pl.BlockSpec((1, TM, 256), lambda bi, i: (bi, i, 0)),
        out_shape=jax.ShapeDtypeStruct((b, tt, 256), BF16),
        compiler_params=_cparams(("parallel", "parallel")),
        name="na_attn",
    )(qn, kn, kn, kn, kn, vn, vn, vn, vn, bias_tab)


def _na_bias_table(na_bias_l, rows):
    ct = rows * GRID_W // TM
    rpt = TM // GRID_W
    wr = min(NA_WIN_R, rows)
    qj = np.arange(GRID_W)
    kc = np.arange(GRID_W)
    cs = np.clip(qj - NA_WIN_C // 2, 0, GRID_W - NA_WIN_C)
    colvalid = (kc[None, :] >= cs[:, None]) & (kc[None, :] < cs[:, None] + NA_WIN_C)
    dc = kc[None, :] - qj[:, None] + (NA_WIN_C - 1)
    onehot = (dc.reshape(1, -1) == np.arange(2 * NA_WIN_C - 1)[:, None]) & colvalid.reshape(1, -1)
    toep = jnp.einsum("hrd,dx->hrx", na_bias_l, jnp.asarray(onehot.astype(np.float32)), precision=HIGHEST)
    toep = jnp.where(jnp.asarray(colvalid.reshape(-1)), toep * np.log2(np.e), NEG)
    toep = toep.reshape(NA_HEADS, 2 * NA_WIN_R - 1, GRID_W, GRID_W)
    toep = jnp.swapaxes(toep, -1, -2)
    neg_blk = jnp.full((NA_HEADS, GRID_W, GRID_W), NEG, F32)
    tabs = []
    for tile_i in (0, 1, ct - 1):
        i0 = tile_i * rpt
        kb = int(np.clip(tile_i - 1, 0, ct - 3)) * rpt
        rs = [int(np.clip(i0 + ri - wr // 2, 0, rows - wr)) for ri in range(rpt)]
        krows = []
        for m in range(3 * rpt):
            blks = [toep[:, kb + m - (i0 + ri) + NA_WIN_R - 1] if rs[ri] <= kb + m < rs[ri] + wr else neg_blk
                    for ri in range(rpt)]
            krows.append(jnp.concatenate(blks, axis=-1))
        tabs.append(jnp.concatenate(krows, axis=-2))
    tabs.append(jnp.full_like(tabs[0], NEG))
    return jnp.stack(tabs, axis=0)


GQA_CB = 256
GQA_UNROLL = 16
GQA_AHEAD = 3


def _gqa_kernel(q_ref, k_ref, v_ref, o_ref, *scratch, n_full, tail, tq):
    nblk = 2 * tq // GQA_CB
    m_refs, acc_refs = scratch[:nblk], scratch[nblk:]
    qt = jnp.concatenate([q_ref[0, :HEAD_DIM, :], q_ref[0, HEAD_DIM:, :]], axis=1)
    for n in range(nblk):
        m_refs[n][...] = jnp.full(m_refs[n].shape, NEG, F32)
        acc_refs[n][...] = jnp.zeros(acc_refs[n].shape, F32)

    def chunks(spans):
        kcs = [k_ref[0, 0, pl.ds(start, size), :] for start, size in spans]
        vts = [v_ref[0, 0, :GQA_VROWS, pl.ds(start, size)] for start, size in spans]
        items = [(c, n) for c in range(len(spans)) for n in range(nblk)]
        score = lambda c, n: jnp.dot(kcs[c], qt[:, n * GQA_CB:(n + 1) * GQA_CB], preferred_element_type=F32)
        ahead = {i: score(*items[i]) for i in range(min(GQA_AHEAD, len(items)))}
        for i, (c, n) in enumerate(items):
            if i + GQA_AHEAD < len(items):
                ahead[i + GQA_AHEAD] = score(*items[i + GQA_AHEAD])
            st = ahead.pop(i)
            m_old = m_refs[n][...]
            m_new = jnp.maximum(m_old, jnp.max(st, axis=0, keepdims=True))
            pt = jnp.exp2(st - m_new).astype(BF16)
            acc_refs[n][...] = (jnp.exp2(m_old - m_new) * acc_refs[n][...]
                                + jnp.dot(vts[c], pt, preferred_element_type=F32))
            m_refs[n][...] = m_new

    n_trips = n_full // GQA_UNROLL
    if n_trips > 0:
        def body(j, carry):
            base = j * (GQA_UNROLL * GQA_TK)
            chunks([(pl.multiple_of(base + u * GQA_TK, GQA_TK), GQA_TK) for u in range(GQA_UNROLL)])
            return carry
        lax.fori_loop(0, n_trips, body, 0)
    rest = [(c * GQA_TK, GQA_TK) for c in range(n_trips * GQA_UNROLL, n_full)]
    if tail > 0:
        rest.append((n_full * GQA_TK, tail))
    if rest:
        chunks(rest)
    acc = jnp.concatenate([r[...] for r in acc_refs], axis=1)
    ot = acc[:HEAD_DIM] * (1.0 / acc[HEAD_DIM:HEAD_DIM + 1])
    ot = jnp.concatenate([ot, jnp.zeros_like(ot)], axis=0)
    o = ot.T
    o_ref[0] = jnp.concatenate([o[:tq, :HEAD_DIM], o[tq:, :HEAD_DIM]], axis=-1).astype(o_ref.dtype)


def _gqa(qg, kg, vg, seq, ctx_len):
    b, _, tt = qg.shape
    scratch = lambda tq: ([pltpu.VMEM((1, GQA_CB), F32)] * (2 * tq // GQA_CB)
                          + [pltpu.VMEM((GQA_VROWS, GQA_CB), F32)] * (2 * tq // GQA_CB))
    y_lat = pl.pallas_call(
        functools.partial(_gqa_kernel, n_full=tt // GQA_TK, tail=tt % GQA_TK, tq=GQA_TQ),
        grid=(b, GQA_KV_HEADS, seq // GQA_TQ),
        in_specs=[
            pl.BlockSpec((1, 2 * HEAD_DIM, GQA_TQ), lambda bi, n, i: (bi, n, i)),
            pl.BlockSpec((1, 1, tt, HEAD_DIM), lambda bi, n, i: (bi, n, 0, 0)),
            pl.BlockSpec((1, 1, 128, tt), lambda bi, n, i: (bi, n, 0, 0)),
        ],
        out_specs=pl.BlockSpec((1, GQA_TQ, 128), lambda bi, n, i: (bi, i, n)),
        out_shape=jax.ShapeDtypeStruct((b, seq, 256), BF16),
        scratch_shapes=scratch(GQA_TQ),
        compiler_params=_cparams(("parallel", "parallel", "parallel")),
        name="gqa_latent",
    )(qg, kg, vg)
    cblk = seq // ctx_len
    y_ctx = pl.pallas_call(
        functools.partial(_gqa_kernel, n_full=0, tail=ctx_len, tq=ctx_len),
        grid=(b, GQA_KV_HEADS),
        in_specs=[
            pl.BlockSpec((1, 2 * HEAD_DIM, ctx_len), lambda bi, n: (bi, n, cblk)),
            pl.BlockSpec((1, 1, ctx_len, HEAD_DIM), lambda bi, n: (bi, n, cblk, 0)),
            pl.BlockSpec((1, 1, 128, ctx_len), lambda bi, n: (bi, n, 0, cblk)),
        ],
        out_specs=pl.BlockSpec((1, ctx_len, 128), lambda bi, n: (bi, 0, n)),
        out_shape=jax.ShapeDtypeStruct((b, ctx_len, 256), BF16),
        scratch_shapes=scratch(ctx_len),
        compiler_params=_cparams(("parallel", "parallel")),
        name="gqa_context",
    )(qg, kg, vg)
    return jnp.concatenate([y_lat, y_ctx], axis=1)


def _bd(x):
    left = lax.broadcasted_iota(jnp.int32, (CH, 128), 1) < HEAD_DIM
    x0, x1 = x[:, :128], x[:, 128:]
    z = jnp.zeros_like(x0)
    keep_l = lambda a: jnp.where(left, a, z)
    keep_r = lambda a: jnp.where(left, z, a)
    return jnp.concatenate([jnp.concatenate([keep_l(x0), z], axis=1), jnp.concatenate([keep_r(x0), z], axis=1),
                            jnp.concatenate([z, keep_l(x1)], axis=1), jnp.concatenate([z, keep_r(x1)], axis=1)],
                           axis=0)


def _fold(x):
    left = lax.broadcasted_iota(jnp.int32, (CH, 128), 1) < HEAD_DIM
    return jnp.concatenate([jnp.where(left, x[0:64, :128], x[64:128, :128]),
                            jnp.where(left, x[128:192, 128:], x[192:256, 128:])], axis=1)


def _chunk_mats(items):
    t = lax.broadcasted_iota(jnp.int32, (CH, RW), 0)
    j = lax.broadcasted_iota(jnp.int32, (CH, RW), 1) & 63
    eye = j == t
    before = {False: j < t, True: j > t}
    incl = {False: j <= t, True: j >= t}
    bf = lambda x: x.astype(BF16)
    n_items = range(len(items))

    lb, mb, lk, mk = [], [], [], []
    for it in items:
        a2 = bf(jnp.concatenate([it["kap"], it["rho"]], axis=0))
        lm_b = _dg(a2, _bd(bf(it["bt"])), 1, 1)
        lm_k = _dg(a2, _bd(bf(it["kt"])), 1, 1)
        lb.append(jnp.where(before[it["rev"]], lm_b[:CH], 0.0))
        mb.append(jnp.where(incl[it["rev"]], lm_b[CH:], 0.0))
        lk.append(jnp.where(before[it["rev"]], lm_k[:CH], 0.0))
        mk.append(jnp.where(incl[it["rev"]], lm_k[CH:], 0.0))

    p, mpow = [], []
    for i in n_items:
        nb = bf(-lb[i])
        p.append(jnp.where(eye, 1.0, 0.0) - lb[i])
        mpow.append(_dg(nb, _bd(nb), 1, 0))
    for _ in range(4):
        for i in n_items:
            mbf = bf(mpow[i])
            pm = _dg(jnp.concatenate([bf(p[i]), mbf], axis=0), _bd(mbf), 1, 0)
            p[i] = p[i] + pm[:CH]
            mpow[i] = pm[CH:]
    tinv = [bf(p[i] + _dg(bf(p[i]), _bd(bf(mpow[i])), 1, 0)) for i in n_items]

    lmv = [_dg(bf(jnp.concatenate([lk[i], mk[i]], axis=0)), _bd(bf(items[i]["v"])), 1, 0) for i in n_items]
    kp = [_dg(tinv[i], _bd(bf(items[i]["kap"])), 1, 0) for i in n_items]
    vp = [_dg(tinv[i], _bd(bf(lmv[i][:CH])), 1, 0) for i in n_items]
    out = []
    for i in n_items:
        it = items[i]
        mbb = bf(mb[i])
        rp = it["rho"] - _dg(mbb, _bd(bf(kp[i])), 1, 0)
        yl = lmv[i][CH:] - _dg(mbb, _bd(bf(vp[i])), 1, 0)
        g = jnp.where(eye, it["gdiag"], 0.0) - _fold(_dg(bf(kp[i]), bf(it["bhat"]), 0, 0))
        hmat = _fold(_dg(bf(jnp.concatenate([it["v"], -vp[i]], axis=0)),
                         bf(jnp.concatenate([it["khat"], it["bhat"]], axis=0)), 0, 0))
        out.append((g, hmat, rp, yl))
    return out


def _rwkv_chunk_kernel(z_ref, hp_ref, hn_ref, mu_ref, w0_ref, w2_ref, a0_ref, a2_ref, kkw_ref, ka_ref, rk_ref,
                       bd_ref, g_ref, h_ref, rp_ref, yl_ref, bonus_ref, *, ct):
    i = pl.program_id(1)
    z = z_ref[0]
    zs = z[:, :RW_SHIFT]
    valid_prev = jnp.where((i != 0) & (i != ct), 1.0, 0.0)
    valid_next = jnp.where((i != ct - 1) & (i != ct), 1.0, 0.0)
    prev_row = hp_ref[0, HALO - 1:HALO, :RW_SHIFT] * valid_prev
    next_row = hn_ref[0, 0:1, :RW_SHIFT] * valid_next
    row = lax.broadcasted_iota(jnp.int32, (TM, RW_SHIFT), 0)
    tt = lax.broadcasted_iota(jnp.int32, (TM, TM), 0)
    jj = lax.broadcasted_iota(jnp.int32, (TM, TM), 1)
    same_chunk = (tt >> 6) == (jj >> 6)
    bd = bd_ref[...]
    def direction(d):
        if d == 0:
            nb = jnp.where(row == 0, prev_row, pltpu.roll(zs, 1, 0))
        else:
            nb = jnp.where(row == TM - 1, next_row, pltpu.roll(zs, TM - 1, 0))
        zd = zs + mu_ref[d:d + 1, :] * (nb - zs)
        r = zd[:, :RW]
        k = zd[:, RW:2 * RW]
        v = zd[:, 2 * RW:3 * RW]
        lw = zd[:, 3 * RW:3 * RW + LORA_W]
        la = zd[:, 3 * RW + LORA_W:]
        w_log = w0_ref[d:d + 1, :] + _dot3(jnp.tanh(lw), w2_ref[d, 0], w2_ref[d, 1])
        a_pre = a0_ref[d:d + 1, :] + _dot3(la, a2_ref[d, 0], a2_ref[d, 1])
        kkr = k * kkw_ref[...]
        ss = _sum_dot(kkr * kkr, bd, 2)
        yield
        sp = jnp.maximum(-w_log, 0.0) + jnp.log(1.0 + jnp.exp(-jnp.abs(w_log)))
        logw = -jnp.exp(-sp - 0.5)
        tri = jnp.where(same_chunk & ((jj >= tt) if d == 1 else (jj <= tt)), 1.0, 0.0).astype(BF16)
        cum, rest = None, logw
        for _ in range(3):
            piece = rest.astype(BF16)
            part = jnp.dot(tri, piece, preferred_element_type=F32)
            cum = part if cum is None else cum + part
            rest = rest - piece.astype(F32)
        a = _sigmoid(a_pre)
        kk = kkr * lax.rsqrt(jnp.maximum(ss, 1e-24))
        k2 = k * (1.0 + (a - 1.0) * ka_ref[...])
        bv = kk * a
        bon = _sum_dot(r * k2 * rk_ref[...], bd, 2) * v
        yield
        e_neg = jnp.exp(-cum)
        streams = dict(kap=kk * jnp.exp(cum - logw), kt=k2 * e_neg, bt=bv * e_neg, rho=r * jnp.exp(cum), v=v)
        chunk_items = []
        for c in range(TM // CH):
            sl = slice(c * CH, (c + 1) * CH)
            it = {name: val[sl] for name, val in streams.items()}
            last = c * CH if d == 1 else (c + 1) * CH - 1
            tot = cum[last:last + 1]
            e_tot = jnp.exp(tot - cum[sl])
            it.update(khat=k2[sl] * e_tot, bhat=bv[sl] * e_tot, gdiag=jnp.exp(tot), rev=d == 1)
            chunk_items.append(it)
        return chunk_items, bon

    gens = [direction(0), direction(1)]
    done = [None, None]
    while any(res is None for res in done):
        for d, gen in enumerate(gens):
            if done[d] is None:
                try:
                    next(gen)
                except StopIteration as stop:
                    done[d] = stop.value
    items = done[0][0] + done[1][0]
    bonus = done[0][1] + done[1][1]
    for idx, (g, hm, rp, yl) in enumerate(_chunk_mats(items)):
        d, c = divmod(idx, TM // CH)
        g_ref[0, d, c] = g.astype(BF16)
        h_ref[0, d, c] = hm
        rp_ref[0, d, c] = rp.astype(BF16)
        yl_ref[0, d, c] = yl
    bonus_ref[0] = bonus


def _rwkv_chunks(zr, mu, w0, w2, a0, a2, kkw, ka, rk, bd256, ct):
    b, tt, _ = zr.shape
    nt = tt // TM
    nch = tt // CH
    cpt = TM // CH
    nhb = tt // HALO
    mats = pl.BlockSpec((1, 2, cpt, CH, RW), lambda bi, i: (bi, 0, i, 0, 0))
    mshape = lambda dt: jax.ShapeDtypeStruct((b, 2, nch, CH, RW), dt)
    return pl.pallas_call(
        functools.partial(_rwkv_chunk_kernel, ct=ct),
        grid=(b, nt),
        in_specs=[
            pl.BlockSpec((1, TM, COLS_RW), lambda bi, i: (bi, i, 0)),
            pl.BlockSpec((1, HALO, COLS_RW), lambda bi, i: (bi, jnp.maximum(i * (TM // HALO) - 1, 0), 0)),
            pl.BlockSpec((1, HALO, COLS_RW), lambda bi, i: (bi, jnp.minimum((i + 1) * (TM // HALO), nhb - 1), 0)),
            _const_spec((2, RW_SHIFT)),
            _const_spec((2, RW)),
            _const_spec((2, 2, LORA_W, RW)),
            _const_spec((2, RW)),
            _const_spec((2, 2, LORA_A, RW)),
            _const_spec((1, RW)),
            _const_spec((1, RW)),
            _const_spec((1, RW)),
            _const_spec((RW, RW)),
        ],
        out_specs=[mats, mats, mats, mats, pl.BlockSpec((1, TM, RW), lambda bi, i: (bi, i, 0))],
        out_shape=[mshape(BF16), mshape(F32), mshape(BF16), mshape(F32), jax.ShapeDtypeStruct((b, tt, RW), F32)],
        compiler_params=_cparams(("parallel", "parallel")),
        name="rwkv_chunks",
    )(zr, zr, zr, mu, w0, w2, a0, a2, kkw, ka, rk, bd256)


CHAIN_GROUP = 4


def _rwkv_chain_kernel(gf_ref, hf_ref, rf_ref, yf_ref, gr_ref, hr_ref, rr_ref, yr_ref, of_ref, or_ref, s_ref,
                       *, nb):
    @pl.when(pl.program_id(0) == 0)
    def _():
        s_ref[...] = jnp.zeros(s_ref.shape, F32)

    dirs = ((gf_ref, hf_ref, rf_ref, yf_ref, of_ref), (gr_ref, hr_ref, rr_ref, yr_ref, or_ref))
    state = [[s_ref[d, bi] for bi in range(nb)] for d in range(2)]
    for step in range(CHAIN_GROUP):
        for d, (g_ref, h_ref, rp_ref, yl_ref, o_ref) in enumerate(dirs):
            c = step if d == 0 else CHAIN_GROUP - 1 - step
            for bi in range(nb):
                sb = state[d][bi].astype(BF16)
                o_ref[bi, c * CH:(c + 1) * CH, :] = _dg(rp_ref[bi, 0, c], _bd(sb), 1, 1) + yl_ref[bi, 0, c]
                state[d][bi] = _dg(sb, _bd(g_ref[bi, 0, c]), 1, 0) + h_ref[bi, 0, c]
    for d in range(2):
        for bi in range(nb):
            s_ref[d, bi] = state[d][bi]


def _rwkv_chain(g, h, rp, yl, seq, ctx_len):
    b, _, nch, _, _ = g.shape
    assert ctx_len == CHAIN_GROUP * CH and seq % (CHAIN_GROUP * CH) == 0
    ngrp = nch // CHAIN_GROUP
    n_lat = seq // (CHAIN_GROUP * CH)

    def gf(s):
        return jnp.where(s == 0, n_lat, s - 1)

    def gr(s):
        return ngrp - 1 - s

    fwd = pl.BlockSpec((b, 1, CHAIN_GROUP, CH, RW), lambda s: (0, 0, gf(s), 0, 0))
    rev = pl.BlockSpec((b, 1, CHAIN_GROUP, CH, RW), lambda s: (0, 1, gr(s), 0, 0))
    yshape = jax.ShapeDtypeStruct((b, nch * CH, RW), F32)
    return pl.pallas_call(
        functools.partial(_rwkv_chain_kernel, nb=b),
        grid=(ngrp,),
        in_specs=[fwd, fwd, fwd, fwd, rev, rev, rev, rev],
        out_specs=[pl.BlockSpec((b, CHAIN_GROUP * CH, RW), lambda s: (0, gf(s), 0)),
                   pl.BlockSpec((b, CHAIN_GROUP * CH, RW), lambda s: (0, gr(s), 0))],
        out_shape=[yshape, yshape],
        scratch_shapes=[pltpu.VMEM((2, b, CH, RW), F32)],
        compiler_params=_cparams(("arbitrary",)),
        name="rwkv_chain",
    )(g, h, rp, yl, g, h, rp, yl)


def _merge_kernel(*refs, nblk, ctx_mode):
    (x_ref, mod_ref, gpre_ref, gpost_ref, za_ref, hp_ref, hn_ref, ca_ref, yb_ref, yc_ref, yf_ref, yr_ref,
     bonus_ref, lg_ref, g2_ref, lnw_ref, lnb_ref, bd_ref, wb_ref, wg_ref, bg_ref, wo_ref) = refs[:22]
    o_ref = refs[-1]
    tm = x_ref.shape[1]
    x = x_ref[0]
    m = mod_ref[0, 0]
    hb = _rms_mod(x, gpre_ref[...], m[0:1], m[1:2]).astype(BF16)

    def gate_mm(bidx):
        return jnp.dot(hb, wg_ref[:, bidx * D_MODEL:(bidx + 1) * D_MODEL], preferred_element_type=F32)

    def gated(bidx, pre, ys):
        gate2 = jnp.tanh(pre + bg_ref[:, bidx * D_MODEL:(bidx + 1) * D_MODEL]) + 1.0
        return gate2 * jnp.dot(ys, wb_ref[bidx], preferred_element_type=F32)

    za = za_ref[0]
    u = za[:, 256:512] * za[:, 512:768]
    valid_prev, valid_next = _edge_valid(pl.program_id(1), nblk, ctx_mode)
    up = hp_ref[0, HALO - 1:HALO, 256:512] * hp_ref[0, HALO - 1:HALO, 512:768] * valid_prev
    un = hn_ref[0, 0:1, 256:512] * hn_ref[0, 0:1, 512:768] * valid_next
    row = lax.broadcasted_iota(jnp.int32, (tm, 256), 0)
    u_prev = jnp.where(row == 0, up, pltpu.roll(u, 1, 0))
    u_next = jnp.where(row == tm - 1, un, pltpu.roll(u, tm - 1, 0))
    ya = za[:, :256] * (u_prev * ca_ref[0:1, :] + u * ca_ref[1:2, :] + u_next * ca_ref[2:3, :])

    bd = bd_ref[...]
    ones_dot = lambda piece: jnp.dot(piece, bd, preferred_element_type=F32)
    y = yf_ref[0] + yr_ref[0]
    p1 = y.astype(BF16)
    s1 = ones_dot(p1)
    pre1 = gate_mm(1)
    r1 = y - p1.astype(F32)
    p2 = r1.astype(BF16)
    s2 = ones_dot(p2)
    acc = gated(1, pre1, yb_ref[0])
    s3 = ones_dot((r1 - p2.astype(F32)).astype(BF16))
    pre2 = gate_mm(2)
    yc0 = y - (s1 + s2 + s3) * (1.0 / HEAD_DIM)
    sq = yc0 * yc0
    q1 = sq.astype(BF16)
    v1 = ones_dot(q1)
    acc = acc + gated(2, pre2, yc_ref[0])
    v2 = ones_dot((sq - q1.astype(F32)).astype(BF16))
    pre0 = gate_mm(0)
    lgate = _dot3(_sigmoid(lg_ref[0]), g2_ref[0], g2_ref[1])
    acc = acc + gated(0, pre0, ya.astype(BF16))
    pre3 = gate_mm(3)
    var = (v1 + v2) * (1.0 / HEAD_DIM)
    yn = yc0 * lax.rsqrt(var + LN_X_EPS) * lnw_ref[...] + lnb_ref[...] + bonus_ref[0]
    acc = acc + gated(3, pre3, (yn * lgate).astype(BF16))
    mo = jnp.dot(acc.astype(BF16), wo_ref[...], preferred_element_type=F32)
    o_ref[0] = x + m[2:3] * _rms(mo, gpost_ref[...])


def _merge(xa, modsel, g_pre, g_post, za, conv_a, yb, yc, yf, yr, bonus, zr, g2, ln_w, ln_b, bd256,
           w_branch, w_gate, b_gate, w_o, seq, tm, prev=None):
    b, tt, _ = xa.shape
    ctx_mode = prev is not None
    nblk, blk, steps, alias_specs = _token_tiling(seq, tm, ctx_mode, 1)
    tile = lambda w: pl.BlockSpec((1, tm, w), lambda bi, i: (bi, blk(i), 0))
    return pl.pallas_call(
        functools.partial(_merge_kernel, nblk=nblk, ctx_mode=ctx_mode),
        grid=(b, steps),
        in_specs=[
            tile(D_MODEL),
            pl.BlockSpec((1, 1, 6, D_MODEL), lambda bi, i: (bi, int(ctx_mode), 0, 0)),
            _const_spec((1, D_MODEL)),
            _const_spec((1, D_MODEL)),
            tile(COLS_A),
            *_halo_specs(COLS_A, tm, blk, tt // HALO),
            _const_spec((3, 256)),
            tile(256), tile(256), tile(256), tile(256), tile(256),
            pl.BlockSpec((1, tm, LORA_G), lambda bi, i: (bi, blk(i), RW_SHIFT // LORA_G)),
            _const_spec((2, LORA_G, RW)),
            _const_spec((1, RW)),
            _const_spec((1, RW)),
            _const_spec((RW, RW)),
            _const_spec((4, 256, D_MODEL)),
            _const_spec((D_MODEL, 4 * D_MODEL)),
            _const_spec((1, 4 * D_MODEL)),
            _const_spec((D_MODEL, D_MODEL)),
        ] + alias_specs,
        out_specs=tile(D_MODEL),
        out_shape=jax.ShapeDtypeStruct((b, tt, D_MODEL), F32),
        input_output_aliases={22: 0} if ctx_mode else {},
        compiler_params=_cparams(("parallel", "parallel")),
        name="merge_ctx" if ctx_mode else "merge",
    )(xa, modsel, g_pre, g_post, za, za, za, conv_a, yb, yc, yf, yr, bonus, zr, g2, ln_w, ln_b, bd256,
      w_branch, w_gate, b_gate, w_o, *((prev,) if ctx_mode else ()))


def _ffn_kernel(*refs, nblk, ctx_mode):
    x_ref, hp_ref, hn_ref, mod_ref, gpre_ref, gpost_ref, wu_ref, cw_ref, wd_ref = refs[:9]
    o_ref, act_ref = refs[-2:]
    tm = x_ref.shape[1]
    x = x_ref[0]
    m = mod_ref[0, 0]
    xx = jnp.concatenate([hp_ref[0], x, hn_ref[0]], axis=0)
    nrow = tm + 2 * HALO
    row = lax.broadcasted_iota(jnp.int32, (nrow, 1), 0)
    valid_prev, valid_next = _edge_valid(pl.program_id(1), nblk, ctx_mode)
    rowmask = jnp.where(row < HALO, valid_prev, jnp.where(row >= tm + HALO, valid_next, 1.0))
    hb = (_rms_mod(xx, gpre_ref[...], m[3:4], m[4:5]) * rowmask).astype(BF16)

    def conv(u, col):
        w = cw_ref[:, col:col + FF_CHUNK]
        c = (pltpu.roll(u, 1, 0) * w[0:1] + u * w[1:2] + pltpu.roll(u, nrow - 1, 0) * w[2:3])
        return c[HALO:HALO + tm]

    for j in range(D_FF // FF_CHUNK):
        ca = conv(jnp.dot(hb, wu_ref[:, j * FF_CHUNK:(j + 1) * FF_CHUNK], preferred_element_type=F32),
                  j * FF_CHUNK)
        cg = conv(jnp.dot(hb, wu_ref[:, D_FF + j * FF_CHUNK:D_FF + (j + 1) * FF_CHUNK],
                          preferred_element_type=F32), D_FF + j * FF_CHUNK)
        act_ref[:, j * FF_CHUNK:(j + 1) * FF_CHUNK] = (ca * (cg * (1.0 + jnp.tanh(cg)))).astype(BF16)
    f = jnp.dot(act_ref[...], wd_ref[...], preferred_element_type=F32)
    o_ref[0] = x + m[5:6] * _rms(f, gpost_ref[...])


def _ffn(xa, modsel, g_pre, g_post, w_up, conv_w, w_down, seq, tm, prev=None, out_rows=None):
    b, tt, _ = xa.shape
    ctx_mode = prev is not None
    nblk, blk, steps, alias_specs = _token_tiling(seq, tm, ctx_mode, 1)
    return pl.pallas_call(
        functools.partial(_ffn_kernel, nblk=nblk, ctx_mode=ctx_mode),
        grid=(b, steps),
        in_specs=[
            pl.BlockSpec((1, tm, D_MODEL), lambda bi, i: (bi, blk(i), 0)),
            *_halo_specs(D_MODEL, tm, blk, tt // HALO),
            pl.BlockSpec((1, 1, 6, D_MODEL), lambda bi, i: (bi, int(ctx_mode), 0, 0)),
            _const_spec((1, D_MODEL)),
            _const_spec((1, D_MODEL)),
            _const_spec((D_MODEL, 2 * D_FF)),
            _const_spec((3, 2 * D_FF)),
            _const_spec((D_FF, D_MODEL)),
        ] + alias_specs,
        out_specs=pl.BlockSpec((1, tm, D_MODEL), lambda bi, i: (bi, blk(i), 0)),
        out_shape=jax.ShapeDtypeStruct((b, out_rows or tt, D_MODEL), F32),
        scratch_shapes=[pltpu.VMEM((tm, D_FF), BF16)],
        input_output_aliases={9: 0} if ctx_mode else {},
        compiler_params=_cparams(("parallel", "parallel")),
        name="conv_ffn_ctx" if ctx_mode else "conv_ffn",
    )(xa, xa, xa, modsel, g_pre, g_post, w_up, conv_w, w_down, *((prev,) if ctx_mode else ()))


def _rope_tables(seq, ctx_len):
    t = np.arange(seq)
    row = (t // GRID_W).astype(np.float32)
    col = (t % GRID_W).astype(np.float32)
    n_freq = HEAD_DIM // 4
    inv_freq = jnp.asarray(ROPE_THETA, F32) ** (-jnp.arange(n_freq, dtype=F32) / n_freq)
    ang = jnp.concatenate([jnp.asarray(row)[:, None] * inv_freq, jnp.asarray(col)[:, None] * inv_freq], axis=-1)
    cos = jnp.repeat(jnp.cos(ang), 2, axis=-1)
    sin = jnp.repeat(jnp.sin(ang), 2, axis=-1) * jnp.tile(jnp.asarray([-1.0, 1.0], F32), HEAD_DIM // 2)
    cos = jnp.concatenate([cos, jnp.ones((ctx_len, HEAD_DIM), F32)], axis=0)
    sin = jnp.concatenate([sin, jnp.zeros((ctx_len, HEAD_DIM), F32)], axis=0)
    nh = GQA_Q_HEADS + GQA_KV_HEADS
    return jnp.tile(cos, (1, nh)), jnp.tile(sin, (1, nh))


def _block_ones(n, scale):
    idx = np.arange(n) // HEAD_DIM
    return jnp.asarray((idx[:, None] == idx[None, :]).astype(np.float32) * scale)


def kernel(x, c, ctx, c_ctx, ada_w, ada_b, norm_mix_pre, norm_mix_post, norm_ffn_pre, norm_ffn_post, w_in, conv_a, na_bias, q_norm, k_norm, rw_mu, rw_w0, rw_w2, rw_a0, rw_a2, rw_kk, rw_ka, rw_rk, rw_g2, rw_ln_w, rw_ln_b, w_branch, w_gate, b_gate, w_o, ffn_up, ffn_conv, ffn_down):
    b, seq, _ = x.shape
    ctx_len = ctx.shape[1]
    depth = ada_w.shape[0]
    assert ctx_len == TM and seq % GQA_TQ == 0 and seq % TML == 0 and seq // TM >= 3 and b + 1 <= 8
    ct = seq // TM
    rows = seq // GRID_W

    cvec = jnp.zeros((8, D_MODEL), F32).at[:b].set(c).at[b].set(c_ctx)
    mods = _ada(cvec, ada_w, ada_b)
    cos_t, sin_t = _rope_tables(seq, ctx_len)
    bd384 = _block_ones(384, 1.0 / HEAD_DIM).astype(BF16)
    bd256 = _block_ones(256, 1.0).astype(BF16)

    def hi_lo(w):
        hi = w.astype(BF16)
        return jnp.stack([hi, (w - hi.astype(F32)).astype(BF16)], axis=-3)

    xa = jnp.concatenate([x, ctx], axis=1)
    for l in range(depth):
        ml = mods[l].reshape(8, 6, D_MODEL)
        modsel = jnp.stack([ml[:b], jnp.broadcast_to(ml[b][None], (b, 6, D_MODEL))], axis=1)
        row2 = lambda a: a.reshape(1, -1)
        gqk = jnp.concatenate([jnp.tile(q_norm[l], GQA_Q_HEADS) * (HEAD_DIM ** -0.5 * np.log2(np.e)),
                               jnp.tile(k_norm[l], GQA_KV_HEADS)]).reshape(1, -1)
        ip_args = (xa, modsel, row2(norm_mix_pre[l]), w_in[l].astype(BF16), cos_t, sin_t, gqk, bd384, seq)
        za, qn, kn, vn, qg, kg, vg, zr = _inproj(*ip_args, ctx_len, prev=_inproj(*ip_args, TML))
        yb = _na(qn, kn, vn, _na_bias_table(na_bias[l], rows), ct)
        yc = _gqa(qg, kg, vg, seq, ctx_len)
        g, h, rp, yl, bonus = _rwkv_chunks(zr, rw_mu[l], rw_w0[l], hi_lo(rw_w2[l]), rw_a0[l], hi_lo(rw_a2[l]),
                                           row2(rw_kk[l]), row2(rw_ka[l]), row2(rw_rk[l]), bd256, ct)
        yf, yr = _rwkv_chain(g, h, rp, yl, seq, ctx_len)
        mg_args = (xa, modsel, row2(norm_mix_pre[l]), row2(norm_mix_post[l]), za, conv_a[l], yb, yc, yf, yr,
                   bonus, zr, hi_lo(rw_g2[l]), row2(rw_ln_w[l]), row2(rw_ln_b[l]), bd256,
                   w_branch[l].astype(BF16), (0.5 * w_gate[l]).astype(BF16), row2(0.5 * b_gate[l]),
                   (0.5 * w_o[l]).astype(BF16), seq)
        xm = _merge(*mg_args, ctx_len, prev=_merge(*mg_args, TML))
        conv_w = jnp.concatenate([ffn_conv[l][:, :D_FF], 0.5 * ffn_conv[l][:, D_FF:]], axis=1)
        ff_args = (xm, modsel, row2(norm_ffn_pre[l]), row2(norm_ffn_post[l]), ffn_up[l].astype(BF16),
                   conv_w, ffn_down[l].astype(BF16), seq)
        if l < depth - 1:
            xa = _ffn(*ff_args, ctx_len, prev=_ffn(*ff_args, TML))
        else:
            xa = _ffn(*ff_args, TML, out_rows=seq)
    return xa
```

```python
import functools

import numpy as np
import jax
import jax.numpy as jnp
from jax import lax
from jax.experimental import pallas as pl
from jax.experimental.pallas import tpu as pltpu

F32 = jnp.float32
BF16 = jnp.bfloat16
HIGHEST = lax.Precision.HIGHEST

D_MODEL = 1024
GRID_W = 64
HEAD_DIM = 64
NA_HEADS = 4
NA_WIN_R = 8
NA_WIN_C = 16
GQA_Q_HEADS = 4
GQA_KV_HEADS = 2
ROPE_THETA = 10000.0
RWKV_HEADS = 4
RW = RWKV_HEADS * HEAD_DIM
LORA_W = 64
LORA_A = 64
LORA_G = 128
RW_SHIFT = 3 * RW + LORA_W + LORA_A
D_FF = 2816
NORM_EPS = 1e-6
LN_X_EPS = 64e-5
COLS_A = 768
COLS_NA = 768
COLS_GQA = 512
COLS_RW = 1024
D_IN = COLS_A + COLS_NA + COLS_GQA + COLS_RW

TM = 256
TML = 512
CH = 64
HALO = 8
FF_CHUNK = 256
GQA_VROWS = 80
GQA_TQ = 512
GQA_TK = 512
NEG = -1e30
VMEM_LIMIT = 56 * 1024 * 1024


def _cparams(sem):
    return pltpu.CompilerParams(dimension_semantics=sem, vmem_limit_bytes=VMEM_LIMIT)


def _const_spec(shape):
    nd = len(shape)
    return pl.BlockSpec(shape, lambda *_: (0,) * nd, pipeline_mode=pl.Buffered(1))


def _token_tiling(seq, tm, ctx_mode, n_out):
    nblk = seq // tm
    blk = (lambda i: i * 0 + nblk) if ctx_mode else (lambda i: i)
    alias_specs = [pl.BlockSpec(memory_space=pl.ANY)] * n_out if ctx_mode else []
    return nblk, blk, (1 if ctx_mode else nblk), alias_specs


def _halo_specs(width, tm, blk, nhb):
    per = tm // HALO
    return [pl.BlockSpec((1, HALO, width), lambda bi, i: (bi, jnp.maximum(blk(i) * per - 1, 0), 0)),
            pl.BlockSpec((1, HALO, width), lambda bi, i: (bi, jnp.minimum((blk(i) + 1) * per, nhb - 1), 0))]


def _edge_valid(i, nblk, ctx_mode):
    if ctx_mode:
        return 0.0, 0.0
    return jnp.where(i != 0, 1.0, 0.0), jnp.where(i != nblk - 1, 1.0, 0.0)


def _dg(a, b, ca, cb, **kw):
    return lax.dot_general(a, b, (((ca,), (cb,)), ((), ())), preferred_element_type=F32, **kw)


def _sum_dot(x, w, terms):
    acc = None
    for _ in range(terms):
        piece = x.astype(BF16)
        part = jnp.dot(piece, w, preferred_element_type=F32)
        acc = part if acc is None else acc + part
        x = x - piece.astype(F32)
    return acc


def _dot3(x, w_hi, w_lo):
    x_hi = x.astype(BF16)
    x_lo = (x - x_hi.astype(F32)).astype(BF16)
    return (jnp.dot(x_hi, w_hi, preferred_element_type=F32) + jnp.dot(x_lo, w_hi, preferred_element_type=F32)
            + jnp.dot(x_hi, w_lo, preferred_element_type=F32))


def _rms_mod(x, g, shift, scale):
    y = x * lax.rsqrt(jnp.mean(x * x, axis=-1, keepdims=True) + NORM_EPS) * g
    return y * (1.0 + scale) + shift


def _rms(x, g):
    return x * lax.rsqrt(jnp.mean(x * x, axis=-1, keepdims=True) + NORM_EPS) * g


def _sigmoid(x):
    return 0.5 * jnp.tanh(0.5 * x) + 0.5


def _ada_kernel(c_ref, w_ref, b_ref, o_ref):
    c = c_ref[...]
    s = c * _sigmoid(c)
    o_ref[0] = jnp.dot(s, w_ref[0], precision=HIGHEST, preferred_element_type=F32) + b_ref[0]


def _ada(cvec, ada_w, ada_b):
    depth = ada_w.shape[0]
    nblk = ada_w.shape[2] // D_MODEL
    return pl.pallas_call(
        _ada_kernel,
        grid=(depth, nblk),
        in_specs=[
            pl.BlockSpec((8, D_MODEL), lambda l, j: (0, 0)),
            pl.BlockSpec((1, D_MODEL, D_MODEL), lambda l, j: (l, 0, j)),
            pl.BlockSpec((1, 1, D_MODEL), lambda l, j: (l, 0, j)),
        ],
        out_specs=pl.BlockSpec((1, 8, D_MODEL), lambda l, j: (l, 0, j)),
        out_shape=jax.ShapeDtypeStruct((depth, 8, ada_w.shape[2]), F32),
        compiler_params=_cparams(("parallel", "parallel")),
        name="ada_mod",
    )(cvec, ada_w, ada_b.reshape(depth, 1, -1))


def _inproj_kernel(*refs):
    x_ref, mod_ref, g_ref, w_ref, cos_ref, sin_ref, gqk_ref, bd_ref = refs[:8]
    za_ref, qn_ref, kn_ref, vn_ref, qg_ref, kg_ref, vg_ref, zr_ref = refs[-8:]
    tm = x_ref.shape[1]
    x = x_ref[0]
    m = mod_ref[0, 0]
    hb = _rms_mod(x, g_ref[...], m[0:1], m[1:2]).astype(BF16)
    e1, e2, e3 = COLS_A, COLS_A + COLS_NA, COLS_A + COLS_NA + COLS_GQA
    g = jnp.dot(hb, w_ref[:, e2:e3], preferred_element_type=F32)
    na = jnp.dot(hb, w_ref[:, e1:e2], preferred_element_type=F32)
    lane = lax.broadcasted_iota(jnp.int32, (tm, 128), 1)
    scale = HEAD_DIM ** -0.5 * np.log2(np.e)
    qn_ref[0] = jnp.concatenate([(na[:, :128] * scale).T, (na[:, 128:256] * scale).T], axis=0).astype(BF16)
    for hd in range(NA_HEADS):
        kn_ref[0, hd] = na[:, 256 + hd * 64:256 + (hd + 1) * 64].astype(BF16)
        vpair = na[:, 512 + (hd // 2) * 128:512 + (hd // 2 + 1) * 128]
        vh = vpair if hd % 2 == 0 else pltpu.roll(vpair, 64, 1)
        vn_ref[0, hd] = jnp.where(lane < 64, vh, jnp.where(lane == 64, 1.0, 0.0)).T.astype(BF16)
    qk = g[:, :384]
    ms = _sum_dot(qk * qk, bd_ref[...], 2)
    za_ref[0] = jnp.dot(hb, w_ref[:, :e1], preferred_element_type=F32)
    zr_ref[0] = jnp.dot(hb, w_ref[:, e3:], preferred_element_type=F32)
    qkn = qk * lax.rsqrt(ms + NORM_EPS) * gqk_ref[...]
    even = (lane & 1) == 0
    parts = []
    for j in range(3):
        s = qkn[:, j * 128:(j + 1) * 128]
        sw = jnp.where(even, pltpu.roll(s, 127, 1), pltpu.roll(s, 1, 1))
        parts.append(s * cos_ref[:, j * 128:(j + 1) * 128] + sw * sin_ref[:, j * 128:(j + 1) * 128])
    qg_ref[0] = jnp.concatenate([parts[0].T, parts[1].T], axis=0).astype(BF16)
    for hd in range(GQA_KV_HEADS):
        kg_ref[0, hd] = parts[2][:, hd * 64:(hd + 1) * 64].astype(BF16)
    vt = g[:, 384:512]
    for hd in range(GQA_KV_HEADS):
        vh = vt if hd == 0 else pltpu.roll(vt, 64, 1)
        vext = jnp.where(lane < 64, vh, jnp.where(lane == 64, 1.0, 0.0))
        vg_ref[0, hd] = vext.T.astype(BF16)


def _inproj(xa, modsel, g_pre, w_in, cos_t, sin_t, gqk, bd384, seq, tm, prev=None):
    b, tt, _ = xa.shape
    ctx_mode = prev is not None
    nblk, blk, steps, alias_specs = _token_tiling(seq, tm, ctx_mode, 8)
    tile = lambda w: pl.BlockSpec((1, tm, w), lambda bi, i: (bi, blk(i), 0))
    heads = lambda nh: pl.BlockSpec((1, nh, tm, 64), lambda bi, i: (bi, 0, blk(i), 0))
    hs = lambda nh: jax.ShapeDtypeStruct((b, nh, tt, 64), BF16)
    return pl.pallas_call(
        _inproj_kernel,
        grid=(b, steps),
        in_specs=[
            tile(D_MODEL),
            pl.BlockSpec((1, 1, 6, D_MODEL), lambda bi, i: (bi, int(ctx_mode), 0, 0)),
            _const_spec((1, D_MODEL)),
            _const_spec((D_MODEL, D_IN)),
            pl.BlockSpec((tm, 384), lambda bi, i: (blk(i), 0)),
            pl.BlockSpec((tm, 384), lambda bi, i: (blk(i), 0)),
            _const_spec((1, 384)),
            _const_spec((384, 384)),
        ] + alias_specs,
        out_specs=[tile(COLS_A),
                   pl.BlockSpec((1, NA_HEADS * HEAD_DIM, tm), lambda bi, i: (bi, 0, blk(i))),
                   heads(4),
                   pl.BlockSpec((1, NA_HEADS, 128, tm), lambda bi, i: (bi, 0, 0, blk(i))),
                   pl.BlockSpec((1, GQA_Q_HEADS * HEAD_DIM, tm), lambda bi, i: (bi, 0, blk(i))),
                   heads(2),
                   pl.BlockSpec((1, 2, 128, tm), lambda bi, i: (bi, 0, 0, blk(i))), tile(COLS_RW)],
        out_shape=[jax.ShapeDtypeStruct((b, tt, COLS_A), F32),
                   jax.ShapeDtypeStruct((b, NA_HEADS * HEAD_DIM, tt), BF16),
                   hs(4),
                   jax.ShapeDtypeStruct((b, NA_HEADS, 128, tt), BF16),
                   jax.ShapeDtypeStruct((b, GQA_Q_HEADS * HEAD_DIM, tt), BF16),
                   hs(2),
                   jax.ShapeDtypeStruct((b, 2, 128, tt), BF16),
                   jax.ShapeDtypeStruct((b, tt, COLS_RW), F32)],
        input_output_aliases={8 + k: k for k in range(8)} if ctx_mode else {},
        compiler_params=_cparams(("parallel", "parallel")),
        name="inproj_ctx" if ctx_mode else "inproj",
    )(xa, modsel, g_pre, w_in, cos_t, sin_t, gqk, bd384, *(prev or ()))


def _na_kernel(q_ref, k0_ref, k1_ref, k2_ref, kc_ref, v0_ref, v1_ref, v2_ref, vc_ref, bias_ref, o_ref):
    scores = []
    for hd in range(NA_HEADS):
        kcat = jnp.concatenate([k0_ref[0, hd], k1_ref[0, hd], k2_ref[0, hd], kc_ref[0, hd]], axis=0)
        scores.append(jnp.dot(kcat, q_ref[0, hd * HEAD_DIM:(hd + 1) * HEAD_DIM, :],
                              preferred_element_type=F32))
    outs = []
    for hd in range(NA_HEADS):
        st = scores[hd]
        st_loc = st[:3 * TM] + bias_ref[0, hd]
        st_ctx = st[3 * TM:]
        m = jnp.maximum(jnp.max(st_loc, axis=0, keepdims=True), jnp.max(st_ctx, axis=0, keepdims=True))
        pt = jnp.concatenate([jnp.exp2(st_loc - m), jnp.exp2(st_ctx - m)], axis=0).astype(BF16)
        vt = jnp.concatenate([r[0, hd, :GQA_VROWS, :] for r in (v0_ref, v1_ref, v2_ref, vc_ref)], axis=1)
        acc = jnp.dot(vt, pt, preferred_element_type=F32)
        outs.append(acc[:HEAD_DIM] * (1.0 / acc[HEAD_DIM:HEAD_DIM + 1]))
    o_ref[0] = jnp.concatenate(outs, axis=0).T.astype(o_ref.dtype)


def _na(qn, kn, vn, bias_tab, ct):
    b, _, tt = qn.shape
    nt = tt // TM

    def key_tile(i, j):
        return ct if j is None else jnp.clip(i - 1, 0, ct - 3) + j

    k_spec = lambda j: pl.BlockSpec((1, 4, TM, 64), lambda bi, i: (bi, 0, key_tile(i, j), 0))
    v_spec = lambda j: pl.BlockSpec((1, 4, 128, TM), lambda bi, i: (bi, 0, 0, key_tile(i, j)))

    def pattern(i):
        return jnp.where(i == ct, 3, jnp.where(i == 0, 0, jnp.where(i == ct - 1, 2, 1)))

    return pl.pallas_call(
        _na_kernel,
        grid=(b, nt),
        in_specs=[
            pl.BlockSpec((1, NA_HEADS * HEAD_DIM, TM), lambda bi, i: (bi, 0, i)),
            k_spec(0), k_spec(1), k_spec(2), k_spec(None),
            v_spec(0), v_spec(1), v_spec(2), v_spec(None),
            pl.BlockSpec((1, 4, 3 * TM, TM), lambda bi, i: (pattern(i), 0, 0, 0)),
        ],
        out_specs=pl.BlockSpec((1, TM, 256), lambda bi, i: (bi, i, 0)),
        out_shape=jax.ShapeDtypeStruct((b, tt, 256), BF16),
        compiler_params=_cparams(("parallel", "parallel")),
        name="na_attn",
    )(qn, kn, kn, kn, kn, vn, vn, vn, vn, bias_tab)


def _na_bias_table(na_bias_l, rows):
    ct = rows * GRID_W // TM
    rpt = TM // GRID_W
    wr = min(NA_WIN_R, rows)
    qj = np.arange(GRID_W)
    kc = np.arange(GRID_W)
    cs = np.clip(qj - NA_WIN_C // 2, 0, GRID_W - NA_WIN_C)
    colvalid = (kc[None, :] >= cs[:, None]) & (kc[None, :] < cs[:, None] + NA_WIN_C)
    dc = kc[None, :] - qj[:, None] + (NA_WIN_C - 1)
    onehot = (dc.reshape(1, -1) == np.arange(2 * NA_WIN_C - 1)[:, None]) & colvalid.reshape(1, -1)
    toep = jnp.einsum("hrd,dx->hrx", na_bias_l, jnp.asarray(onehot.astype(np.float32)), precision=HIGHEST)
    toep = jnp.where(jnp.asarray(colvalid.reshape(-1)), toep * np.log2(np.e), NEG)
    toep = toep.reshape(NA_HEADS, 2 * NA_WIN_R - 1, GRID_W, GRID_W)
    toep = jnp.swapaxes(toep, -1, -2)
    neg_blk = jnp.full((NA_HEADS, GRID_W, GRID_W), NEG, F32)
    tabs = []
    for tile_i in (0, 1, ct - 1):
        i0 = tile_i * rpt
        kb = int(np.clip(tile_i - 1, 0, ct - 3)) * rpt
        rs = [int(np.clip(i0 + ri - wr // 2, 0, rows - wr)) for ri in range(rpt)]
        krows = []
        for m in range(3 * rpt):
            blks = [toep[:, kb + m - (i0 + ri) + NA_WIN_R - 1] if rs[ri] <= kb + m < rs[ri] + wr else neg_blk
                    for ri in range(rpt)]
            krows.append(jnp.concatenate(blks, axis=-1))
        tabs.append(jnp.concatenate(krows, axis=-2))
    tabs.append(jnp.full_like(tabs[0], NEG))
    return jnp.stack(tabs, axis=0)


GQA_CB = 256
GQA_UNROLL = 16
GQA_AHEAD = 3


def _gqa_kernel(q_ref, k_ref, v_ref, o_ref, *scratch, n_full, tail, tq):
    nblk = 2 * tq // GQA_CB
    m_refs, acc_refs, st_refs = scratch[:nblk], scratch[nblk:2 * nblk], scratch[2 * nblk:]
    qt = jnp.concatenate([q_ref[0, :HEAD_DIM, :], q_ref[0, HEAD_DIM:, :]], axis=1)
    for n in range(nblk):
        m_refs[n][...] = jnp.full(m_refs[n].shape, NEG, F32)
        acc_refs[n][...] = jnp.zeros(acc_refs[n].shape, F32)

    def chunks(spans):
        kcs = [k_ref[0, 0, pl.ds(start, size), :] for start, size in spans]
        vts = [v_ref[0, 0, :GQA_VROWS, pl.ds(start, size)] for start, size in spans]
        items = [(c, n) for c in range(len(spans)) for n in range(nblk)]
        score = lambda c, n: jnp.dot(kcs[c], qt[:, n * GQA_CB:(n + 1) * GQA_CB], preferred_element_type=F32)
        def issue(i):
            c, n = items[i]
            st_refs[i % len(st_refs)][:spans[c][1], :] = score(c, n)

        for i in range(min(GQA_AHEAD, len(items))):
            issue(i)
        for i, (c, n) in enumerate(items):
            if i + GQA_AHEAD < len(items):
                issue(i + GQA_AHEAD)
            st = st_refs[i % len(st_refs)][:spans[c][1], :]
            m_old = m_refs[n][...]
            m_new = jnp.maximum(m_old, jnp.max(st, axis=0, keepdims=True))
            pt = jnp.exp2(st - m_new).astype(BF16)
            acc_refs[n][...] = (jnp.exp2(m_old - m_new) * acc_refs[n][...]
                                + jnp.dot(vts[c], pt, preferred_element_type=F32))
            m_refs[n][...] = m_new

    n_trips = n_full // GQA_UNROLL
    if n_trips > 0:
        def body(j, carry):
            base = j * (GQA_UNROLL * GQA_TK)
            chunks([(pl.multiple_of(base + u * GQA_TK, GQA_TK), GQA_TK) for u in range(GQA_UNROLL)])
            return carry
        lax.fori_loop(0, n_trips, body, 0)
    rest = [(c * GQA_TK, GQA_TK) for c in range(n_trips * GQA_UNROLL, n_full)]
    if tail > 0:
        rest.append((n_full * GQA_TK, tail))
    if rest:
        chunks(rest)
    acc = jnp.concatenate([r[...] for r in acc_refs], axis=1)
    ot = acc[:HEAD_DIM] * (1.0 / acc[HEAD_DIM:HEAD_DIM + 1])
    ot = jnp.concatenate([ot, jnp.zeros_like(ot)], axis=0)
    o = ot.T
    o_ref[0] = jnp.concatenate([o[:tq, :HEAD_DIM], o[tq:, :HEAD_DIM]], axis=-1).astype(o_ref.dtype)


def _gqa(qg, kg, vg, seq, ctx_len):
    b, _, tt = qg.shape
    scratch = lambda tq: ([pltpu.VMEM((1, GQA_CB), F32)] * (2 * tq // GQA_CB)
                          + [pltpu.VMEM((GQA_VROWS, GQA_CB), F32)] * (2 * tq // GQA_CB)
                          + [pltpu.VMEM((GQA_TK, GQA_CB), F32)] * (GQA_AHEAD + 1))
    y_lat = pl.pallas_call(
        functools.partial(_gqa_kernel, n_full=tt // GQA_TK, tail=tt % GQA_TK, tq=GQA_TQ),
        grid=(b, GQA_KV_HEADS, seq // GQA_TQ),
        in_specs=[
            pl.BlockSpec((1, 2 * HEAD_DIM, GQA_TQ), lambda bi, n, i: (bi, n, i)),
            pl.BlockSpec((1, 1, tt, HEAD_DIM), lambda bi, n, i: (bi, n, 0, 0)),
            pl.BlockSpec((1, 1, 128, tt), lambda bi, n, i: (bi, n, 0, 0)),
        ],
        out_specs=pl.BlockSpec((1, GQA_TQ, 128), lambda bi, n, i: (bi, i, n)),
        out_shape=jax.ShapeDtypeStruct((b, seq, 256), BF16),
        scratch_shapes=scratch(GQA_TQ),
        compiler_params=_cparams(("parallel", "parallel", "parallel")),
        name="gqa_latent",
    )(qg, kg, vg)
    cblk = seq // ctx_len
    y_ctx = pl.pallas_call(
        functools.partial(_gqa_kernel, n_full=0, tail=ctx_len, tq=ctx_len),
        grid=(b, GQA_KV_HEADS),
        in_specs=[
            pl.BlockSpec((1, 2 * HEAD_DIM, ctx_len), lambda bi, n: (bi, n, cblk)),
            pl.BlockSpec((1, 1, ctx_len, HEAD_DIM), lambda bi, n: (bi, n, cblk, 0)),
            pl.BlockSpec((1, 1, 128, ctx_len), lambda bi, n: (bi, n, 0, cblk)),
        ],
        out_specs=pl.BlockSpec((1, ctx_len, 128), lambda bi, n: (bi, 0, n)),
        out_shape=jax.ShapeDtypeStruct((b, ctx_len, 256), BF16),
        scratch_shapes=scratch(ctx_len),
        compiler_params=_cparams(("parallel", "parallel")),
        name="gqa_context",
    )(qg, kg, vg)
    return jnp.concatenate([y_lat, y_ctx], axis=1)


def _bd(x):
    left = lax.broadcasted_iota(jnp.int32, (CH, 128), 1) < HEAD_DIM
    x0, x1 = x[:, :128], x[:, 128:]
    z = jnp.zeros_like(x0)
    keep_l = lambda a: jnp.where(left, a, z)
    keep_r = lambda a: jnp.where(left, z, a)
    return jnp.concatenate([jnp.concatenate([keep_l(x0), z], axis=1), jnp.concatenate([keep_r(x0), z], axis=1),
                            jnp.concatenate([z, keep_l(x1)], axis=1), jnp.concatenate([z, keep_r(x1)], axis=1)],
                           axis=0)


def _fold(x):
    left = lax.broadcasted_iota(jnp.int32, (CH, 128), 1) < HEAD_DIM
    return jnp.concatenate([jnp.where(left, x[0:64, :128], x[64:128, :128]),
                            jnp.where(left, x[128:192, 128:], x[192:256, 128:])], axis=1)


def _chunk_mats(items):
    t = lax.broadcasted_iota(jnp.int32, (CH, RW), 0)
    j = lax.broadcasted_iota(jnp.int32, (CH, RW), 1) & 63
    eye = j == t
    before = {False: j < t, True: j > t}
    incl = {False: j <= t, True: j >= t}
    bf = lambda x: x.astype(BF16)
    n_items = range(len(items))

    lb, mb, lk, mk = [], [], [], []
    for it in items:
        a2 = bf(jnp.concatenate([it["kap"], it["rho"]], axis=0))
        lm_b = _dg(a2, _bd(bf(it["bt"])), 1, 1)
        lm_k = _dg(a2, _bd(bf(it["kt"])), 1, 1)
        lb.append(jnp.where(before[it["rev"]], lm_b[:CH], 0.0))
        mb.append(jnp.where(incl[it["rev"]], lm_b[CH:], 0.0))
        lk.append(jnp.where(before[it["rev"]], lm_k[:CH], 0.0))
        mk.append(jnp.where(incl[it["rev"]], lm_k[CH:], 0.0))

    p, mpow = [], []
    for i in n_items:
        nb = bf(-lb[i])
        p.append(jnp.where(eye, 1.0, 0.0) - lb[i])
        mpow.append(_dg(nb, _bd(nb), 1, 0))
    for _ in range(4):
        for i in n_items:
            mbf = bf(mpow[i])
            pm = _dg(jnp.concatenate([bf(p[i]), mbf], axis=0), _bd(mbf), 1, 0)
            p[i] = p[i] + pm[:CH]
            mpow[i] = pm[CH:]
    tinv = [bf(p[i] + _dg(bf(p[i]), _bd(bf(mpow[i])), 1, 0)) for i in n_items]

    lmv = [_dg(bf(jnp.concatenate([lk[i], mk[i]], axis=0)), _bd(bf(items[i]["v"])), 1, 0) for i in n_items]
    kp = [_dg(tinv[i], _bd(bf(items[i]["kap"])), 1, 0) for i in n_items]
    vp = [_dg(tinv[i], _bd(bf(lmv[i][:CH])), 1, 0) for i in n_items]
    out = []
    for i in n_items:
        it = items[i]
        mbb = bf(mb[i])
        rp = it["rho"] - _dg(mbb, _bd(bf(kp[i])), 1, 0)
        yl = lmv[i][CH:] - _dg(mbb, _bd(bf(vp[i])), 1, 0)
        g = jnp.where(eye, it["gdiag"], 0.0) - _fold(_dg(bf(kp[i]), bf(it["bhat"]), 0, 0))
        hmat = _fold(_dg(bf(jnp.concatenate([it["v"], -vp[i]], axis=0)),
                         bf(jnp.concatenate([it["khat"], it["bhat"]], axis=0)), 0, 0))
        out.append((g, hmat, rp, yl))
    return out


def _rwkv_chunk_kernel(z_ref, hp_ref, hn_ref, mu_ref, w0_ref, w2_ref, a0_ref, a2_ref, kkw_ref, ka_ref, rk_ref,
                       bd_ref, g_ref, h_ref, rp_ref, yl_ref, bonus_ref, *, ct):
    i = pl.program_id(1)
    z = z_ref[0]
    zs = z[:, :RW_SHIFT]
    valid_prev = jnp.where((i != 0) & (i != ct), 1.0, 0.0)
    valid_next = jnp.where((i != ct - 1) & (i != ct), 1.0, 0.0)
    prev_row = hp_ref[0, HALO - 1:HALO, :RW_SHIFT] * valid_prev
    next_row = hn_ref[0, 0:1, :RW_SHIFT] * valid_next
    row = lax.broadcasted_iota(jnp.int32, (TM, RW_SHIFT), 0)
    tt = lax.broadcasted_iota(jnp.int32, (TM, TM), 0)
    jj = lax.broadcasted_iota(jnp.int32, (TM, TM), 1)
    same_chunk = (tt >> 6) == (jj >> 6)
    bd = bd_ref[...]
    def direction(d):
        if d == 0:
            nb = jnp.where(row == 0, prev_row, pltpu.roll(zs, 1, 0))
        else:
            nb = jnp.where(row == TM - 1, next_row, pltpu.roll(zs, TM - 1, 0))
        zd = zs + mu_ref[d:d + 1, :] * (nb - zs)
        r = zd[:, :RW]
        k = zd[:, RW:2 * RW]
        v = zd[:, 2 * RW:3 * RW]
        lw = zd[:, 3 * RW:3 * RW + LORA_W]
        la = zd[:, 3 * RW + LORA_W:]
        w_log = w0_ref[d:d + 1, :] + _dot3(jnp.tanh(lw), w2_ref[d, 0], w2_ref[d, 1])
        a_pre = a0_ref[d:d + 1, :] + _dot3(la, a2_ref[d, 0], a2_ref[d, 1])
        kkr = k * kkw_ref[...]
        ss = _sum_dot(kkr * kkr, bd, 2)
        yield
        sp = jnp.maximum(-w_log, 0.0) + jnp.log(1.0 + jnp.exp(-jnp.abs(w_log)))
        logw = -jnp.exp(-sp - 0.5)
        tri = jnp.where(same_chunk & ((jj >= tt) if d == 1 else (jj <= tt)), 1.0, 0.0).astype(BF16)
        cum, rest = None, logw
        for _ in range(3):
            piece = rest.astype(BF16)
            part = jnp.dot(tri, piece, preferred_element_type=F32)
            cum = part if cum is None else cum + part
            rest = rest - piece.astype(F32)
        a = _sigmoid(a_pre)
        kk = kkr * lax.rsqrt(jnp.maximum(ss, 1e-24))
        k2 = k * (1.0 + (a - 1.0) * ka_ref[...])
        bv = kk * a
        bon = _sum_dot(r * k2 * rk_ref[...], bd, 2) * v
        yield
        e_neg = jnp.exp(-cum)
        streams = dict(kap=kk * jnp.exp(cum - logw), kt=k2 * e_neg, bt=bv * e_neg, rho=r * jnp.exp(cum), v=v)
        chunk_items = []
        for c in range(TM // CH):
            sl = slice(c * CH, (c + 1) * CH)
            it = {name: val[sl] for name, val in streams.items()}
            last = c * CH if d == 1 else (c + 1) * CH - 1
            tot = cum[last:last + 1]
            e_tot = jnp.exp(tot - cum[sl])
            it.update(khat=k2[sl] * e_tot, bhat=bv[sl] * e_tot, gdiag=jnp.exp(tot), rev=d == 1)
            chunk_items.append(it)
        return chunk_items, bon

    gens = [direction(0), direction(1)]
    done = [None, None]
    while any(res is None for res in done):
        for d, gen in enumerate(gens):
            if done[d] is None:
                try:
                    next(gen)
                except StopIteration as stop:
                    done[d] = stop.value
    items = done[0][0] + done[1][0]
    bonus = done[0][1] + done[1][1]
    for idx, (g, hm, rp, yl) in enumerate(_chunk_mats(items)):
        d, c = divmod(idx, TM // CH)
        g_ref[0, d, c] = g.astype(BF16)
        h_ref[0, d, c] = hm
        rp_ref[0, d, c] = rp.astype(BF16)
        yl_ref[0, d, c] = yl
    bonus_ref[0] = bonus


def _rwkv_chunks(zr, mu, w0, w2, a0, a2, kkw, ka, rk, bd256, ct):
    b, tt, _ = zr.shape
    nt = tt // TM
    nch = tt // CH
    cpt = TM // CH
    nhb = tt // HALO
    mats = pl.BlockSpec((1, 2, cpt, CH, RW), lambda bi, i: (bi, 0, i, 0, 0))
    mshape = lambda dt: jax.ShapeDtypeStruct((b, 2, nch, CH, RW), dt)
    return pl.pallas_call(
        functools.partial(_rwkv_chunk_kernel, ct=ct),
        grid=(b, nt),
        in_specs=[
            pl.BlockSpec((1, TM, COLS_RW), lambda bi, i: (bi, i, 0)),
            pl.BlockSpec((1, HALO, COLS_RW), lambda bi, i: (bi, jnp.maximum(i * (TM // HALO) - 1, 0), 0)),
            pl.BlockSpec((1, HALO, COLS_RW), lambda bi, i: (bi, jnp.minimum((i + 1) * (TM // HALO), nhb - 1), 0)),
            _const_spec((2, RW_SHIFT)),
            _const_spec((2, RW)),
            _const_spec((2, 2, LORA_W, RW)),
            _const_spec((2, RW)),
            _const_spec((2, 2, LORA_A, RW)),
            _const_spec((1, RW)),
            _const_spec((1, RW)),
            _const_spec((1, RW)),
            _const_spec((RW, RW)),
        ],
        out_specs=[mats, mats, mats, mats, pl.BlockSpec((1, TM, RW), lambda bi, i: (bi, i, 0))],
        out_shape=[mshape(BF16), mshape(F32), mshape(BF16), mshape(F32), jax.ShapeDtypeStruct((b, tt, RW), F32)],
        compiler_params=_cparams(("parallel", "parallel")),
        name="rwkv_chunks",
    )(zr, zr, zr, mu, w0, w2, a0, a2, kkw, ka, rk, bd256)


CHAIN_GROUP = 4


def _rwkv_chain_kernel(gf_ref, hf_ref, rf_ref, yf_ref, gr_ref, hr_ref, rr_ref, yr_ref, of_ref, or_ref, s_ref,
                       *, nb):
    @pl.when(pl.program_id(0) == 0)
    def _():
        s_ref[...] = jnp.zeros(s_ref.shape, F32)

    dirs = ((gf_ref, hf_ref, rf_ref, yf_ref, of_ref), (gr_ref, hr_ref, rr_ref, yr_ref, or_ref))
    state = [[s_ref[d, bi] for bi in range(nb)] for d in range(2)]
    for step in range(CHAIN_GROUP):
        for d, (g_ref, h_ref, rp_ref, yl_ref, o_ref) in enumerate(dirs):
            c = step if d == 0 else CHAIN_GROUP - 1 - step
            for bi in range(nb):
                sb = state[d][bi].astype(BF16)
                o_ref[bi, c * CH:(c + 1) * CH, :] = _dg(rp_ref[bi, 0, c], _bd(sb), 1, 1) + yl_ref[bi, 0, c]
                state[d][bi] = _dg(sb, _bd(g_ref[bi, 0, c]), 1, 0) + h_ref[bi, 0, c]
    for d in range(2):
        for bi in range(nb):
            s_ref[d, bi] = state[d][bi]


def _rwkv_chain(g, h, rp, yl, seq, ctx_len):
    b, _, nch, _, _ = g.shape
    assert ctx_len == CHAIN_GROUP * CH and seq % (CHAIN_GROUP * CH) == 0
    ngrp = nch // CHAIN_GROUP
    n_lat = seq // (CHAIN_GROUP * CH)

    def gf(s):
        return jnp.where(s == 0, n_lat, s - 1)

    def gr(s):
        return ngrp - 1 - s

    fwd = pl.BlockSpec((b, 1, CHAIN_GROUP, CH, RW), lambda s: (0, 0, gf(s), 0, 0))
    rev = pl.BlockSpec((b, 1, CHAIN_GROUP, CH, RW), lambda s: (0, 1, gr(s), 0, 0))
    yshape = jax.ShapeDtypeStruct((b, nch * CH, RW), F32)
    return pl.pallas_call(
        functools.partial(_rwkv_chain_kernel, nb=b),
        grid=(ngrp,),
        in_specs=[fwd, fwd, fwd, fwd, rev, rev, rev, rev],
        out_specs=[pl.BlockSpec((b, CHAIN_GROUP * CH, RW), lambda s: (0, gf(s), 0)),
                   pl.BlockSpec((b, CHAIN_GROUP * CH, RW), lambda s: (0, gr(s), 0))],
        out_shape=[yshape, yshape],
        scratch_shapes=[pltpu.VMEM((2, b, CH, RW), F32)],
        compiler_params=_cparams(("arbitrary",)),
        name="rwkv_chain",
    )(g, h, rp, yl, g, h, rp, yl)


def _merge_kernel(*refs, nblk, ctx_mode):
    (x_ref, mod_ref, gpre_ref, gpost_ref, za_ref, hp_ref, hn_ref, ca_ref, yb_ref, yc_ref, yf_ref, yr_ref,
     bonus_ref, lg_ref, g2_ref, lnw_ref, lnb_ref, bd_ref, wb_ref, wg_ref, bg_ref, wo_ref) = refs[:22]
    o_ref = refs[-1]
    tm = x_ref.shape[1]
    x = x_ref[0]
    m = mod_ref[0, 0]
    hb = _rms_mod(x, gpre_ref[...], m[0:1], m[1:2]).astype(BF16)

    def gate_mm(bidx):
        return jnp.dot(hb, wg_ref[:, bidx * D_MODEL:(bidx + 1) * D_MODEL], preferred_element_type=F32)

    def gated(bidx, pre, ys):
        gate2 = jnp.tanh(pre + bg_ref[:, bidx * D_MODEL:(bidx + 1) * D_MODEL]) + 1.0
        return gate2 * jnp.dot(ys, wb_ref[bidx], preferred_element_type=F32)

    za = za_ref[0]
    u = za[:, 256:512] * za[:, 512:768]
    valid_prev, valid_next = _edge_valid(pl.program_id(1), nblk, ctx_mode)
    up = hp_ref[0, HALO - 1:HALO, 256:512] * hp_ref[0, HALO - 1:HALO, 512:768] * valid_prev
    un = hn_ref[0, 0:1, 256:512] * hn_ref[0, 0:1, 512:768] * valid_next
    row = lax.broadcasted_iota(jnp.int32, (tm, 256), 0)
    u_prev = jnp.where(row == 0, up, pltpu.roll(u, 1, 0))
    u_next = jnp.where(row == tm - 1, un, pltpu.roll(u, tm - 1, 0))
    ya = za[:, :256] * (u_prev * ca_ref[0:1, :] + u * ca_ref[1:2, :] + u_next * ca_ref[2:3, :])

    bd = bd_ref[...]
    ones_dot = lambda piece: jnp.dot(piece, bd, preferred_element_type=F32)
    y = yf_ref[0] + yr_ref[0]
    p1 = y.astype(BF16)
    s1 = ones_dot(p1)
    pre1 = gate_mm(1)
    r1 = y - p1.astype(F32)
    p2 = r1.astype(BF16)
    s2 = ones_dot(p2)
    acc = gated(1, pre1, yb_ref[0])
    s3 = ones_dot((r1 - p2.astype(F32)).astype(BF16))
    pre2 = gate_mm(2)
    yc0 = y - (s1 + s2 + s3) * (1.0 / HEAD_DIM)
    sq = yc0 * yc0
    q1 = sq.astype(BF16)
    v1 = ones_dot(q1)
    acc = acc + gated(2, pre2, yc_ref[0])
    v2 = ones_dot((sq - q1.astype(F32)).astype(BF16))
    pre0 = gate_mm(0)
    lgate = _dot3(_sigmoid(lg_ref[0]), g2_ref[0], g2_ref[1])
    acc = acc + gated(0, pre0, ya.astype(BF16))
    pre3 = gate_mm(3)
    var = (v1 + v2) * (1.0 / HEAD_DIM)
    yn = yc0 * lax.rsqrt(var + LN_X_EPS) * lnw_ref[...] + lnb_ref[...] + bonus_ref[0]
    acc = acc + gated(3, pre3, (yn * lgate).astype(BF16))
    mo = jnp.dot(acc.astype(BF16), wo_ref[...], preferred_element_type=F32)
    o_ref[0] = x + m[2:3] * _rms(mo, gpost_ref[...])


def _merge(xa, modsel, g_pre, g_post, za, conv_a, yb, yc, yf, yr, bonus, zr, g2, ln_w, ln_b, bd256,
           w_branch, w_gate, b_gate, w_o, seq, tm, prev=None):
    b, tt, _ = xa.shape
    ctx_mode = prev is not None
    nblk, blk, steps, alias_specs = _token_tiling(seq, tm, ctx_mode, 1)
    tile = lambda w: pl.BlockSpec((1, tm, w), lambda bi, i: (bi, blk(i), 0))
    return pl.pallas_call(
        functools.partial(_merge_kernel, nblk=nblk, ctx_mode=ctx_mode),
        grid=(b, steps),
        in_specs=[
            tile(D_MODEL),
            pl.BlockSpec((1, 1, 6, D_MODEL), lambda bi, i: (bi, int(ctx_mode), 0, 0)),
            _const_spec((1, D_MODEL)),
            _const_spec((1, D_MODEL)),
            tile(COLS_A),
            *_halo_specs(COLS_A, tm, blk, tt // HALO),
            _const_spec((3, 256)),
            tile(256), tile(256), tile(256), tile(256), tile(256),
            pl.BlockSpec((1, tm, LORA_G), lambda bi, i: (bi, blk(i), RW_SHIFT // LORA_G)),
            _const_spec((2, LORA_G, RW)),
            _const_spec((1, RW)),
            _const_spec((1, RW)),
            _const_spec((RW, RW)),
            _const_spec((4, 256, D_MODEL)),
            _const_spec((D_MODEL, 4 * D_MODEL)),
            _const_spec((1, 4 * D_MODEL)),
            _const_spec((D_MODEL, D_MODEL)),
        ] + alias_specs,
        out_specs=tile(D_MODEL),
        out_shape=jax.ShapeDtypeStruct((b, tt, D_MODEL), F32),
        input_output_aliases={22: 0} if ctx_mode else {},
        compiler_params=_cparams(("parallel", "parallel")),
        name="merge_ctx" if ctx_mode else "merge",
    )(xa, modsel, g_pre, g_post, za, za, za, conv_a, yb, yc, yf, yr, bonus, zr, g2, ln_w, ln_b, bd256,
      w_branch, w_gate, b_gate, w_o, *((prev,) if ctx_mode else ()))


def _ffn_kernel(*refs, nblk, ctx_mode):
    x_ref, hp_ref, hn_ref, mod_ref, gpre_ref, gpost_ref, wu_ref, cw_ref, wd_ref = refs[:9]
    o_ref, act_ref = refs[-2:]
    tm = x_ref.shape[1]
    x = x_ref[0]
    m = mod_ref[0, 0]
    xx = jnp.concatenate([hp_ref[0], x, hn_ref[0]], axis=0)
    nrow = tm + 2 * HALO
    row = lax.broadcasted_iota(jnp.int32, (nrow, 1), 0)
    valid_prev, valid_next = _edge_valid(pl.program_id(1), nblk, ctx_mode)
    rowmask = jnp.where(row < HALO, valid_prev, jnp.where(row >= tm + HALO, valid_next, 1.0))
    hb = (_rms_mod(xx, gpre_ref[...], m[3:4], m[4:5]) * rowmask).astype(BF16)

    def conv(u, col):
        w = cw_ref[:, col:col + FF_CHUNK]
        c = (pltpu.roll(u, 1, 0) * w[0:1] + u * w[1:2] + pltpu.roll(u, nrow - 1, 0) * w[2:3])
        return c[HALO:HALO + tm]

    for j in range(D_FF // FF_CHUNK):
        ca = conv(jnp.dot(hb, wu_ref[:, j * FF_CHUNK:(j + 1) * FF_CHUNK], preferred_element_type=F32),
                  j * FF_CHUNK)
        cg = conv(jnp.dot(hb, wu_ref[:, D_FF + j * FF_CHUNK:D_FF + (j + 1) * FF_CHUNK],
                          preferred_element_type=F32), D_FF + j * FF_CHUNK)
        act_ref[:, j * FF_CHUNK:(j + 1) * FF_CHUNK] = (ca * (cg * (1.0 + jnp.tanh(cg)))).astype(BF16)
    f = jnp.dot(act_ref[...], wd_ref[...], preferred_element_type=F32)
    o_ref[0] = x + m[5:6] * _rms(f, gpost_ref[...])


def _ffn(xa, modsel, g_pre, g_post, w_up, conv_w, w_down, seq, tm, prev=None, out_rows=None):
    b, tt, _ = xa.shape
    ctx_mode = prev is not None
    nblk, blk, steps, alias_specs = _token_tiling(seq, tm, ctx_mode, 1)
    return pl.pallas_call(
        functools.partial(_ffn_kernel, nblk=nblk, ctx_mode=ctx_mode),
        grid=(b, steps),
        in_specs=[
            pl.BlockSpec((1, tm, D_MODEL), lambda bi, i: (bi, blk(i), 0)),
            *_halo_specs(D_MODEL, tm, blk, tt // HALO),
            pl.BlockSpec((1, 1, 6, D_MODEL), lambda bi, i: (bi, int(ctx_mode), 0, 0)),
            _const_spec((1, D_MODEL)),
            _const_spec((1, D_MODEL)),
            _const_spec((D_MODEL, 2 * D_FF)),
            _const_spec((3, 2 * D_FF)),
            _const_spec((D_FF, D_MODEL)),
        ] + alias_specs,
        out_specs=pl.BlockSpec((1, tm, D_MODEL), lambda bi, i: (bi, blk(i), 0)),
        out_shape=jax.ShapeDtypeStruct((b, out_rows or tt, D_MODEL), F32),
        scratch_shapes=[pltpu.VMEM((tm, D_FF), BF16)],
        input_output_aliases={9: 0} if ctx_mode else {},
        compiler_params=_cparams(("parallel", "parallel")),
        name="conv_ffn_ctx" if ctx_mode else "conv_ffn",
    )(xa, xa, xa, modsel, g_pre, g_post, w_up, conv_w, w_down, *((prev,) if ctx_mode else ()))


def _rope_tables(seq, ctx_len):
    t = np.arange(seq)
    row = (t // GRID_W).astype(np.float32)
    col = (t % GRID_W).astype(np.float32)
    n_freq = HEAD_DIM // 4
    inv_freq = jnp.asarray(ROPE_THETA, F32) ** (-jnp.arange(n_freq, dtype=F32) / n_freq)
    ang = jnp.concatenate([jnp.asarray(row)[:, None] * inv_freq, jnp.asarray(col)[:, None] * inv_freq], axis=-1)
    cos = jnp.repeat(jnp.cos(ang), 2, axis=-1)
    sin = jnp.repeat(jnp.sin(ang), 2, axis=-1) * jnp.tile(jnp.asarray([-1.0, 1.0], F32), HEAD_DIM // 2)
    cos = jnp.concatenate([cos, jnp.ones((ctx_len, HEAD_DIM), F32)], axis=0)
    sin = jnp.concatenate([sin, jnp.zeros((ctx_len, HEAD_DIM), F32)], axis=0)
    nh = GQA_Q_HEADS + GQA_KV_HEADS
    return jnp.tile(cos, (1, nh)), jnp.tile(sin, (1, nh))


def _block_ones(n, scale):
    idx = np.arange(n) // HEAD_DIM
    return jnp.asarray((idx[:, None] == idx[None, :]).astype(np.float32) * scale)


def kernel(x, c, ctx, c_ctx, ada_w, ada_b, norm_mix_pre, norm_mix_post, norm_ffn_pre, norm_ffn_post, w_in, conv_a, na_bias, q_norm, k_norm, rw_mu, rw_w0, rw_w2, rw_a0, rw_a2, rw_kk, rw_ka, rw_rk, rw_g2, rw_ln_w, rw_ln_b, w_branch, w_gate, b_gate, w_o, ffn_up, ffn_conv, ffn_down):
    b, seq, _ = x.shape
    ctx_len = ctx.shape[1]
    depth = ada_w.shape[0]
    assert ctx_len == TM and seq % GQA_TQ == 0 and seq % TML == 0 and seq // TM >= 3 and b + 1 <= 8
    ct = seq // TM
    rows = seq // GRID_W

    cvec = jnp.zeros((8, D_MODEL), F32).at[:b].set(c).at[b].set(c_ctx)
    mods = _ada(cvec, ada_w, ada_b)
    cos_t, sin_t = _rope_tables(seq, ctx_len)
    bd384 = _block_ones(384, 1.0 / HEAD_DIM).astype(BF16)
    bd256 = _block_ones(256, 1.0).astype(BF16)

    def hi_lo(w):
        hi = w.astype(BF16)
        return jnp.stack([hi, (w - hi.astype(F32)).astype(BF16)], axis=-3)

    xa = jnp.concatenate([x, ctx], axis=1)
    for l in range(depth):
        ml = mods[l].reshape(8, 6, D_MODEL)
        modsel = jnp.stack([ml[:b], jnp.broadcast_to(ml[b][None], (b, 6, D_MODEL))], axis=1)
        row2 = lambda a: a.reshape(1, -1)
        gqk = jnp.concatenate([jnp.tile(q_norm[l], GQA_Q_HEADS) * (HEAD_DIM ** -0.5 * np.log2(np.e)),
                               jnp.tile(k_norm[l], GQA_KV_HEADS)]).reshape(1, -1)
        ip_args = (xa, modsel, row2(norm_mix_pre[l]), w_in[l].astype(BF16), cos_t, sin_t, gqk, bd384, seq)
        za, qn, kn, vn, qg, kg, vg, zr = _inproj(*ip_args, ctx_len, prev=_inproj(*ip_args, TML))
        yb = _na(qn, kn, vn, _na_bias_table(na_bias[l], rows), ct)
        yc = _gqa(qg, kg, vg, seq, ctx_len)
        g, h, rp, yl, bonus = _rwkv_chunks(zr, rw_mu[l], rw_w0[l], hi_lo(rw_w2[l]), rw_a0[l], hi_lo(rw_a2[l]),
                                           row2(rw_kk[l]), row2(rw_ka[l]), row2(rw_rk[l]), bd256, ct)
        yf, yr = _rwkv_chain(g, h, rp, yl, seq, ctx_len)
        mg_args = (xa, modsel, row2(norm_mix_pre[l]), row2(norm_mix_post[l]), za, conv_a[l], yb, yc, yf, yr,
                   bonus, zr, hi_lo(rw_g2[l]), row2(rw_ln_w[l]), row2(rw_ln_b[l]), bd256,
                   w_branch[l].astype(BF16), (0.5 * w_gate[l]).astype(BF16), row2(0.5 * b_gate[l]),
                   (0.5 * w_o[l]).astype(BF16), seq)
        xm = _merge(*mg_args, ctx_len, prev=_merge(*mg_args, TML))
        conv_w = jnp.concatenate([ffn_conv[l][:, :D_FF], 0.5 * ffn_conv[l][:, D_FF:]], axis=1)
        ff_args = (xm, modsel, row2(norm_ffn_pre[l]), row2(norm_ffn_post[l]), ffn_up[l].astype(BF16),
                   conv_w, ffn_down[l].astype(BF16), seq)
        if l < depth - 1:
            xa = _ffn(*ff_args, ctx_len, prev=_ffn(*ff_args, TML))
        else:
            xa = _ffn(*ff_args, TML, out_rows=seq)
    return xa
```

```python
import functools

import numpy as np
import jax
import jax.numpy as jnp
from jax import lax
from jax.experimental import pallas as pl
from jax.experimental.pallas import tpu as pltpu

F32 = jnp.float32
BF16 = jnp.bfloat16
HIGHEST = lax.Precision.HIGHEST

D_MODEL = 1024
GRID_W = 64
HEAD_DIM = 64
NA_HEADS = 4
NA_WIN_R = 8
NA_WIN_C = 16
GQA_Q_HEADS = 4
GQA_KV_HEADS = 2
ROPE_THETA = 10000.0
RWKV_HEADS = 4
RW = RWKV_HEADS * HEAD_DIM
LORA_W = 64
LORA_A = 64
LORA_G = 128
RW_SHIFT = 3 * RW + LORA_W + LORA_A
D_FF = 2816
NORM_EPS = 1e-6
LN_X_EPS = 64e-5
COLS_A = 768
COLS_NA = 768
COLS_GQA = 512
COLS_RW = 1024
D_IN = COLS_A + COLS_NA + COLS_GQA + COLS_RW

TM = 256
TML = 512
CH = 64
HALO = 8
FF_CHUNK = 256
GQA_VROWS = 80
GQA_TQ = 512
GQA_TK = 512
NEG = -1e30
VMEM_LIMIT = 56 * 1024 * 1024


def _cparams(sem):
    return pltpu.CompilerParams(dimension_semantics=sem, vmem_limit_bytes=VMEM_LIMIT)


def _const_spec(shape):
    nd = len(shape)
    return pl.BlockSpec(shape, lambda *_: (0,) * nd, pipeline_mode=pl.Buffered(1))


def _token_tiling(seq, tm, ctx_mode, n_out):
    nblk = seq // tm
    blk = (lambda i: i * 0 + nblk) if ctx_mode else (lambda i: i)
    alias_specs = [pl.BlockSpec(memory_space=pl.ANY)] * n_out if ctx_mode else []
    return nblk, blk, (1 if ctx_mode else nblk), alias_specs


def _halo_specs(width, tm, blk, nhb):
    per = tm // HALO
    return [pl.BlockSpec((1, HALO, width), lambda bi, i: (bi, jnp.maximum(blk(i) * per - 1, 0), 0)),
            pl.BlockSpec((1, HALO, width), lambda bi, i: (bi, jnp.minimum((blk(i) + 1) * per, nhb - 1), 0))]


def _edge_valid(i, nblk, ctx_mode):
    if ctx_mode:
        return 0.0, 0.0
    return jnp.where(i != 0, 1.0, 0.0), jnp.where(i != nblk - 1, 1.0, 0.0)


def _dg(a, b, ca, cb, **kw):
    return lax.dot_general(a, b, (((ca,), (cb,)), ((), ())), preferred_element_type=F32, **kw)


def _sum_dot(x, w, terms):
    acc = None
    for _ in range(terms):
        piece = x.astype(BF16)
        part = jnp.dot(piece, w, preferred_element_type=F32)
        acc = part if acc is None else acc + part
        x = x - piece.astype(F32)
    return acc


def _dot3(x, w_hi, w_lo):
    x_hi = x.astype(BF16)
    x_lo = (x - x_hi.astype(F32)).astype(BF16)
    return (jnp.dot(x_hi, w_hi, preferred_element_type=F32) + jnp.dot(x_lo, w_hi, preferred_element_type=F32)
            + jnp.dot(x_hi, w_lo, preferred_element_type=F32))


def _rms_mod(x, g, shift, scale):
    y = x * lax.rsqrt(jnp.mean(x * x, axis=-1, keepdims=True) + NORM_EPS) * g
    return y * (1.0 + scale) + shift


def _rms(x, g):
    return x * lax.rsqrt(jnp.mean(x * x, axis=-1, keepdims=True) + NORM_EPS) * g


def _sigmoid(x):
    return 0.5 * jnp.tanh(0.5 * x) + 0.5


def _ada_kernel(c_ref, w_ref, b_ref, o_ref):
    c = c_ref[...]
    s = c * _sigmoid(c)
    o_ref[0] = jnp.dot(s, w_ref[0], precision=HIGHEST, preferred_element_type=F32) + b_ref[0]


def _ada(cvec, ada_w, ada_b):
    depth = ada_w.shape[0]
    nblk = ada_w.shape[2] // D_MODEL
    return pl.pallas_call(
        _ada_kernel,
        grid=(depth, nblk),
        in_specs=[
            pl.BlockSpec((8, D_MODEL), lambda l, j: (0, 0)),
            pl.BlockSpec((1, D_MODEL, D_MODEL), lambda l, j: (l, 0, j)),
            pl.BlockSpec((1, 1, D_MODEL), lambda l, j: (l, 0, j)),
        ],
        out_specs=pl.BlockSpec((1, 8, D_MODEL), lambda l, j: (l, 0, j)),
        out_shape=jax.ShapeDtypeStruct((depth, 8, ada_w.shape[2]), F32),
        compiler_params=_cparams(("parallel", "parallel")),
        name="ada_mod",
    )(cvec, ada_w, ada_b.reshape(depth, 1, -1))


def _inproj_kernel(*refs):
    x_ref, mod_ref, g_ref, w_ref, cos_ref, sin_ref, gqk_ref, bd_ref = refs[:8]
    za_ref, qn_ref, kn_ref, vn_ref, qg_ref, kg_ref, vg_ref, zr_ref = refs[-8:]
    tm = x_ref.shape[1]
    x = x_ref[0]
    m = mod_ref[0, 0]
    hb = _rms_mod(x, g_ref[...], m[0:1], m[1:2]).astype(BF16)
    e1, e2, e3 = COLS_A, COLS_A + COLS_NA, COLS_A + COLS_NA + COLS_GQA
    g = jnp.dot(hb, w_ref[:, e2:e3], preferred_element_type=F32)
    na = jnp.dot(hb, w_ref[:, e1:e2], preferred_element_type=F32)
    lane = lax.broadcasted_iota(jnp.int32, (tm, 128), 1)
    scale = HEAD_DIM ** -0.5 * np.log2(np.e)
    qn_ref[0] = jnp.concatenate([(na[:, :128] * scale).T, (na[:, 128:256] * scale).T], axis=0).astype(BF16)
    for hd in range(NA_HEADS):
        kn_ref[0, hd] = na[:, 256 + hd * 64:256 + (hd + 1) * 64].astype(BF16)
        vpair = na[:, 512 + (hd // 2) * 128:512 + (hd // 2 + 1) * 128]
        vh = vpair if hd % 2 == 0 else pltpu.roll(vpair, 64, 1)
        vn_ref[0, hd] = jnp.where(lane < 64, vh, jnp.where(lane == 64, 1.0, 0.0)).T.astype(BF16)
    qk = g[:, :384]
    ms = _sum_dot(qk * qk, bd_ref[...], 2)
    za_ref[0] = jnp.dot(hb, w_ref[:, :e1], preferred_element_type=F32)
    zr_ref[0] = jnp.dot(hb, w_ref[:, e3:], preferred_element_type=F32)
    qkn = qk * lax.rsqrt(ms + NORM_EPS) * gqk_ref[...]
    even = (lane & 1) == 0
    parts = []
    for j in range(3):
        s = qkn[:, j * 128:(j + 1) * 128]
        sw = jnp.where(even, pltpu.roll(s, 127, 1), pltpu.roll(s, 1, 1))
        parts.append(s * cos_ref[:, j * 128:(j + 1) * 128] + sw * sin_ref[:, j * 128:(j + 1) * 128])
    qg_ref[0] = jnp.concatenate([parts[0].T, parts[1].T], axis=0).astype(BF16)
    for hd in range(GQA_KV_HEADS):
        kg_ref[0, hd] = parts[2][:, hd * 64:(hd + 1) * 64].astype(BF16)
    vt = g[:, 384:512]
    for hd in range(GQA_KV_HEADS):
        vh = vt if hd == 0 else pltpu.roll(vt, 64, 1)
        vext = jnp.where(lane < 64, vh, jnp.where(lane == 64, 1.0, 0.0))
        vg_ref[0, hd] = vext.T.astype(BF16)


def _inproj(xa, modsel, g_pre, w_in, cos_t, sin_t, gqk, bd384, seq, tm, prev=None):
    b, tt, _ = xa.shape
    ctx_mode = prev is not None
    nblk, blk, steps, alias_specs = _token_tiling(seq, tm, ctx_mode, 8)
    tile = lambda w: pl.BlockSpec((1, tm, w), lambda bi, i: (bi, blk(i), 0))
    heads = lambda nh: pl.BlockSpec((1, nh, tm, 64), lambda bi, i: (bi, 0, blk(i), 0))
    hs = lambda nh: jax.ShapeDtypeStruct((b, nh, tt, 64), BF16)
    return pl.pallas_call(
        _inproj_kernel,
        grid=(b, steps),
        in_specs=[
            tile(D_MODEL),
            pl.BlockSpec((1, 1, 6, D_MODEL), lambda bi, i: (bi, int(ctx_mode), 0, 0)),
            _const_spec((1, D_MODEL)),
            _const_spec((D_MODEL, D_IN)),
            pl.BlockSpec((tm, 384), lambda bi, i: (blk(i), 0)),
            pl.BlockSpec((tm, 384), lambda bi, i: (blk(i), 0)),
            _const_spec((1, 384)),
            _const_spec((384, 384)),
        ] + alias_specs,
        out_specs=[tile(COLS_A),
                   pl.BlockSpec((1, NA_HEADS * HEAD_DIM, tm), lambda bi, i: (bi, 0, blk(i))),
                   heads(4),
                   pl.BlockSpec((1, NA_HEADS, 128, tm), lambda bi, i: (bi, 0, 0, blk(i))),
                   pl.BlockSpec((1, GQA_Q_HEADS * HEAD_DIM, tm), lambda bi, i: (bi, 0, blk(i))),
                   heads(2),
                   pl.BlockSpec((1, 2, 128, tm), lambda bi, i: (bi, 0, 0, blk(i))), tile(COLS_RW)],
        out_shape=[jax.ShapeDtypeStruct((b, tt, COLS_A), F32),
                   jax.ShapeDtypeStruct((b, NA_HEADS * HEAD_DIM, tt), BF16),
                   hs(4),
                   jax.ShapeDtypeStruct((b, NA_HEADS, 128, tt), BF16),
                   jax.ShapeDtypeStruct((b, GQA_Q_HEADS * HEAD_DIM, tt), BF16),
                   hs(2),
                   jax.ShapeDtypeStruct((b, 2, 128, tt), BF16),
                   jax.ShapeDtypeStruct((b, tt, COLS_RW), F32)],
        input_output_aliases={8 + k: k for k in range(8)} if ctx_mode else {},
        compiler_params=_cparams(("parallel", "parallel")),
        name="inproj_ctx" if ctx_mode else "inproj",
    )(xa, modsel, g_pre, w_in, cos_t, sin_t, gqk, bd384, *(prev or ()))


def _na_kernel(q_ref, k0_ref, k1_ref, k2_ref, kc_ref, v0_ref, v1_ref, v2_ref, vc_ref, bias_ref, o_ref, st_ref):
    slot = pl.program_id(1) % 2
    for hd in range(NA_HEADS):
        kcat = jnp.concatenate([k0_ref[0, hd], k1_ref[0, hd], k2_ref[0, hd], kc_ref[0, hd]], axis=0)
        st_ref[slot, hd] = jnp.dot(kcat, q_ref[0, hd * HEAD_DIM:(hd + 1) * HEAD_DIM, :],
                                   preferred_element_type=F32)
    outs = []
    for hd in range(NA_HEADS):
        st_loc = st_ref[slot, hd, :3 * TM, :] + bias_ref[0, hd]
        st_ctx = st_ref[slot, hd, 3 * TM:, :]
        m = jnp.maximum(jnp.max(st_loc, axis=0, keepdims=True), jnp.max(st_ctx, axis=0, keepdims=True))
        pt = jnp.concatenate([jnp.exp2(st_loc - m), jnp.exp2(st_ctx - m)], axis=0).astype(BF16)
        vt = jnp.concatenate([r[0, hd, :GQA_VROWS, :] for r in (v0_ref, v1_ref, v2_ref, vc_ref)], axis=1)
        acc = jnp.dot(vt, pt, preferred_element_type=F32)
        outs.append(acc[:HEAD_DIM] * (1.0 / acc[HEAD_DIM:HEAD_DIM + 1]))
    o_ref[0] = jnp.concatenate(outs, axis=0).T.astype(o_ref.dtype)


def _na(qn, kn, vn, bias_tab, ct):
    b, _, tt = qn.shape
    nt = tt // TM

    def key_tile(i, j):
        return ct if j is None else jnp.clip(i - 1, 0, ct - 3) + j

    k_spec = lambda j: pl.BlockSpec((1, 4, TM, 64), lambda bi, i: (bi, 0, key_tile(i, j), 0))
    v_spec = lambda j: pl.BlockSpec((1, 4, 128, TM), lambda bi, i: (bi, 0, 0, key_tile(i, j)))

    def pattern(i):
        return jnp.where(i == ct, 3, jnp.where(i == 0, 0, jnp.where(i == ct - 1, 2, 1)))

    return pl.pallas_call(
        _na_kernel,
        grid=(b, nt),
        in_specs=[
            pl.BlockSpec((1, NA_HEADS * HEAD_DIM, TM), lambda bi, i: (bi, 0, i)),
            k_spec(0), k_spec(1), k_spec(2), k_spec(None),
            v_spec(0), v_spec(1), v_spec(2), v_spec(None),
            pl.BlockSpec((1, 4, 3 * TM, TM), lambda bi, i: (pattern(i), 0, 0, 0)),
        ],
        out_specs=pl.BlockSpec((1, TM, 256), lambda bi, i: (bi, i, 0)),
        out_shape=jax.ShapeDtypeStruct((b, tt, 256), BF16),
        scratch_shapes=[pltpu.VMEM((2, NA_HEADS, 4 * TM, TM), F32)],
        compiler_params=_cparams(("parallel", "parallel")),
        name="na_attn",
    )(qn, kn, kn, kn, kn, vn, vn, vn, vn, bias_tab)


def _na_bias_table(na_bias_l, rows):
    ct = rows * GRID_W // TM
    rpt = TM // GRID_W
    wr = min(NA_WIN_R, rows)
    qj = np.arange(GRID_W)
    kc = np.arange(GRID_W)
    cs = np.clip(qj - NA_WIN_C // 2, 0, GRID_W - NA_WIN_C)
    colvalid = (kc[None, :] >= cs[:, None]) & (kc[None, :] < cs[:, None] + NA_WIN_C)
    dc = kc[None, :] - qj[:, None] + (NA_WIN_C - 1)
    onehot = (dc.reshape(1, -1) == np.arange(2 * NA_WIN_C - 1)[:, None]) & colvalid.reshape(1, -1)
    toep = jnp.einsum("hrd,dx->hrx", na_bias_l, jnp.asarray(onehot.astype(np.float32)), precision=HIGHEST)
    toep = jnp.where(jnp.asarray(colvalid.reshape(-1)), toep * np.log2(np.e), NEG)
    toep = toep.reshape(NA_HEADS, 2 * NA_WIN_R - 1, GRID_W, GRID_W)
    toep = jnp.swapaxes(toep, -1, -2)
    neg_blk = jnp.full((NA_HEADS, GRID_W, GRID_W), NEG, F32)
    tabs = []
    for tile_i in (0, 1, ct - 1):
        i0 = tile_i * rpt
        kb = int(np.clip(tile_i - 1, 0, ct - 3)) * rpt
        rs = [int(np.clip(i0 + ri - wr // 2, 0, rows - wr)) for ri in range(rpt)]
        krows = []
        for m in range(3 * rpt):
            blks = [toep[:, kb + m - (i0 + ri) + NA_WIN_R - 1] if rs[ri] <= kb + m < rs[ri] + wr else neg_blk
                    for ri in range(rpt)]
            krows.append(jnp.concatenate(blks, axis=-1))
        tabs.append(jnp.concatenate(krows, axis=-2))
    tabs.append(jnp.full_like(tabs[0], NEG))
    return jnp.stack(tabs, axis=0)


GQA_CB = 256
GQA_UNROLL = 16
GQA_AHEAD = 3


def _gqa_kernel(q_ref, k_ref, v_ref, o_ref, *scratch, n_full, tail, tq):
    nblk = 2 * tq // GQA_CB
    m_refs, acc_refs, st_refs = scratch[:nblk], scratch[nblk:2 * nblk], scratch[2 * nblk:]
    qt = jnp.concatenate([q_ref[0, :HEAD_DIM, :], q_ref[0, HEAD_DIM:, :]], axis=1)
    for n in range(nblk):
        m_refs[n][...] = jnp.full(m_refs[n].shape, NEG, F32)
        acc_refs[n][...] = jnp.zeros(acc_refs[n].shape, F32)

    def chunks(spans):
        kcs = [k_ref[0, 0, pl.ds(start, size), :] for start, size in spans]
        vts = [v_ref[0, 0, :GQA_VROWS, pl.ds(start, size)] for start, size in spans]
        items = [(c, n) for c in range(len(spans)) for n in range(nblk)]
        score = lambda c, n: jnp.dot(kcs[c], qt[:, n * GQA_CB:(n + 1) * GQA_CB], preferred_element_type=F32)
        def issue(i):
            c, n = items[i]
            st_refs[i % len(st_refs)][:spans[c][1], :] = score(c, n)

        for i in range(min(GQA_AHEAD, len(items))):
            issue(i)
        for i, (c, n) in enumerate(items):
            if i + GQA_AHEAD < len(items):
                issue(i + GQA_AHEAD)
            st = st_refs[i % len(st_refs)][:spans[c][1], :]
            m_old = m_refs[n][...]
            m_new = jnp.maximum(m_old, jnp.max(st, axis=0, keepdims=True))
            pt = jnp.exp2(st - m_new).astype(BF16)
            acc_refs[n][...] = (jnp.exp2(m_old - m_new) * acc_refs[n][...]
                                + jnp.dot(vts[c], pt, preferred_element_type=F32))
            m_refs[n][...] = m_new

    n_trips = n_full // GQA_UNROLL
    if n_trips > 0:
        def body(j, carry):
            base = j * (GQA_UNROLL * GQA_TK)
            chunks([(pl.multiple_of(base + u * GQA_TK, GQA_TK), GQA_TK) for u in range(GQA_UNROLL)])
            return carry
        lax.fori_loop(0, n_trips, body, 0)
    rest = [(c * GQA_TK, GQA_TK) for c in range(n_trips * GQA_UNROLL, n_full)]
    if tail > 0:
        rest.append((n_full * GQA_TK, tail))
    if rest:
        chunks(rest)
    acc = jnp.concatenate([r[...] for r in acc_refs], axis=1)
    ot = acc[:HEAD_DIM] * (1.0 / acc[HEAD_DIM:HEAD_DIM + 1])
    ot = jnp.concatenate([ot, jnp.zeros_like(ot)], axis=0)
    o = ot.T
    o_ref[0] = jnp.concatenate([o[:tq, :HEAD_DIM], o[tq:, :HEAD_DIM]], axis=-1).astype(o_ref.dtype)


def _gqa(qg, kg, vg, seq, ctx_len):
    b, _, tt = qg.shape
    scratch = lambda tq: ([pltpu.VMEM((1, GQA_CB), F32)] * (2 * tq // GQA_CB)
                          + [pltpu.VMEM((GQA_VROWS, GQA_CB), F32)] * (2 * tq // GQA_CB)
                          + [pltpu.VMEM((GQA_TK, GQA_CB), F32)] * (GQA_AHEAD + 1))
    y_lat = pl.pallas_call(
        functools.partial(_gqa_kernel, n_full=tt // GQA_TK, tail=tt % GQA_TK, tq=GQA_TQ),
        grid=(b, GQA_KV_HEADS, seq // GQA_TQ),
        in_specs=[
            pl.BlockSpec((1, 2 * HEAD_DIM, GQA_TQ), lambda bi, n, i: (bi, n, i)),
            pl.BlockSpec((1, 1, tt, HEAD_DIM), lambda bi, n, i: (bi, n, 0, 0)),
            pl.BlockSpec((1, 1, 128, tt), lambda bi, n, i: (bi, n, 0, 0)),
        ],
        out_specs=pl.BlockSpec((1, GQA_TQ, 128), lambda bi, n, i: (bi, i, n)),
        out_shape=jax.ShapeDtypeStruct((b, seq, 256), BF16),
        scratch_shapes=scratch(GQA_TQ),
        compiler_params=_cparams(("parallel", "parallel", "parallel")),
        name="gqa_latent",
    )(qg, kg, vg)
    cblk = seq // ctx_len
    y_ctx = pl.pallas_call(
        functools.partial(_gqa_kernel, n_full=0, tail=ctx_len, tq=ctx_len),
        grid=(b, GQA_KV_HEADS),
        in_specs=[
            pl.BlockSpec((1, 2 * HEAD_DIM, ctx_len), lambda bi, n: (bi, n, cblk)),
            pl.BlockSpec((1, 1, ctx_len, HEAD_DIM), lambda bi, n: (bi, n, cblk, 0)),
            pl.BlockSpec((1, 1, 128, ctx_len), lambda bi, n: (bi, n, 0, cblk)),
        ],
        out_specs=pl.BlockSpec((1, ctx_len, 128), lambda bi, n: (bi, 0, n)),
        out_shape=jax.ShapeDtypeStruct((b, ctx_len, 256), BF16),
        scratch_shapes=scratch(ctx_len),
        compiler_params=_cparams(("parallel", "parallel")),
        name="gqa_context",
    )(qg, kg, vg)
    return jnp.concatenate([y_lat, y_ctx], axis=1)


def _bd(x):
    left = lax.broadcasted_iota(jnp.int32, (CH, 128), 1) < HEAD_DIM
    x0, x1 = x[:, :128], x[:, 128:]
    z = jnp.zeros_like(x0)
    keep_l = lambda a: jnp.where(left, a, z)
    keep_r = lambda a: jnp.where(left, z, a)
    return jnp.concatenate([jnp.concatenate([keep_l(x0), z], axis=1), jnp.concatenate([keep_r(x0), z], axis=1),
                            jnp.concatenate([z, keep_l(x1)], axis=1), jnp.concatenate([z, keep_r(x1)], axis=1)],
                           axis=0)


def _fold(x):
    left = lax.broadcasted_iota(jnp.int32, (CH, 128), 1) < HEAD_DIM
    return jnp.concatenate([jnp.where(left, x[0:64, :128], x[64:128, :128]),
                            jnp.where(left, x[128:192, 128:], x[192:256, 128:])], axis=1)


def _chunk_mats(items):
    t = lax.broadcasted_iota(jnp.int32, (CH, RW), 0)
    j = lax.broadcasted_iota(jnp.int32, (CH, RW), 1) & 63
    eye = j == t
    before = {False: j < t, True: j > t}
    incl = {False: j <= t, True: j >= t}
    bf = lambda x: x.astype(BF16)
    n_items = range(len(items))

    lb, mb, lk, mk = [], [], [], []
    for it in items:
        a2 = bf(jnp.concatenate([it["kap"], it["rho"]], axis=0))
        lm_b = _dg(a2, _bd(bf(it["bt"])), 1, 1)
        lm_k = _dg(a2, _bd(bf(it["kt"])), 1, 1)
        lb.append(jnp.where(before[it["rev"]], lm_b[:CH], 0.0))
        mb.append(jnp.where(incl[it["rev"]], lm_b[CH:], 0.0))
        lk.append(jnp.where(before[it["rev"]], lm_k[:CH], 0.0))
        mk.append(jnp.where(incl[it["rev"]], lm_k[CH:], 0.0))

    p, mpow = [], []
    for i in n_items:
        nb = bf(-lb[i])
        p.append(jnp.where(eye, 1.0, 0.0) - lb[i])
        mpow.append(_dg(nb, _bd(nb), 1, 0))
    for _ in range(4):
        for i in n_items:
            mbf = bf(mpow[i])
            pm = _dg(jnp.concatenate([bf(p[i]), mbf], axis=0), _bd(mbf), 1, 0)
            p[i] = p[i] + pm[:CH]
            mpow[i] = pm[CH:]
    tinv = [bf(p[i] + _dg(bf(p[i]), _bd(bf(mpow[i])), 1, 0)) for i in n_items]

    lmv = [_dg(bf(jnp.concatenate([lk[i], mk[i]], axis=0)), _bd(bf(items[i]["v"])), 1, 0) for i in n_items]
    kp = [_dg(tinv[i], _bd(bf(items[i]["kap"])), 1, 0) for i in n_items]
    vp = [_dg(tinv[i], _bd(bf(lmv[i][:CH])), 1, 0) for i in n_items]
    out = []
    for i in n_items:
        it = items[i]
        mbb = bf(mb[i])
        rp = it["rho"] - _dg(mbb, _bd(bf(kp[i])), 1, 0)
        yl = lmv[i][CH:] - _dg(mbb, _bd(bf(vp[i])), 1, 0)
        g = jnp.where(eye, it["gdiag"], 0.0) - _fold(_dg(bf(kp[i]), bf(it["bhat"]), 0, 0))
        hmat = _fold(_dg(bf(jnp.concatenate([it["v"], -vp[i]], axis=0)),
                         bf(jnp.concatenate([it["khat"], it["bhat"]], axis=0)), 0, 0))
        out.append((g, hmat, rp, yl))
    return out


def _rwkv_chunk_kernel(z_ref, hp_ref, hn_ref, mu_ref, w0_ref, w2_ref, a0_ref, a2_ref, kkw_ref, ka_ref, rk_ref,
                       bd_ref, g_ref, h_ref, rp_ref, yl_ref, bonus_ref, *, ct):
    i = pl.program_id(1)
    z = z_ref[0]
    zs = z[:, :RW_SHIFT]
    valid_prev = jnp.where((i != 0) & (i != ct), 1.0, 0.0)
    valid_next = jnp.where((i != ct - 1) & (i != ct), 1.0, 0.0)
    prev_row = hp_ref[0, HALO - 1:HALO, :RW_SHIFT] * valid_prev
    next_row = hn_ref[0, 0:1, :RW_SHIFT] * valid_next
    row = lax.broadcasted_iota(jnp.int32, (TM, RW_SHIFT), 0)
    tt = lax.broadcasted_iota(jnp.int32, (TM, TM), 0)
    jj = lax.broadcasted_iota(jnp.int32, (TM, TM), 1)
    same_chunk = (tt >> 6) == (jj >> 6)
    bd = bd_ref[...]
    def direction(d):
        if d == 0:
            nb = jnp.where(row == 0, prev_row, pltpu.roll(zs, 1, 0))
        else:
            nb = jnp.where(row == TM - 1, next_row, pltpu.roll(zs, TM - 1, 0))
        zd = zs + mu_ref[d:d + 1, :] * (nb - zs)
        r = zd[:, :RW]
        k = zd[:, RW:2 * RW]
        v = zd[:, 2 * RW:3 * RW]
        lw = zd[:, 3 * RW:3 * RW + LORA_W]
        la = zd[:, 3 * RW + LORA_W:]
        w_log = w0_ref[d:d + 1, :] + _dot3(jnp.tanh(lw), w2_ref[d, 0], w2_ref[d, 1])
        a_pre = a0_ref[d:d + 1, :] + _dot3(la, a2_ref[d, 0], a2_ref[d, 1])
        kkr = k * kkw_ref[...]
        ss = _sum_dot(kkr * kkr, bd, 2)
        yield
        sp = jnp.maximum(-w_log, 0.0) + jnp.log(1.0 + jnp.exp(-jnp.abs(w_log)))
        logw = -jnp.exp(-sp - 0.5)
        tri = jnp.where(same_chunk & ((jj >= tt) if d == 1 else (jj <= tt)), 1.0, 0.0).astype(BF16)
        cum, rest = None, logw
        for _ in range(3):
            piece = rest.astype(BF16)
            part = jnp.dot(tri, piece, preferred_element_type=F32)
            cum = part if cum is None else cum + part
            rest = rest - piece.astype(F32)
        a = _sigmoid(a_pre)
        kk = kkr * lax.rsqrt(jnp.maximum(ss, 1e-24))
        k2 = k * (1.0 + (a - 1.0) * ka_ref[...])
        bv = kk * a
        bon = _sum_dot(r * k2 * rk_ref[...], bd, 2) * v
        yield
        e_neg = jnp.exp(-cum)
        streams = dict(kap=kk * jnp.exp(cum - logw), kt=k2 * e_neg, bt=bv * e_neg, rho=r * jnp.exp(cum), v=v)
        chunk_items = []
        for c in range(TM // CH):
            sl = slice(c * CH, (c + 1) * CH)
            it = {name: val[sl] for name, val in streams.items()}
            last = c * CH if d == 1 else (c + 1) * CH - 1
            tot = cum[last:last + 1]
            e_tot = jnp.exp(tot - cum[sl])
            it.update(khat=k2[sl] * e_tot, bhat=bv[sl] * e_tot, gdiag=jnp.exp(tot), rev=d == 1)
            chunk_items.append(it)
        return chunk_items, bon

    gens = [direction(0), direction(1)]
    done = [None, None]
    while any(res is None for res in done):
        for d, gen in enumerate(gens):
            if done[d] is None:
                try:
                    next(gen)
                except StopIteration as stop:
                    done[d] = stop.value
    items = done[0][0] + done[1][0]
    bonus = done[0][1] + done[1][1]
    for idx, (g, hm, rp, yl) in enumerate(_chunk_mats(items)):
        d, c = divmod(idx, TM // CH)
        g_ref[0, d, c] = g.astype(BF16)
        h_ref[0, d, c] = hm
        rp_ref[0, d, c] = rp.astype(BF16)
        yl_ref[0, d, c] = yl
    bonus_ref[0] = bonus


def _rwkv_chunks(zr, mu, w0, w2, a0, a2, kkw, ka, rk, bd256, ct):
    b, tt, _ = zr.shape
    nt = tt // TM
    nch = tt // CH
    cpt = TM // CH
    nhb = tt // HALO
    mats = pl.BlockSpec((1, 2, cpt, CH, RW), lambda bi, i: (bi, 0, i, 0, 0))
    mshape = lambda dt: jax.ShapeDtypeStruct((b, 2, nch, CH, RW), dt)
    return pl.pallas_call(
        functools.partial(_rwkv_chunk_kernel, ct=ct),
        grid=(b, nt),
        in_specs=[
            pl.BlockSpec((1, TM, COLS_RW), lambda bi, i: (bi, i, 0)),
            pl.BlockSpec((1, HALO, COLS_RW), lambda bi, i: (bi, jnp.maximum(i * (TM // HALO) - 1, 0), 0)),
            pl.BlockSpec((1, HALO, COLS_RW), lambda bi, i: (bi, jnp.minimum((i + 1) * (TM // HALO), nhb - 1), 0)),
            _const_spec((2, RW_SHIFT)),
            _const_spec((2, RW)),
            _const_spec((2, 2, LORA_W, RW)),
            _const_spec((2, RW)),
            _const_spec((2, 2, LORA_A, RW)),
            _const_spec((1, RW)),
            _const_spec((1, RW)),
            _const_spec((1, RW)),
            _const_spec((RW, RW)),
        ],
        out_specs=[mats, mats, mats, mats, pl.BlockSpec((1, TM, RW), lambda bi, i: (bi, i, 0))],
        out_shape=[mshape(BF16), mshape(F32), mshape(BF16), mshape(F32), jax.ShapeDtypeStruct((b, tt, RW), F32)],
        compiler_params=_cparams(("parallel", "parallel")),
        name="rwkv_chunks",
    )(zr, zr, zr, mu, w0, w2, a0, a2, kkw, ka, rk, bd256)


CHAIN_GROUP = 4


def _rwkv_chain_kernel(gf_ref, hf_ref, rf_ref, yf_ref, gr_ref, hr_ref, rr_ref, yr_ref, of_ref, or_ref, s_ref,
                       *, nb):
    @pl.when(pl.program_id(0) == 0)
    def _():
        s_ref[...] = jnp.zeros(s_ref.shape, F32)

    dirs = ((gf_ref, hf_ref, rf_ref, yf_ref, of_ref), (gr_ref, hr_ref, rr_ref, yr_ref, or_ref))
    state = [[s_ref[d, bi] for bi in range(nb)] for d in range(2)]
    for step in range(CHAIN_GROUP):
        for d, (g_ref, h_ref, rp_ref, yl_ref, o_ref) in enumerate(dirs):
            c = step if d == 0 else CHAIN_GROUP - 1 - step
            for bi in range(nb):
                sb = state[d][bi].astype(BF16)
                o_ref[bi, c * CH:(c + 1) * CH, :] = _dg(rp_ref[bi, 0, c], _bd(sb), 1, 1) + yl_ref[bi, 0, c]
                state[d][bi] = _dg(sb, _bd(g_ref[bi, 0, c]), 1, 0) + h_ref[bi, 0, c]
    for d in range(2):
        for bi in range(nb):
            s_ref[d, bi] = state[d][bi]


def _rwkv_chain(g, h, rp, yl, seq, ctx_len):
    b, _, nch, _, _ = g.shape
    assert ctx_len == CHAIN_GROUP * CH and seq % (CHAIN_GROUP * CH) == 0
    ngrp = nch // CHAIN_GROUP
    n_lat = seq // (CHAIN_GROUP * CH)

    def gf(s):
        return jnp.where(s == 0, n_lat, s - 1)

    def gr(s):
        return ngrp - 1 - s

    fwd = pl.BlockSpec((b, 1, CHAIN_GROUP, CH, RW), lambda s: (0, 0, gf(s), 0, 0))
    rev = pl.BlockSpec((b, 1, CHAIN_GROUP, CH, RW), lambda s: (0, 1, gr(s), 0, 0))
    yshape = jax.ShapeDtypeStruct((b, nch * CH, RW), F32)
    return pl.pallas_call(
        functools.partial(_rwkv_chain_kernel, nb=b),
        grid=(ngrp,),
        in_specs=[fwd, fwd, fwd, fwd, rev, rev, rev, rev],
        out_specs=[pl.BlockSpec((b, CHAIN_GROUP * CH, RW), lambda s: (0, gf(s), 0)),
                   pl.BlockSpec((b, CHAIN_GROUP * CH, RW), lambda s: (0, gr(s), 0))],
        out_shape=[yshape, yshape],
        scratch_shapes=[pltpu.VMEM((2, b, CH, RW), F32)],
        compiler_params=_cparams(("arbitrary",)),
        name="rwkv_chain",
    )(g, h, rp, yl, g, h, rp, yl)


def _merge_kernel(*refs, nblk, ctx_mode):
    (x_ref, mod_ref, gpre_ref, gpost_ref, za_ref, hp_ref, hn_ref, ca_ref, yb_ref, yc_ref, yf_ref, yr_ref,
     bonus_ref, lg_ref, g2_ref, lnw_ref, lnb_ref, bd_ref, wb_ref, wg_ref, bg_ref, wo_ref) = refs[:22]
    o_ref = refs[-1]
    tm = x_ref.shape[1]
    x = x_ref[0]
    m = mod_ref[0, 0]
    hb = _rms_mod(x, gpre_ref[...], m[0:1], m[1:2]).astype(BF16)

    def gate_mm(bidx):
        return jnp.dot(hb, wg_ref[:, bidx * D_MODEL:(bidx + 1) * D_MODEL], preferred_element_type=F32)

    def gated(bidx, pre, ys):
        gate2 = jnp.tanh(pre + bg_ref[:, bidx * D_MODEL:(bidx + 1) * D_MODEL]) + 1.0
        return gate2 * jnp.dot(ys, wb_ref[bidx], preferred_element_type=F32)

    za = za_ref[0]
    u = za[:, 256:512] * za[:, 512:768]
    valid_prev, valid_next = _edge_valid(pl.program_id(1), nblk, ctx_mode)
    up = hp_ref[0, HALO - 1:HALO, 256:512] * hp_ref[0, HALO - 1:HALO, 512:768] * valid_prev
    un = hn_ref[0, 0:1, 256:512] * hn_ref[0, 0:1, 512:768] * valid_next
    row = lax.broadcasted_iota(jnp.int32, (tm, 256), 0)
    u_prev = jnp.where(row == 0, up, pltpu.roll(u, 1, 0))
    u_next = jnp.where(row == tm - 1, un, pltpu.roll(u, tm - 1, 0))
    ya = za[:, :256] * (u_prev * ca_ref[0:1, :] + u * ca_ref[1:2, :] + u_next * ca_ref[2:3, :])

    bd = bd_ref[...]
    ones_dot = lambda piece: jnp.dot(piece, bd, preferred_element_type=F32)
    y = yf_ref[0] + yr_ref[0]
    p1 = y.astype(BF16)
    s1 = ones_dot(p1)
    pre1 = gate_mm(1)
    r1 = y - p1.astype(F32)
    p2 = r1.astype(BF16)
    s2 = ones_dot(p2)
    acc = gated(1, pre1, yb_ref[0])
    s3 = ones_dot((r1 - p2.astype(F32)).astype(BF16))
    pre2 = gate_mm(2)
    yc0 = y - (s1 + s2 + s3) * (1.0 / HEAD_DIM)
    sq = yc0 * yc0
    q1 = sq.astype(BF16)
    v1 = ones_dot(q1)
    acc = acc + gated(2, pre2, yc_ref[0])
    v2 = ones_dot((sq - q1.astype(F32)).astype(BF16))
    pre0 = gate_mm(0)
    lgate = _dot3(_sigmoid(lg_ref[0]), g2_ref[0], g2_ref[1])
    acc = acc + gated(0, pre0, ya.astype(BF16))
    pre3 = gate_mm(3)
    var = (v1 + v2) * (1.0 / HEAD_DIM)
    yn = yc0 * lax.rsqrt(var + LN_X_EPS) * lnw_ref[...] + lnb_ref[...] + bonus_ref[0]
    acc = acc + gated(3, pre3, (yn * lgate).astype(BF16))
    mo = jnp.dot(acc.astype(BF16), wo_ref[...], preferred_element_type=F32)
    o_ref[0] = x + m[2:3] * _rms(mo, gpost_ref[...])


def _merge(xa, modsel, g_pre, g_post, za, conv_a, yb, yc, yf, yr, bonus, zr, g2, ln_w, ln_b, bd256,
           w_branch, w_gate, b_gate, w_o, seq, tm, prev=None):
    b, tt, _ = xa.shape
    ctx_mode = prev is not None
    nblk, blk, steps, alias_specs = _token_tiling(seq, tm, ctx_mode, 1)
    tile = lambda w: pl.BlockSpec((1, tm, w), lambda bi, i: (bi, blk(i), 0))
    return pl.pallas_call(
        functools.partial(_merge_kernel, nblk=nblk, ctx_mode=ctx_mode),
        grid=(b, steps),
        in_specs=[
            tile(D_MODEL),
            pl.BlockSpec((1, 1, 6, D_MODEL), lambda bi, i: (bi, int(ctx_mode), 0, 0)),
            _const_spec((1, D_MODEL)),
            _const_spec((1, D_MODEL)),
            tile(COLS_A),
            *_halo_specs(COLS_A, tm, blk, tt // HALO),
            _const_spec((3, 256)),
            tile(256), tile(256), tile(256), tile(256), tile(256),
            pl.BlockSpec((1, tm, LORA_G), lambda bi, i: (bi, blk(i), RW_SHIFT // LORA_G)),
            _const_spec((2, LORA_G, RW)),
            _const_spec((1, RW)),
            _const_spec((1, RW)),
            _const_spec((RW, RW)),
            _const_spec((4, 256, D_MODEL)),
            _const_spec((D_MODEL, 4 * D_MODEL)),
            _const_spec((1, 4 * D_MODEL)),
            _const_spec((D_MODEL, D_MODEL)),
        ] + alias_specs,
        out_specs=tile(D_MODEL),
        out_shape=jax.ShapeDtypeStruct((b, tt, D_MODEL), F32),
        input_output_aliases={22: 0} if ctx_mode else {},
        compiler_params=_cparams(("parallel", "parallel")),
        name="merge_ctx" if ctx_mode else "merge",
    )(xa, modsel, g_pre, g_post, za, za, za, conv_a, yb, yc, yf, yr, bonus, zr, g2, ln_w, ln_b, bd256,
      w_branch, w_gate, b_gate, w_o, *((prev,) if ctx_mode else ()))


def _ffn_kernel(*refs, nblk, ctx_mode):
    x_ref, hp_ref, hn_ref, mod_ref, gpre_ref, gpost_ref, wu_ref, cw_ref, wd_ref = refs[:9]
    o_ref, act_ref = refs[-2:]
    tm = x_ref.shape[1]
    x = x_ref[0]
    m = mod_ref[0, 0]
    xx = jnp.concatenate([hp_ref[0], x, hn_ref[0]], axis=0)
    nrow = tm + 2 * HALO
    row = lax.broadcasted_iota(jnp.int32, (nrow, 1), 0)
    valid_prev, valid_next = _edge_valid(pl.program_id(1), nblk, ctx_mode)
    rowmask = jnp.where(row < HALO, valid_prev, jnp.where(row >= tm + HALO, valid_next, 1.0))
    hb = (_rms_mod(xx, gpre_ref[...], m[3:4], m[4:5]) * rowmask).astype(BF16)

    def conv(u, col):
        w = cw_ref[:, col:col + FF_CHUNK]
        c = (pltpu.roll(u, 1, 0) * w[0:1] + u * w[1:2] + pltpu.roll(u, nrow - 1, 0) * w[2:3])
        return c[HALO:HALO + tm]

    for j in range(D_FF // FF_CHUNK):
        ca = conv(jnp.dot(hb, wu_ref[:, j * FF_CHUNK:(j + 1) * FF_CHUNK], preferred_element_type=F32),
                  j * FF_CHUNK)
        cg = conv(jnp.dot(hb, wu_ref[:, D_FF + j * FF_CHUNK:D_FF + (j + 1) * FF_CHUNK],
                          preferred_element_type=F32), D_FF + j * FF_CHUNK)
        act_ref[:, j * FF_CHUNK:(j + 1) * FF_CHUNK] = (ca * (cg * (1.0 + jnp.tanh(cg)))).astype(BF16)
    f = jnp.dot(act_ref[...], wd_ref[...], preferred_element_type=F32)
    o_ref[0] = x + m[5:6] * _rms(f, gpost_ref[...])


def _ffn(xa, modsel, g_pre, g_post, w_up, conv_w, w_down, seq, tm, prev=None, out_rows=None):
    b, tt, _ = xa.shape
    ctx_mode = prev is not None
    nblk, blk, steps, alias_specs = _token_tiling(seq, tm, ctx_mode, 1)
    return pl.pallas_call(
        functools.partial(_ffn_kernel, nblk=nblk, ctx_mode=ctx_mode),
        grid=(b, steps),
        in_specs=[
            pl.BlockSpec((1, tm, D_MODEL), lambda bi, i: (bi, blk(i), 0)),
            *_halo_specs(D_MODEL, tm, blk, tt // HALO),
            pl.BlockSpec((1, 1, 6, D_MODEL), lambda bi, i: (bi, int(ctx_mode), 0, 0)),
            _const_spec((1, D_MODEL)),
            _const_spec((1, D_MODEL)),
            _const_spec((D_MODEL, 2 * D_FF)),
            _const_spec((3, 2 * D_FF)),
            _const_spec((D_FF, D_MODEL)),
        ] + alias_specs,
        out_specs=pl.BlockSpec((1, tm, D_MODEL), lambda bi, i: (bi, blk(i), 0)),
        out_shape=jax.ShapeDtypeStruct((b, out_rows or tt, D_MODEL), F32),
        scratch_shapes=[pltpu.VMEM((tm, D_FF), BF16)],
        input_output_aliases={9: 0} if ctx_mode else {},
        compiler_params=_cparams(("parallel", "parallel")),
        name="conv_ffn_ctx" if ctx_mode else "conv_ffn",
    )(xa, xa, xa, modsel, g_pre, g_post, w_up, conv_w, w_down, *((prev,) if ctx_mode else ()))


def _rope_tables(seq, ctx_len):
    t = np.arange(seq)
    row = (t // GRID_W).astype(np.float32)
    col = (t % GRID_W).astype(np.float32)
    n_freq = HEAD_DIM // 4
    inv_freq = jnp.asarray(ROPE_THETA, F32) ** (-jnp.arange(n_freq, dtype=F32) / n_freq)
    ang = jnp.concatenate([jnp.asarray(row)[:, None] * inv_freq, jnp.asarray(col)[:, None] * inv_freq], axis=-1)
    cos = jnp.repeat(jnp.cos(ang), 2, axis=-1)
    sin = jnp.repeat(jnp.sin(ang), 2, axis=-1) * jnp.tile(jnp.asarray([-1.0, 1.0], F32), HEAD_DIM // 2)
    cos = jnp.concatenate([cos, jnp.ones((ctx_len, HEAD_DIM), F32)], axis=0)
    sin = jnp.concatenate([sin, jnp.zeros((ctx_len, HEAD_DIM), F32)], axis=0)
    nh = GQA_Q_HEADS + GQA_KV_HEADS
    return jnp.tile(cos, (1, nh)), jnp.tile(sin, (1, nh))


def _block_ones(n, scale):
    idx = np.arange(n) // HEAD_DIM
    return jnp.asarray((idx[:, None] == idx[None, :]).astype(np.float32) * scale)


def kernel(x, c, ctx, c_ctx, ada_w, ada_b, norm_mix_pre, norm_mix_post, norm_ffn_pre, norm_ffn_post, w_in, conv_a, na_bias, q_norm, k_norm, rw_mu, rw_w0, rw_w2, rw_a0, rw_a2, rw_kk, rw_ka, rw_rk, rw_g2, rw_ln_w, rw_ln_b, w_branch, w_gate, b_gate, w_o, ffn_up, ffn_conv, ffn_down):
    b, seq, _ = x.shape
    ctx_len = ctx.shape[1]
    depth = ada_w.shape[0]
    assert ctx_len == TM and seq % GQA_TQ == 0 and seq % TML == 0 and seq // TM >= 3 and b + 1 <= 8
    ct = seq // TM
    rows = seq // GRID_W

    cvec = jnp.zeros((8, D_MODEL), F32).at[:b].set(c).at[b].set(c_ctx)
    mods = _ada(cvec, ada_w, ada_b)
    cos_t, sin_t = _rope_tables(seq, ctx_len)
    bd384 = _block_ones(384, 1.0 / HEAD_DIM).astype(BF16)
    bd256 = _block_ones(256, 1.0).astype(BF16)

    def hi_lo(w):
        hi = w.astype(BF16)
        return jnp.stack([hi, (w - hi.astype(F32)).astype(BF16)], axis=-3)

    xa = jnp.concatenate([x, ctx], axis=1)
    for l in range(depth):
        ml = mods[l].reshape(8, 6, D_MODEL)
        modsel = jnp.stack([ml[:b], jnp.broadcast_to(ml[b][None], (b, 6, D_MODEL))], axis=1)
        row2 = lambda a: a.reshape(1, -1)
        gqk = jnp.concatenate([jnp.tile(q_norm[l], GQA_Q_HEADS) * (HEAD_DIM ** -0.5 * np.log2(np.e)),
                               jnp.tile(k_norm[l], GQA_KV_HEADS)]).reshape(1, -1)
        ip_args = (xa, modsel, row2(norm_mix_pre[l]), w_in[l].astype(BF16), cos_t, sin_t, gqk, bd384, seq)
        za, qn, kn, vn, qg, kg, vg, zr = _inproj(*ip_args, ctx_len, prev=_inproj(*ip_args, TML))
        yb = _na(qn, kn, vn, _na_bias_table(na_bias[l], rows), ct)
        yc = _gqa(qg, kg, vg, seq, ctx_len)
        g, h, rp, yl, bonus = _rwkv_chunks(zr, rw_mu[l], rw_w0[l], hi_lo(rw_w2[l]), rw_a0[l], hi_lo(rw_a2[l]),
                                           row2(rw_kk[l]), row2(rw_ka[l]), row2(rw_rk[l]), bd256, ct)
        yf, yr = _rwkv_chain(g, h, rp, yl, seq, ctx_len)
        mg_args = (xa, modsel, row2(norm_mix_pre[l]), row2(norm_mix_post[l]), za, conv_a[l], yb, yc, yf, yr,
                   bonus, zr, hi_lo(rw_g2[l]), row2(rw_ln_w[l]), row2(rw_ln_b[l]), bd256,
                   w_branch[l].astype(BF16), (0.5 * w_gate[l]).astype(BF16), row2(0.5 * b_gate[l]),
                   (0.5 * w_o[l]).astype(BF16), seq)
        xm = _merge(*mg_args, ctx_len, prev=_merge(*mg_args, TML))
        conv_w = jnp.concatenate([ffn_conv[l][:, :D_FF], 0.5 * ffn_conv[l][:, D_FF:]], axis=1)
        ff_args = (xm, modsel, row2(norm_ffn_pre[l]), row2(norm_ffn_post[l]), ffn_up[l].astype(BF16),
                   conv_w, ffn_down[l].astype(BF16), seq)
        if l < depth - 1:
            xa = _ffn(*ff_args, ctx_len, prev=_ffn(*ff_args, TML))
        else:
            xa = _ffn(*ff_args, TML, out_rows=seq)
    return xa
```

```python
import functools

import numpy as np
import jax
import jax.numpy as jnp
from jax import lax
from jax.experimental import pallas as pl
from jax.experimental.pallas import tpu as pltpu

F32 = jnp.float32
BF16 = jnp.bfloat16
HIGHEST = lax.Precision.HIGHEST

D_MODEL = 1024
GRID_W = 64
HEAD_DIM = 64
NA_HEADS = 4
NA_WIN_R = 8
NA_WIN_C = 16
GQA_Q_HEADS = 4
GQA_KV_HEADS = 2
ROPE_THETA = 10000.0
RWKV_HEADS = 4
RW = RWKV_HEADS * HEAD_DIM
LORA_W = 64
LORA_A = 64
LORA_G = 128
RW_SHIFT = 3 * RW + LORA_W + LORA_A
D_FF = 2816
NORM_EPS = 1e-6
LN_X_EPS = 64e-5
COLS_A = 768
COLS_NA = 768
COLS_GQA = 512
COLS_RW = 1024
D_IN = COLS_A + COLS_NA + COLS_GQA + COLS_RW

TM = 256
TML = 512
CH = 64
HALO = 8
FF_CHUNK = 256
GQA_VROWS = 80
GQA_TQ = 1024
GQA_TK = 512
NEG = -1e30
VMEM_LIMIT = 56 * 1024 * 1024


def _cparams(sem):
    return pltpu.CompilerParams(dimension_semantics=sem, vmem_limit_bytes=VMEM_LIMIT)


def _const_spec(shape):
    nd = len(shape)
    return pl.BlockSpec(shape, lambda *_: (0,) * nd, pipeline_mode=pl.Buffered(1))


def _token_tiling(seq, tm, ctx_mode, n_out):
    nblk = seq // tm
    blk = (lambda i: i * 0 + nblk) if ctx_mode else (lambda i: i)
    alias_specs = [pl.BlockSpec(memory_space=pl.ANY)] * n_out if ctx_mode else []
    return nblk, blk, (1 if ctx_mode else nblk), alias_specs


def _halo_specs(width, tm, blk, nhb):
    per = tm // HALO
    return [pl.BlockSpec((1, HALO, width), lambda bi, i: (bi, jnp.maximum(blk(i) * per - 1, 0), 0)),
            pl.BlockSpec((1, HALO, width), lambda bi, i: (bi, jnp.minimum((blk(i) + 1) * per, nhb - 1), 0))]


def _edge_valid(i, nblk, ctx_mode):
    if ctx_mode:
        return 0.0, 0.0
    return jnp.where(i != 0, 1.0, 0.0), jnp.where(i != nblk - 1, 1.0, 0.0)


def _dg(a, b, ca, cb, **kw):
    return lax.dot_general(a, b, (((ca,), (cb,)), ((), ())), preferred_element_type=F32, **kw)


def _sum_dot(x, w, terms):
    acc = None
    for _ in range(terms):
        piece = x.astype(BF16)
        part = jnp.dot(piece, w, preferred_element_type=F32)
        acc = part if acc is None else acc + part
        x = x - piece.astype(F32)
    return acc


def _dot3(x, w_hi, w_lo):
    x_hi = x.astype(BF16)
    x_lo = (x - x_hi.astype(F32)).astype(BF16)
    return (jnp.dot(x_hi, w_hi, preferred_element_type=F32) + jnp.dot(x_lo, w_hi, preferred_element_type=F32)
            + jnp.dot(x_hi, w_lo, preferred_element_type=F32))


def _rms_mod(x, g, shift, scale):
    y = x * lax.rsqrt(jnp.mean(x * x, axis=-1, keepdims=True) + NORM_EPS) * g
    return y * (1.0 + scale) + shift


def _rms(x, g):
    return x * lax.rsqrt(jnp.mean(x * x, axis=-1, keepdims=True) + NORM_EPS) * g


def _sigmoid(x):
    return 0.5 * jnp.tanh(0.5 * x) + 0.5


def _ada_kernel(c_ref, w_ref, b_ref, o_ref):
    c = c_ref[...]
    s = c * _sigmoid(c)
    o_ref[0] = jnp.dot(s, w_ref[0], precision=HIGHEST, preferred_element_type=F32) + b_ref[0]


def _ada(cvec, ada_w, ada_b):
    depth = ada_w.shape[0]
    nblk = ada_w.shape[2] // D_MODEL
    return pl.pallas_call(
        _ada_kernel,
        grid=(depth, nblk),
        in_specs=[
            pl.BlockSpec((8, D_MODEL), lambda l, j: (0, 0)),
            pl.BlockSpec((1, D_MODEL, D_MODEL), lambda l, j: (l, 0, j)),
            pl.BlockSpec((1, 1, D_MODEL), lambda l, j: (l, 0, j)),
        ],
        out_specs=pl.BlockSpec((1, 8, D_MODEL), lambda l, j: (l, 0, j)),
        out_shape=jax.ShapeDtypeStruct((depth, 8, ada_w.shape[2]), F32),
        compiler_params=_cparams(("parallel", "parallel")),
        name="ada_mod",
    )(cvec, ada_w, ada_b.reshape(depth, 1, -1))


def _inproj_kernel(*refs):
    x_ref, mod_ref, g_ref, w_ref, cos_ref, sin_ref, gqk_ref, bd_ref = refs[:8]
    za_ref, qn_ref, kn_ref, vn_ref, qg_ref, kg_ref, vg_ref, zr_ref = refs[-8:]
    tm = x_ref.shape[1]
    x = x_ref[0]
    m = mod_ref[0, 0]
    hb = _rms_mod(x, g_ref[...], m[0:1], m[1:2]).astype(BF16)
    e1, e2, e3 = COLS_A, COLS_A + COLS_NA, COLS_A + COLS_NA + COLS_GQA
    g = jnp.dot(hb, w_ref[:, e2:e3], preferred_element_type=F32)
    na = jnp.dot(hb, w_ref[:, e1:e2], preferred_element_type=F32)
    lane = lax.broadcasted_iota(jnp.int32, (tm, 128), 1)
    scale = HEAD_DIM ** -0.5 * np.log2(np.e)
    qn_ref[0] = jnp.concatenate([(na[:, :128] * scale).T, (na[:, 128:256] * scale).T], axis=0).astype(BF16)
    for hd in range(NA_HEADS):
        kn_ref[0, hd] = na[:, 256 + hd * 64:256 + (hd + 1) * 64].astype(BF16)
        vpair = na[:, 512 + (hd // 2) * 128:512 + (hd // 2 + 1) * 128]
        vh = vpair if hd % 2 == 0 else pltpu.roll(vpair, 64, 1)
        vn_ref[0, hd] = jnp.where(lane < 64, vh, jnp.where(lane == 64, 1.0, 0.0)).T.astype(BF16)
    qk = g[:, :384]
    ms = _sum_dot(qk * qk, bd_ref[...], 2)
    za_ref[0] = jnp.dot(hb, w_ref[:, :e1], preferred_element_type=F32)
    zr_ref[0] = jnp.dot(hb, w_ref[:, e3:], preferred_element_type=F32)
    qkn = qk * lax.rsqrt(ms + NORM_EPS) * gqk_ref[...]
    even = (lane & 1) == 0
    parts = []
    for j in range(3):
        s = qkn[:, j * 128:(j + 1) * 128]
        sw = jnp.where(even, pltpu.roll(s, 127, 1), pltpu.roll(s, 1, 1))
        parts.append(s * cos_ref[:, j * 128:(j + 1) * 128] + sw * sin_ref[:, j * 128:(j + 1) * 128])
    qg_ref[0] = jnp.concatenate([parts[0].T, parts[1].T], axis=0).astype(BF16)
    for hd in range(GQA_KV_HEADS):
        kg_ref[0, hd] = parts[2][:, hd * 64:(hd + 1) * 64].astype(BF16)
    vt = g[:, 384:512]
    for hd in range(GQA_KV_HEADS):
        vh = vt if hd == 0 else pltpu.roll(vt, 64, 1)
        vext = jnp.where(lane < 64, vh, jnp.where(lane == 64, 1.0, 0.0))
        vg_ref[0, hd] = vext.T.astype(BF16)


def _inproj(xa, modsel, g_pre, w_in, cos_t, sin_t, gqk, bd384, seq, tm, prev=None):
    b, tt, _ = xa.shape
    ctx_mode = prev is not None
    nblk, blk, steps, alias_specs = _token_tiling(seq, tm, ctx_mode, 8)
    tile = lambda w: pl.BlockSpec((1, tm, w), lambda bi, i: (bi, blk(i), 0))
    heads = lambda nh: pl.BlockSpec((1, nh, tm, 64), lambda bi, i: (bi, 0, blk(i), 0))
    hs = lambda nh: jax.ShapeDtypeStruct((b, nh, tt, 64), BF16)
    return pl.pallas_call(
        _inproj_kernel,
        grid=(b, steps),
        in_specs=[
            tile(D_MODEL),
            pl.BlockSpec((1, 1, 6, D_MODEL), lambda bi, i: (bi, int(ctx_mode), 0, 0)),
            _const_spec((1, D_MODEL)),
            _const_spec((D_MODEL, D_IN)),
            pl.BlockSpec((tm, 384), lambda bi, i: (blk(i), 0)),
            pl.BlockSpec((tm, 384), lambda bi, i: (blk(i), 0)),
            _const_spec((1, 384)),
            _const_spec((384, 384)),
        ] + alias_specs,
        out_specs=[tile(COLS_A),
                   pl.BlockSpec((1, NA_HEADS * HEAD_DIM, tm), lambda bi, i: (bi, 0, blk(i))),
                   heads(4),
                   pl.BlockSpec((1, NA_HEADS, 128, tm), lambda bi, i: (bi, 0, 0, blk(i))),
                   pl.BlockSpec((1, GQA_Q_HEADS * HEAD_DIM, tm), lambda bi, i: (bi, 0, blk(i))),
                   heads(2),
                   pl.BlockSpec((1, 2, 128, tm), lambda bi, i: (bi, 0, 0, blk(i))), tile(COLS_RW)],
        out_shape=[jax.ShapeDtypeStruct((b, tt, COLS_A), F32),
                   jax.ShapeDtypeStruct((b, NA_HEADS * HEAD_DIM, tt), BF16),
                   hs(4),
                   jax.ShapeDtypeStruct((b, NA_HEADS, 128, tt), BF16),
                   jax.ShapeDtypeStruct((b, GQA_Q_HEADS * HEAD_DIM, tt), BF16),
                   hs(2),
                   jax.ShapeDtypeStruct((b, 2, 128, tt), BF16),
                   jax.ShapeDtypeStruct((b, tt, COLS_RW), F32)],
        input_output_aliases={8 + k: k for k in range(8)} if ctx_mode else {},
        compiler_params=_cparams(("parallel", "parallel")),
        name="inproj_ctx" if ctx_mode else "inproj",
    )(xa, modsel, g_pre, w_in, cos_t, sin_t, gqk, bd384, *(prev or ()))


def _na_kernel(q_ref, k0_ref, k1_ref, k2_ref, kc_ref, v0_ref, v1_ref, v2_ref, vc_ref, bias_ref, o_ref, st_ref):
    slot = pl.program_id(1) % 2
    for hd in range(NA_HEADS):
        kcat = jnp.concatenate([k0_ref[0, hd], k1_ref[0, hd], k2_ref[0, hd], kc_ref[0, hd]], axis=0)
        st_ref[slot, hd] = jnp.dot(kcat, q_ref[0, hd * HEAD_DIM:(hd + 1) * HEAD_DIM, :],
                                   preferred_element_type=F32)
    outs = []
    for hd in range(NA_HEADS):
        st_loc = st_ref[slot, hd, :3 * TM, :] + bias_ref[0, hd]
        st_ctx = st_ref[slot, hd, 3 * TM:, :]
        m = jnp.maximum(jnp.max(st_loc, axis=0, keepdims=True), jnp.max(st_ctx, axis=0, keepdims=True))
        pt = jnp.concatenate([jnp.exp2(st_loc - m), jnp.exp2(st_ctx - m)], axis=0).astype(BF16)
        vt = jnp.concatenate([r[0, hd, :GQA_VROWS, :] for r in (v0_ref, v1_ref, v2_ref, vc_ref)], axis=1)
        acc = jnp.dot(vt, pt, preferred_element_type=F32)
        outs.append(acc[:HEAD_DIM] * (1.0 / acc[HEAD_DIM:HEAD_DIM + 1]))
    o_ref[0] = jnp.concatenate(outs, axis=0).T.astype(o_ref.dtype)


def _na(qn, kn, vn, bias_tab, ct):
    b, _, tt = qn.shape
    nt = tt // TM

    def key_tile(i, j):
        return ct if j is None else jnp.clip(i - 1, 0, ct - 3) + j

    k_spec = lambda j: pl.BlockSpec((1, 4, TM, 64), lambda bi, i: (bi, 0, key_tile(i, j), 0))
    v_spec = lambda j: pl.BlockSpec((1, 4, 128, TM), lambda bi, i: (bi, 0, 0, key_tile(i, j)))

    def pattern(i):
        return jnp.where(i == ct, 3, jnp.where(i == 0, 0, jnp.where(i == ct - 1, 2, 1)))

    return pl.pallas_call(
        _na_kernel,
        grid=(b, nt),
        in_specs=[
            pl.BlockSpec((1, NA_HEADS * HEAD_DIM, TM), lambda bi, i: (bi, 0, i)),
            k_spec(0), k_spec(1), k_spec(2), k_spec(None),
            v_spec(0), v_spec(1), v_spec(2), v_spec(None),
            pl.BlockSpec((1, 4, 3 * TM, TM), lambda bi, i: (pattern(i), 0, 0, 0)),
        ],
        out_specs=pl.BlockSpec((1, TM, 256), lambda bi, i: (bi, i, 0)),
        out_shape=jax.ShapeDtypeStruct((b, tt, 256), BF16),
        scratch_shapes=[pltpu.VMEM((2, NA_HEADS, 4 * TM, TM), F32)],
        compiler_params=_cparams(("parallel", "parallel")),
        name="na_attn",
    )(qn, kn, kn, kn, kn, vn, vn, vn, vn, bias_tab)


def _na_bias_table(na_bias_l, rows):
    ct = rows * GRID_W // TM
    rpt = TM // GRID_W
    wr = min(NA_WIN_R, rows)
    qj = np.arange(GRID_W)
    kc = np.arange(GRID_W)
    cs = np.clip(qj - NA_WIN_C // 2, 0, GRID_W - NA_WIN_C)
    colvalid = (kc[None, :] >= cs[:, None]) & (kc[None, :] < cs[:, None] + NA_WIN_C)
    dc = kc[None, :] - qj[:, None] + (NA_WIN_C - 1)
    onehot = (dc.reshape(1, -1) == np.arange(2 * NA_WIN_C - 1)[:, None]) & colvalid.reshape(1, -1)
    toep = jnp.einsum("hrd,dx->hrx", na_bias_l, jnp.asarray(onehot.astype(np.float32)), precision=HIGHEST)
    toep = jnp.where(jnp.asarray(colvalid.reshape(-1)), toep * np.log2(np.e), NEG)
    toep = toep.reshape(NA_HEADS, 2 * NA_WIN_R - 1, GRID_W, GRID_W)
    toep = jnp.swapaxes(toep, -1, -2)
    neg_blk = jnp.full((NA_HEADS, GRID_W, GRID_W), NEG, F32)
    tabs = []
    for tile_i in (0, 1, ct - 1):
        i0 = tile_i * rpt
        kb = int(np.clip(tile_i - 1, 0, ct - 3)) * rpt
        rs = [int(np.clip(i0 + ri - wr // 2, 0, rows - wr)) for ri in range(rpt)]
        krows = []
        for m in range(3 * rpt):
            blks = [toep[:, kb + m - (i0 + ri) + NA_WIN_R - 1] if rs[ri] <= kb + m < rs[ri] + wr else neg_blk
                    for ri in range(rpt)]
            krows.append(jnp.concatenate(blks, axis=-1))
        tabs.append(jnp.concatenate(krows, axis=-2))
    tabs.append(jnp.full_like(tabs[0], NEG))
    return jnp.stack(tabs, axis=0)


GQA_CB = 256
GQA_UNROLL = 8
GQA_AHEAD = 3


def _gqa_kernel(q_ref, k_ref, v_ref, o_ref, *scratch, n_full, tail, tq):
    nblk = 2 * tq // GQA_CB
    m_refs, acc_refs, st_refs = scratch[:nblk], scratch[nblk:2 * nblk], scratch[2 * nblk:]
    qt = jnp.concatenate([q_ref[0, :HEAD_DIM, :], q_ref[0, HEAD_DIM:, :]], axis=1)
    for n in range(nblk):
        m_refs[n][...] = jnp.full(m_refs[n].shape, NEG, F32)
        acc_refs[n][...] = jnp.zeros(acc_refs[n].shape, F32)

    def chunks(spans):
        kcs = [k_ref[0, 0, pl.ds(start, size), :] for start, size in spans]
        vts = [v_ref[0, 0, :GQA_VROWS, pl.ds(start, size)] for start, size in spans]
        items = [(c, n) for c in range(len(spans)) for n in range(nblk)]
        score = lambda c, n: jnp.dot(kcs[c], qt[:, n * GQA_CB:(n + 1) * GQA_CB], preferred_element_type=F32)
        def issue(i):
            c, n = items[i]
            st_refs[i % len(st_refs)][:spans[c][1], :] = score(c, n)

        for i in range(min(GQA_AHEAD, len(items))):
            issue(i)
        for i, (c, n) in enumerate(items):
            if i + GQA_AHEAD < len(items):
                issue(i + GQA_AHEAD)
            st = st_refs[i % len(st_refs)][:spans[c][1], :]
            m_old = m_refs[n][...]
            m_new = jnp.maximum(m_old, jnp.max(st, axis=0, keepdims=True))
            pt = jnp.exp2(st - m_new).astype(BF16)
            acc_refs[n][...] = (jnp.exp2(m_old - m_new) * acc_refs[n][...]
                                + jnp.dot(vts[c], pt, preferred_element_type=F32))
            m_refs[n][...] = m_new

    n_trips = n_full // GQA_UNROLL
    if n_trips > 0:
        def body(j, carry):
            base = j * (GQA_UNROLL * GQA_TK)
            chunks([(pl.multiple_of(base + u * GQA_TK, GQA_TK), GQA_TK) for u in range(GQA_UNROLL)])
            return carry
        lax.fori_loop(0, n_trips, body, 0)
    rest = [(c * GQA_TK, GQA_TK) for c in range(n_trips * GQA_UNROLL, n_full)]
    if tail > 0:
        rest.append((n_full * GQA_TK, tail))
    if rest:
        chunks(rest)
    acc = jnp.concatenate([r[...] for r in acc_refs], axis=1)
    ot = acc[:HEAD_DIM] * (1.0 / acc[HEAD_DIM:HEAD_DIM + 1])
    ot = jnp.concatenate([ot, jnp.zeros_like(ot)], axis=0)
    o = ot.T
    o_ref[0] = jnp.concatenate([o[:tq, :HEAD_DIM], o[tq:, :HEAD_DIM]], axis=-1).astype(o_ref.dtype)


def _gqa(qg, kg, vg, seq, ctx_len):
    b, _, tt = qg.shape
    scratch = lambda tq: ([pltpu.VMEM((1, GQA_CB), F32)] * (2 * tq // GQA_CB)
                          + [pltpu.VMEM((GQA_VROWS, GQA_CB), F32)] * (2 * tq // GQA_CB)
                          + [pltpu.VMEM((GQA_TK, GQA_CB), F32)] * (GQA_AHEAD + 1))
    y_lat = pl.pallas_call(
        functools.partial(_gqa_kernel, n_full=tt // GQA_TK, tail=tt % GQA_TK, tq=GQA_TQ),
        grid=(b, GQA_KV_HEADS, seq // GQA_TQ),
        in_specs=[
            pl.BlockSpec((1, 2 * HEAD_DIM, GQA_TQ), lambda bi, n, i: (bi, n, i)),
            pl.BlockSpec((1, 1, tt, HEAD_DIM), lambda bi, n, i: (bi, n, 0, 0)),
            pl.BlockSpec((1, 1, 128, tt), lambda bi, n, i: (bi, n, 0, 0)),
        ],
        out_specs=pl.BlockSpec((1, GQA_TQ, 128), lambda bi, n, i: (bi, i, n)),
        out_shape=jax.ShapeDtypeStruct((b, seq, 256), BF16),
        scratch_shapes=scratch(GQA_TQ),
        compiler_params=_cparams(("parallel", "parallel", "parallel")),
        name="gqa_latent",
    )(qg, kg, vg)
    cblk = seq // ctx_len
    y_ctx = pl.pallas_call(
        functools.partial(_gqa_kernel, n_full=0, tail=ctx_len, tq=ctx_len),
        grid=(b, GQA_KV_HEADS),
        in_specs=[
            pl.BlockSpec((1, 2 * HEAD_DIM, ctx_len), lambda bi, n: (bi, n, cblk)),
            pl.BlockSpec((1, 1, ctx_len, HEAD_DIM), lambda bi, n: (bi, n, cblk, 0)),
            pl.BlockSpec((1, 1, 128, ctx_len), lambda bi, n: (bi, n, 0, cblk)),
        ],
        out_specs=pl.BlockSpec((1, ctx_len, 128), lambda bi, n: (bi, 0, n)),
        out_shape=jax.ShapeDtypeStruct((b, ctx_len, 256), BF16),
        scratch_shapes=scratch(ctx_len),
        compiler_params=_cparams(("parallel", "parallel")),
        name="gqa_context",
    )(qg, kg, vg)
    return jnp.concatenate([y_lat, y_ctx], axis=1)


def _bd(x):
    left = lax.broadcasted_iota(jnp.int32, (CH, 128), 1) < HEAD_DIM
    x0, x1 = x[:, :128], x[:, 128:]
    z = jnp.zeros_like(x0)
    keep_l = lambda a: jnp.where(left, a, z)
    keep_r = lambda a: jnp.where(left, z, a)
    return jnp.concatenate([jnp.concatenate([keep_l(x0), z], axis=1), jnp.concatenate([keep_r(x0), z], axis=1),
                            jnp.concatenate([z, keep_l(x1)], axis=1), jnp.concatenate([z, keep_r(x1)], axis=1)],
                           axis=0)


def _fold(x):
    left = lax.broadcasted_iota(jnp.int32, (CH, 128), 1) < HEAD_DIM
    return jnp.concatenate([jnp.where(left, x[0:64, :128], x[64:128, :128]),
                            jnp.where(left, x[128:192, 128:], x[192:256, 128:])], axis=1)


def _chunk_mats(items):
    t = lax.broadcasted_iota(jnp.int32, (CH, RW), 0)
    j = lax.broadcasted_iota(jnp.int32, (CH, RW), 1) & 63
    eye = j == t
    before = {False: j < t, True: j > t}
    incl = {False: j <= t, True: j >= t}
    bf = lambda x: x.astype(BF16)
    n_items = range(len(items))

    lb, mb, lk, mk = [], [], [], []
    for it in items:
        a2 = bf(jnp.concatenate([it["kap"], it["rho"]], axis=0))
        lm_b = _dg(a2, _bd(bf(it["bt"])), 1, 1)
        lm_k = _dg(a2, _bd(bf(it["kt"])), 1, 1)
        lb.append(jnp.where(before[it["rev"]], lm_b[:CH], 0.0))
        mb.append(jnp.where(incl[it["rev"]], lm_b[CH:], 0.0))
        lk.append(jnp.where(before[it["rev"]], lm_k[:CH], 0.0))
        mk.append(jnp.where(incl[it["rev"]], lm_k[CH:], 0.0))

    p, mpow = [], []
    for i in n_items:
        nb = bf(-lb[i])
        p.append(jnp.where(eye, 1.0, 0.0) - lb[i])
        mpow.append(_dg(nb, _bd(nb), 1, 0))
    for _ in range(4):
        for i in n_items:
            mbf = bf(mpow[i])
            pm = _dg(jnp.concatenate([bf(p[i]), mbf], axis=0), _bd(mbf), 1, 0)
            p[i] = p[i] + pm[:CH]
            mpow[i] = pm[CH:]
    tinv = [bf(p[i] + _dg(bf(p[i]), _bd(bf(mpow[i])), 1, 0)) for i in n_items]

    lmv = [_dg(bf(jnp.concatenate([lk[i], mk[i]], axis=0)), _bd(bf(items[i]["v"])), 1, 0) for i in n_items]
    kp = [_dg(tinv[i], _bd(bf(items[i]["kap"])), 1, 0) for i in n_items]
    vp = [_dg(tinv[i], _bd(bf(lmv[i][:CH])), 1, 0) for i in n_items]
    out = []
    for i in n_items:
        it = items[i]
        mbb = bf(mb[i])
        rp = it["rho"] - _dg(mbb, _bd(bf(kp[i])), 1, 0)
        yl = lmv[i][CH:] - _dg(mbb, _bd(bf(vp[i])), 1, 0)
        g = jnp.where(eye, it["gdiag"], 0.0) - _fold(_dg(bf(kp[i]), bf(it["bhat"]), 0, 0))
        hmat = _fold(_dg(bf(jnp.concatenate([it["v"], -vp[i]], axis=0)),
                         bf(jnp.concatenate([it["khat"], it["bhat"]], axis=0)), 0, 0))
        out.append((g, hmat, rp, yl))
    return out


def _rwkv_chunk_kernel(z_ref, hp_ref, hn_ref, mu_ref, w0_ref, w2_ref, a0_ref, a2_ref, kkw_ref, ka_ref, rk_ref,
                       bd_ref, g_ref, h_ref, rp_ref, yl_ref, bonus_ref, *, ct):
    i = pl.program_id(1)
    z = z_ref[0]
    zs = z[:, :RW_SHIFT]
    valid_prev = jnp.where((i != 0) & (i != ct), 1.0, 0.0)
    valid_next = jnp.where((i != ct - 1) & (i != ct), 1.0, 0.0)
    prev_row = hp_ref[0, HALO - 1:HALO, :RW_SHIFT] * valid_prev
    next_row = hn_ref[0, 0:1, :RW_SHIFT] * valid_next
    row = lax.broadcasted_iota(jnp.int32, (TM, RW_SHIFT), 0)
    tt = lax.broadcasted_iota(jnp.int32, (TM, TM), 0)
    jj = lax.broadcasted_iota(jnp.int32, (TM, TM), 1)
    same_chunk = (tt >> 6) == (jj >> 6)
    bd = bd_ref[...]
    def direction(d):
        if d == 0:
            nb = jnp.where(row == 0, prev_row, pltpu.roll(zs, 1, 0))
        else:
            nb = jnp.where(row == TM - 1, next_row, pltpu.roll(zs, TM - 1, 0))
        zd = zs + mu_ref[d:d + 1, :] * (nb - zs)
        r = zd[:, :RW]
        k = zd[:, RW:2 * RW]
        v = zd[:, 2 * RW:3 * RW]
        lw = zd[:, 3 * RW:3 * RW + LORA_W]
        la = zd[:, 3 * RW + LORA_W:]
        w_log = w0_ref[d:d + 1, :] + _dot3(jnp.tanh(lw), w2_ref[d, 0], w2_ref[d, 1])
        a_pre = a0_ref[d:d + 1, :] + _dot3(la, a2_ref[d, 0], a2_ref[d, 1])
        kkr = k * kkw_ref[...]
        ss = _sum_dot(kkr * kkr, bd, 2)
        yield
        sp = jnp.maximum(-w_log, 0.0) + jnp.log(1.0 + jnp.exp(-jnp.abs(w_log)))
        logw = -jnp.exp(-sp - 0.5)
        tri = jnp.where(same_chunk & ((jj >= tt) if d == 1 else (jj <= tt)), 1.0, 0.0).astype(BF16)
        cum, rest = None, logw
        for _ in range(3):
            piece = rest.astype(BF16)
            part = jnp.dot(tri, piece, preferred_element_type=F32)
            cum = part if cum is None else cum + part
            rest = rest - piece.astype(F32)
        a = _sigmoid(a_pre)
        kk = kkr * lax.rsqrt(jnp.maximum(ss, 1e-24))
        k2 = k * (1.0 + (a - 1.0) * ka_ref[...])
        bv = kk * a
        bon = _sum_dot(r * k2 * rk_ref[...], bd, 2) * v
        yield
        e_neg = jnp.exp(-cum)
        streams = dict(kap=kk * jnp.exp(cum - logw), kt=k2 * e_neg, bt=bv * e_neg, rho=r * jnp.exp(cum), v=v)
        chunk_items = []
        for c in range(TM // CH):
            sl = slice(c * CH, (c + 1) * CH)
            it = {name: val[sl] for name, val in streams.items()}
            last = c * CH if d == 1 else (c + 1) * CH - 1
            tot = cum[last:last + 1]
            e_tot = jnp.exp(tot - cum[sl])
            it.update(khat=k2[sl] * e_tot, bhat=bv[sl] * e_tot, gdiag=jnp.exp(tot), rev=d == 1)
            chunk_items.append(it)
        return chunk_items, bon

    gens = [direction(0), direction(1)]
    done = [None, None]
    while any(res is None for res in done):
        for d, gen in enumerate(gens):
            if done[d] is None:
                try:
                    next(gen)
                except StopIteration as stop:
                    done[d] = stop.value
    items = done[0][0] + done[1][0]
    bonus = done[0][1] + done[1][1]
    for idx, (g, hm, rp, yl) in enumerate(_chunk_mats(items)):
        d, c = divmod(idx, TM // CH)
        g_ref[0, d, c] = g.astype(BF16)
        h_ref[0, d, c] = hm
        rp_ref[0, d, c] = rp.astype(BF16)
        yl_ref[0, d, c] = yl
    bonus_ref[0] = bonus


def _rwkv_chunks(zr, mu, w0, w2, a0, a2, kkw, ka, rk, bd256, ct):
    b, tt, _ = zr.shape
    nt = tt // TM
    nch = tt // CH
    cpt = TM // CH
    nhb = tt // HALO
    mats = pl.BlockSpec((1, 2, cpt, CH, RW), lambda bi, i: (bi, 0, i, 0, 0))
    mshape = lambda dt: jax.ShapeDtypeStruct((b, 2, nch, CH, RW), dt)
    return pl.pallas_call(
        functools.partial(_rwkv_chunk_kernel, ct=ct),
        grid=(b, nt),
        in_specs=[
            pl.BlockSpec((1, TM, COLS_RW), lambda bi, i: (bi, i, 0)),
            pl.BlockSpec((1, HALO, COLS_RW), lambda bi, i: (bi, jnp.maximum(i * (TM // HALO) - 1, 0), 0)),
            pl.BlockSpec((1, HALO, COLS_RW), lambda bi, i: (bi, jnp.minimum((i + 1) * (TM // HALO), nhb - 1), 0)),
            _const_spec((2, RW_SHIFT)),
            _const_spec((2, RW)),
            _const_spec((2, 2, LORA_W, RW)),
            _const_spec((2, RW)),
            _const_spec((2, 2, LORA_A, RW)),
            _const_spec((1, RW)),
            _const_spec((1, RW)),
            _const_spec((1, RW)),
            _const_spec((RW, RW)),
        ],
        out_specs=[mats, mats, mats, mats, pl.BlockSpec((1, TM, RW), lambda bi, i: (bi, i, 0))],
        out_shape=[mshape(BF16), mshape(F32), mshape(BF16), mshape(F32), jax.ShapeDtypeStruct((b, tt, RW), F32)],
        compiler_params=_cparams(("parallel", "parallel")),
        name="rwkv_chunks",
    )(zr, zr, zr, mu, w0, w2, a0, a2, kkw, ka, rk, bd256)


CHAIN_GROUP = 4


def _rwkv_chain_kernel(gf_ref, hf_ref, rf_ref, yf_ref, gr_ref, hr_ref, rr_ref, yr_ref, of_ref, or_ref, s_ref,
                       *, nb):
    @pl.when(pl.program_id(0) == 0)
    def _():
        s_ref[...] = jnp.zeros(s_ref.shape, F32)

    dirs = ((gf_ref, hf_ref, rf_ref, yf_ref, of_ref), (gr_ref, hr_ref, rr_ref, yr_ref, or_ref))
    state = [[s_ref[d, bi] for bi in range(nb)] for d in range(2)]
    for step in range(CHAIN_GROUP):
        for d, (g_ref, h_ref, rp_ref, yl_ref, o_ref) in enumerate(dirs):
            c = step if d == 0 else CHAIN_GROUP - 1 - step
            for bi in range(nb):
                sb = state[d][bi].astype(BF16)
                o_ref[bi, c * CH:(c + 1) * CH, :] = _dg(rp_ref[bi, 0, c], _bd(sb), 1, 1) + yl_ref[bi, 0, c]
                state[d][bi] = _dg(sb, _bd(g_ref[bi, 0, c]), 1, 0) + h_ref[bi, 0, c]
    for d in range(2):
        for bi in range(nb):
            s_ref[d, bi] = state[d][bi]


def _rwkv_chain(g, h, rp, yl, seq, ctx_len):
    b, _, nch, _, _ = g.shape
    assert ctx_len == CHAIN_GROUP * CH and seq % (CHAIN_GROUP * CH) == 0
    ngrp = nch // CHAIN_GROUP
    n_lat = seq // (CHAIN_GROUP * CH)

    def gf(s):
        return jnp.where(s == 0, n_lat, s - 1)

    def gr(s):
        return ngrp - 1 - s

    fwd = pl.BlockSpec((b, 1, CHAIN_GROUP, CH, RW), lambda s: (0, 0, gf(s), 0, 0))
    rev = pl.BlockSpec((b, 1, CHAIN_GROUP, CH, RW), lambda s: (0, 1, gr(s), 0, 0))
    yshape = jax.ShapeDtypeStruct((b, nch * CH, RW), F32)
    return pl.pallas_call(
        functools.partial(_rwkv_chain_kernel, nb=b),
        grid=(ngrp,),
        in_specs=[fwd, fwd, fwd, fwd, rev, rev, rev, rev],
        out_specs=[pl.BlockSpec((b, CHAIN_GROUP * CH, RW), lambda s: (0, gf(s), 0)),
                   pl.BlockSpec((b, CHAIN_GROUP * CH, RW), lambda s: (0, gr(s), 0))],
        out_shape=[yshape, yshape],
        scratch_shapes=[pltpu.VMEM((2, b, CH, RW), F32)],
        compiler_params=_cparams(("arbitrary",)),
        name="rwkv_chain",
    )(g, h, rp, yl, g, h, rp, yl)


def _merge_kernel(*refs, nblk, ctx_mode):
    (x_ref, mod_ref, gpre_ref, gpost_ref, za_ref, hp_ref, hn_ref, ca_ref, yb_ref, yc_ref, yf_ref, yr_ref,
     bonus_ref, lg_ref, g2_ref, lnw_ref, lnb_ref, bd_ref, wb_ref, wg_ref, bg_ref, wo_ref) = refs[:22]
    o_ref = refs[-1]
    tm = x_ref.shape[1]
    x = x_ref[0]
    m = mod_ref[0, 0]
    hb = _rms_mod(x, gpre_ref[...], m[0:1], m[1:2]).astype(BF16)

    def gate_mm(bidx):
        return jnp.dot(hb, wg_ref[:, bidx * D_MODEL:(bidx + 1) * D_MODEL], preferred_element_type=F32)

    def gated(bidx, pre, ys):
        gate2 = jnp.tanh(pre + bg_ref[:, bidx * D_MODEL:(bidx + 1) * D_MODEL]) + 1.0
        return gate2 * jnp.dot(ys, wb_ref[bidx], preferred_element_type=F32)

    za = za_ref[0]
    u = za[:, 256:512] * za[:, 512:768]
    valid_prev, valid_next = _edge_valid(pl.program_id(1), nblk, ctx_mode)
    up = hp_ref[0, HALO - 1:HALO, 256:512] * hp_ref[0, HALO - 1:HALO, 512:768] * valid_prev
    un = hn_ref[0, 0:1, 256:512] * hn_ref[0, 0:1, 512:768] * valid_next
    row = lax.broadcasted_iota(jnp.int32, (tm, 256), 0)
    u_prev = jnp.where(row == 0, up, pltpu.roll(u, 1, 0))
    u_next = jnp.where(row == tm - 1, un, pltpu.roll(u, tm - 1, 0))
    ya = za[:, :256] * (u_prev * ca_ref[0:1, :] + u * ca_ref[1:2, :] + u_next * ca_ref[2:3, :])

    bd = bd_ref[...]
    ones_dot = lambda piece: jnp.dot(piece, bd, preferred_element_type=F32)
    y = yf_ref[0] + yr_ref[0]
    p1 = y.astype(BF16)
    s1 = ones_dot(p1)
    pre1 = gate_mm(1)
    r1 = y - p1.astype(F32)
    p2 = r1.astype(BF16)
    s2 = ones_dot(p2)
    acc = gated(1, pre1, yb_ref[0])
    s3 = ones_dot((r1 - p2.astype(F32)).astype(BF16))
    pre2 = gate_mm(2)
    yc0 = y - (s1 + s2 + s3) * (1.0 / HEAD_DIM)
    sq = yc0 * yc0
    q1 = sq.astype(BF16)
    v1 = ones_dot(q1)
    acc = acc + gated(2, pre2, yc_ref[0])
    v2 = ones_dot((sq - q1.astype(F32)).astype(BF16))
    pre0 = gate_mm(0)
    lgate = _dot3(_sigmoid(lg_ref[0]), g2_ref[0], g2_ref[1])
    acc = acc + gated(0, pre0, ya.astype(BF16))
    pre3 = gate_mm(3)
    var = (v1 + v2) * (1.0 / HEAD_DIM)
    yn = yc0 * lax.rsqrt(var + LN_X_EPS) * lnw_ref[...] + lnb_ref[...] + bonus_ref[0]
    acc = acc + gated(3, pre3, (yn * lgate).astype(BF16))
    mo = jnp.dot(acc.astype(BF16), wo_ref[...], preferred_element_type=F32)
    o_ref[0] = x + m[2:3] * _rms(mo, gpost_ref[...])


def _merge(xa, modsel, g_pre, g_post, za, conv_a, yb, yc, yf, yr, bonus, zr, g2, ln_w, ln_b, bd256,
           w_branch, w_gate, b_gate, w_o, seq, tm, prev=None):
    b, tt, _ = xa.shape
    ctx_mode = prev is not None
    nblk, blk, steps, alias_specs = _token_tiling(seq, tm, ctx_mode, 1)
    tile = lambda w: pl.BlockSpec((1, tm, w), lambda bi, i: (bi, blk(i), 0))
    return pl.pallas_call(
        functools.partial(_merge_kernel, nblk=nblk, ctx_mode=ctx_mode),
        grid=(b, steps),
        in_specs=[
            tile(D_MODEL),
            pl.BlockSpec((1, 1, 6, D_MODEL), lambda bi, i: (bi, int(ctx_mode), 0, 0)),
            _const_spec((1, D_MODEL)),
            _const_spec((1, D_MODEL)),
            tile(COLS_A),
            *_halo_specs(COLS_A, tm, blk, tt // HALO),
            _const_spec((3, 256)),
            tile(256), tile(256), tile(256), tile(256), tile(256),
            pl.BlockSpec((1, tm, LORA_G), lambda bi, i: (bi, blk(i), RW_SHIFT // LORA_G)),
            _const_spec((2, LORA_G, RW)),
            _const_spec((1, RW)),
            _const_spec((1, RW)),
            _const_spec((RW, RW)),
            _const_spec((4, 256, D_MODEL)),
            _const_spec((D_MODEL, 4 * D_MODEL)),
            _const_spec((1, 4 * D_MODEL)),
            _const_spec((D_MODEL, D_MODEL)),
        ] + alias_specs,
        out_specs=tile(D_MODEL),
        out_shape=jax.ShapeDtypeStruct((b, tt, D_MODEL), F32),
        input_output_aliases={22: 0} if ctx_mode else {},
        compiler_params=_cparams(("parallel", "parallel")),
        name="merge_ctx" if ctx_mode else "merge",
    )(xa, modsel, g_pre, g_post, za, za, za, conv_a, yb, yc, yf, yr, bonus, zr, g2, ln_w, ln_b, bd256,
      w_branch, w_gate, b_gate, w_o, *((prev,) if ctx_mode else ()))


def _ffn_kernel(*refs, nblk, ctx_mode):
    x_ref, hp_ref, hn_ref, mod_ref, gpre_ref, gpost_ref, wu_ref, cw_ref, wd_ref = refs[:9]
    o_ref, act_ref = refs[-2:]
    tm = x_ref.shape[1]
    x = x_ref[0]
    m = mod_ref[0, 0]
    xx = jnp.concatenate([hp_ref[0], x, hn_ref[0]], axis=0)
    nrow = tm + 2 * HALO
    row = lax.broadcasted_iota(jnp.int32, (nrow, 1), 0)
    valid_prev, valid_next = _edge_valid(pl.program_id(1), nblk, ctx_mode)
    rowmask = jnp.where(row < HALO, valid_prev, jnp.where(row >= tm + HALO, valid_next, 1.0))
    hb = (_rms_mod(xx, gpre_ref[...], m[3:4], m[4:5]) * rowmask).astype(BF16)

    def conv(u, col):
        w = cw_ref[:, col:col + FF_CHUNK]
        c = (pltpu.roll(u, 1, 0) * w[0:1] + u * w[1:2] + pltpu.roll(u, nrow - 1, 0) * w[2:3])
        return c[HALO:HALO + tm]

    for j in range(D_FF // FF_CHUNK):
        ca = conv(jnp.dot(hb, wu_ref[:, j * FF_CHUNK:(j + 1) * FF_CHUNK], preferred_element_type=F32),
                  j * FF_CHUNK)
        cg = conv(jnp.dot(hb, wu_ref[:, D_FF + j * FF_CHUNK:D_FF + (j + 1) * FF_CHUNK],
                          preferred_element_type=F32), D_FF + j * FF_CHUNK)
        act_ref[:, j * FF_CHUNK:(j + 1) * FF_CHUNK] = (ca * (cg * (1.0 + jnp.tanh(cg)))).astype(BF16)
    f = jnp.dot(act_ref[...], wd_ref[...], preferred_element_type=F32)
    o_ref[0] = x + m[5:6] * _rms(f, gpost_ref[...])


def _ffn(xa, modsel, g_pre, g_post, w_up, conv_w, w_down, seq, tm, prev=None, out_rows=None):
    b, tt, _ = xa.shape
    ctx_mode = prev is not None
    nblk, blk, steps, alias_specs = _token_tiling(seq, tm, ctx_mode, 1)
    return pl.pallas_call(
        functools.partial(_ffn_kernel, nblk=nblk, ctx_mode=ctx_mode),
        grid=(b, steps),
        in_specs=[
            pl.BlockSpec((1, tm, D_MODEL), lambda bi, i: (bi, blk(i), 0)),
            *_halo_specs(D_MODEL, tm, blk, tt // HALO),
            pl.BlockSpec((1, 1, 6, D_MODEL), lambda bi, i: (bi, int(ctx_mode), 0, 0)),
            _const_spec((1, D_MODEL)),
            _const_spec((1, D_MODEL)),
            _const_spec((D_MODEL, 2 * D_FF)),
            _const_spec((3, 2 * D_FF)),
            _const_spec((D_FF, D_MODEL)),
        ] + alias_specs,
        out_specs=pl.BlockSpec((1, tm, D_MODEL), lambda bi, i: (bi, blk(i), 0)),
        out_shape=jax.ShapeDtypeStruct((b, out_rows or tt, D_MODEL), F32),
        scratch_shapes=[pltpu.VMEM((tm, D_FF), BF16)],
        input_output_aliases={9: 0} if ctx_mode else {},
        compiler_params=_cparams(("parallel", "parallel")),
        name="conv_ffn_ctx" if ctx_mode else "conv_ffn",
    )(xa, xa, xa, modsel, g_pre, g_post, w_up, conv_w, w_down, *((prev,) if ctx_mode else ()))


def _rope_tables(seq, ctx_len):
    t = np.arange(seq)
    row = (t // GRID_W).astype(np.float32)
    col = (t % GRID_W).astype(np.float32)
    n_freq = HEAD_DIM // 4
    inv_freq = jnp.asarray(ROPE_THETA, F32) ** (-jnp.arange(n_freq, dtype=F32) / n_freq)
    ang = jnp.concatenate([jnp.asarray(row)[:, None] * inv_freq, jnp.asarray(col)[:, None] * inv_freq], axis=-1)
    cos = jnp.repeat(jnp.cos(ang), 2, axis=-1)
    sin = jnp.repeat(jnp.sin(ang), 2, axis=-1) * jnp.tile(jnp.asarray([-1.0, 1.0], F32), HEAD_DIM // 2)
    cos = jnp.concatenate([cos, jnp.ones((ctx_len, HEAD_DIM), F32)], axis=0)
    sin = jnp.concatenate([sin, jnp.zeros((ctx_len, HEAD_DIM), F32)], axis=0)
    nh = GQA_Q_HEADS + GQA_KV_HEADS
    return jnp.tile(cos, (1, nh)), jnp.tile(sin, (1, nh))


def _block_ones(n, scale):
    idx = np.arange(n) // HEAD_DIM
    return jnp.asarray((idx[:, None] == idx[None, :]).astype(np.float32) * scale)


def kernel(x, c, ctx, c_ctx, ada_w, ada_b, norm_mix_pre, norm_mix_post, norm_ffn_pre, norm_ffn_post, w_in, conv_a, na_bias, q_norm, k_norm, rw_mu, rw_w0, rw_w2, rw_a0, rw_a2, rw_kk, rw_ka, rw_rk, rw_g2, rw_ln_w, rw_ln_b, w_branch, w_gate, b_gate, w_o, ffn_up, ffn_conv, ffn_down):
    b, seq, _ = x.shape
    ctx_len = ctx.shape[1]
    depth = ada_w.shape[0]
    assert ctx_len == TM and seq % GQA_TQ == 0 and seq % TML == 0 and seq // TM >= 3 and b + 1 <= 8
    ct = seq // TM
    rows = seq // GRID_W

    cvec = jnp.zeros((8, D_MODEL), F32).at[:b].set(c).at[b].set(c_ctx)
    mods = _ada(cvec, ada_w, ada_b)
    cos_t, sin_t = _rope_tables(seq, ctx_len)
    bd384 = _block_ones(384, 1.0 / HEAD_DIM).astype(BF16)
    bd256 = _block_ones(256, 1.0).astype(BF16)

    def hi_lo(w):
        hi = w.astype(BF16)
        return jnp.stack([hi, (w - hi.astype(F32)).astype(BF16)], axis=-3)

    xa = jnp.concatenate([x, ctx], axis=1)
    for l in range(depth):
        ml = mods[l].reshape(8, 6, D_MODEL)
        modsel = jnp.stack([ml[:b], jnp.broadcast_to(ml[b][None], (b, 6, D_MODEL))], axis=1)
        row2 = lambda a: a.reshape(1, -1)
        gqk = jnp.concatenate([jnp.tile(q_norm[l], GQA_Q_HEADS) * (HEAD_DIM ** -0.5 * np.log2(np.e)),
                               jnp.tile(k_norm[l], GQA_KV_HEADS)]).reshape(1, -1)
        ip_args = (xa, modsel, row2(norm_mix_pre[l]), w_in[l].astype(BF16), cos_t, sin_t, gqk, bd384, seq)
        za, qn, kn, vn, qg, kg, vg, zr = _inproj(*ip_args, ctx_len, prev=_inproj(*ip_args, TML))
        yb = _na(qn, kn, vn, _na_bias_table(na_bias[l], rows), ct)
        yc = _gqa(qg, kg, vg, seq, ctx_len)
        g, h, rp, yl, bonus = _rwkv_chunks(zr, rw_mu[l], rw_w0[l], hi_lo(rw_w2[l]), rw_a0[l], hi_lo(rw_a2[l]),
                                           row2(rw_kk[l]), row2(rw_ka[l]), row2(rw_rk[l]), bd256, ct)
        yf, yr = _rwkv_chain(g, h, rp, yl, seq, ctx_len)
        mg_args = (xa, modsel, row2(norm_mix_pre[l]), row2(norm_mix_post[l]), za, conv_a[l], yb, yc, yf, yr,
                   bonus, zr, hi_lo(rw_g2[l]), row2(rw_ln_w[l]), row2(rw_ln_b[l]), bd256,
                   w_branch[l].astype(BF16), (0.5 * w_gate[l]).astype(BF16), row2(0.5 * b_gate[l]),
                   (0.5 * w_o[l]).astype(BF16), seq)
        xm = _merge(*mg_args, ctx_len, prev=_merge(*mg_args, TML))
        conv_w = jnp.concatenate([ffn_conv[l][:, :D_FF], 0.5 * ffn_conv[l][:, D_FF:]], axis=1)
        ff_args = (xm, modsel, row2(norm_ffn_pre[l]), row2(norm_ffn_post[l]), ffn_up[l].astype(BF16),
                   conv_w, ffn_down[l].astype(BF16), seq)
        if l < depth - 1:
            xa = _ffn(*ff_args, ctx_len, prev=_ffn(*ff_args, TML))
        else:
            xa = _ffn(*ff_args, TML, out_rows=seq)
    return xa
```

```python
import functools

import numpy as np
import jax
import jax.numpy as jnp
from jax import lax
from jax.experimental import pallas as pl
from jax.experimental.pallas import tpu as pltpu

F32 = jnp.float32
BF16 = jnp.bfloat16
HIGHEST = lax.Precision.HIGHEST

D_MODEL = 1024
GRID_W = 64
HEAD_DIM = 64
NA_HEADS = 4
NA_WIN_R = 8
NA_WIN_C = 16
GQA_Q_HEADS = 4
GQA_KV_HEADS = 2
ROPE_THETA = 10000.0
RWKV_HEADS = 4
RW = RWKV_HEADS * HEAD_DIM
LORA_W = 64
LORA_A = 64
LORA_G = 128
RW_SHIFT = 3 * RW + LORA_W + LORA_A
D_FF = 2816
NORM_EPS = 1e-6
LN_X_EPS = 64e-5
COLS_A = 768
COLS_NA = 768
COLS_GQA = 512
COLS_RW = 1024
D_IN = COLS_A + COLS_NA + COLS_GQA + COLS_RW

TM = 256
TML = 512
CH = 64
HALO = 8
FF_CHUNK = 256
GQA_VROWS = 80
GQA_TQ = 2048
GQA_TK = 512
NEG = -1e30
VMEM_LIMIT = 56 * 1024 * 1024


def _cparams(sem):
    return pltpu.CompilerParams(dimension_semantics=sem, vmem_limit_bytes=VMEM_LIMIT)


def _const_spec(shape):
    nd = len(shape)
    return pl.BlockSpec(shape, lambda *_: (0,) * nd, pipeline_mode=pl.Buffered(1))


def _token_tiling(seq, tm, ctx_mode, n_out):
    nblk = seq // tm
    blk = (lambda i: i * 0 + nblk) if ctx_mode else (lambda i: i)
    alias_specs = [pl.BlockSpec(memory_space=pl.ANY)] * n_out if ctx_mode else []
    return nblk, blk, (1 if ctx_mode else nblk), alias_specs


def _halo_specs(width, tm, blk, nhb):
    per = tm // HALO
    return [pl.BlockSpec((1, HALO, width), lambda bi, i: (bi, jnp.maximum(blk(i) * per - 1, 0), 0)),
            pl.BlockSpec((1, HALO, width), lambda bi, i: (bi, jnp.minimum((blk(i) + 1) * per, nhb - 1), 0))]


def _edge_valid(i, nblk, ctx_mode):
    if ctx_mode:
        return 0.0, 0.0
    return jnp.where(i != 0, 1.0, 0.0), jnp.where(i != nblk - 1, 1.0, 0.0)


def _dg(a, b, ca, cb, **kw):
    return lax.dot_general(a, b, (((ca,), (cb,)), ((), ())), preferred_element_type=F32, **kw)


def _sum_dot(x, w, terms):
    acc = None
    for _ in range(terms):
        piece = x.astype(BF16)
        part = jnp.dot(piece, w, preferred_element_type=F32)
        acc = part if acc is None else acc + part
        x = x - piece.astype(F32)
    return acc


def _dot3(x, w_hi, w_lo):
    x_hi = x.astype(BF16)
    x_lo = (x - x_hi.astype(F32)).astype(BF16)
    return (jnp.dot(x_hi, w_hi, preferred_element_type=F32) + jnp.dot(x_lo, w_hi, preferred_element_type=F32)
            + jnp.dot(x_hi, w_lo, preferred_element_type=F32))


def _rms_mod(x, g, shift, scale):
    y = x * lax.rsqrt(jnp.mean(x * x, axis=-1, keepdims=True) + NORM_EPS) * g
    return y * (1.0 + scale) + shift


def _rms(x, g):
    return x * lax.rsqrt(jnp.mean(x * x, axis=-1, keepdims=True) + NORM_EPS) * g


def _sigmoid(x):
    return 0.5 * jnp.tanh(0.5 * x) + 0.5


def _ada_kernel(c_ref, w_ref, b_ref, o_ref):
    c = c_ref[...]
    s = c * _sigmoid(c)
    o_ref[0] = jnp.dot(s, w_ref[0], precision=HIGHEST, preferred_element_type=F32) + b_ref[0]


def _ada(cvec, ada_w, ada_b):
    depth = ada_w.shape[0]
    nblk = ada_w.shape[2] // D_MODEL
    return pl.pallas_call(
        _ada_kernel,
        grid=(depth, nblk),
        in_specs=[
            pl.BlockSpec((8, D_MODEL), lambda l, j: (0, 0)),
            pl.BlockSpec((1, D_MODEL, D_MODEL), lambda l, j: (l, 0, j)),
            pl.BlockSpec((1, 1, D_MODEL), lambda l, j: (l, 0, j)),
        ],
        out_specs=pl.BlockSpec((1, 8, D_MODEL), lambda l, j: (l, 0, j)),
        out_shape=jax.ShapeDtypeStruct((depth, 8, ada_w.shape[2]), F32),
        compiler_params=_cparams(("parallel", "parallel")),
        name="ada_mod",
    )(cvec, ada_w, ada_b.reshape(depth, 1, -1))


def _inproj_kernel(*refs):
    x_ref, mod_ref, g_ref, w_ref, cos_ref, sin_ref, gqk_ref, bd_ref = refs[:8]
    za_ref, qn_ref, kn_ref, vn_ref, qg_ref, kg_ref, vg_ref, zr_ref = refs[-8:]
    tm = x_ref.shape[1]
    x = x_ref[0]
    m = mod_ref[0, 0]
    hb = _rms_mod(x, g_ref[...], m[0:1], m[1:2]).astype(BF16)
    e1, e2, e3 = COLS_A, COLS_A + COLS_NA, COLS_A + COLS_NA + COLS_GQA
    g = jnp.dot(hb, w_ref[:, e2:e3], preferred_element_type=F32)
    na = jnp.dot(hb, w_ref[:, e1:e2], preferred_element_type=F32)
    lane = lax.broadcasted_iota(jnp.int32, (tm, 128), 1)
    scale = HEAD_DIM ** -0.5 * np.log2(np.e)
    qn_ref[0] = jnp.concatenate([(na[:, :128] * scale).T, (na[:, 128:256] * scale).T], axis=0).astype(BF16)
    for hd in range(NA_HEADS):
        kn_ref[0, hd] = na[:, 256 + hd * 64:256 + (hd + 1) * 64].astype(BF16)
        vpair = na[:, 512 + (hd // 2) * 128:512 + (hd // 2 + 1) * 128]
        vh = vpair if hd % 2 == 0 else pltpu.roll(vpair, 64, 1)
        vn_ref[0, hd] = jnp.where(lane < 64, vh, jnp.where(lane == 64, 1.0, 0.0)).T.astype(BF16)
    qk = g[:, :384]
    ms = _sum_dot(qk * qk, bd_ref[...], 2)
    za_ref[0] = jnp.dot(hb, w_ref[:, :e1], preferred_element_type=F32)
    zr_ref[0] = jnp.dot(hb, w_ref[:, e3:], preferred_element_type=F32)
    qkn = qk * lax.rsqrt(ms + NORM_EPS) * gqk_ref[...]
    even = (lane & 1) == 0
    parts = []
    for j in range(3):
        s = qkn[:, j * 128:(j + 1) * 128]
        sw = jnp.where(even, pltpu.roll(s, 127, 1), pltpu.roll(s, 1, 1))
        parts.append(s * cos_ref[:, j * 128:(j + 1) * 128] + sw * sin_ref[:, j * 128:(j + 1) * 128])
    qg_ref[0] = jnp.concatenate([parts[0].T, parts[1].T], axis=0).astype(BF16)
    for hd in range(GQA_KV_HEADS):
        kg_ref[0, hd] = parts[2][:, hd * 64:(hd + 1) * 64].astype(BF16)
    vt = g[:, 384:512]
    for hd in range(GQA_KV_HEADS):
        vh = vt if hd == 0 else pltpu.roll(vt, 64, 1)
        vext = jnp.where(lane < 64, vh, jnp.where(lane == 64, 1.0, 0.0))
        vg_ref[0, hd] = vext.T.astype(BF16)


def _inproj(xa, modsel, g_pre, w_in, cos_t, sin_t, gqk, bd384, seq, tm, prev=None):
    b, tt, _ = xa.shape
    ctx_mode = prev is not None
    nblk, blk, steps, alias_specs = _token_tiling(seq, tm, ctx_mode, 8)
    tile = lambda w: pl.BlockSpec((1, tm, w), lambda bi, i: (bi, blk(i), 0))
    heads = lambda nh: pl.BlockSpec((1, nh, tm, 64), lambda bi, i: (bi, 0, blk(i), 0))
    hs = lambda nh: jax.ShapeDtypeStruct((b, nh, tt, 64), BF16)
    return pl.pallas_call(
        _inproj_kernel,
        grid=(b, steps),
        in_specs=[
            tile(D_MODEL),
            pl.BlockSpec((1, 1, 6, D_MODEL), lambda bi, i: (bi, int(ctx_mode), 0, 0)),
            _const_spec((1, D_MODEL)),
            _const_spec((D_MODEL, D_IN)),
            pl.BlockSpec((tm, 384), lambda bi, i: (blk(i), 0)),
            pl.BlockSpec((tm, 384), lambda bi, i: (blk(i), 0)),
            _const_spec((1, 384)),
            _const_spec((384, 384)),
        ] + alias_specs,
        out_specs=[tile(COLS_A),
                   pl.BlockSpec((1, NA_HEADS * HEAD_DIM, tm), lambda bi, i: (bi, 0, blk(i))),
                   heads(4),
                   pl.BlockSpec((1, NA_HEADS, 128, tm), lambda bi, i: (bi, 0, 0, blk(i))),
                   pl.BlockSpec((1, GQA_Q_HEADS * HEAD_DIM, tm), lambda bi, i: (bi, 0, blk(i))),
                   heads(2),
                   pl.BlockSpec((1, 2, 128, tm), lambda bi, i: (bi, 0, 0, blk(i))), tile(COLS_RW)],
        out_shape=[jax.ShapeDtypeStruct((b, tt, COLS_A), F32),
                   jax.ShapeDtypeStruct((b, NA_HEADS * HEAD_DIM, tt), BF16),
                   hs(4),
                   jax.ShapeDtypeStruct((b, NA_HEADS, 128, tt), BF16),
                   jax.ShapeDtypeStruct((b, GQA_Q_HEADS * HEAD_DIM, tt), BF16),
                   hs(2),
                   jax.ShapeDtypeStruct((b, 2, 128, tt), BF16),
                   jax.ShapeDtypeStruct((b, tt, COLS_RW), F32)],
        input_output_aliases={8 + k: k for k in range(8)} if ctx_mode else {},
        compiler_params=_cparams(("parallel", "parallel")),
        name="inproj_ctx" if ctx_mode else "inproj",
    )(xa, modsel, g_pre, w_in, cos_t, sin_t, gqk, bd384, *(prev or ()))


def _na_kernel(q_ref, k0_ref, k1_ref, k2_ref, kc_ref, v0_ref, v1_ref, v2_ref, vc_ref, bias_ref, o_ref, st_ref):
    slot = pl.program_id(1) % 2
    for hd in range(NA_HEADS):
        kcat = jnp.concatenate([k0_ref[0, hd], k1_ref[0, hd], k2_ref[0, hd], kc_ref[0, hd]], axis=0)
        st_ref[slot, hd] = jnp.dot(kcat, q_ref[0, hd * HEAD_DIM:(hd + 1) * HEAD_DIM, :],
                                   preferred_element_type=F32)
    outs = []
    for hd in range(NA_HEADS):
        st_loc = st_ref[slot, hd, :3 * TM, :] + bias_ref[0, hd]
        st_ctx = st_ref[slot, hd, 3 * TM:, :]
        m = jnp.maximum(jnp.max(st_loc, axis=0, keepdims=True), jnp.max(st_ctx, axis=0, keepdims=True))
        pt = jnp.concatenate([jnp.exp2(st_loc - m), jnp.exp2(st_ctx - m)], axis=0).astype(BF16)
        vt = jnp.concatenate([r[0, hd, :GQA_VROWS, :] for r in (v0_ref, v1_ref, v2_ref, vc_ref)], axis=1)
        acc = jnp.dot(vt, pt, preferred_element_type=F32)
        outs.append(acc[:HEAD_DIM] * (1.0 / acc[HEAD_DIM:HEAD_DIM + 1]))
    o_ref[0] = jnp.concatenate(outs, axis=0).T.astype(o_ref.dtype)


def _na(qn, kn, vn, bias_tab, ct):
    b, _, tt = qn.shape
    nt = tt // TM

    def key_tile(i, j):
        return ct if j is None else jnp.clip(i - 1, 0, ct - 3) + j

    k_spec = lambda j: pl.BlockSpec((1, 4, TM, 64), lambda bi, i: (bi, 0, key_tile(i, j), 0))
    v_spec = lambda j: pl.BlockSpec((1, 4, 128, TM), lambda bi, i: (bi, 0, 0, key_tile(i, j)))

    def pattern(i):
        return jnp.where(i == ct, 3, jnp.where(i == 0, 0, jnp.where(i == ct - 1, 2, 1)))

    return pl.pallas_call(
        _na_kernel,
        grid=(b, nt),
        in_specs=[
            pl.BlockSpec((1, NA_HEADS * HEAD_DIM, TM), lambda bi, i: (bi, 0, i)),
            k_spec(0), k_spec(1), k_spec(2), k_spec(None),
            v_spec(0), v_spec(1), v_spec(2), v_spec(None),
            pl.BlockSpec((1, 4, 3 * TM, TM), lambda bi, i: (pattern(i), 0, 0, 0)),
        ],
        out_specs=pl.BlockSpec((1, TM, 256), lambda bi, i: (bi, i, 0)),
        out_shape=jax.ShapeDtypeStruct((b, tt, 256), BF16),
        scratch_shapes=[pltpu.VMEM((2, NA_HEADS, 4 * TM, TM), F32)],
        compiler_params=_cparams(("parallel", "parallel")),
        name="na_attn",
    )(qn, kn, kn, kn, kn, vn, vn, vn, vn, bias_tab)


def _na_bias_table(na_bias_l, rows):
    ct = rows * GRID_W // TM
    rpt = TM // GRID_W
    wr = min(NA_WIN_R, rows)
    qj = np.arange(GRID_W)
    kc = np.arange(GRID_W)
    cs = np.clip(qj - NA_WIN_C // 2, 0, GRID_W - NA_WIN_C)
    colvalid = (kc[None, :] >= cs[:, None]) & (kc[None, :] < cs[:, None] + NA_WIN_C)
    dc = kc[None, :] - qj[:, None] + (NA_WIN_C - 1)
    onehot = (dc.reshape(1, -1) == np.arange(2 * NA_WIN_C - 1)[:, None]) & colvalid.reshape(1, -1)
    toep = jnp.einsum("hrd,dx->hrx", na_bias_l, jnp.asarray(onehot.astype(np.float32)), precision=HIGHEST)
    toep = jnp.where(jnp.asarray(colvalid.reshape(-1)), toep * np.log2(np.e), NEG)
    toep = toep.reshape(NA_HEADS, 2 * NA_WIN_R - 1, GRID_W, GRID_W)
    toep = jnp.swapaxes(toep, -1, -2)
    neg_blk = jnp.full((NA_HEADS, GRID_W, GRID_W), NEG, F32)
    tabs = []
    for tile_i in (0, 1, ct - 1):
        i0 = tile_i * rpt
        kb = int(np.clip(tile_i - 1, 0, ct - 3)) * rpt
        rs = [int(np.clip(i0 + ri - wr // 2, 0, rows - wr)) for ri in range(rpt)]
        krows = []
        for m in range(3 * rpt):
            blks = [toep[:, kb + m - (i0 + ri) + NA_WIN_R - 1] if rs[ri] <= kb + m < rs[ri] + wr else neg_blk
                    for ri in range(rpt)]
            krows.append(jnp.concatenate(blks, axis=-1))
        tabs.append(jnp.concatenate(krows, axis=-2))
    tabs.append(jnp.full_like(tabs[0], NEG))
    return jnp.stack(tabs, axis=0)


GQA_CB = 256
GQA_UNROLL = 4
GQA_AHEAD = 3


def _gqa_kernel(q_ref, k_ref, v_ref, o_ref, *scratch, n_full, tail, tq):
    nblk = 2 * tq // GQA_CB
    m_refs, acc_refs, st_refs = scratch[:nblk], scratch[nblk:2 * nblk], scratch[2 * nblk:]
    qt = jnp.concatenate([q_ref[0, :HEAD_DIM, :], q_ref[0, HEAD_DIM:, :]], axis=1)
    for n in range(nblk):
        m_refs[n][...] = jnp.full(m_refs[n].shape, NEG, F32)
        acc_refs[n][...] = jnp.zeros(acc_refs[n].shape, F32)

    def chunks(spans):
        kcs = [k_ref[0, 0, pl.ds(start, size), :] for start, size in spans]
        vts = [v_ref[0, 0, :GQA_VROWS, pl.ds(start, size)] for start, size in spans]
        items = [(c, n) for c in range(len(spans)) for n in range(nblk)]
        score = lambda c, n: jnp.dot(kcs[c], qt[:, n * GQA_CB:(n + 1) * GQA_CB], preferred_element_type=F32)
        def issue(i):
            c, n = items[i]
            st_refs[i % len(st_refs)][:spans[c][1], :] = score(c, n)

        for i in range(min(GQA_AHEAD, len(items))):
            issue(i)
        for i, (c, n) in enumerate(items):
            if i + GQA_AHEAD < len(items):
                issue(i + GQA_AHEAD)
            st = st_refs[i % len(st_refs)][:spans[c][1], :]
            m_old = m_refs[n][...]
            m_new = jnp.maximum(m_old, jnp.max(st, axis=0, keepdims=True))
            pt = jnp.exp2(st - m_new).astype(BF16)
            acc_refs[n][...] = (jnp.exp2(m_old - m_new) * acc_refs[n][...]
                                + jnp.dot(vts[c], pt, preferred_element_type=F32))
            m_refs[n][...] = m_new

    n_trips = n_full // GQA_UNROLL
    if n_trips > 0:
        def body(j, carry):
            base = j * (GQA_UNROLL * GQA_TK)
            chunks([(pl.multiple_of(base + u * GQA_TK, GQA_TK), GQA_TK) for u in range(GQA_UNROLL)])
            return carry
        lax.fori_loop(0, n_trips, body, 0)
    rest = [(c * GQA_TK, GQA_TK) for c in range(n_trips * GQA_UNROLL, n_full)]
    if tail > 0:
        rest.append((n_full * GQA_TK, tail))
    if rest:
        chunks(rest)
    acc = jnp.concatenate([r[...] for r in acc_refs], axis=1)
    ot = acc[:HEAD_DIM] * (1.0 / acc[HEAD_DIM:HEAD_DIM + 1])
    ot = jnp.concatenate([ot, jnp.zeros_like(ot)], axis=0)
    o = ot.T
    o_ref[0] = jnp.concatenate([o[:tq, :HEAD_DIM], o[tq:, :HEAD_DIM]], axis=-1).astype(o_ref.dtype)


def _gqa(qg, kg, vg, seq, ctx_len):
    b, _, tt = qg.shape
    scratch = lambda tq: ([pltpu.VMEM((1, GQA_CB), F32)] * (2 * tq // GQA_CB)
                          + [pltpu.VMEM((GQA_VROWS, GQA_CB), F32)] * (2 * tq // GQA_CB)
                          + [pltpu.VMEM((GQA_TK, GQA_CB), F32)] * (GQA_AHEAD + 1))
    y_lat = pl.pallas_call(
        functools.partial(_gqa_kernel, n_full=tt // GQA_TK, tail=tt % GQA_TK, tq=GQA_TQ),
        grid=(b, GQA_KV_HEADS, seq // GQA_TQ),
        in_specs=[
            pl.BlockSpec((1, 2 * HEAD_DIM, GQA_TQ), lambda bi, n, i: (bi, n, i)),
            pl.BlockSpec((1, 1, tt, HEAD_DIM), lambda bi, n, i: (bi, n, 0, 0)),
            pl.BlockSpec((1, 1, 128, tt), lambda bi, n, i: (bi, n, 0, 0)),
        ],
        out_specs=pl.BlockSpec((1, GQA_TQ, 128), lambda bi, n, i: (bi, i, n)),
        out_shape=jax.ShapeDtypeStruct((b, seq, 256), BF16),
        scratch_shapes=scratch(GQA_TQ),
        compiler_params=_cparams(("parallel", "parallel", "parallel")),
        name="gqa_latent",
    )(qg, kg, vg)
    cblk = seq // ctx_len
    y_ctx = pl.pallas_call(
        functools.partial(_gqa_kernel, n_full=0, tail=ctx_len, tq=ctx_len),
        grid=(b, GQA_KV_HEADS),
        in_specs=[
            pl.BlockSpec((1, 2 * HEAD_DIM, ctx_len), lambda bi, n: (bi, n, cblk)),
            pl.BlockSpec((1, 1, ctx_len, HEAD_DIM), lambda bi, n: (bi, n, cblk, 0)),
            pl.BlockSpec((1, 1, 128, ctx_len), lambda bi, n: (bi, n, 0, cblk)),
        ],
        out_specs=pl.BlockSpec((1, ctx_len, 128), lambda bi, n: (bi, 0, n)),
        out_shape=jax.ShapeDtypeStruct((b, ctx_len, 256), BF16),
        scratch_shapes=scratch(ctx_len),
        compiler_params=_cparams(("parallel", "parallel")),
        name="gqa_context",
    )(qg, kg, vg)
    return jnp.concatenate([y_lat, y_ctx], axis=1)


def _bd(x):
    left = lax.broadcasted_iota(jnp.int32, (CH, 128), 1) < HEAD_DIM
    x0, x1 = x[:, :128], x[:, 128:]
    z = jnp.zeros_like(x0)
    keep_l = lambda a: jnp.where(left, a, z)
    keep_r = lambda a: jnp.where(left, z, a)
    return jnp.concatenate([jnp.concatenate([keep_l(x0), z], axis=1), jnp.concatenate([keep_r(x0), z], axis=1),
                            jnp.concatenate([z, keep_l(x1)], axis=1), jnp.concatenate([z, keep_r(x1)], axis=1)],
                           axis=0)


def _fold(x):
    left = lax.broadcasted_iota(jnp.int32, (CH, 128), 1) < HEAD_DIM
    return jnp.concatenate([jnp.where(left, x[0:64, :128], x[64:128, :128]),
                            jnp.where(left, x[128:192, 128:], x[192:256, 128:])], axis=1)


def _chunk_mats(items):
    t = lax.broadcasted_iota(jnp.int32, (CH, RW), 0)
    j = lax.broadcasted_iota(jnp.int32, (CH, RW), 1) & 63
    eye = j == t
    before = {False: j < t, True: j > t}
    incl = {False: j <= t, True: j >= t}
    bf = lambda x: x.astype(BF16)
    n_items = range(len(items))

    lb, mb, lk, mk = [], [], [], []
    for it in items:
        a2 = bf(jnp.concatenate([it["kap"], it["rho"]], axis=0))
        lm_b = _dg(a2, _bd(bf(it["bt"])), 1, 1)
        lm_k = _dg(a2, _bd(bf(it["kt"])), 1, 1)
        lb.append(jnp.where(before[it["rev"]], lm_b[:CH], 0.0))
        mb.append(jnp.where(incl[it["rev"]], lm_b[CH:], 0.0))
        lk.append(jnp.where(before[it["rev"]], lm_k[:CH], 0.0))
        mk.append(jnp.where(incl[it["rev"]], lm_k[CH:], 0.0))

    p, mpow = [], []
    for i in n_items:
        nb = bf(-lb[i])
        p.append(jnp.where(eye, 1.0, 0.0) - lb[i])
        mpow.append(_dg(nb, _bd(nb), 1, 0))
    for _ in range(4):
        for i in n_items:
            mbf = bf(mpow[i])
            pm = _dg(jnp.concatenate([bf(p[i]), mbf], axis=0), _bd(mbf), 1, 0)
            p[i] = p[i] + pm[:CH]
            mpow[i] = pm[CH:]
    tinv = [bf(p[i] + _dg(bf(p[i]), _bd(bf(mpow[i])), 1, 0)) for i in n_items]

    lmv = [_dg(bf(jnp.concatenate([lk[i], mk[i]], axis=0)), _bd(bf(items[i]["v"])), 1, 0) for i in n_items]
    kp = [_dg(tinv[i], _bd(bf(items[i]["kap"])), 1, 0) for i in n_items]
    vp = [_dg(tinv[i], _bd(bf(lmv[i][:CH])), 1, 0) for i in n_items]
    out = []
    for i in n_items:
        it = items[i]
        mbb = bf(mb[i])
        rp = it["rho"] - _dg(mbb, _bd(bf(kp[i])), 1, 0)
        yl = lmv[i][CH:] - _dg(mbb, _bd(bf(vp[i])), 1, 0)
        g = jnp.where(eye, it["gdiag"], 0.0) - _fold(_dg(bf(kp[i]), bf(it["bhat"]), 0, 0))
        hmat = _fold(_dg(bf(jnp.concatenate([it["v"], -vp[i]], axis=0)),
                         bf(jnp.concatenate([it["khat"], it["bhat"]], axis=0)), 0, 0))
        out.append((g, hmat, rp, yl))
    return out


def _rwkv_chunk_kernel(z_ref, hp_ref, hn_ref, mu_ref, w0_ref, w2_ref, a0_ref, a2_ref, kkw_ref, ka_ref, rk_ref,
                       bd_ref, g_ref, h_ref, rp_ref, yl_ref, bonus_ref, *, ct):
    i = pl.program_id(1)
    z = z_ref[0]
    zs = z[:, :RW_SHIFT]
    valid_prev = jnp.where((i != 0) & (i != ct), 1.0, 0.0)
    valid_next = jnp.where((i != ct - 1) & (i != ct), 1.0, 0.0)
    prev_row = hp_ref[0, HALO - 1:HALO, :RW_SHIFT] * valid_prev
    next_row = hn_ref[0, 0:1, :RW_SHIFT] * valid_next
    row = lax.broadcasted_iota(jnp.int32, (TM, RW_SHIFT), 0)
    tt = lax.broadcasted_iota(jnp.int32, (TM, TM), 0)
    jj = lax.broadcasted_iota(jnp.int32, (TM, TM), 1)
    same_chunk = (tt >> 6) == (jj >> 6)
    bd = bd_ref[...]
    def direction(d):
        if d == 0:
            nb = jnp.where(row == 0, prev_row, pltpu.roll(zs, 1, 0))
        else:
            nb = jnp.where(row == TM - 1, next_row, pltpu.roll(zs, TM - 1, 0))
        zd = zs + mu_ref[d:d + 1, :] * (nb - zs)
        r = zd[:, :RW]
        k = zd[:, RW:2 * RW]
        v = zd[:, 2 * RW:3 * RW]
        lw = zd[:, 3 * RW:3 * RW + LORA_W]
        la = zd[:, 3 * RW + LORA_W:]
        w_log = w0_ref[d:d + 1, :] + _dot3(jnp.tanh(lw), w2_ref[d, 0], w2_ref[d, 1])
        a_pre = a0_ref[d:d + 1, :] + _dot3(la, a2_ref[d, 0], a2_ref[d, 1])
        kkr = k * kkw_ref[...]
        ss = _sum_dot(kkr * kkr, bd, 2)
        yield
        sp = jnp.maximum(-w_log, 0.0) + jnp.log(1.0 + jnp.exp(-jnp.abs(w_log)))
        logw = -jnp.exp(-sp - 0.5)
        tri = jnp.where(same_chunk & ((jj >= tt) if d == 1 else (jj <= tt)), 1.0, 0.0).astype(BF16)
        cum, rest = None, logw
        for _ in range(3):
            piece = rest.astype(BF16)
            part = jnp.dot(tri, piece, preferred_element_type=F32)
            cum = part if cum is None else cum + part
            rest = rest - piece.astype(F32)
        a = _sigmoid(a_pre)
        kk = kkr * lax.rsqrt(jnp.maximum(ss, 1e-24))
        k2 = k * (1.0 + (a - 1.0) * ka_ref[...])
        bv = kk * a
        bon = _sum_dot(r * k2 * rk_ref[...], bd, 2) * v
        yield
        e_neg = jnp.exp(-cum)
        streams = dict(kap=kk * jnp.exp(cum - logw), kt=k2 * e_neg, bt=bv * e_neg, rho=r * jnp.exp(cum), v=v)
        chunk_items = []
        for c in range(TM // CH):
            sl = slice(c * CH, (c + 1) * CH)
            it = {name: val[sl] for name, val in streams.items()}
            last = c * CH if d == 1 else (c + 1) * CH - 1
            tot = cum[last:last + 1]
            e_tot = jnp.exp(tot - cum[sl])
            it.update(khat=k2[sl] * e_tot, bhat=bv[sl] * e_tot, gdiag=jnp.exp(tot), rev=d == 1)
            chunk_items.append(it)
        return chunk_items, bon

    gens = [direction(0), direction(1)]
    done = [None, None]
    while any(res is None for res in done):
        for d, gen in enumerate(gens):
            if done[d] is None:
                try:
                    next(gen)
                except StopIteration as stop:
                    done[d] = stop.value
    items = done[0][0] + done[1][0]
    bonus = done[0][1] + done[1][1]
    for idx, (g, hm, rp, yl) in enumerate(_chunk_mats(items)):
        d, c = divmod(idx, TM // CH)
        g_ref[0, d, c] = g.astype(BF16)
        h_ref[0, d, c] = hm
        rp_ref[0, d, c] = rp.astype(BF16)
        yl_ref[0, d, c] = yl
    bonus_ref[0] = bonus


def _rwkv_chunks(zr, mu, w0, w2, a0, a2, kkw, ka, rk, bd256, ct):
    b, tt, _ = zr.shape
    nt = tt // TM
    nch = tt // CH
    cpt = TM // CH
    nhb = tt // HALO
    mats = pl.BlockSpec((1, 2, cpt, CH, RW), lambda bi, i: (bi, 0, i, 0, 0))
    mshape = lambda dt: jax.ShapeDtypeStruct((b, 2, nch, CH, RW), dt)
    return pl.pallas_call(
        functools.partial(_rwkv_chunk_kernel, ct=ct),
        grid=(b, nt),
        in_specs=[
            pl.BlockSpec((1, TM, COLS_RW), lambda bi, i: (bi, i, 0)),
            pl.BlockSpec((1, HALO, COLS_RW), lambda bi, i: (bi, jnp.maximum(i * (TM // HALO) - 1, 0), 0)),
            pl.BlockSpec((1, HALO, COLS_RW), lambda bi, i: (bi, jnp.minimum((i + 1) * (TM // HALO), nhb - 1), 0)),
            _const_spec((2, RW_SHIFT)),
            _const_spec((2, RW)),
            _const_spec((2, 2, LORA_W, RW)),
            _const_spec((2, RW)),
            _const_spec((2, 2, LORA_A, RW)),
            _const_spec((1, RW)),
            _const_spec((1, RW)),
            _const_spec((1, RW)),
            _const_spec((RW, RW)),
        ],
        out_specs=[mats, mats, mats, mats, pl.BlockSpec((1, TM, RW), lambda bi, i: (bi, i, 0))],
        out_shape=[mshape(BF16), mshape(F32), mshape(BF16), mshape(F32), jax.ShapeDtypeStruct((b, tt, RW), F32)],
        compiler_params=_cparams(("parallel", "parallel")),
        name="rwkv_chunks",
    )(zr, zr, zr, mu, w0, w2, a0, a2, kkw, ka, rk, bd256)


CHAIN_GROUP = 4


def _rwkv_chain_kernel(gf_ref, hf_ref, rf_ref, yf_ref, gr_ref, hr_ref, rr_ref, yr_ref, of_ref, or_ref, s_ref,
                       *, nb):
    @pl.when(pl.program_id(0) == 0)
    def _():
        s_ref[...] = jnp.zeros(s_ref.shape, F32)

    dirs = ((gf_ref, hf_ref, rf_ref, yf_ref, of_ref), (gr_ref, hr_ref, rr_ref, yr_ref, or_ref))
    state = [[s_ref[d, bi] for bi in range(nb)] for d in range(2)]
    for step in range(CHAIN_GROUP):
        for d, (g_ref, h_ref, rp_ref, yl_ref, o_ref) in enumerate(dirs):
            c = step if d == 0 else CHAIN_GROUP - 1 - step
            for bi in range(nb):
                sb = state[d][bi].astype(BF16)
                o_ref[bi, c * CH:(c + 1) * CH, :] = _dg(rp_ref[bi, 0, c], _bd(sb), 1, 1) + yl_ref[bi, 0, c]
                state[d][bi] = _dg(sb, _bd(g_ref[bi, 0, c]), 1, 0) + h_ref[bi, 0, c]
    for d in range(2):
        for bi in range(nb):
            s_ref[d, bi] = state[d][bi]


def _rwkv_chain(g, h, rp, yl, seq, ctx_len):
    b, _, nch, _, _ = g.shape
    assert ctx_len == CHAIN_GROUP * CH and seq % (CHAIN_GROUP * CH) == 0
    ngrp = nch // CHAIN_GROUP
    n_lat = seq // (CHAIN_GROUP * CH)

    def gf(s):
        return jnp.where(s == 0, n_lat, s - 1)

    def gr(s):
        return ngrp - 1 - s

    fwd = pl.BlockSpec((b, 1, CHAIN_GROUP, CH, RW), lambda s: (0, 0, gf(s), 0, 0))
    rev = pl.BlockSpec((b, 1, CHAIN_GROUP, CH, RW), lambda s: (0, 1, gr(s), 0, 0))
    yshape = jax.ShapeDtypeStruct((b, nch * CH, RW), F32)
    return pl.pallas_call(
        functools.partial(_rwkv_chain_kernel, nb=b),
        grid=(ngrp,),
        in_specs=[fwd, fwd, fwd, fwd, rev, rev, rev, rev],
        out_specs=[pl.BlockSpec((b, CHAIN_GROUP * CH, RW), lambda s: (0, gf(s), 0)),
                   pl.BlockSpec((b, CHAIN_GROUP * CH, RW), lambda s: (0, gr(s), 0))],
        out_shape=[yshape, yshape],
        scratch_shapes=[pltpu.VMEM((2, b, CH, RW), F32)],
        compiler_params=_cparams(("arbitrary",)),
        name="rwkv_chain",
    )(g, h, rp, yl, g, h, rp, yl)


def _merge_kernel(*refs, nblk, ctx_mode):
    (x_ref, mod_ref, gpre_ref, gpost_ref, za_ref, hp_ref, hn_ref, ca_ref, yb_ref, yc_ref, yf_ref, yr_ref,
     bonus_ref, lg_ref, g2_ref, lnw_ref, lnb_ref, bd_ref, wb_ref, wg_ref, bg_ref, wo_ref) = refs[:22]
    o_ref = refs[-1]
    tm = x_ref.shape[1]
    x = x_ref[0]
    m = mod_ref[0, 0]
    hb = _rms_mod(x, gpre_ref[...], m[0:1], m[1:2]).astype(BF16)

    def gate_mm(bidx):
        return jnp.dot(hb, wg_ref[:, bidx * D_MODEL:(bidx + 1) * D_MODEL], preferred_element_type=F32)

    def gated(bidx, pre, ys):
        gate2 = jnp.tanh(pre + bg_ref[:, bidx * D_MODEL:(bidx + 1) * D_MODEL]) + 1.0
        return gate2 * jnp.dot(ys, wb_ref[bidx], preferred_element_type=F32)

    za = za_ref[0]
    u = za[:, 256:512] * za[:, 512:768]
    valid_prev, valid_next = _edge_valid(pl.program_id(1), nblk, ctx_mode)
    up = hp_ref[0, HALO - 1:HALO, 256:512] * hp_ref[0, HALO - 1:HALO, 512:768] * valid_prev
    un = hn_ref[0, 0:1, 256:512] * hn_ref[0, 0:1, 512:768] * valid_next
    row = lax.broadcasted_iota(jnp.int32, (tm, 256), 0)
    u_prev = jnp.where(row == 0, up, pltpu.roll(u, 1, 0))
    u_next = jnp.where(row == tm - 1, un, pltpu.roll(u, tm - 1, 0))
    ya = za[:, :256] * (u_prev * ca_ref[0:1, :] + u * ca_ref[1:2, :] + u_next * ca_ref[2:3, :])

    bd = bd_ref[...]
    ones_dot = lambda piece: jnp.dot(piece, bd, preferred_element_type=F32)
    y = yf_ref[0] + yr_ref[0]
    p1 = y.astype(BF16)
    s1 = ones_dot(p1)
    pre1 = gate_mm(1)
    r1 = y - p1.astype(F32)
    p2 = r1.astype(BF16)
    s2 = ones_dot(p2)
    acc = gated(1, pre1, yb_ref[0])
    s3 = ones_dot((r1 - p2.astype(F32)).astype(BF16))
    pre2 = gate_mm(2)
    yc0 = y - (s1 + s2 + s3) * (1.0 / HEAD_DIM)
    sq = yc0 * yc0
    q1 = sq.astype(BF16)
    v1 = ones_dot(q1)
    acc = acc + gated(2, pre2, yc_ref[0])
    v2 = ones_dot((sq - q1.astype(F32)).astype(BF16))
    pre0 = gate_mm(0)
    lgate = _dot3(_sigmoid(lg_ref[0]), g2_ref[0], g2_ref[1])
    acc = acc + gated(0, pre0, ya.astype(BF16))
    pre3 = gate_mm(3)
    var = (v1 + v2) * (1.0 / HEAD_DIM)
    yn = yc0 * lax.rsqrt(var + LN_X_EPS) * lnw_ref[...] + lnb_ref[...] + bonus_ref[0]
    acc = acc + gated(3, pre3, (yn * lgate).astype(BF16))
    mo = jnp.dot(acc.astype(BF16), wo_ref[...], preferred_element_type=F32)
    o_ref[0] = x + m[2:3] * _rms(mo, gpost_ref[...])


def _merge(xa, modsel, g_pre, g_post, za, conv_a, yb, yc, yf, yr, bonus, zr, g2, ln_w, ln_b, bd256,
           w_branch, w_gate, b_gate, w_o, seq, tm, prev=None):
    b, tt, _ = xa.shape
    ctx_mode = prev is not None
    nblk, blk, steps, alias_specs = _token_tiling(seq, tm, ctx_mode, 1)
    tile = lambda w: pl.BlockSpec((1, tm, w), lambda bi, i: (bi, blk(i), 0))
    return pl.pallas_call(
        functools.partial(_merge_kernel, nblk=nblk, ctx_mode=ctx_mode),
        grid=(b, steps),
        in_specs=[
            tile(D_MODEL),
            pl.BlockSpec((1, 1, 6, D_MODEL), lambda bi, i: (bi, int(ctx_mode), 0, 0)),
            _const_spec((1, D_MODEL)),
            _const_spec((1, D_MODEL)),
            tile(COLS_A),
            *_halo_specs(COLS_A, tm, blk, tt // HALO),
            _const_spec((3, 256)),
            tile(256), tile(256), tile(256), tile(256), tile(256),
            pl.BlockSpec((1, tm, LORA_G), lambda bi, i: (bi, blk(i), RW_SHIFT // LORA_G)),
            _const_spec((2, LORA_G, RW)),
            _const_spec((1, RW)),
            _const_spec((1, RW)),
            _const_spec((RW, RW)),
            _const_spec((4, 256, D_MODEL)),
            _const_spec((D_MODEL, 4 * D_MODEL)),
            _const_spec((1, 4 * D_MODEL)),
            _const_spec((D_MODEL, D_MODEL)),
        ] + alias_specs,
        out_specs=tile(D_MODEL),
        out_shape=jax.ShapeDtypeStruct((b, tt, D_MODEL), F32),
        input_output_aliases={22: 0} if ctx_mode else {},
        compiler_params=_cparams(("parallel", "parallel")),
        name="merge_ctx" if ctx_mode else "merge",
    )(xa, modsel, g_pre, g_post, za, za, za, conv_a, yb, yc, yf, yr, bonus, zr, g2, ln_w, ln_b, bd256,
      w_branch, w_gate, b_gate, w_o, *((prev,) if ctx_mode else ()))


def _ffn_kernel(*refs, nblk, ctx_mode):
    x_ref, hp_ref, hn_ref, mod_ref, gpre_ref, gpost_ref, wu_ref, cw_ref, wd_ref = refs[:9]
    o_ref, act_ref = refs[-2:]
    tm = x_ref.shape[1]
    x = x_ref[0]
    m = mod_ref[0, 0]
    xx = jnp.concatenate([hp_ref[0], x, hn_ref[0]], axis=0)
    nrow = tm + 2 * HALO
    row = lax.broadcasted_iota(jnp.int32, (nrow, 1), 0)
    valid_prev, valid_next = _edge_valid(pl.program_id(1), nblk, ctx_mode)
    rowmask = jnp.where(row < HALO, valid_prev, jnp.where(row >= tm + HALO, valid_next, 1.0))
    hb = (_rms_mod(xx, gpre_ref[...], m[3:4], m[4:5]) * rowmask).astype(BF16)

    def conv(u, col):
        w = cw_ref[:, col:col + FF_CHUNK]
        c = (pltpu.roll(u, 1, 0) * w[0:1] + u * w[1:2] + pltpu.roll(u, nrow - 1, 0) * w[2:3])
        return c[HALO:HALO + tm]

    for j in range(D_FF // FF_CHUNK):
        ca = conv(jnp.dot(hb, wu_ref[:, j * FF_CHUNK:(j + 1) * FF_CHUNK], preferred_element_type=F32),
                  j * FF_CHUNK)
        cg = conv(jnp.dot(hb, wu_ref[:, D_FF + j * FF_CHUNK:D_FF + (j + 1) * FF_CHUNK],
                          preferred_element_type=F32), D_FF + j * FF_CHUNK)
        act_ref[:, j * FF_CHUNK:(j + 1) * FF_CHUNK] = (ca * (cg * (1.0 + jnp.tanh(cg)))).astype(BF16)
    f = jnp.dot(act_ref[...], wd_ref[...], preferred_element_type=F32)
    o_ref[0] = x + m[5:6] * _rms(f, gpost_ref[...])


def _ffn(xa, modsel, g_pre, g_post, w_up, conv_w, w_down, seq, tm, prev=None, out_rows=None):
    b, tt, _ = xa.shape
    ctx_mode = prev is not None
    nblk, blk, steps, alias_specs = _token_tiling(seq, tm, ctx_mode, 1)
    return pl.pallas_call(
        functools.partial(_ffn_kernel, nblk=nblk, ctx_mode=ctx_mode),
        grid=(b, steps),
        in_specs=[
            pl.BlockSpec((1, tm, D_MODEL), lambda bi, i: (bi, blk(i), 0)),
            *_halo_specs(D_MODEL, tm, blk, tt // HALO),
            pl.BlockSpec((1, 1, 6, D_MODEL), lambda bi, i: (bi, int(ctx_mode), 0, 0)),
            _const_spec((1, D_MODEL)),
            _const_spec((1, D_MODEL)),
            _const_spec((D_MODEL, 2 * D_FF)),
            _const_spec((3, 2 * D_FF)),
            _const_spec((D_FF, D_MODEL)),
        ] + alias_specs,
        out_specs=pl.BlockSpec((1, tm, D_MODEL), lambda bi, i: (bi, blk(i), 0)),
        out_shape=jax.ShapeDtypeStruct((b, out_rows or tt, D_MODEL), F32),
        scratch_shapes=[pltpu.VMEM((tm, D_FF), BF16)],
        input_output_aliases={9: 0} if ctx_mode else {},
        compiler_params=_cparams(("parallel", "parallel")),
        name="conv_ffn_ctx" if ctx_mode else "conv_ffn",
    )(xa, xa, xa, modsel, g_pre, g_post, w_up, conv_w, w_down, *((prev,) if ctx_mode else ()))


def _rope_tables(seq, ctx_len):
    t = np.arange(seq)
    row = (t // GRID_W).astype(np.float32)
    col = (t % GRID_W).astype(np.float32)
    n_freq = HEAD_DIM // 4
    inv_freq = jnp.asarray(ROPE_THETA, F32) ** (-jnp.arange(n_freq, dtype=F32) / n_freq)
    ang = jnp.concatenate([jnp.asarray(row)[:, None] * inv_freq, jnp.asarray(col)[:, None] * inv_freq], axis=-1)
    cos = jnp.repeat(jnp.cos(ang), 2, axis=-1)
    sin = jnp.repeat(jnp.sin(ang), 2, axis=-1) * jnp.tile(jnp.asarray([-1.0, 1.0], F32), HEAD_DIM // 2)
    cos = jnp.concatenate([cos, jnp.ones((ctx_len, HEAD_DIM), F32)], axis=0)
    sin = jnp.concatenate([sin, jnp.zeros((ctx_len, HEAD_DIM), F32)], axis=0)
    nh = GQA_Q_HEADS + GQA_KV_HEADS
    return jnp.tile(cos, (1, nh)), jnp.tile(sin, (1, nh))


def _block_ones(n, scale):
    idx = np.arange(n) // HEAD_DIM
    return jnp.asarray((idx[:, None] == idx[None, :]).astype(np.float32) * scale)


def kernel(x, c, ctx, c_ctx, ada_w, ada_b, norm_mix_pre, norm_mix_post, norm_ffn_pre, norm_ffn_post, w_in, conv_a, na_bias, q_norm, k_norm, rw_mu, rw_w0, rw_w2, rw_a0, rw_a2, rw_kk, rw_ka, rw_rk, rw_g2, rw_ln_w, rw_ln_b, w_branch, w_gate, b_gate, w_o, ffn_up, ffn_conv, ffn_down):
    b, seq, _ = x.shape
    ctx_len = ctx.shape[1]
    depth = ada_w.shape[0]
    assert ctx_len == TM and seq % GQA_TQ == 0 and seq % TML == 0 and seq // TM >= 3 and b + 1 <= 8
    ct = seq // TM
    rows = seq // GRID_W

    cvec = jnp.zeros((8, D_MODEL), F32).at[:b].set(c).at[b].set(c_ctx)
    mods = _ada(cvec, ada_w, ada_b)
    cos_t, sin_t = _rope_tables(seq, ctx_len)
    bd384 = _block_ones(384, 1.0 / HEAD_DIM).astype(BF16)
    bd256 = _block_ones(256, 1.0).astype(BF16)

    def hi_lo(w):
        hi = w.astype(BF16)
        return jnp.stack([hi, (w - hi.astype(F32)).astype(BF16)], axis=-3)

    xa = jnp.concatenate([x, ctx], axis=1)
    for l in range(depth):
        ml = mods[l].reshape(8, 6, D_MODEL)
        modsel = jnp.stack([ml[:b], jnp.broadcast_to(ml[b][None], (b, 6, D_MODEL))], axis=1)
        row2 = lambda a: a.reshape(1, -1)
        gqk = jnp.concatenate([jnp.tile(q_norm[l], GQA_Q_HEADS) * (HEAD_DIM ** -0.5 * np.log2(np.e)),
                               jnp.tile(k_norm[l], GQA_KV_HEADS)]).reshape(1, -1)
        ip_args = (xa, modsel, row2(norm_mix_pre[l]), w_in[l].astype(BF16), cos_t, sin_t, gqk, bd384, seq)
        za, qn, kn, vn, qg, kg, vg, zr = _inproj(*ip_args, ctx_len, prev=_inproj(*ip_args, TML))
        yb = _na(qn, kn, vn, _na_bias_table(na_bias[l], rows), ct)
        yc = _gqa(qg, kg, vg, seq, ctx_len)
        g, h, rp, yl, bonus = _rwkv_chunks(zr, rw_mu[l], rw_w0[l], hi_lo(rw_w2[l]), rw_a0[l], hi_lo(rw_a2[l]),
                                           row2(rw_kk[l]), row2(rw_ka[l]), row2(rw_rk[l]), bd256, ct)
        yf, yr = _rwkv_chain(g, h, rp, yl, seq, ctx_len)
        mg_args = (xa, modsel, row2(norm_mix_pre[l]), row2(norm_mix_post[l]), za, conv_a[l], yb, yc, yf, yr,
                   bonus, zr, hi_lo(rw_g2[l]), row2(rw_ln_w[l]), row2(rw_ln_b[l]), bd256,
                   w_branch[l].astype(BF16), (0.5 * w_gate[l]).astype(BF16), row2(0.5 * b_gate[l]),
                   (0.5 * w_o[l]).astype(BF16), seq)
        xm = _merge(*mg_args, ctx_len, prev=_merge(*mg_args, TML))
        conv_w = jnp.concatenate([ffn_conv[l][:, :D_FF], 0.5 * ffn_conv[l][:, D_FF:]], axis=1)
        ff_args = (xm, modsel, row2(norm_ffn_pre[l]), row2(norm_ffn_post[l]), ffn_up[l].astype(BF16),
                   conv_w, ffn_down[l].astype(BF16), seq)
        if l < depth - 1:
            xa = _ffn(*ff_args, ctx_len, prev=_ffn(*ff_args, TML))
        else:
            xa = _ffn(*ff_args, TML, out_rows=seq)
    return xa
```
